```python
import jax, jax.numpy as jnp
from jax import lax
import numpy as np

D_MODEL = 1024
BATCH = 16
SEQ = 2048
DEPTH = 1

MEM_LEN = 256
EPS = 1e-6
ROPE_THETA = 10000.0
HG_HEADS = 4
HG_DK = 128
HG_DV = 128
HG_WIDTH = HG_HEADS * HG_DK
HG_CHUNK = 64
DSA_HEADS = 8
DSA_Q_LORA = 256
DSA_KV_LORA = 128
DSA_NOPE = 64
DSA_ROPE = 32
DSA_VDIM = 64
DSA_WIDTH = DSA_HEADS * DSA_VDIM
IDX_HEADS = 8
IDX_DIM = 64
TOPK_MAX = 256
Q_BLOCK = 128
MEM_HEADS = 4
MEM_DH = 128
MEM_WIDTH = MEM_HEADS * MEM_DH
D_FF = 4 * D_MODEL
N_BRANCH = 3
IN_SPLITS = (HG_WIDTH, HG_WIDTH, HG_WIDTH, HG_WIDTH,
             DSA_Q_LORA, DSA_KV_LORA, DSA_ROPE, IDX_DIM, IDX_HEADS,
             MEM_WIDTH,
             N_BRANCH * D_MODEL)
IN_WIDTH = sum(IN_SPLITS)

kernel_name = 'hybrid_hgrn2_dsa_memxattn_block'


def rmsnorm(x, w):
    x32 = x.astype(jnp.float32)
    y = x32 * lax.rsqrt(jnp.mean(x32 * x32, axis=-1, keepdims=True) + EPS)
    return (y * w.astype(jnp.float32)).astype(x.dtype)


def rope(x, pos):
    d = x.shape[-1]
    inv = ROPE_THETA ** (-jnp.arange(0, d, 2, dtype=jnp.float32) / d)
    ang = pos.astype(jnp.float32)[:, None] * inv[None, :]
    cos = jnp.cos(ang)[None, :, None, :]
    sin = jnp.sin(ang)[None, :, None, :]
    x32 = x.astype(jnp.float32)
    x1, x2 = x32[..., : d // 2], x32[..., d // 2:]
    return jnp.concatenate([x1 * cos - x2 * sin, x2 * cos + x1 * sin], axis=-1).astype(x.dtype)


def hgrn2_mix(q, f_logit, v, lb):
    B, T, _ = q.shape
    nc = T // HG_CHUNK
    f = lb + (1.0 - lb) * jax.nn.sigmoid(f_logit.astype(jnp.float32))
    log_f = jnp.log(f)
    k = 1.0 - f

    def to_chunks(a, d):
        return a.astype(jnp.float32).reshape(B, nc, HG_CHUNK, HG_HEADS, d).transpose(1, 0, 3, 2, 4)

    qc, kc, vc = to_chunks(q, HG_DK), to_chunks(k, HG_DK), to_chunks(v, HG_DV)
    bc = jnp.cumsum(to_chunks(log_f, HG_DK), axis=3)
    causal = jnp.tril(jnp.ones((HG_CHUNK, HG_CHUNK), dtype=bool))

    def step(S, inp):
        q_, k_, v_, b_ = inp
        o_inter = jnp.einsum('bhtk,bhkv->bhtv', q_ * jnp.exp(b_), S)
        diff = b_[:, :, :, None, :] - b_[:, :, None, :, :]
        decay = jnp.exp(jnp.where(causal[:, :, None], diff, -jnp.inf))
        A = jnp.einsum('bhtk,bhsk,bhtsk->bhts', q_, k_, decay)
        o = o_inter + jnp.einsum('bhts,bhsv->bhtv', A, v_)
        b_last = b_[:, :, -1:, :]
        S_new = (jnp.exp(b_last[:, :, 0, :])[..., None] * S
                 + jnp.einsum('bhsk,bhsv->bhkv', k_ * jnp.exp(b_last - b_), v_))
        return S_new, o

    S0 = jnp.zeros((B, HG_HEADS, HG_DK, HG_DV), jnp.float32)
    _, o = lax.scan(step, S0, (qc, kc, vc, bc))
    return o.transpose(1, 0, 3, 2, 4).reshape(B, T, HG_HEADS, HG_DV).astype(q.dtype)


def dsa_mix(c_q, c_kv, k_rope, k_idx, w_idx, pos, w_uq, w_qr, w_uk, w_uv, w_qidx, topk):
    B, T, _ = c_q.shape
    q_nope = jnp.einsum('btc,chn->bthn', c_q, w_uq)
    q_rope = rope(jnp.einsum('btc,chr->bthr', c_q, w_qr), pos)
    q_abs = jnp.einsum('bthn,hnc->bthc', q_nope, w_uk)
    q_cat = jnp.concatenate([q_abs, q_rope], axis=-1)
    q_idx = rope(jnp.einsum('btc,chd->bthd', c_q, w_qidx), pos)
    k_rope = rope(k_rope[:, :, None, :], pos)[:, :, 0]
    k_idx = rope(k_idx[:, :, None, :], pos)[:, :, 0]
    kv = jnp.concatenate([c_kv, k_rope], axis=-1)
    scale = (DSA_NOPE + DSA_ROPE) ** -0.5
    nb = T // Q_BLOCK
    key_pos = jnp.arange(T, dtype=jnp.int32)

    def blocks(a):
        return a.reshape(B, nb, Q_BLOCK, *a.shape[2:]).swapaxes(0, 1)

    def one_block(inp):
        qc_, qi_, wi_, tpos = inp
        s = jax.nn.relu(jnp.einsum('bthd,bsd->bths', qi_, k_idx).astype(jnp.float32) * IDX_DIM ** -0.5)
        score = jnp.einsum('bths,bth->bts', s, wi_.astype(jnp.float32) * IDX_HEADS ** -0.5)
        causal = key_pos[None, :] <= tpos[:, None]
        score = jnp.where(causal[None], score, -jnp.inf)
        _, idx = lax.top_k(score, topk)
        valid = idx <= tpos[None, :, None]
        sel = jax.vmap(lambda kv_b, idx_b: kv_b[idx_b])(kv, idx)
        logits = jnp.einsum('bthc,btkc->bthk', qc_, sel).astype(jnp.float32) * scale
        logits = jnp.where(valid[:, :, None, :], logits, -jnp.inf)
        p = jax.nn.softmax(logits, axis=-1).astype(sel.dtype)
        return jnp.einsum('bthk,btkc->bthc', p, sel[..., :DSA_KV_LORA])

    o_lat = lax.map(one_block, (blocks(q_cat), blocks(q_idx), blocks(w_idx), pos.reshape(nb, Q_BLOCK)))
    o_lat = o_lat.swapaxes(0, 1).reshape(B, T, DSA_HEADS, DSA_KV_LORA)
    return jnp.einsum('bthc,hcv->bthv', o_lat, w_uv).reshape(B, T, DSA_WIDTH)


def mem_cross(q, mem, mem_norm_w, w_mem_kv):
    B, T, _ = q.shape
    mn = rmsnorm(mem, mem_norm_w)
    kvm = mn @ w_mem_kv
    k_m = kvm[..., :MEM_WIDTH].reshape(B, -1, MEM_HEADS, MEM_DH)
    v_m = kvm[..., MEM_WIDTH:].reshape(B, -1, MEM_HEADS, MEM_DH)
    q_m = q.reshape(B, T, MEM_HEADS, MEM_DH)
    logits = jnp.einsum('bthd,bmhd->bhtm', q_m, k_m).astype(jnp.float32) * MEM_DH ** -0.5
    p = jax.nn.softmax(logits, axis=-1).astype(v_m.dtype)
    return jnp.einsum('bhtm,bmhd->bthd', p, v_m).reshape(B, T, MEM_WIDTH)


def setup_inputs(seed: int = 0) -> dict:
    key = jax.random.key(seed)
    ks = jax.random.split(key, 32)
    f32 = jnp.float32

    def nrm(k, shape, fan_in):
        return jax.random.normal(k, shape, f32) * fan_in ** -0.5

    def gain(k, shape):
        return 1.0 + 0.05 * jax.random.normal(k, shape, f32)

    L = DEPTH
    return {
        'x': jax.random.normal(ks[0], (BATCH, SEQ, D_MODEL), f32),
        'mem': jax.random.normal(ks[1], (BATCH, MEM_LEN, D_MODEL), f32),
        'attn_norm_w': gain(ks[2], (L, D_MODEL)),
        'w_in': nrm(ks[3], (L, D_MODEL, IN_WIDTH), D_MODEL),
        'hgrn_lower_bounds': 0.5 * jax.random.normal(ks[4], (L + 1, HG_WIDTH), f32),
        'hgrn_out_norm_w': gain(ks[5], (L, HG_WIDTH)),
        'dsa_q_norm_w': gain(ks[6], (L, DSA_Q_LORA)),
        'dsa_kv_norm_w': gain(ks[7], (L, DSA_KV_LORA)),
        'dsa_kidx_norm_w': gain(ks[8], (L, IDX_DIM)),
        'dsa_w_uq': nrm(ks[9], (L, DSA_Q_LORA, DSA_HEADS, DSA_NOPE), DSA_Q_LORA),
        'dsa_w_qr': nrm(ks[10], (L, DSA_Q_LORA, DSA_HEADS, DSA_ROPE), DSA_Q_LORA),
        'dsa_w_uk': nrm(ks[11], (L, DSA_HEADS, DSA_NOPE, DSA_KV_LORA), DSA_NOPE),
        'dsa_w_uv': nrm(ks[12], (L, DSA_HEADS, DSA_KV_LORA, DSA_VDIM), DSA_KV_LORA),
        'dsa_w_qidx': nrm(ks[13], (L, DSA_Q_LORA, IDX_HEADS, IDX_DIM), DSA_Q_LORA),
        'mem_norm_w': gain(ks[14], (L, D_MODEL)),
        'w_mem_kv': nrm(ks[15], (L, D_MODEL, 2 * MEM_WIDTH), D_MODEL),
        'w_br_hgrn': nrm(ks[16], (L, HG_WIDTH, D_MODEL), HG_WIDTH),
        'w_br_dsa': nrm(ks[17], (L, DSA_WIDTH, D_MODEL), DSA_WIDTH),
        'w_br_mem': nrm(ks[18], (L, MEM_WIDTH, D_MODEL), MEM_WIDTH),
        'w_out': nrm(ks[19], (L, D_MODEL, D_MODEL), D_MODEL),
        'mlp_norm_w': gain(ks[20], (L, D_MODEL)),
        'w_mlp_up': nrm(ks[21], (L, D_MODEL, D_FF), D_MODEL),
        'w_mlp_down': nrm(ks[22], (L, D_FF, D_MODEL), D_FF),
        'final_norm_w': gain(ks[23], (D_MODEL,)),
    }


def reference(x, mem, attn_norm_w, w_in, hgrn_lower_bounds, hgrn_out_norm_w, dsa_q_norm_w, dsa_kv_norm_w,
              dsa_kidx_norm_w, dsa_w_uq, dsa_w_qr, dsa_w_uk, dsa_w_uv, dsa_w_qidx, mem_norm_w, w_mem_kv,
              w_br_hgrn, w_br_dsa, w_br_mem, w_out, mlp_norm_w, w_mlp_up, w_mlp_down, final_norm_w):
    B, T, D = x.shape
    pos = jnp.arange(T, dtype=jnp.int32)
    topk = min(TOPK_MAX, T // 4)
    split_at = [int(s) for s in np.cumsum(IN_SPLITS)[:-1]]
    lb_all = jnp.cumsum(jax.nn.softmax(hgrn_lower_bounds.astype(jnp.float32), axis=0), axis=0)
    h = x
    for l in range(DEPTH):
        xn = rmsnorm(h, attn_norm_w[l])
        proj = xn @ w_in[l]
        hq, hf, hv, hg, cq, ckv, kr, kidx, widx, mq, gate_logits = jnp.split(proj, split_at, axis=-1)
        o_a = hgrn2_mix(hq, hf, hv, lb_all[l])
        o_a = rmsnorm(o_a, hgrn_out_norm_w[l].reshape(HG_HEADS, HG_DV)) * jax.nn.silu(hg.reshape(B, T, HG_HEADS, HG_DV))
        o_a = o_a.reshape(B, T, HG_WIDTH)
        o_b = dsa_mix(rmsnorm(cq, dsa_q_norm_w[l]), rmsnorm(ckv, dsa_kv_norm_w[l]), kr,
                      rmsnorm(kidx, dsa_kidx_norm_w[l]), widx, pos,
                      dsa_w_uq[l], dsa_w_qr[l], dsa_w_uk[l], dsa_w_uv[l], dsa_w_qidx[l], topk)
        o_c = mem_cross(mq, mem, mem_norm_w[l], w_mem_kv[l])
        gates = jax.nn.sigmoid(gate_logits).reshape(B, T, N_BRANCH, D)
        merged = (gates[:, :, 0] * (o_a @ w_br_hgrn[l])
                  + gates[:, :, 1] * (o_b @ w_br_dsa[l])
                  + gates[:, :, 2] * (o_c @ w_br_mem[l]))
        h = h + merged @ w_out[l]
        m = rmsnorm(h, mlp_norm_w[l])
        h = h + jnp.square(jax.nn.relu(m @ w_mlp_up[l])) @ w_mlp_down[l]
    return rmsnorm(h, final_norm_w)
```

```python
import functools

import jax
import jax.numpy as jnp
import numpy as np
from jax import lax
from jax.experimental import pallas as pl
from jax.experimental.pallas import tpu as pltpu

F32 = jnp.float32
BF16 = jnp.bfloat16
I32 = jnp.int32

EPS = 1e-6
ROPE_THETA = 10000.0
LANES = 128
HG_HEADS, HG_D, HG_CHUNK = 4, 128, 64
HG_WIDTH = HG_HEADS * HG_D
DSA_HEADS, DSA_Q_LORA, DSA_KV_LORA = 8, 256, 128
DSA_NOPE, DSA_ROPE, DSA_VDIM = 64, 32, 64
IDX_HEADS, IDX_DIM = 8, 64
TOPK_MAX = 256
MEM_HEADS, MEM_DH = 4, 128
MEM_WIDTH = MEM_HEADS * MEM_DH
N_BRANCH = 3
DSA_GROUP = 6 * LANES
VMEM_LIMIT = 56 * 1024 * 1024
INT_MIN = np.int32(-2**31)

DSA_TQ = 128
DSA_TK = 512


def _cparams(sem):
    return pltpu.CompilerParams(dimension_semantics=sem, vmem_limit_bytes=VMEM_LIMIT)


def _rms(x, w, n=None):
    n = x.shape[-1] if n is None else n
    ms = jnp.sum(x * x, axis=-1, keepdims=True) * (1.0 / n)
    return x * lax.rsqrt(ms + EPS) * w


def _sigmoid(x):
    return 1.0 / (1.0 + jnp.exp(-x))


def _dot(a, b):
    return jnp.dot(a, b, preferred_element_type=F32)


def _dot_nt(a, b):
    return lax.dot_general(a, b, (((1,), (1,)), ((), ())), preferred_element_type=F32)


def _dot_tn(a, b):
    return lax.dot_general(a, b, (((0,), (0,)), ((), ())), preferred_element_type=F32)


def _norm_proj_kernel(x_ref, nw_ref, w_ref, *out_refs):
    xn = _rms(x_ref[...], nw_ref[...]).astype(BF16)
    off = 0
    for o_ref in out_refs:
        wd = o_ref.shape[-1]
        o_ref[...] = _dot(xn, w_ref[:, off:off + wd]).astype(o_ref.dtype)
        off += wd


def _norm_proj(x, nw, w, widths, dtypes, tm):
    n, d = x.shape
    return pl.pallas_call(
        _norm_proj_kernel,
        grid=(n // tm,),
        in_specs=[pl.BlockSpec((tm, d), lambda i: (i, 0)),
                  pl.BlockSpec((1, d), lambda i: (0, 0)),
                  pl.BlockSpec(w.shape, lambda i: (0, 0))],
        out_specs=[pl.BlockSpec((tm, wd), lambda i: (i, 0)) for wd in widths],
        out_shape=[jax.ShapeDtypeStruct((n, wd), dt) for wd, dt in zip(widths, dtypes)],
        compiler_params=_cparams(("parallel",)),
        name="norm_proj",
    )(x, nw.reshape(1, d), w)


def _hgrn_kernel(q_ref, f_ref, v_ref, g_ref, lb_ref, onw_ref, o_ref, st_ref, b_ref, k_ref):
    c = pl.program_id(1)

    @pl.when(c == 0)
    def _():
        st_ref[...] = jnp.zeros_like(st_ref)

    C = HG_CHUNK
    lb = lb_ref[...]
    f = lb + (1.0 - lb) * _sigmoid(f_ref[...])
    lf = jnp.log(f)
    hi = lf.astype(BF16)
    r1 = lf - hi.astype(F32)
    mid = r1.astype(BF16)
    lo = (r1 - mid.astype(F32)).astype(BF16)
    ri = lax.broadcasted_iota(I32, (C, C), 0)
    ci = lax.broadcasted_iota(I32, (C, C), 1)
    tri = jnp.where(ri >= ci, 1.0, 0.0).astype(BF16)
    b_ref[...] = _dot(tri, hi) + _dot(tri, mid) + _dot(tri, lo)
    k_ref[...] = 1.0 - f

    for h in range(HG_HEADS):
        sl = slice(h * HG_D, (h + 1) * HG_D)
        qh = q_ref[:, sl]
        vh = v_ref[:, sl]
        bh = b_ref[:, sl]
        kh = k_ref[:, sl]
        a_mat = jnp.zeros((C, C), F32)
        for sb in range(C // 8):
            r0 = sb * 8
            bt = bh[r0:, :]
            qt = qh[r0:, :]
            rows = C - r0
            row_i = lax.broadcasted_iota(I32, (rows, HG_D), 0)
            lane_s = lax.broadcasted_iota(I32, (rows, C), 1)
            a_bot = a_mat[r0:, :]
            for j in range(8):
                s = r0 + j
                brow = b_ref[s:s + 1, sl]
                krow = k_ref[s:s + 1, sl]
                e = jnp.exp(jnp.where(row_i >= j, bt - brow, -jnp.inf))
                r = jnp.sum(e * (qt * krow), axis=-1, keepdims=True)
                a_bot = jnp.where(lane_s == s, r, a_bot)
            a_mat = a_bot if r0 == 0 else jnp.concatenate([a_mat[:r0, :], a_bot], axis=0)
        st = st_ref[h]
        qe = (qh * jnp.exp(bh)).astype(BF16)
        vb = vh.astype(BF16)
        o = _dot_nt(qe, st.astype(BF16)) + _dot(a_mat.astype(BF16), vb)
        blast = bh[C - 1:C, :]
        kd = (kh * jnp.exp(blast - bh)).astype(BF16)
        st_ref[h] = st * jnp.exp(blast) + _dot_tn(vb, kd)
        gh = g_ref[:, sl]
        o_ref[:, sl] = (_rms(o, onw_ref[:, sl]) * (gh * _sigmoid(gh))).astype(o_ref.dtype)


def _hgrn(proj_h, lb, onw, batch, seq):
    n = proj_h.shape[0]
    nc = seq // HG_CHUNK
    blk = lambda col: pl.BlockSpec((HG_CHUNK, HG_WIDTH), lambda b, c, col=col: (b * nc + c, col))
    vec = pl.BlockSpec((1, HG_WIDTH), lambda b, c: (0, 0))
    return pl.pallas_call(
        _hgrn_kernel,
        grid=(batch, nc),
        in_specs=[blk(0), blk(1), blk(2), blk(3), vec, vec],
        out_specs=pl.BlockSpec((HG_CHUNK, HG_WIDTH), lambda b, c: (b * nc + c, 0)),
        out_shape=jax.ShapeDtypeStruct((n, HG_WIDTH), BF16),
        scratch_shapes=[pltpu.VMEM((HG_HEADS, HG_D, HG_D), F32),
                        pltpu.VMEM((HG_CHUNK, HG_WIDTH), F32),
                        pltpu.VMEM((HG_CHUNK, HG_WIDTH), F32)],
        compiler_params=_cparams(("parallel", "arbitrary")),
        name="hgrn2",
    )(proj_h, proj_h, proj_h, proj_h, lb.reshape(1, HG_WIDTH), onw.reshape(1, HG_WIDTH))


def _rope_lanes(x, cosf, sins, half):
    nl = x.shape[-1]
    lane = lax.broadcasted_iota(I32, x.shape, 1)
    partner = jnp.where(lane % LANES < half, pltpu.roll(x, nl - half, 1), pltpu.roll(x, half, 1))
    return x * cosf + partner * sins


def _dsa_prep_kernel(pd_ref, qnw_ref, kvnw_ref, kinw_ref, wuq_ref, wqr_ref, wuk_ref, wqi_ref,
                     c32_ref, s32_ref, c64_ref, s64_ref,
                     qcat_ref, qidx_ref, kv_ref, kidx_ref, widx_ref):
    scale = float((DSA_NOPE + DSA_ROPE) ** -0.5)
    c32, s32, c64, s64 = c32_ref[...], s32_ref[...], c64_ref[...], s64_ref[...]
    cqn = _rms(pd_ref[:, 0:DSA_Q_LORA], qnw_ref[...]).astype(BF16)
    qn = _dot(cqn, wuq_ref[...]).astype(BF16)
    qr = _dot(cqn, wqr_ref[...])
    qi = _dot(cqn, wqi_ref[...])
    for h in range(DSA_HEADS):
        sl = slice(h * LANES, (h + 1) * LANES)
        qcat_ref[h, :, 0:LANES] = (_dot(qn[:, sl], wuk_ref[h]) * scale).astype(BF16)
        qcat_ref[h, :, LANES:2 * LANES] = (_rope_lanes(qr[:, sl], c32, s32, DSA_ROPE // 2) * scale).astype(BF16)
        qidx_ref[h] = (_rope_lanes(qi[:, sl], c64, s64, IDX_DIM // 2) * float(IDX_DIM ** -0.5)).astype(BF16)
    kv_ref[:, 0:LANES] = _rms(pd_ref[:, 256:384], kvnw_ref[...]).astype(BF16)
    kv_ref[:, LANES:2 * LANES] = _rope_lanes(pd_ref[:, 384:512], c32, s32, DSA_ROPE // 2).astype(BF16)
    kin = _rms(pd_ref[:, 512:640], kinw_ref[...], n=IDX_DIM)
    kidx_ref[...] = _rope_lanes(kin, c64, s64, IDX_DIM // 2).astype(BF16)
    widx_ref[...] = pd_ref[:, 640:768] * float(IDX_HEADS ** -0.5)


def _dsa_prep(proj_d, qnw, kvnw, kinw_pad, wuq_pad, wqr_pad, wuk_pad, wqi_pad, tabs, seq, tm):
    n = proj_d.shape[0]
    nt = seq // tm
    full = lambda a: pl.BlockSpec(a.shape, lambda i: (0,) * a.ndim)
    tab = pl.BlockSpec((tm, LANES), lambda i: (i % nt, 0))
    hd = DSA_HEADS
    return pl.pallas_call(
        _dsa_prep_kernel,
        grid=(n // tm,),
        in_specs=[pl.BlockSpec((tm, DSA_GROUP), lambda i: (i, 0)),
                  full(qnw), full(kvnw), full(kinw_pad), full(wuq_pad), full(wqr_pad), full(wuk_pad),
                  full(wqi_pad), tab, tab, tab, tab],
        out_specs=[pl.BlockSpec((hd, tm, 2 * LANES), lambda i: (0, i, 0)),
                   pl.BlockSpec((hd, tm, LANES), lambda i: (0, i, 0)),
                   pl.BlockSpec((tm, 2 * LANES), lambda i: (i, 0)),
                   pl.BlockSpec((tm, LANES), lambda i: (i, 0)),
                   pl.BlockSpec((tm, LANES), lambda i: (i, 0))],
        out_shape=[jax.ShapeDtypeStruct((hd, n, 2 * LANES), BF16),
                   jax.ShapeDtypeStruct((hd, n, LANES), BF16),
                   jax.ShapeDtypeStruct((n, 2 * LANES), BF16),
                   jax.ShapeDtypeStruct((n, LANES), BF16),
                   jax.ShapeDtypeStruct((n, LANES), F32)],
        compiler_params=_cparams(("parallel",)),
        name="dsa_prep",
    )(proj_d, qnw, kvnw, kinw_pad, wuq_pad, wqr_pad, wuk_pad, wqi_pad, *tabs)


def _dsa_kernel(qcat_ref, qidx_ref, widx_ref, kv_ref, kidx_ref, wuv_ref, o_ref,
                sc_ref, ik_ref, lg_ref, *, topk):
    TQ, TK, H = DSA_TQ, DSA_TK, DSA_HEADS
    qi = pl.program_id(1)
    nkt = (qi * TQ + TQ + TK - 1) // TK
    tpos = qi * TQ + lax.broadcasted_iota(I32, (TQ, TK), 0)
    lane_k = lax.broadcasted_iota(I32, (TQ, TK), 1)
    qidx = qidx_ref[...].reshape(H * TQ, LANES)
    qcat = qcat_ref[...].reshape(H * TQ, 2 * LANES)
    wcols = [widx_ref[:, h:h + 1] for h in range(IDX_HEADS)]

    def key_rows(kt):
        return pl.ds(pl.multiple_of(kt * TK, TK), TK)

    def score_tile(kt, carry):
        s_all = jnp.maximum(_dot_nt(qidx, kidx_ref[key_rows(kt), :]), 0.0)
        score = s_all[0:TQ] * wcols[0]
        for h in range(1, IDX_HEADS):
            score = score + s_all[h * TQ:(h + 1) * TQ] * wcols[h]
        bits = lax.bitcast_convert_type(score + 0.0, I32)
        key = jnp.where(bits < 0, bits ^ np.int32(0x7FFFFFFF), bits)
        ik_ref[kt] = jnp.where(kt * TK + lane_k <= tpos, key, INT_MIN)
        return carry

    lax.fori_loop(0, nkt, score_tile, 0)

    def count(pred_fn):
        def body(kt, acc):
            m = jnp.where(pred_fn(ik_ref[kt], kt), 1.0, 0.0)
            part = m[:, 0:LANES]
            for g in range(1, TK // LANES):
                part = part + m[:, g * LANES:(g + 1) * LANES]
            return acc + part
        acc = lax.fori_loop(0, nkt, body, jnp.zeros((TQ, LANES), F32))
        return jnp.sum(acc, axis=-1, keepdims=True)

    kf = float(topk)

    def bit_step(i, tu):
        cand = tu | lax.shift_left(np.int32(1), np.int32(31) - i.astype(I32))
        cs = cand ^ INT_MIN
        cnt = count(lambda ik, kt: ik >= cs)
        return jnp.where(cnt >= kf, cand, tu)

    tu = lax.fori_loop(0, 32, bit_step, jnp.zeros((TQ, 1), I32))
    thr = tu ^ INT_MIN
    cnt_gt = count(lambda ik, kt: ik > thr)
    cnt_ge = count(lambda ik, kt: ik >= thr)
    need = kf - cnt_gt

    has_tie = jnp.max(jnp.where((cnt_ge > kf) & (thr > INT_MIN), 1.0, 0.0)) > 0.0
    nbits = int(np.log2(TK)) + 3

    def tie_search(_):
        def idx_step(i, x):
            cand = x | lax.shift_left(np.int32(1), np.int32(nbits - 1) - i.astype(I32))
            g = count(lambda ik, kt: (ik == thr) & (kt * TK + lane_k < cand))
            return jnp.where(g < need, cand, x)
        return lax.fori_loop(0, nbits, idx_step, jnp.zeros((TQ, 1), I32))

    jcut = lax.cond(has_tie, tie_search, lambda _: jnp.full((TQ, 1), 2**30, I32), 0)
    jcut = jnp.where(thr > INT_MIN, jcut, -1)

    def bias_tile(kt, carry):
        ik = ik_ref[kt]
        sel = (ik > thr) | ((ik == thr) & (kt * TK + lane_k <= jcut))
        sc_ref[kt] = jnp.where(sel, 0.0, -jnp.inf)
        return carry

    lax.fori_loop(0, nkt, bias_tile, 0)

    def logit_tile(kt, m_run):
        lg = _dot_nt(qcat, kv_ref[key_rows(kt), :]).reshape(H, TQ, TK) + sc_ref[kt][None]
        lg = lg.reshape(H * TQ, TK)
        lg_ref[kt] = lg
        part = lg[:, 0:LANES]
        for g in range(1, TK // LANES):
            part = jnp.maximum(part, lg[:, g * LANES:(g + 1) * LANES])
        return jnp.maximum(m_run, part)

    m_run = lax.fori_loop(0, nkt, logit_tile, jnp.full((H * TQ, LANES), -jnp.inf, F32))
    m_row = jnp.max(m_run, axis=-1, keepdims=True)

    def pv_tile(kt, carry):
        l_run, acc = carry
        p = jnp.exp(lg_ref[kt] - m_row)
        part = p[:, 0:LANES]
        for g in range(1, TK // LANES):
            part = part + p[:, g * LANES:(g + 1) * LANES]
        acc = acc + _dot(p.astype(BF16), kv_ref[key_rows(kt), 0:LANES])
        return l_run + part, acc

    l_run, acc = lax.fori_loop(0, nkt, pv_tile,
                               (jnp.zeros((H * TQ, LANES), F32), jnp.zeros((H * TQ, LANES), F32)))
    o_lat = (acc / jnp.sum(l_run, axis=-1, keepdims=True)).astype(BF16)
    for pr in range(H // 2):
        h0, h1 = 2 * pr, 2 * pr + 1
        o_ref[:, pr * LANES:(pr + 1) * LANES] = (
            _dot(o_lat[h0 * TQ:(h0 + 1) * TQ], wuv_ref[h0]) + _dot(o_lat[h1 * TQ:(h1 + 1) * TQ], wuv_ref[h1])
        ).astype(o_ref.dtype)


def _dsa(qcat, qidx, widx, kv, kidx, wuv_pad, batch, seq, topk):
    n = kv.shape[0]
    nq = seq // DSA_TQ
    nkt_max = seq // DSA_TK
    H = DSA_HEADS
    return pl.pallas_call(
        functools.partial(_dsa_kernel, topk=topk),
        grid=(batch, nq),
        in_specs=[pl.BlockSpec((H, DSA_TQ, 2 * LANES), lambda b, q: (0, b * nq + q, 0)),
                  pl.BlockSpec((H, DSA_TQ, LANES), lambda b, q: (0, b * nq + q, 0)),
                  pl.BlockSpec((DSA_TQ, LANES), lambda b, q: (b * nq + q, 0)),
                  pl.BlockSpec((seq, 2 * LANES), lambda b, q: (b, 0)),
                  pl.BlockSpec((seq, LANES), lambda b, q: (b, 0)),
                  pl.BlockSpec(wuv_pad.shape, lambda b, q: (0, 0, 0))],
        out_specs=pl.BlockSpec((DSA_TQ, H * DSA_VDIM), lambda b, q: (b * nq + q, 0)),
        out_shape=jax.ShapeDtypeStruct((n, H * DSA_VDIM), BF16),
        scratch_shapes=[pltpu.VMEM((nkt_max, DSA_TQ, DSA_TK), F32),
                        pltpu.VMEM((nkt_max, DSA_TQ, DSA_TK), I32),
                        pltpu.VMEM((nkt_max, H * DSA_TQ, DSA_TK), F32)],
        compiler_params=_cparams(("parallel", "arbitrary")),
        name="dsa_attn",
    )(qcat, qidx, widx, kv, kidx, wuv_pad)


def _mem_attn_kernel(q_ref, k_ref, v_ref, o_ref):
    scale = float(MEM_DH ** -0.5)
    for h in range(MEM_HEADS):
        sl = slice(h * MEM_DH, (h + 1) * MEM_DH)
        lg = _dot_nt((q_ref[:, sl] * scale).astype(BF16), k_ref[:, sl])
        p = jnp.exp(lg - jnp.max(lg, axis=-1, keepdims=True))
        o = _dot(p.astype(BF16), v_ref[:, sl]) / jnp.sum(p, axis=-1, keepdims=True)
        o_ref[:, sl] = o.astype(o_ref.dtype)


def _mem_attn(proj_m, k_m, v_m, seq, tm):
    n = proj_m.shape[0]
    nt = seq // tm
    m = k_m.shape[0] // (n // seq)
    return pl.pallas_call(
        _mem_attn_kernel,
        grid=(n // tm,),
        in_specs=[pl.BlockSpec((tm, MEM_WIDTH), lambda i: (i, 0)),
                  pl.BlockSpec((m, MEM_WIDTH), lambda i: (i // nt, 0)),
                  pl.BlockSpec((m, MEM_WIDTH), lambda i: (i // nt, 0))],
        out_specs=pl.BlockSpec((tm, MEM_WIDTH), lambda i: (i, 0)),
        out_shape=jax.ShapeDtypeStruct((n, MEM_WIDTH), BF16),
        compiler_params=_cparams(("parallel",)),
        name="mem_attn",
    )(proj_m, k_m, v_m)


def _merge_kernel(x_ref, nw_ref, oa_ref, ob_ref, oc_ref, wg_ref, wbr_ref, wo_ref, h_ref):
    x = x_ref[...]
    d = x.shape[-1]
    xn = _rms(x, nw_ref[...]).astype(BF16)
    merged = None
    for br, o_br in enumerate((oa_ref, ob_ref, oc_ref)):
        gate = _sigmoid(_dot(xn, wg_ref[:, br * d:(br + 1) * d]))
        term = gate * _dot(o_br[...], wbr_ref[br])
        merged = term if merged is None else merged + term
    h_ref[...] = x + _dot(merged.astype(BF16), wo_ref[...])


def _merge(x, nw, o_a, o_b, o_c, w_gate, w_br, w_out, tm):
    n, d = x.shape
    row = lambda wd: pl.BlockSpec((tm, wd), lambda i: (i, 0))
    full = lambda a: pl.BlockSpec(a.shape, lambda i: (0,) * a.ndim)
    return pl.pallas_call(
        _merge_kernel,
        grid=(n // tm,),
        in_specs=[row(d), full(nw), row(o_a.shape[1]), row(o_b.shape[1]), row(o_c.shape[1]),
                  full(w_gate), full(w_br), full(w_out)],
        out_specs=row(d),
        out_shape=jax.ShapeDtypeStruct((n, d), F32),
        compiler_params=_cparams(("parallel",)),
        name="merge_out",
    )(x, nw, o_a, o_b, o_c, w_gate, w_br, w_out)


def _mlp_kernel(h_ref, nw_ref, wu_ref, wd_ref, fw_ref, o_ref, *, ff_chunk):
    h = h_ref[...]
    m = _rms(h, nw_ref[...]).astype(BF16)
    acc = h
    for c0 in range(0, wu_ref.shape[1], ff_chunk):
        u = jnp.maximum(_dot(m, wu_ref[:, c0:c0 + ff_chunk]), 0.0)
        acc = acc + _dot((u * u).astype(BF16), wd_ref[c0:c0 + ff_chunk, :])
    o_ref[...] = _rms(acc, fw_ref[...])


def _mlp(h, nw, w_up, w_down, fw, tm, ff_chunk=1024):
    n, d = h.shape
    full = lambda a: pl.BlockSpec(a.shape, lambda i: (0,) * a.ndim)
    return pl.pallas_call(
        functools.partial(_mlp_kernel, ff_chunk=ff_chunk),
        grid=(n // tm,),
        in_specs=[pl.BlockSpec((tm, d), lambda i: (i, 0)), full(nw), full(w_up), full(w_down), full(fw)],
        out_specs=pl.BlockSpec((tm, d), lambda i: (i, 0)),
        out_shape=jax.ShapeDtypeStruct((n, d), F32),
        compiler_params=_cparams(("parallel",)),
        name="mlp_final",
    )(h, nw, w_up, w_down, fw)


def _pad_cols(w, width):
    return jnp.pad(w, ((0, 0), (0, width - w.shape[1])))


def _head_pad(w, width):
    return jnp.pad(w, ((0, 0), (0, 0), (0, width - w.shape[2]))).reshape(w.shape[0], -1)


def _rope_tables(seq, d):
    inv = ROPE_THETA ** (-jnp.arange(0, d, 2, dtype=F32) / d)
    ang = jnp.arange(seq, dtype=F32)[:, None] * inv[None, :]
    cos, sin = jnp.cos(ang), jnp.sin(ang)
    cosf = _pad_cols(jnp.concatenate([cos, cos], axis=1), LANES)
    sins = _pad_cols(jnp.concatenate([-sin, sin], axis=1), LANES)
    return cosf, sins


def kernel(x, mem, attn_norm_w, w_in, hgrn_lower_bounds, hgrn_out_norm_w, dsa_q_norm_w, dsa_kv_norm_w,
           dsa_kidx_norm_w, dsa_w_uq, dsa_w_qr, dsa_w_uk, dsa_w_uv, dsa_w_qidx, mem_norm_w, w_mem_kv,
           w_br_hgrn, w_br_dsa, w_br_mem, w_out, mlp_norm_w, w_mlp_up, w_mlp_down, final_norm_w):
    B, T, D = x.shape
    N = B * T
    M = mem.shape[1]
    l = 0
    topk = min(TOPK_MAX, T // 4)
    tm = 512
    x2 = x.reshape(N, D)

    wi = w_in[l]
    o = 0
    cols = {}
    for name, wd in (("hq", HG_WIDTH), ("hf", HG_WIDTH), ("hv", HG_WIDTH), ("hg", HG_WIDTH),
                     ("cq", DSA_Q_LORA), ("ckv", DSA_KV_LORA), ("kr", DSA_ROPE), ("kidx", IDX_DIM),
                     ("widx", IDX_HEADS), ("mq", MEM_WIDTH), ("gate", N_BRANCH * D)):
        cols[name] = wi[:, o:o + wd]
        o += wd
    w_proj = jnp.concatenate(
        [cols["hq"], cols["hf"], cols["hv"], cols["hg"], cols["mq"], cols["cq"], cols["ckv"],
         _pad_cols(cols["kr"], LANES), _pad_cols(cols["kidx"], LANES), _pad_cols(cols["widx"], LANES)],
        axis=1).astype(BF16)
    w_gate = cols["gate"].astype(BF16)

    proj_h, proj_m, proj_d = _norm_proj(x2, attn_norm_w[l], w_proj, (4 * HG_WIDTH, MEM_WIDTH, DSA_GROUP),
                                        (F32, F32, F32), tm)

    lb_all = jnp.cumsum(jax.nn.softmax(hgrn_lower_bounds.astype(F32), axis=0), axis=0)
    o_a = _hgrn(proj_h, lb_all[l], hgrn_out_norm_w[l], B, T)

    wuq_pad = _head_pad(dsa_w_uq[l], LANES).astype(BF16)
    wqr_pad = _head_pad(dsa_w_qr[l], LANES).astype(BF16)
    wqi_pad = _head_pad(dsa_w_qidx[l], LANES).astype(BF16)
    wuk_pad = jnp.pad(dsa_w_uk[l], ((0, 0), (0, LANES - DSA_NOPE), (0, 0))).astype(BF16)
    wuv = dsa_w_uv[l]
    wuv_pad = jnp.stack([jnp.pad(wuv[h], ((0, 0), ((h % 2) * DSA_VDIM, LANES - DSA_VDIM - (h % 2) * DSA_VDIM)))
                         for h in range(DSA_HEADS)]).astype(BF16)
    kinw_pad = _pad_cols(dsa_kidx_norm_w[l].reshape(1, IDX_DIM), LANES)
    tabs = _rope_tables(T, DSA_ROPE) + _rope_tables(T, IDX_DIM)
    qcat, qidx, kv, kidx, widx = _dsa_prep(
        proj_d, dsa_q_norm_w[l].reshape(1, -1), dsa_kv_norm_w[l].reshape(1, -1), kinw_pad,
        wuq_pad, wqr_pad, wuk_pad, wqi_pad, tabs, T, tm)
    o_b = _dsa(qcat, qidx, widx, kv, kidx, wuv_pad, B, T, topk)

    k_m, v_m = _norm_proj(mem.reshape(B * M, D), mem_norm_w[l], w_mem_kv[l].astype(BF16),
                          (MEM_WIDTH, MEM_WIDTH), (BF16, BF16), min(tm, B * M))
    o_c = _mem_attn(proj_m, k_m, v_m, T, tm)

    w_br = jnp.stack([w_br_hgrn[l], w_br_dsa[l], w_br_mem[l]]).astype(BF16)
    h = _merge(x2, attn_norm_w[l].reshape(1, D), o_a, o_b, o_c, w_gate, w_br, w_out[l].astype(BF16), tm)
    out = _mlp(h, mlp_norm_w[l].reshape(1, D), w_mlp_up[l].astype(BF16), w_mlp_down[l].astype(BF16),
               final_norm_w.reshape(1, D), tm)
    return out.reshape(B, T, D)
```

```python
import functools

import jax
import jax.numpy as jnp
import numpy as np
from jax import lax
from jax.experimental import pallas as pl
from jax.experimental.pallas import tpu as pltpu

F32 = jnp.float32
BF16 = jnp.bfloat16
I32 = jnp.int32
I16 = jnp.int16

EPS = 1e-6
ROPE_THETA = 10000.0
LANES = 128
HG_HEADS, HG_D, HG_CHUNK = 4, 128, 64
HG_WIDTH = HG_HEADS * HG_D
DSA_HEADS, DSA_Q_LORA, DSA_KV_LORA = 8, 256, 128
DSA_NOPE, DSA_ROPE, DSA_VDIM = 64, 32, 64
IDX_HEADS, IDX_DIM = 8, 64
TOPK_MAX = 256
MEM_HEADS, MEM_DH = 4, 128
MEM_WIDTH = MEM_HEADS * MEM_DH
N_BRANCH = 3
DSA_GROUP = 6 * LANES
VMEM_LIMIT = 56 * 1024 * 1024
INT_MIN = np.int32(-2**31)

DSA_TQ = 128
DSA_TS = 512
DSA_TK = 512


def _cparams(sem):
    return pltpu.CompilerParams(dimension_semantics=sem, vmem_limit_bytes=VMEM_LIMIT)


def _rms(x, w, n=None):
    n = x.shape[-1] if n is None else n
    ms = jnp.sum(x * x, axis=-1, keepdims=True) * (1.0 / n)
    return x * lax.rsqrt(ms + EPS) * w


def _sigmoid(x):
    return 1.0 / (1.0 + jnp.exp(-x))


def _dot(a, b):
    return jnp.dot(a, b, preferred_element_type=F32)


def _dot_nt(a, b):
    return lax.dot_general(a, b, (((1,), (1,)), ((), ())), preferred_element_type=F32)


def _dot_tn(a, b):
    return lax.dot_general(a, b, (((0,), (0,)), ((), ())), preferred_element_type=F32)


def _norm_proj_kernel(x_ref, nw_ref, w_ref, *out_refs):
    xn = _rms(x_ref[...], nw_ref[...]).astype(BF16)
    off = 0
    for o_ref in out_refs:
        wd = o_ref.shape[-1]
        o_ref[...] = _dot(xn, w_ref[:, off:off + wd]).astype(o_ref.dtype)
        off += wd


def _norm_proj(x, nw, w, widths, dtypes, tm):
    n, d = x.shape
    return pl.pallas_call(
        _norm_proj_kernel,
        grid=(n // tm,),
        in_specs=[pl.BlockSpec((tm, d), lambda i: (i, 0)),
                  pl.BlockSpec((1, d), lambda i: (0, 0)),
                  pl.BlockSpec(w.shape, lambda i: (0, 0))],
        out_specs=[pl.BlockSpec((tm, wd), lambda i: (i, 0)) for wd in widths],
        out_shape=[jax.ShapeDtypeStruct((n, wd), dt) for wd, dt in zip(widths, dtypes)],
        compiler_params=_cparams(("parallel",)),
        name="norm_proj",
    )(x, nw.reshape(1, d), w)


def _hgrn_kernel(q_ref, f_ref, v_ref, g_ref, lb_ref, onw_ref, o_ref, st_ref, b_ref, k_ref):
    c = pl.program_id(1)

    @pl.when(c == 0)
    def _():
        st_ref[...] = jnp.zeros_like(st_ref)

    C = HG_CHUNK
    lb = lb_ref[...]
    f = lb + (1.0 - lb) * _sigmoid(f_ref[...])
    lf = jnp.log(f)
    hi = lf.astype(BF16)
    r1 = lf - hi.astype(F32)
    mid = r1.astype(BF16)
    lo = (r1 - mid.astype(F32)).astype(BF16)
    ri = lax.broadcasted_iota(I32, (C, C), 0)
    ci = lax.broadcasted_iota(I32, (C, C), 1)
    tri = jnp.where(ri >= ci, 1.0, 0.0).astype(BF16)
    b_ref[...] = _dot(tri, hi) + _dot(tri, mid) + _dot(tri, lo)
    k_ref[...] = 1.0 - f

    for h in range(HG_HEADS):
        sl = slice(h * HG_D, (h + 1) * HG_D)
        qh = q_ref[:, sl]
        vh = v_ref[:, sl]
        bh = b_ref[:, sl]
        kh = k_ref[:, sl]
        a_mat = jnp.zeros((C, C), F32)
        for sb in range(C // 8):
            r0 = sb * 8
            bt = bh[r0:, :]
            qt = qh[r0:, :]
            rows = C - r0
            row_i = lax.broadcasted_iota(I32, (rows, HG_D), 0)
            lane_s = lax.broadcasted_iota(I32, (rows, C), 1)
            a_bot = a_mat[r0:, :]
            for j in range(8):
                s = r0 + j
                brow = b_ref[s:s + 1, sl]
                krow = k_ref[s:s + 1, sl]
                e = jnp.exp(jnp.where(row_i >= j, bt - brow, -jnp.inf))
                r = jnp.sum(e * (qt * krow), axis=-1, keepdims=True)
                a_bot = jnp.where(lane_s == s, r, a_bot)
            a_mat = a_bot if r0 == 0 else jnp.concatenate([a_mat[:r0, :], a_bot], axis=0)
        st = st_ref[h]
        qe = (qh * jnp.exp(bh)).astype(BF16)
        vb = vh.astype(BF16)
        o = _dot_nt(qe, st.astype(BF16)) + _dot(a_mat.astype(BF16), vb)
        blast = bh[C - 1:C, :]
        kd = (kh * jnp.exp(blast - bh)).astype(BF16)
        st_ref[h] = st * jnp.exp(blast) + _dot_tn(vb, kd)
        gh = g_ref[:, sl]
        o_ref[:, sl] = (_rms(o, onw_ref[:, sl]) * (gh * _sigmoid(gh))).astype(o_ref.dtype)


def _hgrn(proj_h, lb, onw, batch, seq):
    n = proj_h.shape[0]
    nc = seq // HG_CHUNK
    blk = lambda col: pl.BlockSpec((HG_CHUNK, HG_WIDTH), lambda b, c, col=col: (b * nc + c, col))
    vec = pl.BlockSpec((1, HG_WIDTH), lambda b, c: (0, 0))
    return pl.pallas_call(
        _hgrn_kernel,
        grid=(batch, nc),
        in_specs=[blk(0), blk(1), blk(2), blk(3), vec, vec],
        out_specs=pl.BlockSpec((HG_CHUNK, HG_WIDTH), lambda b, c: (b * nc + c, 0)),
        out_shape=jax.ShapeDtypeStruct((n, HG_WIDTH), BF16),
        scratch_shapes=[pltpu.VMEM((HG_HEADS, HG_D, HG_D), F32),
                        pltpu.VMEM((HG_CHUNK, HG_WIDTH), F32),
                        pltpu.VMEM((HG_CHUNK, HG_WIDTH), F32)],
        compiler_params=_cparams(("parallel", "arbitrary")),
        name="hgrn2",
    )(proj_h, proj_h, proj_h, proj_h, lb.reshape(1, HG_WIDTH), onw.reshape(1, HG_WIDTH))


def _rope_lanes(x, cosf, sins, half):
    nl = x.shape[-1]
    lane = lax.broadcasted_iota(I32, x.shape, 1)
    partner = jnp.where(lane % LANES < half, pltpu.roll(x, nl - half, 1), pltpu.roll(x, half, 1))
    return x * cosf + partner * sins


def _dsa_prep_kernel(pd_ref, qnw_ref, kvnw_ref, kinw_ref, wuq_ref, wqr_ref, wuk_ref, wqi_ref,
                     c32_ref, s32_ref, c64_ref, s64_ref,
                     qcat_ref, qidx_ref, kv_ref, kidx_ref, widx_ref):
    scale = float((DSA_NOPE + DSA_ROPE) ** -0.5)
    c32, s32, c64, s64 = c32_ref[...], s32_ref[...], c64_ref[...], s64_ref[...]
    cqn = _rms(pd_ref[:, 0:DSA_Q_LORA], qnw_ref[...]).astype(BF16)
    qn = _dot(cqn, wuq_ref[...]).astype(BF16)
    qr = _dot(cqn, wqr_ref[...])
    qi = _dot(cqn, wqi_ref[...])
    for h in range(DSA_HEADS):
        sl = slice(h * LANES, (h + 1) * LANES)
        qcat_ref[h, :, 0:LANES] = (_dot(qn[:, sl], wuk_ref[h]) * scale).astype(BF16)
        qcat_ref[h, :, LANES:2 * LANES] = (_rope_lanes(qr[:, sl], c32, s32, DSA_ROPE // 2) * scale).astype(BF16)
        qidx_ref[h] = (_rope_lanes(qi[:, sl], c64, s64, IDX_DIM // 2) * float(IDX_DIM ** -0.5)).astype(BF16)
    kv_ref[:, 0:LANES] = _rms(pd_ref[:, 256:384], kvnw_ref[...]).astype(BF16)
    kv_ref[:, LANES:2 * LANES] = _rope_lanes(pd_ref[:, 384:512], c32, s32, DSA_ROPE // 2).astype(BF16)
    kin = _rms(pd_ref[:, 512:640], kinw_ref[...], n=IDX_DIM)
    kidx_ref[...] = _rope_lanes(kin, c64, s64, IDX_DIM // 2).astype(BF16)
    widx_ref[...] = pd_ref[:, 640:768] * float(IDX_HEADS ** -0.5)


def _dsa_prep(proj_d, qnw, kvnw, kinw_pad, wuq_pad, wqr_pad, wuk_pad, wqi_pad, tabs, seq, tm):
    n = proj_d.shape[0]
    nt = seq // tm
    full = lambda a: pl.BlockSpec(a.shape, lambda i: (0,) * a.ndim)
    tab = pl.BlockSpec((tm, LANES), lambda i: (i % nt, 0))
    hd = DSA_HEADS
    return pl.pallas_call(
        _dsa_prep_kernel,
        grid=(n // tm,),
        in_specs=[pl.BlockSpec((tm, DSA_GROUP), lambda i: (i, 0)),
                  full(qnw), full(kvnw), full(kinw_pad), full(wuq_pad), full(wqr_pad), full(wuk_pad),
                  full(wqi_pad), tab, tab, tab, tab],
        out_specs=[pl.BlockSpec((hd, tm, 2 * LANES), lambda i: (0, i, 0)),
                   pl.BlockSpec((hd, tm, LANES), lambda i: (0, i, 0)),
                   pl.BlockSpec((tm, 2 * LANES), lambda i: (i, 0)),
                   pl.BlockSpec((tm, LANES), lambda i: (i, 0)),
                   pl.BlockSpec((tm, LANES), lambda i: (i, 0))],
        out_shape=[jax.ShapeDtypeStruct((hd, n, 2 * LANES), BF16),
                   jax.ShapeDtypeStruct((hd, n, LANES), BF16),
                   jax.ShapeDtypeStruct((n, 2 * LANES), BF16),
                   jax.ShapeDtypeStruct((n, LANES), BF16),
                   jax.ShapeDtypeStruct((n, LANES), F32)],
        compiler_params=_cparams(("parallel",)),
        name="dsa_prep",
    )(proj_d, qnw, kvnw, kinw_pad, wuq_pad, wqr_pad, wuk_pad, wqi_pad, *tabs)


def _dsa_kernel(qcat_ref, qidx_ref, widx_ref, kv_ref, kidx_ref, wuv_ref, o_ref,
                sc_ref, ik_ref, hi_ref, lo_ref, lg_ref, *, topk):
    TQ, TK, H, TS = DSA_TQ, DSA_TK, DSA_HEADS, DSA_TS
    NG = TK // LANES
    step = pl.program_id(1)
    nkt = (step * TS + TS + TK - 1) // TK
    row_q = lax.broadcasted_iota(I32, (TQ, TK), 0)
    lane_q = lax.broadcasted_iota(I32, (TQ, TK), 1)
    lane_g = lax.broadcasted_iota(I32, (TS, LANES), 1)

    def key_rows(kt):
        return pl.ds(pl.multiple_of(kt * TK, TK), TK)

    for r in range(TS // TQ):
        rows = slice(r * TQ, (r + 1) * TQ)
        qidx = qidx_ref[:, rows, :].reshape(H * TQ, LANES)
        wcols = [widx_ref[rows, h:h + 1] for h in range(IDX_HEADS)]
        tpos = step * TS + r * TQ + row_q

        def score_tile(kt, carry, rows=rows, qidx=qidx, wcols=wcols, tpos=tpos):
            s_all = jnp.maximum(_dot_nt(qidx, kidx_ref[key_rows(kt), :]), 0.0)
            score = s_all[0:TQ] * wcols[0]
            for h in range(1, IDX_HEADS):
                score = score + s_all[h * TQ:(h + 1) * TQ] * wcols[h]
            bits = lax.bitcast_convert_type(score + 0.0, I32)
            key = jnp.where(bits < 0, bits ^ np.int32(0x7FFFFFFF), bits)
            key = jnp.where(kt * TK + lane_q <= tpos, key, INT_MIN)
            ik_ref[kt, rows, :] = key
            hi_ref[kt, rows, :] = lax.shift_right_arithmetic(key, np.int32(16)).astype(I16)
            return carry

        lax.fori_loop(0, nkt, score_tile, 0)

    ones_l = jnp.ones((LANES, LANES), BF16)

    def lane_sum(acc):
        return _dot(acc.astype(BF16), ones_l)

    def count(pred_fn):
        def body(kt, acc):
            ik = ik_ref[kt]
            for g in range(NG):
                acc = acc + jnp.where(
                    pred_fn(ik[:, g * LANES:(g + 1) * LANES], kt * TK + g * LANES + lane_g), 1.0, 0.0)
            return acc
        return lane_sum(lax.fori_loop(0, nkt, body, jnp.zeros((TS, LANES), F32)))

    def count16(x_ref, cs16, strict=False):
        one, zero = jnp.ones((), I16), jnp.zeros((), I16)

        def body(kt, acc):
            x = x_ref[kt]
            for g in range(NG):
                xg = x[:, g * LANES:(g + 1) * LANES]
                acc = acc + jnp.where((xg > cs16) if strict else (xg >= cs16), one, zero)
            return acc
        acc = lax.fori_loop(0, nkt, body, jnp.zeros((TS, LANES), I16))
        return lane_sum(acc.astype(I32).astype(F32))

    def search16(x_ref, kneed):
        def bit_step(i, tu):
            cand = tu | lax.shift_left(np.int32(1), np.int32(15) - i.astype(I32))
            cnt = count16(x_ref, (cand - np.int32(32768)).astype(I16))
            return jnp.where(cnt >= kneed, cand, tu)
        return lax.fori_loop(0, 16, bit_step, jnp.zeros((TS, LANES), I32))

    kf = float(topk)
    p_hi = search16(hi_ref, kf) - np.int32(32768)
    p_hi16 = p_hi.astype(I16)
    k_lo = kf - count16(hi_ref, p_hi16, strict=True)

    def low_tile(kt, carry):
        for g in range(NG):
            gs = slice(g * LANES, (g + 1) * LANES)
            lo = ((ik_ref[kt, :, gs] & np.int32(0xFFFF)) - np.int32(32768)).astype(I16)
            lo_ref[kt, :, gs] = jnp.where(hi_ref[kt, :, gs] == p_hi16, lo, jnp.full((), -32768, I16))
        return carry

    lax.fori_loop(0, nkt, low_tile, 0)
    thr = p_hi * np.int32(65536) + search16(lo_ref, k_lo)
    cnt_gt = count(lambda ik, pos: ik > thr)
    cnt_ge = count(lambda ik, pos: ik >= thr)
    need = kf - cnt_gt

    has_tie = jnp.max(jnp.where((cnt_ge > kf) & (thr > INT_MIN), 1.0, 0.0)) > 0.0
    nbits = int(np.log2(TK)) + 3

    def tie_search(_):
        def idx_step(i, x):
            cand = x | lax.shift_left(np.int32(1), np.int32(nbits - 1) - i.astype(I32))
            g = count(lambda ik, pos: (ik == thr) & (pos < cand))
            return jnp.where(g < need, cand, x)
        return lax.fori_loop(0, nbits, idx_step, jnp.zeros((TS, LANES), I32))

    jcut = lax.cond(has_tie, tie_search, lambda _: jnp.full((TS, LANES), 2**30, I32), 0)
    jcut = jnp.where(thr > INT_MIN, jcut, -1)

    def bias_tile(kt, carry):
        ik = ik_ref[kt]
        for g in range(NG):
            ikg = ik[:, g * LANES:(g + 1) * LANES]
            sel = (ikg > thr) | ((ikg == thr) & (kt * TK + g * LANES + lane_g <= jcut))
            sc_ref[kt, :, g * LANES:(g + 1) * LANES] = jnp.where(sel, 0.0, -jnp.inf)
        return carry

    lax.fori_loop(0, nkt, bias_tile, 0)

    for r in range(TS // TQ):
        rows = slice(r * TQ, (r + 1) * TQ)
        qcat = qcat_ref[:, rows, :].reshape(H * TQ, 2 * LANES)

        def logit_tile(kt, m_run, rows=rows, qcat=qcat):
            lg = _dot_nt(qcat, kv_ref[key_rows(kt), :]).reshape(H, TQ, TK) + sc_ref[kt, rows, :][None]
            lg = lg.reshape(H * TQ, TK)
            lg_ref[kt] = lg
            part = lg[:, 0:LANES]
            for g in range(1, NG):
                part = jnp.maximum(part, lg[:, g * LANES:(g + 1) * LANES])
            return jnp.maximum(m_run, part)

        m_run = lax.fori_loop(0, nkt, logit_tile, jnp.full((H * TQ, LANES), -jnp.inf, F32))
        m_row = jnp.max(m_run, axis=-1, keepdims=True)

        def pv_tile(kt, carry, m_row=m_row):
            l_run, acc = carry
            p = jnp.exp(lg_ref[kt] - m_row)
            part = p[:, 0:LANES]
            for g in range(1, NG):
                part = part + p[:, g * LANES:(g + 1) * LANES]
            acc = acc + _dot(p.astype(BF16), kv_ref[key_rows(kt), 0:LANES])
            return l_run + part, acc

        l_run, acc = lax.fori_loop(0, nkt, pv_tile,
                                   (jnp.zeros((H * TQ, LANES), F32), jnp.zeros((H * TQ, LANES), F32)))
        o_lat = (acc / jnp.sum(l_run, axis=-1, keepdims=True)).astype(BF16)
        for pr in range(H // 2):
            h0, h1 = 2 * pr, 2 * pr + 1
            o_ref[rows, pr * LANES:(pr + 1) * LANES] = (
                _dot(o_lat[h0 * TQ:(h0 + 1) * TQ], wuv_ref[h0]) + _dot(o_lat[h1 * TQ:(h1 + 1) * TQ], wuv_ref[h1])
            ).astype(o_ref.dtype)


def _dsa(qcat, qidx, widx, kv, kidx, wuv_pad, batch, seq, topk):
    n = kv.shape[0]
    ns = seq // DSA_TS
    nkt_max = seq // DSA_TK
    H = DSA_HEADS
    return pl.pallas_call(
        functools.partial(_dsa_kernel, topk=topk),
        grid=(batch, ns),
        in_specs=[pl.BlockSpec((H, DSA_TS, 2 * LANES), lambda b, s: (0, b * ns + s, 0)),
                  pl.BlockSpec((H, DSA_TS, LANES), lambda b, s: (0, b * ns + s, 0)),
                  pl.BlockSpec((DSA_TS, LANES), lambda b, s: (b * ns + s, 0)),
                  pl.BlockSpec((seq, 2 * LANES), lambda b, s: (b, 0)),
                  pl.BlockSpec((seq, LANES), lambda b, s: (b, 0)),
                  pl.BlockSpec(wuv_pad.shape, lambda b, s: (0, 0, 0))],
        out_specs=pl.BlockSpec((DSA_TS, H * DSA_VDIM), lambda b, s: (b * ns + s, 0)),
        out_shape=jax.ShapeDtypeStruct((n, H * DSA_VDIM), BF16),
        scratch_shapes=[pltpu.VMEM((nkt_max, DSA_TS, DSA_TK), F32),
                        pltpu.VMEM((nkt_max, DSA_TS, DSA_TK), I32),
                        pltpu.VMEM((nkt_max, DSA_TS, DSA_TK), I16),
                        pltpu.VMEM((nkt_max, DSA_TS, DSA_TK), I16),
                        pltpu.VMEM((nkt_max, H * DSA_TQ, DSA_TK), F32)],
        compiler_params=_cparams(("parallel", "arbitrary")),
        name="dsa_attn",
    )(qcat, qidx, widx, kv, kidx, wuv_pad)


def _mem_attn_kernel(q_ref, k_ref, v_ref, o_ref):
    scale = float(MEM_DH ** -0.5)
    for h in range(MEM_HEADS):
        sl = slice(h * MEM_DH, (h + 1) * MEM_DH)
        lg = _dot_nt((q_ref[:, sl] * scale).astype(BF16), k_ref[:, sl])
        p = jnp.exp(lg - jnp.max(lg, axis=-1, keepdims=True))
        o = _dot(p.astype(BF16), v_ref[:, sl]) / jnp.sum(p, axis=-1, keepdims=True)
        o_ref[:, sl] = o.astype(o_ref.dtype)


def _mem_attn(proj_m, k_m, v_m, seq, tm):
    n = proj_m.shape[0]
    nt = seq // tm
    m = k_m.shape[0] // (n // seq)
    return pl.pallas_call(
        _mem_attn_kernel,
        grid=(n // tm,),
        in_specs=[pl.BlockSpec((tm, MEM_WIDTH), lambda i: (i, 0)),
                  pl.BlockSpec((m, MEM_WIDTH), lambda i: (i // nt, 0)),
                  pl.BlockSpec((m, MEM_WIDTH), lambda i: (i // nt, 0))],
        out_specs=pl.BlockSpec((tm, MEM_WIDTH), lambda i: (i, 0)),
        out_shape=jax.ShapeDtypeStruct((n, MEM_WIDTH), BF16),
        compiler_params=_cparams(("parallel",)),
        name="mem_attn",
    )(proj_m, k_m, v_m)


def _merge_kernel(x_ref, nw_ref, oa_ref, ob_ref, oc_ref, wg_ref, wbr_ref, wo_ref, h_ref):
    x = x_ref[...]
    d = x.shape[-1]
    xn = _rms(x, nw_ref[...]).astype(BF16)
    merged = None
    for br, o_br in enumerate((oa_ref, ob_ref, oc_ref)):
        gate = _sigmoid(_dot(xn, wg_ref[:, br * d:(br + 1) * d]))
        term = gate * _dot(o_br[...], wbr_ref[br])
        merged = term if merged is None else merged + term
    h_ref[...] = x + _dot(merged.astype(BF16), wo_ref[...])


def _merge(x, nw, o_a, o_b, o_c, w_gate, w_br, w_out, tm):
    n, d = x.shape
    row = lambda wd: pl.BlockSpec((tm, wd), lambda i: (i, 0))
    full = lambda a: pl.BlockSpec(a.shape, lambda i: (0,) * a.ndim)
    return pl.pallas_call(
        _merge_kernel,
        grid=(n // tm,),
        in_specs=[row(d), full(nw), row(o_a.shape[1]), row(o_b.shape[1]), row(o_c.shape[1]),
                  full(w_gate), full(w_br), full(w_out)],
        out_specs=row(d),
        out_shape=jax.ShapeDtypeStruct((n, d), F32),
        compiler_params=_cparams(("parallel",)),
        name="merge_out",
    )(x, nw, o_a, o_b, o_c, w_gate, w_br, w_out)


def _mlp_kernel(h_ref, nw_ref, wu_ref, wd_ref, fw_ref, o_ref, *, ff_chunk):
    h = h_ref[...]
    m = _rms(h, nw_ref[...]).astype(BF16)
    acc = h
    for c0 in range(0, wu_ref.shape[1], ff_chunk):
        u = jnp.maximum(_dot(m, wu_ref[:, c0:c0 + ff_chunk]), 0.0)
        acc = acc + _dot((u * u).astype(BF16), wd_ref[c0:c0 + ff_chunk, :])
    o_ref[...] = _rms(acc, fw_ref[...])


def _mlp(h, nw, w_up, w_down, fw, tm, ff_chunk=1024):
    n, d = h.shape
    full = lambda a: pl.BlockSpec(a.shape, lambda i: (0,) * a.ndim)
    return pl.pallas_call(
        functools.partial(_mlp_kernel, ff_chunk=ff_chunk),
        grid=(n // tm,),
        in_specs=[pl.BlockSpec((tm, d), lambda i: (i, 0)), full(nw), full(w_up), full(w_down), full(fw)],
        out_specs=pl.BlockSpec((tm, d), lambda i: (i, 0)),
        out_shape=jax.ShapeDtypeStruct((n, d), F32),
        compiler_params=_cparams(("parallel",)),
        name="mlp_final",
    )(h, nw, w_up, w_down, fw)


def _pad_cols(w, width):
    return jnp.pad(w, ((0, 0), (0, width - w.shape[1])))


def _head_pad(w, width):
    return jnp.pad(w, ((0, 0), (0, 0), (0, width - w.shape[2]))).reshape(w.shape[0], -1)


def _rope_tables(seq, d):
    inv = ROPE_THETA ** (-jnp.arange(0, d, 2, dtype=F32) / d)
    ang = jnp.arange(seq, dtype=F32)[:, None] * inv[None, :]
    cos, sin = jnp.cos(ang), jnp.sin(ang)
    cosf = _pad_cols(jnp.concatenate([cos, cos], axis=1), LANES)
    sins = _pad_cols(jnp.concatenate([-sin, sin], axis=1), LANES)
    return cosf, sins


def kernel(x, mem, attn_norm_w, w_in, hgrn_lower_bounds, hgrn_out_norm_w, dsa_q_norm_w, dsa_kv_norm_w,
           dsa_kidx_norm_w, dsa_w_uq, dsa_w_qr, dsa_w_uk, dsa_w_uv, dsa_w_qidx, mem_norm_w, w_mem_kv,
           w_br_hgrn, w_br_dsa, w_br_mem, w_out, mlp_norm_w, w_mlp_up, w_mlp_down, final_norm_w):
    B, T, D = x.shape
    N = B * T
    M = mem.shape[1]
    l = 0
    topk = min(TOPK_MAX, T // 4)
    tm = 512
    x2 = x.reshape(N, D)

    wi = w_in[l]
    o = 0
    cols = {}
    for name, wd in (("hq", HG_WIDTH), ("hf", HG_WIDTH), ("hv", HG_WIDTH), ("hg", HG_WIDTH),
                     ("cq", DSA_Q_LORA), ("ckv", DSA_KV_LORA), ("kr", DSA_ROPE), ("kidx", IDX_DIM),
                     ("widx", IDX_HEADS), ("mq", MEM_WIDTH), ("gate", N_BRANCH * D)):
        cols[name] = wi[:, o:o + wd]
        o += wd
    w_proj = jnp.concatenate(
        [cols["hq"], cols["hf"], cols["hv"], cols["hg"], cols["mq"], cols["cq"], cols["ckv"],
         _pad_cols(cols["kr"], LANES), _pad_cols(cols["kidx"], LANES), _pad_cols(cols["widx"], LANES)],
        axis=1).astype(BF16)
    w_gate = cols["gate"].astype(BF16)

    proj_h, proj_m, proj_d = _norm_proj(x2, attn_norm_w[l], w_proj, (4 * HG_WIDTH, MEM_WIDTH, DSA_GROUP),
                                        (F32, F32, F32), tm)

    lb_all = jnp.cumsum(jax.nn.softmax(hgrn_lower_bounds.astype(F32), axis=0), axis=0)
    o_a = _hgrn(proj_h, lb_all[l], hgrn_out_norm_w[l], B, T)

    wuq_pad = _head_pad(dsa_w_uq[l], LANES).astype(BF16)
    wqr_pad = _head_pad(dsa_w_qr[l], LANES).astype(BF16)
    wqi_pad = _head_pad(dsa_w_qidx[l], LANES).astype(BF16)
    wuk_pad = jnp.pad(dsa_w_uk[l], ((0, 0), (0, LANES - DSA_NOPE), (0, 0))).astype(BF16)
    wuv = dsa_w_uv[l]
    wuv_pad = jnp.stack([jnp.pad(wuv[h], ((0, 0), ((h % 2) * DSA_VDIM, LANES - DSA_VDIM - (h % 2) * DSA_VDIM)))
                         for h in range(DSA_HEADS)]).astype(BF16)
    kinw_pad = _pad_cols(dsa_kidx_norm_w[l].reshape(1, IDX_DIM), LANES)
    tabs = _rope_tables(T, DSA_ROPE) + _rope_tables(T, IDX_DIM)
    qcat, qidx, kv, kidx, widx = _dsa_prep(
        proj_d, dsa_q_norm_w[l].reshape(1, -1), dsa_kv_norm_w[l].reshape(1, -1), kinw_pad,
        wuq_pad, wqr_pad, wuk_pad, wqi_pad, tabs, T, tm)
    o_b = _dsa(qcat, qidx, widx, kv, kidx, wuv_pad, B, T, topk)

    k_m, v_m = _norm_proj(mem.reshape(B * M, D), mem_norm_w[l], w_mem_kv[l].astype(BF16),
                          (MEM_WIDTH, MEM_WIDTH), (BF16, BF16), min(tm, B * M))
    o_c = _mem_attn(proj_m, k_m, v_m, T, tm)

    w_br = jnp.stack([w_br_hgrn[l], w_br_dsa[l], w_br_mem[l]]).astype(BF16)
    h = _merge(x2, attn_norm_w[l].reshape(1, D), o_a, o_b, o_c, w_gate, w_br, w_out[l].astype(BF16), tm)
    out = _mlp(h, mlp_norm_w[l].reshape(1, D), w_mlp_up[l].astype(BF16), w_mlp_down[l].astype(BF16),
               final_norm_w.reshape(1, D), tm)
    return out.reshape(B, T, D)
```

```python
import functools

import jax
import jax.numpy as jnp
import numpy as np
from jax import lax
from jax.experimental import pallas as pl
from jax.experimental.pallas import tpu as pltpu

F32 = jnp.float32
BF16 = jnp.bfloat16
I32 = jnp.int32
I16 = jnp.int16

EPS = 1e-6
ROPE_THETA = 10000.0
LANES = 128
SUBLANES = 8
HG_HEADS, HG_D, HG_CHUNK = 4, 128, 64
HG_WIDTH = HG_HEADS * HG_D
DSA_HEADS, DSA_Q_LORA, DSA_KV_LORA = 8, 256, 128
DSA_NOPE, DSA_ROPE, DSA_VDIM = 64, 32, 64
IDX_HEADS, IDX_DIM = 8, 64
TOPK_MAX = 256
MEM_HEADS, MEM_DH = 4, 128
MEM_WIDTH = MEM_HEADS * MEM_DH
N_BRANCH = 3
DSA_GROUP = 6 * LANES
VMEM_LIMIT = 56 * 1024 * 1024
INT_MIN = np.int32(-2**31)

DSA_TQ = 128
DSA_TS = 512
DSA_TK = 512


def _cparams(sem):
    return pltpu.CompilerParams(dimension_semantics=sem, vmem_limit_bytes=VMEM_LIMIT)


def _rms(x, w, n=None):
    n = x.shape[-1] if n is None else n
    ms = jnp.sum(x * x, axis=-1, keepdims=True) * (1.0 / n)
    return x * lax.rsqrt(ms + EPS) * w


def _sigmoid(x):
    return 1.0 / (1.0 + jnp.exp(-x))


def _dot(a, b):
    return jnp.dot(a, b, preferred_element_type=F32)


def _dot_nt(a, b):
    return lax.dot_general(a, b, (((1,), (1,)), ((), ())), preferred_element_type=F32)


def _dot_tn(a, b):
    return lax.dot_general(a, b, (((0,), (0,)), ((), ())), preferred_element_type=F32)


def _norm_proj_kernel(x_ref, nw_ref, w_ref, *out_refs):
    xn = _rms(x_ref[...], nw_ref[...]).astype(BF16)
    off = 0
    for o_ref in out_refs:
        wd = o_ref.shape[-1]
        o_ref[...] = _dot(xn, w_ref[:, off:off + wd]).astype(o_ref.dtype)
        off += wd


def _norm_proj(x, nw, w, widths, dtypes, tm):
    n, d = x.shape
    return pl.pallas_call(
        _norm_proj_kernel,
        grid=(n // tm,),
        in_specs=[pl.BlockSpec((tm, d), lambda i: (i, 0)),
                  pl.BlockSpec((1, d), lambda i: (0, 0)),
                  pl.BlockSpec(w.shape, lambda i: (0, 0))],
        out_specs=[pl.BlockSpec((tm, wd), lambda i: (i, 0)) for wd in widths],
        out_shape=[jax.ShapeDtypeStruct((n, wd), dt) for wd, dt in zip(widths, dtypes)],
        compiler_params=_cparams(("parallel",)),
        name="norm_proj",
    )(x, nw.reshape(1, d), w)


def _hgrn_kernel(q_ref, f_ref, v_ref, g_ref, lb_ref, onw_ref, o_ref, st_ref, b_ref, k_ref):
    c = pl.program_id(1)

    @pl.when(c == 0)
    def _():
        st_ref[...] = jnp.zeros_like(st_ref)

    C = HG_CHUNK
    lb = lb_ref[...]
    f = lb + (1.0 - lb) * _sigmoid(f_ref[...])
    lf = jnp.log(f)
    hi = lf.astype(BF16)
    r1 = lf - hi.astype(F32)
    mid = r1.astype(BF16)
    lo = (r1 - mid.astype(F32)).astype(BF16)
    ri = lax.broadcasted_iota(I32, (C, C), 0)
    ci = lax.broadcasted_iota(I32, (C, C), 1)
    tri = jnp.where(ri >= ci, 1.0, 0.0).astype(BF16)
    b_ref[...] = _dot(tri, hi) + _dot(tri, mid) + _dot(tri, lo)
    k_ref[...] = 1.0 - f

    for h in range(HG_HEADS):
        sl = slice(h * HG_D, (h + 1) * HG_D)
        qh = q_ref[:, sl]
        vh = v_ref[:, sl]
        bh = b_ref[:, sl]
        kh = k_ref[:, sl]
        a_mat = jnp.zeros((C, C), F32)
        for sb in range(C // 8):
            r0 = sb * 8
            bt = bh[r0:, :]
            qt = qh[r0:, :]
            rows = C - r0
            row_i = lax.broadcasted_iota(I32, (rows, HG_D), 0)
            lane_s = lax.broadcasted_iota(I32, (rows, C), 1)
            a_bot = a_mat[r0:, :]
            for j in range(8):
                s = r0 + j
                brow = b_ref[s:s + 1, sl]
                krow = k_ref[s:s + 1, sl]
                e = jnp.exp(jnp.where(row_i >= j, bt - brow, -jnp.inf))
                r = jnp.sum(e * (qt * krow), axis=-1, keepdims=True)
                a_bot = jnp.where(lane_s == s, r, a_bot)
            a_mat = a_bot if r0 == 0 else jnp.concatenate([a_mat[:r0, :], a_bot], axis=0)
        st = st_ref[h]
        qe = (qh * jnp.exp(bh)).astype(BF16)
        vb = vh.astype(BF16)
        o = _dot_nt(qe, st.astype(BF16)) + _dot(a_mat.astype(BF16), vb)
        blast = bh[C - 1:C, :]
        kd = (kh * jnp.exp(blast - bh)).astype(BF16)
        st_ref[h] = st * jnp.exp(blast) + _dot_tn(vb, kd)
        gh = g_ref[:, sl]
        o_ref[:, sl] = (_rms(o, onw_ref[:, sl]) * (gh * _sigmoid(gh))).astype(o_ref.dtype)


def _hgrn(proj_h, lb, onw, batch, seq):
    n = proj_h.shape[0]
    nc = seq // HG_CHUNK
    blk = lambda col: pl.BlockSpec((HG_CHUNK, HG_WIDTH), lambda b, c, col=col: (b * nc + c, col))
    vec = pl.BlockSpec((1, HG_WIDTH), lambda b, c: (0, 0))
    return pl.pallas_call(
        _hgrn_kernel,
        grid=(batch, nc),
        in_specs=[blk(0), blk(1), blk(2), blk(3), vec, vec],
        out_specs=pl.BlockSpec((HG_CHUNK, HG_WIDTH), lambda b, c: (b * nc + c, 0)),
        out_shape=jax.ShapeDtypeStruct((n, HG_WIDTH), BF16),
        scratch_shapes=[pltpu.VMEM((HG_HEADS, HG_D, HG_D), F32),
                        pltpu.VMEM((HG_CHUNK, HG_WIDTH), F32),
                        pltpu.VMEM((HG_CHUNK, HG_WIDTH), F32)],
        compiler_params=_cparams(("parallel", "arbitrary")),
        name="hgrn2",
    )(proj_h, proj_h, proj_h, proj_h, lb.reshape(1, HG_WIDTH), onw.reshape(1, HG_WIDTH))


def _rope_lanes(x, cosf, sins, half):
    nl = x.shape[-1]
    lane = lax.broadcasted_iota(I32, x.shape, 1)
    partner = jnp.where(lane % LANES < half, pltpu.roll(x, nl - half, 1), pltpu.roll(x, half, 1))
    return x * cosf + partner * sins


def _dsa_prep_kernel(pd_ref, qnw_ref, kvnw_ref, kinw_ref, wuq_ref, wqr_ref, wuk_ref, wqi_ref,
                     c32_ref, s32_ref, c64_ref, s64_ref,
                     qcat_ref, qidx_ref, kv_ref, kidx_ref, widx_ref):
    scale = float((DSA_NOPE + DSA_ROPE) ** -0.5)
    c32, s32, c64, s64 = c32_ref[...], s32_ref[...], c64_ref[...], s64_ref[...]
    cqn = _rms(pd_ref[:, 0:DSA_Q_LORA], qnw_ref[...]).astype(BF16)
    qn = _dot(cqn, wuq_ref[...]).astype(BF16)
    qr = _dot(cqn, wqr_ref[...])
    qi = _dot(cqn, wqi_ref[...])
    for h in range(DSA_HEADS):
        sl = slice(h * LANES, (h + 1) * LANES)
        qcat_ref[h, :, 0:LANES] = (_dot(qn[:, sl], wuk_ref[h]) * scale).astype(BF16)
        qcat_ref[h, :, LANES:2 * LANES] = (_rope_lanes(qr[:, sl], c32, s32, DSA_ROPE // 2) * scale).astype(BF16)
        qidx_ref[h] = (_rope_lanes(qi[:, sl], c64, s64, IDX_DIM // 2) * float(IDX_DIM ** -0.5)).astype(BF16)
    kv_ref[:, 0:LANES] = _rms(pd_ref[:, 256:384], kvnw_ref[...]).astype(BF16)
    kv_ref[:, LANES:2 * LANES] = _rope_lanes(pd_ref[:, 384:512], c32, s32, DSA_ROPE // 2).astype(BF16)
    kin = _rms(pd_ref[:, 512:640], kinw_ref[...], n=IDX_DIM)
    kidx_ref[...] = _rope_lanes(kin, c64, s64, IDX_DIM // 2).astype(BF16)
    widx_ref[...] = jnp.transpose(pd_ref[:, 640:768] * float(IDX_HEADS ** -0.5))[0:IDX_HEADS, :]


def _dsa_prep(proj_d, qnw, kvnw, kinw_pad, wuq_pad, wqr_pad, wuk_pad, wqi_pad, tabs, seq, tm):
    n = proj_d.shape[0]
    nt = seq // tm
    full = lambda a: pl.BlockSpec(a.shape, lambda i: (0,) * a.ndim)
    tab = pl.BlockSpec((tm, LANES), lambda i: (i % nt, 0))
    hd = DSA_HEADS
    return pl.pallas_call(
        _dsa_prep_kernel,
        grid=(n // tm,),
        in_specs=[pl.BlockSpec((tm, DSA_GROUP), lambda i: (i, 0)),
                  full(qnw), full(kvnw), full(kinw_pad), full(wuq_pad), full(wqr_pad), full(wuk_pad),
                  full(wqi_pad), tab, tab, tab, tab],
        out_specs=[pl.BlockSpec((hd, tm, 2 * LANES), lambda i: (0, i, 0)),
                   pl.BlockSpec((hd, tm, LANES), lambda i: (0, i, 0)),
                   pl.BlockSpec((tm, 2 * LANES), lambda i: (i, 0)),
                   pl.BlockSpec((tm, LANES), lambda i: (i, 0)),
                   pl.BlockSpec((IDX_HEADS, tm), lambda i: (0, i))],
        out_shape=[jax.ShapeDtypeStruct((hd, n, 2 * LANES), BF16),
                   jax.ShapeDtypeStruct((hd, n, LANES), BF16),
                   jax.ShapeDtypeStruct((n, 2 * LANES), BF16),
                   jax.ShapeDtypeStruct((n, LANES), BF16),
                   jax.ShapeDtypeStruct((IDX_HEADS, n), F32)],
        compiler_params=_cparams(("parallel",)),
        name="dsa_prep",
    )(proj_d, qnw, kvnw, kinw_pad, wuq_pad, wqr_pad, wuk_pad, wqi_pad, *tabs)


def _dsa_kernel(qcat_ref, qidx_ref, widx_ref, kv_ref, kidx_ref, wuv_ref, o_ref,
                sc_ref, ik_ref, lg_ref, *, topk):
    TQ, TK, H, TS = DSA_TQ, DSA_TK, DSA_HEADS, DSA_TS
    NG = TK // LANES
    NR = TS // TQ
    step = pl.program_id(1)
    nkt = (step * TS + TS + TK - 1) // TK
    key_q = lax.broadcasted_iota(I32, (TK, TQ), 0)
    qry_q = lax.broadcasted_iota(I32, (TK, TQ), 1)
    key_s = lax.broadcasted_iota(I32, (TK, TS), 0)

    def key_rows(kt):
        return pl.ds(pl.multiple_of(kt * TK, TK), TK)

    def as_i32(i):
        return jnp.asarray(i, dtype=I32)

    for r in range(NR):
        cols = slice(r * TQ, (r + 1) * TQ)
        qidx = qidx_ref[:, cols, :].reshape(H * TQ, LANES)
        wrows = [widx_ref[h:h + 1, cols] for h in range(IDX_HEADS)]
        tpos = step * TS + r * TQ + qry_q

        def score_tile(kt, carry, cols=cols, qidx=qidx, wrows=wrows, tpos=tpos):
            s_all = jnp.maximum(_dot_nt(kidx_ref[key_rows(kt), :], qidx), 0.0)
            score = s_all[:, 0:TQ] * wrows[0]
            for h in range(1, IDX_HEADS):
                score = score + s_all[:, h * TQ:(h + 1) * TQ] * wrows[h]
            bits = lax.bitcast_convert_type(score + 0.0, I32)
            key = jnp.where(bits < 0, bits ^ np.int32(0x7FFFFFFF), bits)
            ik_ref[kt, :, cols] = jnp.where(kt * TK + key_q <= tpos, key, INT_MIN)
            return carry

        lax.fori_loop(0, nkt, score_tile, 0)

    one, zero, ninf = jnp.float32(1.0), jnp.float32(0.0), jnp.float32(-jnp.inf)

    def count(pred_fn):
        def body(kt, acc):
            m = jnp.where(pred_fn(ik_ref[kt], kt * TK + key_s), one, zero)
            return acc + jnp.sum(m.reshape(TK // SUBLANES, SUBLANES, TS), axis=0)
        acc = lax.fori_loop(0, nkt, body, jnp.zeros((SUBLANES, TS), F32))
        return jnp.sum(acc, axis=0, keepdims=True)

    kf = float(topk)

    def bit_step(i, tu):
        cand = tu | lax.shift_left(np.int32(1), np.int32(31) - as_i32(i))
        cs = cand ^ INT_MIN
        cnt = count(lambda ik, pos: ik >= cs)
        return jnp.where(cnt >= kf, cand, tu)

    tu = lax.fori_loop(0, 32, bit_step, jnp.zeros((1, TS), I32))
    thr = tu ^ INT_MIN
    cnt_gt = count(lambda ik, pos: ik > thr)
    cnt_ge = count(lambda ik, pos: ik >= thr)
    need = kf - cnt_gt

    has_tie = jnp.max(jnp.where((cnt_ge > kf) & (thr > INT_MIN), one, zero)) > zero
    nbits = int(np.log2(TK)) + 3

    def tie_search(_):
        def idx_step(i, x):
            cand = x | lax.shift_left(np.int32(1), np.int32(nbits - 1) - as_i32(i))
            g = count(lambda ik, pos: (ik == thr) & (pos < cand))
            return jnp.where(g < need, cand, x)
        return lax.fori_loop(0, nbits, idx_step, jnp.zeros((1, TS), I32))

    jcut = lax.cond(has_tie, tie_search, lambda _: jnp.full((1, TS), 2**30, I32), 0)
    jcut = jnp.where(thr > INT_MIN, jcut, -1)

    def bias_tile(kt, carry):
        ik = ik_ref[kt]
        sel = (ik > thr) | ((ik == thr) & (kt * TK + key_s <= jcut))
        bias_t = jnp.where(sel, zero, ninf)
        for r in range(NR):
            sc_ref[kt, r * TQ:(r + 1) * TQ, :] = jnp.transpose(bias_t[:, r * TQ:(r + 1) * TQ])
        return carry

    lax.fori_loop(0, nkt, bias_tile, 0)

    for r in range(NR):
        rows = slice(r * TQ, (r + 1) * TQ)
        qcat = qcat_ref[:, rows, :].reshape(H * TQ, 2 * LANES)

        def logit_tile(kt, m_run, rows=rows, qcat=qcat):
            lg = _dot_nt(qcat, kv_ref[key_rows(kt), :]).reshape(H, TQ, TK) + sc_ref[kt, rows, :][None]
            lg = lg.reshape(H * TQ, TK)
            lg_ref[kt] = lg
            part = lg[:, 0:LANES]
            for g in range(1, NG):
                part = jnp.maximum(part, lg[:, g * LANES:(g + 1) * LANES])
            return jnp.maximum(m_run, part)

        m_run = lax.fori_loop(0, nkt, logit_tile, jnp.full((H * TQ, LANES), -jnp.inf, F32))
        m_row = jnp.max(m_run, axis=-1, keepdims=True)

        def pv_tile(kt, carry, m_row=m_row):
            l_run, acc = carry
            p = jnp.exp(lg_ref[kt] - m_row)
            part = p[:, 0:LANES]
            for g in range(1, NG):
                part = part + p[:, g * LANES:(g + 1) * LANES]
            acc = acc + _dot(p.astype(BF16), kv_ref[key_rows(kt), 0:LANES])
            return l_run + part, acc

        l_run, acc = lax.fori_loop(0, nkt, pv_tile,
                                   (jnp.zeros((H * TQ, LANES), F32), jnp.zeros((H * TQ, LANES), F32)))
        o_lat = (acc / jnp.sum(l_run, axis=-1, keepdims=True)).astype(BF16)
        for pr in range(H // 2):
            h0, h1 = 2 * pr, 2 * pr + 1
            o_ref[rows, pr * LANES:(pr + 1) * LANES] = (
                _dot(o_lat[h0 * TQ:(h0 + 1) * TQ], wuv_ref[h0]) + _dot(o_lat[h1 * TQ:(h1 + 1) * TQ], wuv_ref[h1])
            ).astype(o_ref.dtype)


def _dsa(qcat, qidx, widx, kv, kidx, wuv_pad, batch, seq, topk):
    n = kv.shape[0]
    ns = seq // DSA_TS
    nkt_max = seq // DSA_TK
    H = DSA_HEADS
    return pl.pallas_call(
        functools.partial(_dsa_kernel, topk=topk),
        grid=(batch, ns),
        in_specs=[pl.BlockSpec((H, DSA_TS, 2 * LANES), lambda b, s: (0, b * ns + s, 0)),
                  pl.BlockSpec((H, DSA_TS, LANES), lambda b, s: (0, b * ns + s, 0)),
                  pl.BlockSpec((IDX_HEADS, DSA_TS), lambda b, s: (0, b * ns + s)),
                  pl.BlockSpec((seq, 2 * LANES), lambda b, s: (b, 0)),
                  pl.BlockSpec((seq, LANES), lambda b, s: (b, 0)),
                  pl.BlockSpec(wuv_pad.shape, lambda b, s: (0, 0, 0))],
        out_specs=pl.BlockSpec((DSA_TS, H * DSA_VDIM), lambda b, s: (b * ns + s, 0)),
        out_shape=jax.ShapeDtypeStruct((n, H * DSA_VDIM), BF16),
        scratch_shapes=[pltpu.VMEM((nkt_max, DSA_TS, DSA_TK), F32),
                        pltpu.VMEM((nkt_max, DSA_TK, DSA_TS), I32),
                        pltpu.VMEM((nkt_max, H * DSA_TQ, DSA_TK), F32)],
        compiler_params=_cparams(("parallel", "arbitrary")),
        name="dsa_attn",
    )(qcat, qidx, widx, kv, kidx, wuv_pad)


def _mem_attn_kernel(q_ref, k_ref, v_ref, o_ref):
    scale = float(MEM_DH ** -0.5)
    for h in range(MEM_HEADS):
        sl = slice(h * MEM_DH, (h + 1) * MEM_DH)
        lg = _dot_nt((q_ref[:, sl] * scale).astype(BF16), k_ref[:, sl])
        p = jnp.exp(lg - jnp.max(lg, axis=-1, keepdims=True))
        o = _dot(p.astype(BF16), v_ref[:, sl]) / jnp.sum(p, axis=-1, keepdims=True)
        o_ref[:, sl] = o.astype(o_ref.dtype)


def _mem_attn(proj_m, k_m, v_m, seq, tm):
    n = proj_m.shape[0]
    nt = seq // tm
    m = k_m.shape[0] // (n // seq)
    return pl.pallas_call(
        _mem_attn_kernel,
        grid=(n // tm,),
        in_specs=[pl.BlockSpec((tm, MEM_WIDTH), lambda i: (i, 0)),
                  pl.BlockSpec((m, MEM_WIDTH), lambda i: (i // nt, 0)),
                  pl.BlockSpec((m, MEM_WIDTH), lambda i: (i // nt, 0))],
        out_specs=pl.BlockSpec((tm, MEM_WIDTH), lambda i: (i, 0)),
        out_shape=jax.ShapeDtypeStruct((n, MEM_WIDTH), BF16),
        compiler_params=_cparams(("parallel",)),
        name="mem_attn",
    )(proj_m, k_m, v_m)


def _merge_kernel(x_ref, nw_ref, oa_ref, ob_ref, oc_ref, wg_ref, wbr_ref, wo_ref, h_ref):
    x = x_ref[...]
    d = x.shape[-1]
    xn = _rms(x, nw_ref[...]).astype(BF16)
    merged = None
    for br, o_br in enumerate((oa_ref, ob_ref, oc_ref)):
        gate = _sigmoid(_dot(xn, wg_ref[:, br * d:(br + 1) * d]))
        term = gate * _dot(o_br[...], wbr_ref[br])
        merged = term if merged is None else merged + term
    h_ref[...] = x + _dot(merged.astype(BF16), wo_ref[...])


def _merge(x, nw, o_a, o_b, o_c, w_gate, w_br, w_out, tm):
    n, d = x.shape
    row = lambda wd: pl.BlockSpec((tm, wd), lambda i: (i, 0))
    full = lambda a: pl.BlockSpec(a.shape, lambda i: (0,) * a.ndim)
    return pl.pallas_call(
        _merge_kernel,
        grid=(n // tm,),
        in_specs=[row(d), full(nw), row(o_a.shape[1]), row(o_b.shape[1]), row(o_c.shape[1]),
                  full(w_gate), full(w_br), full(w_out)],
        out_specs=row(d),
        out_shape=jax.ShapeDtypeStruct((n, d), F32),
        compiler_params=_cparams(("parallel",)),
        name="merge_out",
    )(x, nw, o_a, o_b, o_c, w_gate, w_br, w_out)


def _mlp_kernel(h_ref, nw_ref, wu_ref, wd_ref, fw_ref, o_ref, *, ff_chunk):
    h = h_ref[...]
    m = _rms(h, nw_ref[...]).astype(BF16)
    acc = h
    for c0 in range(0, wu_ref.shape[1], ff_chunk):
        u = jnp.maximum(_dot(m, wu_ref[:, c0:c0 + ff_chunk]), 0.0)
        acc = acc + _dot((u * u).astype(BF16), wd_ref[c0:c0 + ff_chunk, :])
    o_ref[...] = _rms(acc, fw_ref[...])


def _mlp(h, nw, w_up, w_down, fw, tm, ff_chunk=1024):
    n, d = h.shape
    full = lambda a: pl.BlockSpec(a.shape, lambda i: (0,) * a.ndim)
    return pl.pallas_call(
        functools.partial(_mlp_kernel, ff_chunk=ff_chunk),
        grid=(n // tm,),
        in_specs=[pl.BlockSpec((tm, d), lambda i: (i, 0)), full(nw), full(w_up), full(w_down), full(fw)],
        out_specs=pl.BlockSpec((tm, d), lambda i: (i, 0)),
        out_shape=jax.ShapeDtypeStruct((n, d), F32),
        compiler_params=_cparams(("parallel",)),
        name="mlp_final",
    )(h, nw, w_up, w_down, fw)


def _pad_cols(w, width):
    return jnp.pad(w, ((0, 0), (0, width - w.shape[1])))


def _head_pad(w, width):
    return jnp.pad(w, ((0, 0), (0, 0), (0, width - w.shape[2]))).reshape(w.shape[0], -1)


def _rope_tables(seq, d):
    inv = ROPE_THETA ** (-jnp.arange(0, d, 2, dtype=F32) / d)
    ang = jnp.arange(seq, dtype=F32)[:, None] * inv[None, :]
    cos, sin = jnp.cos(ang), jnp.sin(ang)
    cosf = _pad_cols(jnp.concatenate([cos, cos], axis=1), LANES)
    sins = _pad_cols(jnp.concatenate([-sin, sin], axis=1), LANES)
    return cosf, sins


def kernel(x, mem, attn_norm_w, w_in, hgrn_lower_bounds, hgrn_out_norm_w, dsa_q_norm_w, dsa_kv_norm_w,
           dsa_kidx_norm_w, dsa_w_uq, dsa_w_qr, dsa_w_uk, dsa_w_uv, dsa_w_qidx, mem_norm_w, w_mem_kv,
           w_br_hgrn, w_br_dsa, w_br_mem, w_out, mlp_norm_w, w_mlp_up, w_mlp_down, final_norm_w):
    B, T, D = x.shape
    N = B * T
    M = mem.shape[1]
    l = 0
    topk = min(TOPK_MAX, T // 4)
    tm = 512
    x2 = x.reshape(N, D)

    wi = w_in[l]
    o = 0
    cols = {}
    for name, wd in (("hq", HG_WIDTH), ("hf", HG_WIDTH), ("hv", HG_WIDTH), ("hg", HG_WIDTH),
                     ("cq", DSA_Q_LORA), ("ckv", DSA_KV_LORA), ("kr", DSA_ROPE), ("kidx", IDX_DIM),
                     ("widx", IDX_HEADS), ("mq", MEM_WIDTH), ("gate", N_BRANCH * D)):
        cols[name] = wi[:, o:o + wd]
        o += wd
    w_proj = jnp.concatenate(
        [cols["hq"], cols["hf"], cols["hv"], cols["hg"], cols["mq"], cols["cq"], cols["ckv"],
         _pad_cols(cols["kr"], LANES), _pad_cols(cols["kidx"], LANES), _pad_cols(cols["widx"], LANES)],
        axis=1).astype(BF16)
    w_gate = cols["gate"].astype(BF16)

    proj_h, proj_m, proj_d = _norm_proj(x2, attn_norm_w[l], w_proj, (4 * HG_WIDTH, MEM_WIDTH, DSA_GROUP),
                                        (F32, F32, F32), tm)

    lb_all = jnp.cumsum(jax.nn.softmax(hgrn_lower_bounds.astype(F32), axis=0), axis=0)
    o_a = _hgrn(proj_h, lb_all[l], hgrn_out_norm_w[l], B, T)

    wuq_pad = _head_pad(dsa_w_uq[l], LANES).astype(BF16)
    wqr_pad = _head_pad(dsa_w_qr[l], LANES).astype(BF16)
    wqi_pad = _head_pad(dsa_w_qidx[l], LANES).astype(BF16)
    wuk_pad = jnp.pad(dsa_w_uk[l], ((0, 0), (0, LANES - DSA_NOPE), (0, 0))).astype(BF16)
    wuv = dsa_w_uv[l]
    wuv_pad = jnp.stack([jnp.pad(wuv[h], ((0, 0), ((h % 2) * DSA_VDIM, LANES - DSA_VDIM - (h % 2) * DSA_VDIM)))
                         for h in range(DSA_HEADS)]).astype(BF16)
    kinw_pad = _pad_cols(dsa_kidx_norm_w[l].reshape(1, IDX_DIM), LANES)
    tabs = _rope_tables(T, DSA_ROPE) + _rope_tables(T, IDX_DIM)
    qcat, qidx, kv, kidx, widx = _dsa_prep(
        proj_d, dsa_q_norm_w[l].reshape(1, -1), dsa_kv_norm_w[l].reshape(1, -1), kinw_pad,
        wuq_pad, wqr_pad, wuk_pad, wqi_pad, tabs, T, tm)
    o_b = _dsa(qcat, qidx, widx, kv, kidx, wuv_pad, B, T, topk)

    k_m, v_m = _norm_proj(mem.reshape(B * M, D), mem_norm_w[l], w_mem_kv[l].astype(BF16),
                          (MEM_WIDTH, MEM_WIDTH), (BF16, BF16), min(tm, B * M))
    o_c = _mem_attn(proj_m, k_m, v_m, T, tm)

    w_br = jnp.stack([w_br_hgrn[l], w_br_dsa[l], w_br_mem[l]]).astype(BF16)
    h = _merge(x2, attn_norm_w[l].reshape(1, D), o_a, o_b, o_c, w_gate, w_br, w_out[l].astype(BF16), tm)
    out = _mlp(h, mlp_norm_w[l].reshape(1, D), w_mlp_up[l].astype(BF16), w_mlp_down[l].astype(BF16),
               final_norm_w.reshape(1, D), tm)
    return out.reshape(B, T, D)
```

```python
import functools

import jax
import jax.numpy as jnp
import numpy as np
from jax import lax
from jax.experimental import pallas as pl
from jax.experimental.pallas import tpu as pltpu

F32 = jnp.float32
BF16 = jnp.bfloat16
I32 = jnp.int32
I16 = jnp.int16

EPS = 1e-6
ROPE_THETA = 10000.0
LANES = 128
SUBLANES = 8
HG_HEADS, HG_D, HG_CHUNK = 4, 128, 64
HG_WIDTH = HG_HEADS * HG_D
HG_SUB = 16
HG_STEP = 256
DSA_HEADS, DSA_Q_LORA, DSA_KV_LORA = 8, 256, 128
DSA_NOPE, DSA_ROPE, DSA_VDIM = 64, 32, 64
IDX_HEADS, IDX_DIM = 8, 64
TOPK_MAX = 256
MEM_HEADS, MEM_DH = 4, 128
MEM_WIDTH = MEM_HEADS * MEM_DH
N_BRANCH = 3
DSA_GROUP = 6 * LANES
VMEM_LIMIT = 56 * 1024 * 1024
INT_MIN = np.int32(-2**31)

DSA_TQ = 128
DSA_TS = 512
DSA_TK = 512


def _cparams(sem):
    return pltpu.CompilerParams(dimension_semantics=sem, vmem_limit_bytes=VMEM_LIMIT)


def _rms(x, w, n=None):
    n = x.shape[-1] if n is None else n
    ms = jnp.sum(x * x, axis=-1, keepdims=True) * (1.0 / n)
    return x * lax.rsqrt(ms + EPS) * w


def _sigmoid(x):
    return 1.0 / (1.0 + jnp.exp(-x))


def _dot(a, b):
    return jnp.dot(a, b, preferred_element_type=F32)


def _dot_nt(a, b):
    return lax.dot_general(a, b, (((1,), (1,)), ((), ())), preferred_element_type=F32)


def _dot_tn(a, b):
    return lax.dot_general(a, b, (((0,), (0,)), ((), ())), preferred_element_type=F32)


def _norm_proj_kernel(x_ref, nw_ref, w_ref, *out_refs):
    xn = _rms(x_ref[...], nw_ref[...]).astype(BF16)
    off = 0
    for o_ref in out_refs:
        wd = o_ref.shape[-1]
        o_ref[...] = _dot(xn, w_ref[:, off:off + wd]).astype(o_ref.dtype)
        off += wd


def _norm_proj(x, nw, w, widths, dtypes, tm):
    n, d = x.shape
    return pl.pallas_call(
        _norm_proj_kernel,
        grid=(n // tm,),
        in_specs=[pl.BlockSpec((tm, d), lambda i: (i, 0)),
                  pl.BlockSpec((1, d), lambda i: (0, 0)),
                  pl.BlockSpec(w.shape, lambda i: (0, 0))],
        out_specs=[pl.BlockSpec((tm, wd), lambda i: (i, 0)) for wd in widths],
        out_shape=[jax.ShapeDtypeStruct((n, wd), dt) for wd, dt in zip(widths, dtypes)],
        compiler_params=_cparams(("parallel",)),
        name="norm_proj",
    )(x, nw.reshape(1, d), w)


def _hgrn_kernel(q_ref, f_ref, v_ref, g_ref, lb_ref, onw_ref, o_ref, st_ref, b_ref, k_ref):
    @pl.when(pl.program_id(1) == 0)
    def _():
        st_ref[...] = jnp.zeros_like(st_ref)

    C, SB = HG_CHUNK, HG_SUB
    lb = lb_ref[...]
    ri = lax.broadcasted_iota(I32, (C, C), 0)
    ci = lax.broadcasted_iota(I32, (C, C), 1)
    tri = jnp.where(ri >= ci, 1.0, 0.0).astype(BF16)
    row_c = lax.broadcasted_iota(I32, (C, HG_D), 0)
    row_2 = lax.broadcasted_iota(I32, (SB, HG_D), 0)
    row_1 = lax.broadcasted_iota(I32, (SUBLANES, HG_D), 0)
    lane_2 = lax.broadcasted_iota(I32, (SB, C), 1)
    lane_1 = lax.broadcasted_iota(I32, (SUBLANES, C), 1)

    def chunk(ci_, carry):
        rows = pl.ds(pl.multiple_of(ci_ * C, C), C)
        f = lb + (1.0 - lb) * _sigmoid(f_ref[rows, :])
        lf = jnp.log(f)
        hi = lf.astype(BF16)
        r1 = lf - hi.astype(F32)
        mid = r1.astype(BF16)
        lo = (r1 - mid.astype(F32)).astype(BF16)
        b_ref[...] = _dot(tri, hi) + _dot(tri, mid) + _dot(tri, lo)
        k_ref[...] = 1.0 - f

        for h in range(HG_HEADS):
            sl = slice(h * HG_D, (h + 1) * HG_D)
            qh = q_ref[rows, sl]
            vh = v_ref[rows, sl]
            bh = b_ref[:, sl]
            kh = k_ref[:, sl]
            a_rows = [jnp.zeros((SB, C), F32)]
            for i in range(1, C // SB):
                bref = b_ref[i * SB:i * SB + 1, sl]
                qi = (qh[i * SB:(i + 1) * SB] * jnp.exp(bh[i * SB:(i + 1) * SB] - bref)).astype(BF16)
                ki = (kh * jnp.exp(jnp.where(row_c < i * SB, bref - bh, -jnp.inf))).astype(BF16)
                a_rows.append(_dot_nt(qi, ki))
            for i in range(C // SB):
                b2, q2 = bh[i * SB:(i + 1) * SB], qh[i * SB:(i + 1) * SB]
                b1, q1 = b2[SUBLANES:], q2[SUBLANES:]
                a2 = a_rows[i]
                for j in range(SB):
                    s = i * SB + j
                    brow = b_ref[s:s + 1, sl]
                    krow = k_ref[s:s + 1, sl]
                    if j < SUBLANES:
                        e = jnp.exp(jnp.where(row_2 >= j, b2 - brow, -jnp.inf))
                        r = jnp.sum(e * (q2 * krow), axis=-1, keepdims=True)
                        a2 = jnp.where(lane_2 == s, r, a2)
                    else:
                        e = jnp.exp(jnp.where(row_1 >= j - SUBLANES, b1 - brow, -jnp.inf))
                        r = jnp.sum(e * (q1 * krow), axis=-1, keepdims=True)
                        a2 = jnp.concatenate([a2[:SUBLANES], jnp.where(lane_1 == s, r, a2[SUBLANES:])], axis=0)
                a_rows[i] = a2
            a_mat = jnp.concatenate(a_rows, axis=0)
            st = st_ref[h]
            qe = (qh * jnp.exp(bh)).astype(BF16)
            vb = vh.astype(BF16)
            o = _dot_nt(qe, st.astype(BF16)) + _dot(a_mat.astype(BF16), vb)
            blast = bh[C - 1:C, :]
            kd = (kh * jnp.exp(blast - bh)).astype(BF16)
            st_ref[h] = st * jnp.exp(blast) + _dot_tn(vb, kd)
            gh = g_ref[rows, sl]
            o_ref[rows, sl] = (_rms(o, onw_ref[:, sl]) * (gh * _sigmoid(gh))).astype(o_ref.dtype)
        return carry

    lax.fori_loop(0, HG_STEP // C, chunk, 0)


def _hgrn(proj_h, lb, onw, batch, seq):
    n = proj_h.shape[0]
    nc = seq // HG_STEP
    blk = lambda col: pl.BlockSpec((HG_STEP, HG_WIDTH), lambda b, c, col=col: (b * nc + c, col))
    vec = pl.BlockSpec((1, HG_WIDTH), lambda b, c: (0, 0))
    return pl.pallas_call(
        _hgrn_kernel,
        grid=(batch, nc),
        in_specs=[blk(0), blk(1), blk(2), blk(3), vec, vec],
        out_specs=pl.BlockSpec((HG_STEP, HG_WIDTH), lambda b, c: (b * nc + c, 0)),
        out_shape=jax.ShapeDtypeStruct((n, HG_WIDTH), BF16),
        scratch_shapes=[pltpu.VMEM((HG_HEADS, HG_D, HG_D), F32),
                        pltpu.VMEM((HG_CHUNK, HG_WIDTH), F32),
                        pltpu.VMEM((HG_CHUNK, HG_WIDTH), F32)],
        compiler_params=_cparams(("parallel", "arbitrary")),
        name="hgrn2",
    )(proj_h, proj_h, proj_h, proj_h, lb.reshape(1, HG_WIDTH), onw.reshape(1, HG_WIDTH))


def _rope_lanes(x, cosf, sins, half):
    nl = x.shape[-1]
    lane = lax.broadcasted_iota(I32, x.shape, 1)
    partner = jnp.where(lane % LANES < half, pltpu.roll(x, nl - half, 1), pltpu.roll(x, half, 1))
    return x * cosf + partner * sins


def _dsa_prep_kernel(pd_ref, qnw_ref, kvnw_ref, kinw_ref, wuq_ref, wqr_ref, wuk_ref, wqi_ref,
                     c32_ref, s32_ref, c64_ref, s64_ref,
                     qcat_ref, qidx_ref, kv_ref, kidx_ref, widx_ref):
    scale = float((DSA_NOPE + DSA_ROPE) ** -0.5)
    c32, s32, c64, s64 = c32_ref[...], s32_ref[...], c64_ref[...], s64_ref[...]
    cqn = _rms(pd_ref[:, 0:DSA_Q_LORA], qnw_ref[...]).astype(BF16)
    qn = _dot(cqn, wuq_ref[...]).astype(BF16)
    qr = _dot(cqn, wqr_ref[...])
    qi = _dot(cqn, wqi_ref[...])
    for h in range(DSA_HEADS):
        sl = slice(h * LANES, (h + 1) * LANES)
        qcat_ref[h, :, 0:LANES] = (_dot(qn[:, sl], wuk_ref[h]) * scale).astype(BF16)
        qcat_ref[h, :, LANES:2 * LANES] = (_rope_lanes(qr[:, sl], c32, s32, DSA_ROPE // 2) * scale).astype(BF16)
        qidx_ref[h] = (_rope_lanes(qi[:, sl], c64, s64, IDX_DIM // 2) * float(IDX_DIM ** -0.5)).astype(BF16)
    kv_ref[:, 0:LANES] = _rms(pd_ref[:, 256:384], kvnw_ref[...]).astype(BF16)
    kv_ref[:, LANES:2 * LANES] = _rope_lanes(pd_ref[:, 384:512], c32, s32, DSA_ROPE // 2).astype(BF16)
    kin = _rms(pd_ref[:, 512:640], kinw_ref[...], n=IDX_DIM)
    kidx_ref[...] = _rope_lanes(kin, c64, s64, IDX_DIM // 2).astype(BF16)
    widx_ref[...] = jnp.transpose(pd_ref[:, 640:768] * float(IDX_HEADS ** -0.5))[0:IDX_HEADS, :]


def _dsa_prep(proj_d, qnw, kvnw, kinw_pad, wuq_pad, wqr_pad, wuk_pad, wqi_pad, tabs, seq, tm):
    n = proj_d.shape[0]
    nt = seq // tm
    full = lambda a: pl.BlockSpec(a.shape, lambda i: (0,) * a.ndim)
    tab = pl.BlockSpec((tm, LANES), lambda i: (i % nt, 0))
    hd = DSA_HEADS
    return pl.pallas_call(
        _dsa_prep_kernel,
        grid=(n // tm,),
        in_specs=[pl.BlockSpec((tm, DSA_GROUP), lambda i: (i, 0)),
                  full(qnw), full(kvnw), full(kinw_pad), full(wuq_pad), full(wqr_pad), full(wuk_pad),
                  full(wqi_pad), tab, tab, tab, tab],
        out_specs=[pl.BlockSpec((hd, tm, 2 * LANES), lambda i: (0, i, 0)),
                   pl.BlockSpec((hd, tm, LANES), lambda i: (0, i, 0)),
                   pl.BlockSpec((tm, 2 * LANES), lambda i: (i, 0)),
                   pl.BlockSpec((tm, LANES), lambda i: (i, 0)),
                   pl.BlockSpec((IDX_HEADS, tm), lambda i: (0, i))],
        out_shape=[jax.ShapeDtypeStruct((hd, n, 2 * LANES), BF16),
                   jax.ShapeDtypeStruct((hd, n, LANES), BF16),
                   jax.ShapeDtypeStruct((n, 2 * LANES), BF16),
                   jax.ShapeDtypeStruct((n, LANES), BF16),
                   jax.ShapeDtypeStruct((IDX_HEADS, n), F32)],
        compiler_params=_cparams(("parallel",)),
        name="dsa_prep",
    )(proj_d, qnw, kvnw, kinw_pad, wuq_pad, wqr_pad, wuk_pad, wqi_pad, *tabs)


def _dsa_kernel(qcat_ref, qidx_ref, widx_ref, kv_ref, kidx_ref, wuv_ref, o_ref,
                sc_ref, ik_ref, lg_ref, *, topk):
    TQ, TK, H, TS = DSA_TQ, DSA_TK, DSA_HEADS, DSA_TS
    NG = TK // LANES
    NR = TS // TQ
    step = pl.program_id(1)
    nkt = (step * TS + TS + TK - 1) // TK
    key_q = lax.broadcasted_iota(I32, (TK, TQ), 0)
    qry_q = lax.broadcasted_iota(I32, (TK, TQ), 1)
    key_s = lax.broadcasted_iota(I32, (TK, TS), 0)

    def key_rows(kt):
        return pl.ds(pl.multiple_of(kt * TK, TK), TK)

    def as_i32(i):
        return jnp.asarray(i, dtype=I32)

    for r in range(NR):
        cols = slice(r * TQ, (r + 1) * TQ)
        qidx = qidx_ref[:, cols, :].reshape(H * TQ, LANES)
        wrows = [widx_ref[h:h + 1, cols] for h in range(IDX_HEADS)]
        tpos = step * TS + r * TQ + qry_q

        def score_tile(kt, carry, cols=cols, qidx=qidx, wrows=wrows, tpos=tpos):
            s_all = jnp.maximum(_dot_nt(kidx_ref[key_rows(kt), :], qidx), 0.0)
            score = s_all[:, 0:TQ] * wrows[0]
            for h in range(1, IDX_HEADS):
                score = score + s_all[:, h * TQ:(h + 1) * TQ] * wrows[h]
            bits = lax.bitcast_convert_type(score + 0.0, I32)
            key = jnp.where(bits < 0, bits ^ np.int32(0x7FFFFFFF), bits)
            ik_ref[kt, :, cols] = jnp.where(kt * TK + key_q <= tpos, key, INT_MIN)
            return carry

        lax.fori_loop(0, nkt, score_tile, 0)

    one, zero, ninf = jnp.float32(1.0), jnp.float32(0.0), jnp.float32(-jnp.inf)

    def count(pred_fn):
        def body(kt, acc):
            m = jnp.where(pred_fn(ik_ref[kt], kt * TK + key_s), one, zero)
            return acc + jnp.sum(m.reshape(TK // SUBLANES, SUBLANES, TS), axis=0)
        acc = lax.fori_loop(0, nkt, body, jnp.zeros((SUBLANES, TS), F32))
        return jnp.sum(acc, axis=0, keepdims=True)

    kf = float(topk)

    def bit_step(i, tu):
        cand = tu | lax.shift_left(np.int32(1), np.int32(31) - as_i32(i))
        cs = cand ^ INT_MIN
        cnt = count(lambda ik, pos: ik >= cs)
        return jnp.where(cnt >= kf, cand, tu)

    tu = lax.fori_loop(0, 32, bit_step, jnp.zeros((1, TS), I32))
    thr = tu ^ INT_MIN
    cnt_gt = count(lambda ik, pos: ik > thr)
    cnt_ge = count(lambda ik, pos: ik >= thr)
    need = kf - cnt_gt

    has_tie = jnp.max(jnp.where((cnt_ge > kf) & (thr > INT_MIN), one, zero)) > zero
    nbits = int(np.log2(TK)) + 3

    def tie_search(_):
        def idx_step(i, x):
            cand = x | lax.shift_left(np.int32(1), np.int32(nbits - 1) - as_i32(i))
            g = count(lambda ik, pos: (ik == thr) & (pos < cand))
            return jnp.where(g < need, cand, x)
        return lax.fori_loop(0, nbits, idx_step, jnp.zeros((1, TS), I32))

    jcut = lax.cond(has_tie, tie_search, lambda _: jnp.full((1, TS), 2**30, I32), 0)
    jcut = jnp.where(thr > INT_MIN, jcut, -1)

    def bias_tile(kt, carry):
        ik = ik_ref[kt]
        sel = (ik > thr) | ((ik == thr) & (kt * TK + key_s <= jcut))
        bias_t = jnp.where(sel, zero, ninf)
        for r in range(NR):
            sc_ref[kt, r * TQ:(r + 1) * TQ, :] = jnp.transpose(bias_t[:, r * TQ:(r + 1) * TQ])
        return carry

    lax.fori_loop(0, nkt, bias_tile, 0)

    for r in range(NR):
        rows = slice(r * TQ, (r + 1) * TQ)
        qcat = qcat_ref[:, rows, :].reshape(H * TQ, 2 * LANES)

        def logit_tile(kt, m_run, rows=rows, qcat=qcat):
            lg = _dot_nt(qcat, kv_ref[key_rows(kt), :]).reshape(H, TQ, TK) + sc_ref[kt, rows, :][None]
            lg = lg.reshape(H * TQ, TK)
            lg_ref[kt] = lg
            part = lg[:, 0:LANES]
            for g in range(1, NG):
                part = jnp.maximum(part, lg[:, g * LANES:(g + 1) * LANES])
            return jnp.maximum(m_run, part)

        m_run = lax.fori_loop(0, nkt, logit_tile, jnp.full((H * TQ, LANES), -jnp.inf, F32))
        m_row = jnp.max(m_run, axis=-1, keepdims=True)

        def pv_tile(kt, carry, m_row=m_row):
            l_run, acc = carry
            p = jnp.exp(lg_ref[kt] - m_row)
            part = p[:, 0:LANES]
            for g in range(1, NG):
                part = part + p[:, g * LANES:(g + 1) * LANES]
            acc = acc + _dot(p.astype(BF16), kv_ref[key_rows(kt), 0:LANES])
            return l_run + part, acc

        l_run, acc = lax.fori_loop(0, nkt, pv_tile,
                                   (jnp.zeros((H * TQ, LANES), F32), jnp.zeros((H * TQ, LANES), F32)))
        o_lat = (acc / jnp.sum(l_run, axis=-1, keepdims=True)).astype(BF16)
        for pr in range(H // 2):
            h0, h1 = 2 * pr, 2 * pr + 1
            o_ref[rows, pr * LANES:(pr + 1) * LANES] = (
                _dot(o_lat[h0 * TQ:(h0 + 1) * TQ], wuv_ref[h0]) + _dot(o_lat[h1 * TQ:(h1 + 1) * TQ], wuv_ref[h1])
            ).astype(o_ref.dtype)


def _dsa(qcat, qidx, widx, kv, kidx, wuv_pad, batch, seq, topk):
    n = kv.shape[0]
    ns = seq // DSA_TS
    nkt_max = seq // DSA_TK
    H = DSA_HEADS
    return pl.pallas_call(
        functools.partial(_dsa_kernel, topk=topk),
        grid=(batch, ns),
        in_specs=[pl.BlockSpec((H, DSA_TS, 2 * LANES), lambda b, s: (0, b * ns + s, 0)),
                  pl.BlockSpec((H, DSA_TS, LANES), lambda b, s: (0, b * ns + s, 0)),
                  pl.BlockSpec((IDX_HEADS, DSA_TS), lambda b, s: (0, b * ns + s)),
                  pl.BlockSpec((seq, 2 * LANES), lambda b, s: (b, 0)),
                  pl.BlockSpec((seq, LANES), lambda b, s: (b, 0)),
                  pl.BlockSpec(wuv_pad.shape, lambda b, s: (0, 0, 0))],
        out_specs=pl.BlockSpec((DSA_TS, H * DSA_VDIM), lambda b, s: (b * ns + s, 0)),
        out_shape=jax.ShapeDtypeStruct((n, H * DSA_VDIM), BF16),
        scratch_shapes=[pltpu.VMEM((nkt_max, DSA_TS, DSA_TK), F32),
                        pltpu.VMEM((nkt_max, DSA_TK, DSA_TS), I32),
                        pltpu.VMEM((nkt_max, H * DSA_TQ, DSA_TK), F32)],
        compiler_params=_cparams(("parallel", "arbitrary")),
        name="dsa_attn",
    )(qcat, qidx, widx, kv, kidx, wuv_pad)


def _mem_attn_kernel(q_ref, k_ref, v_ref, o_ref):
    scale = float(MEM_DH ** -0.5)
    for h in range(MEM_HEADS):
        sl = slice(h * MEM_DH, (h + 1) * MEM_DH)
        lg = _dot_nt((q_ref[:, sl] * scale).astype(BF16), k_ref[:, sl])
        p = jnp.exp(lg - jnp.max(lg, axis=-1, keepdims=True))
        o = _dot(p.astype(BF16), v_ref[:, sl]) / jnp.sum(p, axis=-1, keepdims=True)
        o_ref[:, sl] = o.astype(o_ref.dtype)


def _mem_attn(proj_m, k_m, v_m, seq, tm):
    n = proj_m.shape[0]
    nt = seq // tm
    m = k_m.shape[0] // (n // seq)
    return pl.pallas_call(
        _mem_attn_kernel,
        grid=(n // tm,),
        in_specs=[pl.BlockSpec((tm, MEM_WIDTH), lambda i: (i, 0)),
                  pl.BlockSpec((m, MEM_WIDTH), lambda i: (i // nt, 0)),
                  pl.BlockSpec((m, MEM_WIDTH), lambda i: (i // nt, 0))],
        out_specs=pl.BlockSpec((tm, MEM_WIDTH), lambda i: (i, 0)),
        out_shape=jax.ShapeDtypeStruct((n, MEM_WIDTH), BF16),
        compiler_params=_cparams(("parallel",)),
        name="mem_attn",
    )(proj_m, k_m, v_m)


def _merge_kernel(x_ref, nw_ref, oa_ref, ob_ref, oc_ref, wg_ref, wbr_ref, wo_ref, h_ref):
    x = x_ref[...]
    d = x.shape[-1]
    xn = _rms(x, nw_ref[...]).astype(BF16)
    merged = None
    for br, o_br in enumerate((oa_ref, ob_ref, oc_ref)):
        gate = _sigmoid(_dot(xn, wg_ref[:, br * d:(br + 1) * d]))
        term = gate * _dot(o_br[...], wbr_ref[br])
        merged = term if merged is None else merged + term
    h_ref[...] = x + _dot(merged.astype(BF16), wo_ref[...])


def _merge(x, nw, o_a, o_b, o_c, w_gate, w_br, w_out, tm):
    n, d = x.shape
    row = lambda wd: pl.BlockSpec((tm, wd), lambda i: (i, 0))
    full = lambda a: pl.BlockSpec(a.shape, lambda i: (0,) * a.ndim)
    return pl.pallas_call(
        _merge_kernel,
        grid=(n // tm,),
        in_specs=[row(d), full(nw), row(o_a.shape[1]), row(o_b.shape[1]), row(o_c.shape[1]),
                  full(w_gate), full(w_br), full(w_out)],
        out_specs=row(d),
        out_shape=jax.ShapeDtypeStruct((n, d), F32),
        compiler_params=_cparams(("parallel",)),
        name="merge_out",
    )(x, nw, o_a, o_b, o_c, w_gate, w_br, w_out)


def _mlp_kernel(h_ref, nw_ref, wu_ref, wd_ref, fw_ref, o_ref, *, ff_chunk):
    h = h_ref[...]
    m = _rms(h, nw_ref[...]).astype(BF16)
    acc = h
    for c0 in range(0, wu_ref.shape[1], ff_chunk):
        u = jnp.maximum(_dot(m, wu_ref[:, c0:c0 + ff_chunk]), 0.0)
        acc = acc + _dot((u * u).astype(BF16), wd_ref[c0:c0 + ff_chunk, :])
    o_ref[...] = _rms(acc, fw_ref[...])


def _mlp(h, nw, w_up, w_down, fw, tm, ff_chunk=1024):
    n, d = h.shape
    full = lambda a: pl.BlockSpec(a.shape, lambda i: (0,) * a.ndim)
    return pl.pallas_call(
        functools.partial(_mlp_kernel, ff_chunk=ff_chunk),
        grid=(n // tm,),
        in_specs=[pl.BlockSpec((tm, d), lambda i: (i, 0)), full(nw), full(w_up), full(w_down), full(fw)],
        out_specs=pl.BlockSpec((tm, d), lambda i: (i, 0)),
        out_shape=jax.ShapeDtypeStruct((n, d), F32),
        compiler_params=_cparams(("parallel",)),
        name="mlp_final",
    )(h, nw, w_up, w_down, fw)


def _pad_cols(w, width):
    return jnp.pad(w, ((0, 0), (0, width - w.shape[1])))


def _head_pad(w, width):
    return jnp.pad(w, ((0, 0), (0, 0), (0, width - w.shape[2]))).reshape(w.shape[0], -1)


def _rope_tables(seq, d):
    inv = ROPE_THETA ** (-jnp.arange(0, d, 2, dtype=F32) / d)
    ang = jnp.arange(seq, dtype=F32)[:, None] * inv[None, :]
    cos, sin = jnp.cos(ang), jnp.sin(ang)
    cosf = _pad_cols(jnp.concatenate([cos, cos], axis=1), LANES)
    sins = _pad_cols(jnp.concatenate([-sin, sin], axis=1), LANES)
    return cosf, sins


def kernel(x, mem, attn_norm_w, w_in, hgrn_lower_bounds, hgrn_out_norm_w, dsa_q_norm_w, dsa_kv_norm_w,
           dsa_kidx_norm_w, dsa_w_uq, dsa_w_qr, dsa_w_uk, dsa_w_uv, dsa_w_qidx, mem_norm_w, w_mem_kv,
           w_br_hgrn, w_br_dsa, w_br_mem, w_out, mlp_norm_w, w_mlp_up, w_mlp_down, final_norm_w):
    B, T, D = x.shape
    N = B * T
    M = mem.shape[1]
    l = 0
    topk = min(TOPK_MAX, T // 4)
    tm = 512
    x2 = x.reshape(N, D)

    wi = w_in[l]
    o = 0
    cols = {}
    for name, wd in (("hq", HG_WIDTH), ("hf", HG_WIDTH), ("hv", HG_WIDTH), ("hg", HG_WIDTH),
                     ("cq", DSA_Q_LORA), ("ckv", DSA_KV_LORA), ("kr", DSA_ROPE), ("kidx", IDX_DIM),
                     ("widx", IDX_HEADS), ("mq", MEM_WIDTH), ("gate", N_BRANCH * D)):
        cols[name] = wi[:, o:o + wd]
        o += wd
    w_proj = jnp.concatenate(
        [cols["hq"], cols["hf"], cols["hv"], cols["hg"], cols["mq"], cols["cq"], cols["ckv"],
         _pad_cols(cols["kr"], LANES), _pad_cols(cols["kidx"], LANES), _pad_cols(cols["widx"], LANES)],
        axis=1).astype(BF16)
    w_gate = cols["gate"].astype(BF16)

    proj_h, proj_m, proj_d = _norm_proj(x2, attn_norm_w[l], w_proj, (4 * HG_WIDTH, MEM_WIDTH, DSA_GROUP),
                                        (F32, F32, F32), tm)

    lb_all = jnp.cumsum(jax.nn.softmax(hgrn_lower_bounds.astype(F32), axis=0), axis=0)
    o_a = _hgrn(proj_h, lb_all[l], hgrn_out_norm_w[l], B, T)

    wuq_pad = _head_pad(dsa_w_uq[l], LANES).astype(BF16)
    wqr_pad = _head_pad(dsa_w_qr[l], LANES).astype(BF16)
    wqi_pad = _head_pad(dsa_w_qidx[l], LANES).astype(BF16)
    wuk_pad = jnp.pad(dsa_w_uk[l], ((0, 0), (0, LANES - DSA_NOPE), (0, 0))).astype(BF16)
    wuv = dsa_w_uv[l]
    wuv_pad = jnp.stack([jnp.pad(wuv[h], ((0, 0), ((h % 2) * DSA_VDIM, LANES - DSA_VDIM - (h % 2) * DSA_VDIM)))
                         for h in range(DSA_HEADS)]).astype(BF16)
    kinw_pad = _pad_cols(dsa_kidx_norm_w[l].reshape(1, IDX_DIM), LANES)
    tabs = _rope_tables(T, DSA_ROPE) + _rope_tables(T, IDX_DIM)
    qcat, qidx, kv, kidx, widx = _dsa_prep(
        proj_d, dsa_q_norm_w[l].reshape(1, -1), dsa_kv_norm_w[l].reshape(1, -1), kinw_pad,
        wuq_pad, wqr_pad, wuk_pad, wqi_pad, tabs, T, tm)
    o_b = _dsa(qcat, qidx, widx, kv, kidx, wuv_pad, B, T, topk)

    k_m, v_m = _norm_proj(mem.reshape(B * M, D), mem_norm_w[l], w_mem_kv[l].astype(BF16),
                          (MEM_WIDTH, MEM_WIDTH), (BF16, BF16), min(tm, B * M))
    o_c = _mem_attn(proj_m, k_m, v_m, T, tm)

    w_br = jnp.stack([w_br_hgrn[l], w_br_dsa[l], w_br_mem[l]]).astype(BF16)
    h = _merge(x2, attn_norm_w[l].reshape(1, D), o_a, o_b, o_c, w_gate, w_br, w_out[l].astype(BF16), tm)
    out = _mlp(h, mlp_norm_w[l].reshape(1, D), w_mlp_up[l].astype(BF16), w_mlp_down[l].astype(BF16),
               final_norm_w.reshape(1, D), tm)
    return out.reshape(B, T, D)
```

```python
import functools

import jax
import jax.numpy as jnp
import numpy as np
from jax import lax
from jax.experimental import pallas as pl
from jax.experimental.pallas import tpu as pltpu

F32 = jnp.float32
BF16 = jnp.bfloat16
I32 = jnp.int32
I16 = jnp.int16

EPS = 1e-6
ROPE_THETA = 10000.0
LANES = 128
SUBLANES = 8
HG_HEADS, HG_D, HG_CHUNK = 4, 128, 64
HG_WIDTH = HG_HEADS * HG_D
HG_SUB = 16
HG_STEP = 256
DSA_HEADS, DSA_Q_LORA, DSA_KV_LORA = 8, 256, 128
DSA_NOPE, DSA_ROPE, DSA_VDIM = 64, 32, 64
IDX_HEADS, IDX_DIM = 8, 64
TOPK_MAX = 256
MEM_HEADS, MEM_DH = 4, 128
MEM_WIDTH = MEM_HEADS * MEM_DH
N_BRANCH = 3
DSA_GROUP = 6 * LANES
VMEM_LIMIT = 56 * 1024 * 1024
INT_MIN = np.int32(-2**31)

DSA_TQ = 128
DSA_TS = 512
DSA_TK = 512


def _cparams(sem):
    return pltpu.CompilerParams(dimension_semantics=sem, vmem_limit_bytes=VMEM_LIMIT)


def _rms(x, w, n=None):
    n = x.shape[-1] if n is None else n
    ms = jnp.sum(x * x, axis=-1, keepdims=True) * (1.0 / n)
    return x * lax.rsqrt(ms + EPS) * w


def _sigmoid(x):
    return 1.0 / (1.0 + jnp.exp(-x))


def _dot(a, b):
    return jnp.dot(a, b, preferred_element_type=F32)


def _dot_nt(a, b):
    return lax.dot_general(a, b, (((1,), (1,)), ((), ())), preferred_element_type=F32)


def _dot_tn(a, b):
    return lax.dot_general(a, b, (((0,), (0,)), ((), ())), preferred_element_type=F32)


def _norm_proj_kernel(x_ref, nw_ref, w_ref, *out_refs):
    xn = _rms(x_ref[...], nw_ref[...]).astype(BF16)
    off = 0
    for o_ref in out_refs:
        wd = o_ref.shape[-1]
        o_ref[...] = _dot(xn, w_ref[:, off:off + wd]).astype(o_ref.dtype)
        off += wd


def _norm_proj(x, nw, w, widths, dtypes, tm):
    n, d = x.shape
    return pl.pallas_call(
        _norm_proj_kernel,
        grid=(n // tm,),
        in_specs=[pl.BlockSpec((tm, d), lambda i: (i, 0)),
                  pl.BlockSpec((1, d), lambda i: (0, 0)),
                  pl.BlockSpec(w.shape, lambda i: (0, 0))],
        out_specs=[pl.BlockSpec((tm, wd), lambda i: (i, 0)) for wd in widths],
        out_shape=[jax.ShapeDtypeStruct((n, wd), dt) for wd, dt in zip(widths, dtypes)],
        compiler_params=_cparams(("parallel",)),
        name="norm_proj",
    )(x, nw.reshape(1, d), w)


def _hgrn_kernel(q_ref, f_ref, v_ref, g_ref, lb_ref, onw_ref, o_ref, st_ref, b_ref, k_ref):
    @pl.when(pl.program_id(1) == 0)
    def _():
        st_ref[...] = jnp.zeros_like(st_ref)

    C, SB = HG_CHUNK, HG_SUB
    lb = lb_ref[...]
    ri = lax.broadcasted_iota(I32, (C, C), 0)
    ci = lax.broadcasted_iota(I32, (C, C), 1)
    tri = jnp.where(ri >= ci, 1.0, 0.0).astype(BF16)
    row_c = lax.broadcasted_iota(I32, (C, HG_D), 0)
    row_2 = lax.broadcasted_iota(I32, (SB, HG_D), 0)
    row_1 = lax.broadcasted_iota(I32, (SUBLANES, HG_D), 0)
    lane_2 = lax.broadcasted_iota(I32, (SB, C), 1)
    lane_1 = lax.broadcasted_iota(I32, (SUBLANES, C), 1)

    def chunk(ci_, carry):
        rows = pl.ds(pl.multiple_of(ci_ * C, C), C)
        f = lb + (1.0 - lb) * _sigmoid(f_ref[rows, :])
        lf = jnp.log(f)
        hi = lf.astype(BF16)
        r1 = lf - hi.astype(F32)
        mid = r1.astype(BF16)
        lo = (r1 - mid.astype(F32)).astype(BF16)
        b_ref[...] = _dot(tri, hi) + _dot(tri, mid) + _dot(tri, lo)
        k_ref[...] = 1.0 - f

        for h in range(HG_HEADS):
            sl = slice(h * HG_D, (h + 1) * HG_D)
            qh = q_ref[rows, sl]
            vh = v_ref[rows, sl]
            bh = b_ref[:, sl]
            kh = k_ref[:, sl]
            a_rows = [jnp.zeros((SB, C), F32)]
            for i in range(1, C // SB):
                bref = b_ref[i * SB:i * SB + 1, sl]
                qi = (qh[i * SB:(i + 1) * SB] * jnp.exp(bh[i * SB:(i + 1) * SB] - bref)).astype(BF16)
                ki = (kh * jnp.exp(jnp.where(row_c < i * SB, bref - bh, -jnp.inf))).astype(BF16)
                a_rows.append(_dot_nt(qi, ki))
            for i in range(C // SB):
                b2, q2 = bh[i * SB:(i + 1) * SB], qh[i * SB:(i + 1) * SB]
                b1, q1 = b2[SUBLANES:], q2[SUBLANES:]
                a2 = a_rows[i]
                for j in range(SB):
                    s = i * SB + j
                    brow = b_ref[s:s + 1, sl]
                    krow = k_ref[s:s + 1, sl]
                    if j < SUBLANES:
                        e = jnp.exp(jnp.where(row_2 >= j, b2 - brow, -jnp.inf))
                        r = jnp.sum(e * (q2 * krow), axis=-1, keepdims=True)
                        a2 = jnp.where(lane_2 == s, r, a2)
                    else:
                        e = jnp.exp(jnp.where(row_1 >= j - SUBLANES, b1 - brow, -jnp.inf))
                        r = jnp.sum(e * (q1 * krow), axis=-1, keepdims=True)
                        a2 = jnp.concatenate([a2[:SUBLANES], jnp.where(lane_1 == s, r, a2[SUBLANES:])], axis=0)
                a_rows[i] = a2
            a_mat = jnp.concatenate(a_rows, axis=0)
            st = st_ref[h]
            qe = (qh * jnp.exp(bh)).astype(BF16)
            vb = vh.astype(BF16)
            o = _dot_nt(qe, st.astype(BF16)) + _dot(a_mat.astype(BF16), vb)
            blast = bh[C - 1:C, :]
            kd = (kh * jnp.exp(blast - bh)).astype(BF16)
            st_ref[h] = st * jnp.exp(blast) + _dot_tn(vb, kd)
            gh = g_ref[rows, sl]
            o_ref[rows, sl] = (_rms(o, onw_ref[:, sl]) * (gh * _sigmoid(gh))).astype(o_ref.dtype)
        return carry

    lax.fori_loop(0, HG_STEP // C, chunk, 0)


def _hgrn(proj_h, lb, onw, batch, seq):
    n = proj_h.shape[0]
    nc = seq // HG_STEP
    blk = lambda col: pl.BlockSpec((HG_STEP, HG_WIDTH), lambda b, c, col=col: (b * nc + c, col))
    vec = pl.BlockSpec((1, HG_WIDTH), lambda b, c: (0, 0))
    return pl.pallas_call(
        _hgrn_kernel,
        grid=(batch, nc),
        in_specs=[blk(0), blk(1), blk(2), blk(3), vec, vec],
        out_specs=pl.BlockSpec((HG_STEP, HG_WIDTH), lambda b, c: (b * nc + c, 0)),
        out_shape=jax.ShapeDtypeStruct((n, HG_WIDTH), BF16),
        scratch_shapes=[pltpu.VMEM((HG_HEADS, HG_D, HG_D), F32),
                        pltpu.VMEM((HG_CHUNK, HG_WIDTH), F32),
                        pltpu.VMEM((HG_CHUNK, HG_WIDTH), F32)],
        compiler_params=_cparams(("parallel", "arbitrary")),
        name="hgrn2",
    )(proj_h, proj_h, proj_h, proj_h, lb.reshape(1, HG_WIDTH), onw.reshape(1, HG_WIDTH))


def _rope_lanes(x, cosf, sins, half):
    nl = x.shape[-1]
    lane = lax.broadcasted_iota(I32, x.shape, 1)
    partner = jnp.where(lane % LANES < half, pltpu.roll(x, nl - half, 1), pltpu.roll(x, half, 1))
    return x * cosf + partner * sins


def _dsa_prep_kernel(pd_ref, qnw_ref, kvnw_ref, kinw_ref, wuq_ref, wqr_ref, wuk_ref, wqi_ref,
                     c32_ref, s32_ref, c64_ref, s64_ref,
                     qcat_ref, qidx_ref, kv_ref, kidx_ref, widx_ref):
    scale = float((DSA_NOPE + DSA_ROPE) ** -0.5)
    c32, s32, c64, s64 = c32_ref[...], s32_ref[...], c64_ref[...], s64_ref[...]
    cqn = _rms(pd_ref[:, 0:DSA_Q_LORA], qnw_ref[...]).astype(BF16)
    qn = _dot(cqn, wuq_ref[...]).astype(BF16)
    qr = _dot(cqn, wqr_ref[...])
    qi = _dot(cqn, wqi_ref[...])
    for h in range(DSA_HEADS):
        sl = slice(h * LANES, (h + 1) * LANES)
        qcat_ref[h, :, 0:LANES] = (_dot(qn[:, sl], wuk_ref[h]) * scale).astype(BF16)
        qcat_ref[h, :, LANES:2 * LANES] = (_rope_lanes(qr[:, sl], c32, s32, DSA_ROPE // 2) * scale).astype(BF16)
        qidx_ref[h] = (_rope_lanes(qi[:, sl], c64, s64, IDX_DIM // 2) * float(IDX_DIM ** -0.5)).astype(BF16)
    kv_ref[:, 0:LANES] = _rms(pd_ref[:, 256:384], kvnw_ref[...]).astype(BF16)
    kv_ref[:, LANES:2 * LANES] = _rope_lanes(pd_ref[:, 384:512], c32, s32, DSA_ROPE // 2).astype(BF16)
    kin = _rms(pd_ref[:, 512:640], kinw_ref[...], n=IDX_DIM)
    kidx_ref[...] = _rope_lanes(kin, c64, s64, IDX_DIM // 2).astype(BF16)
    widx_ref[...] = jnp.transpose(pd_ref[:, 640:768] * float(IDX_HEADS ** -0.5))[0:IDX_HEADS, :]


def _dsa_prep(proj_d, qnw, kvnw, kinw_pad, wuq_pad, wqr_pad, wuk_pad, wqi_pad, tabs, seq, tm):
    n = proj_d.shape[0]
    nt = seq // tm
    full = lambda a: pl.BlockSpec(a.shape, lambda i: (0,) * a.ndim)
    tab = pl.BlockSpec((tm, LANES), lambda i: (i % nt, 0))
    hd = DSA_HEADS
    return pl.pallas_call(
        _dsa_prep_kernel,
        grid=(n // tm,),
        in_specs=[pl.BlockSpec((tm, DSA_GROUP), lambda i: (i, 0)),
                  full(qnw), full(kvnw), full(kinw_pad), full(wuq_pad), full(wqr_pad), full(wuk_pad),
                  full(wqi_pad), tab, tab, tab, tab],
        out_specs=[pl.BlockSpec((hd, tm, 2 * LANES), lambda i: (0, i, 0)),
                   pl.BlockSpec((hd, tm, LANES), lambda i: (0, i, 0)),
                   pl.BlockSpec((tm, 2 * LANES), lambda i: (i, 0)),
                   pl.BlockSpec((tm, LANES), lambda i: (i, 0)),
                   pl.BlockSpec((IDX_HEADS, tm), lambda i: (0, i))],
        out_shape=[jax.ShapeDtypeStruct((hd, n, 2 * LANES), BF16),
                   jax.ShapeDtypeStruct((hd, n, LANES), BF16),
                   jax.ShapeDtypeStruct((n, 2 * LANES), BF16),
                   jax.ShapeDtypeStruct((n, LANES), BF16),
                   jax.ShapeDtypeStruct((IDX_HEADS, n), F32)],
        compiler_params=_cparams(("parallel",)),
        name="dsa_prep",
    )(proj_d, qnw, kvnw, kinw_pad, wuq_pad, wqr_pad, wuk_pad, wqi_pad, *tabs)


def _dsa_kernel(qcat_ref, qidx_ref, widx_ref, kv_ref, kidx_ref, wuv_ref, o_ref,
                sc_ref, ik_ref, hi_ref, lo_ref, lg_ref, *, topk):
    TQ, TK, H, TS = DSA_TQ, DSA_TK, DSA_HEADS, DSA_TS
    NG = TK // LANES
    NR = TS // TQ
    step = pl.program_id(1)
    nkt = (step * TS + TS + TK - 1) // TK
    key_q = lax.broadcasted_iota(I32, (TK, TQ), 0)
    qry_q = lax.broadcasted_iota(I32, (TK, TQ), 1)
    key_s = lax.broadcasted_iota(I32, (TK, TS), 0)

    def key_rows(kt):
        return pl.ds(pl.multiple_of(kt * TK, TK), TK)

    def as_i32(i):
        return jnp.asarray(i, dtype=I32)

    for r in range(NR):
        cols = slice(r * TQ, (r + 1) * TQ)
        qidx = qidx_ref[:, cols, :].reshape(H * TQ, LANES)
        wrows = [widx_ref[h:h + 1, cols] for h in range(IDX_HEADS)]
        tpos = step * TS + r * TQ + qry_q

        def score_tile(kt, carry, cols=cols, qidx=qidx, wrows=wrows, tpos=tpos):
            s_all = jnp.maximum(_dot_nt(kidx_ref[key_rows(kt), :], qidx), 0.0)
            score = s_all[:, 0:TQ] * wrows[0]
            for h in range(1, IDX_HEADS):
                score = score + s_all[:, h * TQ:(h + 1) * TQ] * wrows[h]
            bits = lax.bitcast_convert_type(score + 0.0, I32)
            key = jnp.where(bits < 0, bits ^ np.int32(0x7FFFFFFF), bits)
            key = jnp.where(kt * TK + key_q <= tpos, key, INT_MIN)
            ik_ref[kt, :, cols] = key
            hi_ref[kt, :, cols] = lax.shift_right_arithmetic(key, np.int32(16)).astype(I16)
            return carry

        lax.fori_loop(0, nkt, score_tile, 0)

    one, zero, ninf = jnp.float32(1.0), jnp.float32(0.0), jnp.float32(-jnp.inf)

    def count(pred_fn):
        def body(kt, acc):
            m = jnp.where(pred_fn(ik_ref[kt], kt * TK + key_s), one, zero)
            return acc + jnp.sum(m.reshape(TK // SUBLANES, SUBLANES, TS), axis=0)
        acc = lax.fori_loop(0, nkt, body, jnp.zeros((SUBLANES, TS), F32))
        return jnp.sum(acc, axis=0, keepdims=True)

    PK = 2 * SUBLANES

    def count16(x_ref, t16, strict=False):
        one16, zero16 = jnp.ones((), I16), jnp.zeros((), I16)

        def body(kt, acc):
            x = x_ref[kt]
            m = jnp.where((x > t16) if strict else (x >= t16), one16, zero16)
            for g in range(TK // PK):
                acc = acc + m[g * PK:(g + 1) * PK, :]
            return acc
        acc = lax.fori_loop(0, nkt, body, jnp.zeros((PK, TS), I16))
        return jnp.sum(acc.astype(I32), axis=0, keepdims=True).astype(F32)

    def search16(x_ref, kneed):
        def bit_step(i, tu):
            cand = tu | lax.shift_left(np.int32(1), np.int32(15) - as_i32(i))
            cnt = count16(x_ref, (cand - np.int32(32768)).astype(I16))
            return jnp.where(cnt >= kneed, cand, tu)
        return lax.fori_loop(0, 16, bit_step, jnp.zeros((1, TS), I32))

    kf = float(topk)
    p_hi = search16(hi_ref, kf) - np.int32(32768)
    p_hi16 = p_hi.astype(I16)
    k_lo = kf - count16(hi_ref, p_hi16, strict=True)

    def low_tile(kt, carry):
        lo = ((ik_ref[kt] & np.int32(0xFFFF)) - np.int32(32768)).astype(I16)
        lo_ref[kt] = jnp.where(hi_ref[kt] == p_hi16, lo, jnp.full((), -32768, I16))
        return carry

    lax.fori_loop(0, nkt, low_tile, 0)
    thr = p_hi * np.int32(65536) + search16(lo_ref, k_lo)
    cnt_gt = count(lambda ik, pos: ik > thr)
    cnt_ge = count(lambda ik, pos: ik >= thr)
    need = kf - cnt_gt

    has_tie = jnp.max(jnp.where((cnt_ge > kf) & (thr > INT_MIN), one, zero)) > zero
    nbits = int(np.log2(TK)) + 3

    def tie_search(_):
        def idx_step(i, x):
            cand = x | lax.shift_left(np.int32(1), np.int32(nbits - 1) - as_i32(i))
            g = count(lambda ik, pos: (ik == thr) & (pos < cand))
            return jnp.where(g < need, cand, x)
        return lax.fori_loop(0, nbits, idx_step, jnp.zeros((1, TS), I32))

    jcut = lax.cond(has_tie, tie_search, lambda _: jnp.full((1, TS), 2**30, I32), 0)
    jcut = jnp.where(thr > INT_MIN, jcut, -1)

    def bias_tile(kt, carry):
        ik = ik_ref[kt]
        sel = (ik > thr) | ((ik == thr) & (kt * TK + key_s <= jcut))
        bias_t = jnp.where(sel, zero, ninf)
        for r in range(NR):
            sc_ref[kt, r * TQ:(r + 1) * TQ, :] = jnp.transpose(bias_t[:, r * TQ:(r + 1) * TQ])
        return carry

    lax.fori_loop(0, nkt, bias_tile, 0)

    for r in range(NR):
        rows = slice(r * TQ, (r + 1) * TQ)
        qcat = qcat_ref[:, rows, :].reshape(H * TQ, 2 * LANES)

        def logit_tile(kt, m_run, rows=rows, qcat=qcat):
            lg = _dot_nt(qcat, kv_ref[key_rows(kt), :]).reshape(H, TQ, TK) + sc_ref[kt, rows, :][None]
            lg = lg.reshape(H * TQ, TK)
            lg_ref[kt] = lg
            part = lg[:, 0:LANES]
            for g in range(1, NG):
                part = jnp.maximum(part, lg[:, g * LANES:(g + 1) * LANES])
            return jnp.maximum(m_run, part)

        m_run = lax.fori_loop(0, nkt, logit_tile, jnp.full((H * TQ, LANES), -jnp.inf, F32))
        m_row = jnp.max(m_run, axis=-1, keepdims=True)

        def pv_tile(kt, carry, m_row=m_row):
            l_run, acc = carry
            p = jnp.exp(lg_ref[kt] - m_row)
            part = p[:, 0:LANES]
            for g in range(1, NG):
                part = part + p[:, g * LANES:(g + 1) * LANES]
            acc = acc + _dot(p.astype(BF16), kv_ref[key_rows(kt), 0:LANES])
            return l_run + part, acc

        l_run, acc = lax.fori_loop(0, nkt, pv_tile,
                                   (jnp.zeros((H * TQ, LANES), F32), jnp.zeros((H * TQ, LANES), F32)))
        o_lat = (acc / jnp.sum(l_run, axis=-1, keepdims=True)).astype(BF16)
        for pr in range(H // 2):
            h0, h1 = 2 * pr, 2 * pr + 1
            o_ref[rows, pr * LANES:(pr + 1) * LANES] = (
                _dot(o_lat[h0 * TQ:(h0 + 1) * TQ], wuv_ref[h0]) + _dot(o_lat[h1 * TQ:(h1 + 1) * TQ], wuv_ref[h1])
            ).astype(o_ref.dtype)


def _dsa(qcat, qidx, widx, kv, kidx, wuv_pad, batch, seq, topk):
    n = kv.shape[0]
    ns = seq // DSA_TS
    nkt_max = seq // DSA_TK
    H = DSA_HEADS
    return pl.pallas_call(
        functools.partial(_dsa_kernel, topk=topk),
        grid=(batch, ns),
        in_specs=[pl.BlockSpec((H, DSA_TS, 2 * LANES), lambda b, s: (0, b * ns + s, 0)),
                  pl.BlockSpec((H, DSA_TS, LANES), lambda b, s: (0, b * ns + s, 0)),
                  pl.BlockSpec((IDX_HEADS, DSA_TS), lambda b, s: (0, b * ns + s)),
                  pl.BlockSpec((seq, 2 * LANES), lambda b, s: (b, 0)),
                  pl.BlockSpec((seq, LANES), lambda b, s: (b, 0)),
                  pl.BlockSpec(wuv_pad.shape, lambda b, s: (0, 0, 0))],
        out_specs=pl.BlockSpec((DSA_TS, H * DSA_VDIM), lambda b, s: (b * ns + s, 0)),
        out_shape=jax.ShapeDtypeStruct((n, H * DSA_VDIM), BF16),
        scratch_shapes=[pltpu.VMEM((nkt_max, DSA_TS, DSA_TK), F32),
                        pltpu.VMEM((nkt_max, DSA_TK, DSA_TS), I32),
                        pltpu.VMEM((nkt_max, DSA_TK, DSA_TS), I16),
                        pltpu.VMEM((nkt_max, DSA_TK, DSA_TS), I16),
                        pltpu.VMEM((nkt_max, H * DSA_TQ, DSA_TK), F32)],
        compiler_params=_cparams(("parallel", "arbitrary")),
        name="dsa_attn",
    )(qcat, qidx, widx, kv, kidx, wuv_pad)


def _mem_attn_kernel(q_ref, k_ref, v_ref, o_ref):
    scale = float(MEM_DH ** -0.5)
    for h in range(MEM_HEADS):
        sl = slice(h * MEM_DH, (h + 1) * MEM_DH)
        lg = _dot_nt((q_ref[:, sl] * scale).astype(BF16), k_ref[:, sl])
        p = jnp.exp(lg - jnp.max(lg, axis=-1, keepdims=True))
        o = _dot(p.astype(BF16), v_ref[:, sl]) / jnp.sum(p, axis=-1, keepdims=True)
        o_ref[:, sl] = o.astype(o_ref.dtype)


def _mem_attn(proj_m, k_m, v_m, seq, tm):
    n = proj_m.shape[0]
    nt = seq // tm
    m = k_m.shape[0] // (n // seq)
    return pl.pallas_call(
        _mem_attn_kernel,
        grid=(n // tm,),
        in_specs=[pl.BlockSpec((tm, MEM_WIDTH), lambda i: (i, 0)),
                  pl.BlockSpec((m, MEM_WIDTH), lambda i: (i // nt, 0)),
                  pl.BlockSpec((m, MEM_WIDTH), lambda i: (i // nt, 0))],
        out_specs=pl.BlockSpec((tm, MEM_WIDTH), lambda i: (i, 0)),
        out_shape=jax.ShapeDtypeStruct((n, MEM_WIDTH), BF16),
        compiler_params=_cparams(("parallel",)),
        name="mem_attn",
    )(proj_m, k_m, v_m)


def _merge_kernel(x_ref, nw_ref, oa_ref, ob_ref, oc_ref, wg_ref, wbr_ref, wo_ref, h_ref):
    x = x_ref[...]
    d = x.shape[-1]
    xn = _rms(x, nw_ref[...]).astype(BF16)
    merged = None
    for br, o_br in enumerate((oa_ref, ob_ref, oc_ref)):
        gate = _sigmoid(_dot(xn, wg_ref[:, br * d:(br + 1) * d]))
        term = gate * _dot(o_br[...], wbr_ref[br])
        merged = term if merged is None else merged + term
    h_ref[...] = x + _dot(merged.astype(BF16), wo_ref[...])


def _merge(x, nw, o_a, o_b, o_c, w_gate, w_br, w_out, tm):
    n, d = x.shape
    row = lambda wd: pl.BlockSpec((tm, wd), lambda i: (i, 0))
    full = lambda a: pl.BlockSpec(a.shape, lambda i: (0,) * a.ndim)
    return pl.pallas_call(
        _merge_kernel,
        grid=(n // tm,),
        in_specs=[row(d), full(nw), row(o_a.shape[1]), row(o_b.shape[1]), row(o_c.shape[1]),
                  full(w_gate), full(w_br), full(w_out)],
        out_specs=row(d),
        out_shape=jax.ShapeDtypeStruct((n, d), F32),
        compiler_params=_cparams(("parallel",)),
        name="merge_out",
    )(x, nw, o_a, o_b, o_c, w_gate, w_br, w_out)


def _mlp_kernel(h_ref, nw_ref, wu_ref, wd_ref, fw_ref, o_ref, *, ff_chunk):
    h = h_ref[...]
    m = _rms(h, nw_ref[...]).astype(BF16)
    acc = h
    for c0 in range(0, wu_ref.shape[1], ff_chunk):
        u = jnp.maximum(_dot(m, wu_ref[:, c0:c0 + ff_chunk]), 0.0)
        acc = acc + _dot((u * u).astype(BF16), wd_ref[c0:c0 + ff_chunk, :])
    o_ref[...] = _rms(acc, fw_ref[...])


def _mlp(h, nw, w_up, w_down, fw, tm, ff_chunk=1024):
    n, d = h.shape
    full = lambda a: pl.BlockSpec(a.shape, lambda i: (0,) * a.ndim)
    return pl.pallas_call(
        functools.partial(_mlp_kernel, ff_chunk=ff_chunk),
        grid=(n // tm,),
        in_specs=[pl.BlockSpec((tm, d), lambda i: (i, 0)), full(nw), full(w_up), full(w_down), full(fw)],
        out_specs=pl.BlockSpec((tm, d), lambda i: (i, 0)),
        out_shape=jax.ShapeDtypeStruct((n, d), F32),
        compiler_params=_cparams(("parallel",)),
        name="mlp_final",
    )(h, nw, w_up, w_down, fw)


def _pad_cols(w, width):
    return jnp.pad(w, ((0, 0), (0, width - w.shape[1])))


def _head_pad(w, width):
    return jnp.pad(w, ((0, 0), (0, 0), (0, width - w.shape[2]))).reshape(w.shape[0], -1)


def _rope_tables(seq, d):
    inv = ROPE_THETA ** (-jnp.arange(0, d, 2, dtype=F32) / d)
    ang = jnp.arange(seq, dtype=F32)[:, None] * inv[None, :]
    cos, sin = jnp.cos(ang), jnp.sin(ang)
    cosf = _pad_cols(jnp.concatenate([cos, cos], axis=1), LANES)
    sins = _pad_cols(jnp.concatenate([-sin, sin], axis=1), LANES)
    return cosf, sins


def kernel(x, mem, attn_norm_w, w_in, hgrn_lower_bounds, hgrn_out_norm_w, dsa_q_norm_w, dsa_kv_norm_w,
           dsa_kidx_norm_w, dsa_w_uq, dsa_w_qr, dsa_w_uk, dsa_w_uv, dsa_w_qidx, mem_norm_w, w_mem_kv,
           w_br_hgrn, w_br_dsa, w_br_mem, w_out, mlp_norm_w, w_mlp_up, w_mlp_down, final_norm_w):
    B, T, D = x.shape
    N = B * T
    M = mem.shape[1]
    l = 0
    topk = min(TOPK_MAX, T // 4)
    tm = 512
    x2 = x.reshape(N, D)

    wi = w_in[l]
    o = 0
    cols = {}
    for name, wd in (("hq", HG_WIDTH), ("hf", HG_WIDTH), ("hv", HG_WIDTH), ("hg", HG_WIDTH),
                     ("cq", DSA_Q_LORA), ("ckv", DSA_KV_LORA), ("kr", DSA_ROPE), ("kidx", IDX_DIM),
                     ("widx", IDX_HEADS), ("mq", MEM_WIDTH), ("gate", N_BRANCH * D)):
        cols[name] = wi[:, o:o + wd]
        o += wd
    w_proj = jnp.concatenate(
        [cols["hq"], cols["hf"], cols["hv"], cols["hg"], cols["mq"], cols["cq"], cols["ckv"],
         _pad_cols(cols["kr"], LANES), _pad_cols(cols["kidx"], LANES), _pad_cols(cols["widx"], LANES)],
        axis=1).astype(BF16)
    w_gate = cols["gate"].astype(BF16)

    proj_h, proj_m, proj_d = _norm_proj(x2, attn_norm_w[l], w_proj, (4 * HG_WIDTH, MEM_WIDTH, DSA_GROUP),
                                        (F32, F32, F32), tm)

    lb_all = jnp.cumsum(jax.nn.softmax(hgrn_lower_bounds.astype(F32), axis=0), axis=0)
    o_a = _hgrn(proj_h, lb_all[l], hgrn_out_norm_w[l], B, T)

    wuq_pad = _head_pad(dsa_w_uq[l], LANES).astype(BF16)
    wqr_pad = _head_pad(dsa_w_qr[l], LANES).astype(BF16)
    wqi_pad = _head_pad(dsa_w_qidx[l], LANES).astype(BF16)
    wuk_pad = jnp.pad(dsa_w_uk[l], ((0, 0), (0, LANES - DSA_NOPE), (0, 0))).astype(BF16)
    wuv = dsa_w_uv[l]
    wuv_pad = jnp.stack([jnp.pad(wuv[h], ((0, 0), ((h % 2) * DSA_VDIM, LANES - DSA_VDIM - (h % 2) * DSA_VDIM)))
                         for h in range(DSA_HEADS)]).astype(BF16)
    kinw_pad = _pad_cols(dsa_kidx_norm_w[l].reshape(1, IDX_DIM), LANES)
    tabs = _rope_tables(T, DSA_ROPE) + _rope_tables(T, IDX_DIM)
    qcat, qidx, kv, kidx, widx = _dsa_prep(
        proj_d, dsa_q_norm_w[l].reshape(1, -1), dsa_kv_norm_w[l].reshape(1, -1), kinw_pad,
        wuq_pad, wqr_pad, wuk_pad, wqi_pad, tabs, T, tm)
    o_b = _dsa(qcat, qidx, widx, kv, kidx, wuv_pad, B, T, topk)

    k_m, v_m = _norm_proj(mem.reshape(B * M, D), mem_norm_w[l], w_mem_kv[l].astype(BF16),
                          (MEM_WIDTH, MEM_WIDTH), (BF16, BF16), min(tm, B * M))
    o_c = _mem_attn(proj_m, k_m, v_m, T, tm)

    w_br = jnp.stack([w_br_hgrn[l], w_br_dsa[l], w_br_mem[l]]).astype(BF16)
    h = _merge(x2, attn_norm_w[l].reshape(1, D), o_a, o_b, o_c, w_gate, w_br, w_out[l].astype(BF16), tm)
    out = _mlp(h, mlp_norm_w[l].reshape(1, D), w_mlp_up[l].astype(BF16), w_mlp_down[l].astype(BF16),
               final_norm_w.reshape(1, D), tm)
    return out.reshape(B, T, D)
```

```python
import functools

import jax
import jax.numpy as jnp
import numpy as np
from jax import lax
from jax.experimental import pallas as pl
from jax.experimental.pallas import tpu as pltpu

F32 = jnp.float32
BF16 = jnp.bfloat16
I32 = jnp.int32
I16 = jnp.int16

EPS = 1e-6
ROPE_THETA = 10000.0
LANES = 128
SUBLANES = 8
HG_HEADS, HG_D, HG_CHUNK = 4, 128, 64
HG_WIDTH = HG_HEADS * HG_D
HG_SUB = 16
HG_STEP = 256
DSA_HEADS, DSA_Q_LORA, DSA_KV_LORA = 8, 256, 128
DSA_NOPE, DSA_ROPE, DSA_VDIM = 64, 32, 64
IDX_HEADS, IDX_DIM = 8, 64
TOPK_MAX = 256
MEM_HEADS, MEM_DH = 4, 128
MEM_WIDTH = MEM_HEADS * MEM_DH
N_BRANCH = 3
DSA_GROUP = 6 * LANES
VMEM_LIMIT = 56 * 1024 * 1024
INT_MIN = np.int32(-2**31)

DSA_TQ = 128
DSA_TS = 512
DSA_TK = 512


def _cparams(sem):
    return pltpu.CompilerParams(dimension_semantics=sem, vmem_limit_bytes=VMEM_LIMIT)


def _rms(x, w, n=None):
    n = x.shape[-1] if n is None else n
    ms = jnp.sum(x * x, axis=-1, keepdims=True) * (1.0 / n)
    return x * lax.rsqrt(ms + EPS) * w


def _sigmoid(x):
    return 1.0 / (1.0 + jnp.exp(-x))


def _dot(a, b):
    return jnp.dot(a, b, preferred_element_type=F32)


def _dot_nt(a, b):
    return lax.dot_general(a, b, (((1,), (1,)), ((), ())), preferred_element_type=F32)


def _dot_tn(a, b):
    return lax.dot_general(a, b, (((0,), (0,)), ((), ())), preferred_element_type=F32)


def _norm_proj_kernel(x_ref, nw_ref, w_ref, *out_refs):
    xn = _rms(x_ref[...], nw_ref[...]).astype(BF16)
    off = 0
    for o_ref in out_refs:
        wd = o_ref.shape[-1]
        o_ref[...] = _dot(xn, w_ref[:, off:off + wd]).astype(o_ref.dtype)
        off += wd


def _norm_proj(x, nw, w, widths, dtypes, tm):
    n, d = x.shape
    return pl.pallas_call(
        _norm_proj_kernel,
        grid=(n // tm,),
        in_specs=[pl.BlockSpec((tm, d), lambda i: (i, 0)),
                  pl.BlockSpec((1, d), lambda i: (0, 0)),
                  pl.BlockSpec(w.shape, lambda i: (0, 0))],
        out_specs=[pl.BlockSpec((tm, wd), lambda i: (i, 0)) for wd in widths],
        out_shape=[jax.ShapeDtypeStruct((n, wd), dt) for wd, dt in zip(widths, dtypes)],
        compiler_params=_cparams(("parallel",)),
        name="norm_proj",
    )(x, nw.reshape(1, d), w)


def _hgrn_kernel(q_ref, f_ref, v_ref, g_ref, lb_ref, onw_ref, o_ref, st_ref, b_ref, k_ref):
    @pl.when(pl.program_id(1) == 0)
    def _():
        st_ref[...] = jnp.zeros_like(st_ref)

    C, SB = HG_CHUNK, HG_SUB
    lb = lb_ref[...]
    ri = lax.broadcasted_iota(I32, (C, C), 0)
    ci = lax.broadcasted_iota(I32, (C, C), 1)
    tri = jnp.where(ri >= ci, 1.0, 0.0).astype(BF16)
    row_c = lax.broadcasted_iota(I32, (C, HG_D), 0)
    row_2 = lax.broadcasted_iota(I32, (SB, HG_D), 0)
    row_1 = lax.broadcasted_iota(I32, (SUBLANES, HG_D), 0)
    lane_2 = lax.broadcasted_iota(I32, (SB, C), 1)
    lane_1 = lax.broadcasted_iota(I32, (SUBLANES, C), 1)

    def chunk(ci_, carry):
        rows = pl.ds(pl.multiple_of(ci_ * C, C), C)
        f = lb + (1.0 - lb) * _sigmoid(f_ref[rows, :])
        lf = jnp.log(f)
        hi = lf.astype(BF16)
        r1 = lf - hi.astype(F32)
        mid = r1.astype(BF16)
        lo = (r1 - mid.astype(F32)).astype(BF16)
        b_ref[...] = _dot(tri, hi) + _dot(tri, mid) + _dot(tri, lo)
        k_ref[...] = 1.0 - f

        for h in range(HG_HEADS):
            sl = slice(h * HG_D, (h + 1) * HG_D)
            qh = q_ref[rows, sl]
            vh = v_ref[rows, sl]
            bh = b_ref[:, sl]
            kh = k_ref[:, sl]
            a_rows = [jnp.zeros((SB, C), F32)]
            for i in range(1, C // SB):
                bref = b_ref[i * SB:i * SB + 1, sl]
                qi = (qh[i * SB:(i + 1) * SB] * jnp.exp(bh[i * SB:(i + 1) * SB] - bref)).astype(BF16)
                ki = (kh * jnp.exp(jnp.where(row_c < i * SB, bref - bh, -jnp.inf))).astype(BF16)
                a_rows.append(_dot_nt(qi, ki))
            for i in range(C // SB):
                b2, q2 = bh[i * SB:(i + 1) * SB], qh[i * SB:(i + 1) * SB]
                b1, q1 = b2[SUBLANES:], q2[SUBLANES:]
                a2 = a_rows[i]
                for j in range(SB):
                    s = i * SB + j
                    brow = b_ref[s:s + 1, sl]
                    krow = k_ref[s:s + 1, sl]
                    if j < SUBLANES:
                        e = jnp.exp(jnp.where(row_2 >= j, b2 - brow, -jnp.inf))
                        r = jnp.sum(e * (q2 * krow), axis=-1, keepdims=True)
                        a2 = jnp.where(lane_2 == s, r, a2)
                    else:
                        e = jnp.exp(jnp.where(row_1 >= j - SUBLANES, b1 - brow, -jnp.inf))
                        r = jnp.sum(e * (q1 * krow), axis=-1, keepdims=True)
                        a2 = jnp.concatenate([a2[:SUBLANES], jnp.where(lane_1 == s, r, a2[SUBLANES:])], axis=0)
                a_rows[i] = a2
            a_mat = jnp.concatenate(a_rows, axis=0)
            st = st_ref[h]
            qe = (qh * jnp.exp(bh)).astype(BF16)
            vb = vh.astype(BF16)
            o = _dot_nt(qe, st.astype(BF16)) + _dot(a_mat.astype(BF16), vb)
            blast = bh[C - 1:C, :]
            kd = (kh * jnp.exp(blast - bh)).astype(BF16)
            st_ref[h] = st * jnp.exp(blast) + _dot_tn(vb, kd)
            gh = g_ref[rows, sl]
            o_ref[rows, sl] = (_rms(o, onw_ref[:, sl]) * (gh * _sigmoid(gh))).astype(o_ref.dtype)
        return carry

    lax.fori_loop(0, HG_STEP // C, chunk, 0)


def _hgrn(proj_h, lb, onw, batch, seq):
    n = proj_h.shape[0]
    nc = seq // HG_STEP
    blk = lambda col: pl.BlockSpec((HG_STEP, HG_WIDTH), lambda b, c, col=col: (b * nc + c, col))
    vec = pl.BlockSpec((1, HG_WIDTH), lambda b, c: (0, 0))
    return pl.pallas_call(
        _hgrn_kernel,
        grid=(batch, nc),
        in_specs=[blk(0), blk(1), blk(2), blk(3), vec, vec],
        out_specs=pl.BlockSpec((HG_STEP, HG_WIDTH), lambda b, c: (b * nc + c, 0)),
        out_shape=jax.ShapeDtypeStruct((n, HG_WIDTH), BF16),
        scratch_shapes=[pltpu.VMEM((HG_HEADS, HG_D, HG_D), F32),
                        pltpu.VMEM((HG_CHUNK, HG_WIDTH), F32),
                        pltpu.VMEM((HG_CHUNK, HG_WIDTH), F32)],
        compiler_params=_cparams(("parallel", "arbitrary")),
        name="hgrn2",
    )(proj_h, proj_h, proj_h, proj_h, lb.reshape(1, HG_WIDTH), onw.reshape(1, HG_WIDTH))


def _rope_lanes(x, cosf, sins, half):
    nl = x.shape[-1]
    lane = lax.broadcasted_iota(I32, x.shape, 1)
    partner = jnp.where(lane % LANES < half, pltpu.roll(x, nl - half, 1), pltpu.roll(x, half, 1))
    return x * cosf + partner * sins


def _dsa_prep_kernel(pd_ref, qnw_ref, kvnw_ref, kinw_ref, wuq_ref, wqr_ref, wuk_ref, wqi_ref,
                     c32_ref, s32_ref, c64_ref, s64_ref,
                     qcat_ref, qidx_ref, kv_ref, kidx_ref, widx_ref):
    scale = float((DSA_NOPE + DSA_ROPE) ** -0.5)
    c32, s32, c64, s64 = c32_ref[...], s32_ref[...], c64_ref[...], s64_ref[...]
    cqn = _rms(pd_ref[:, 0:DSA_Q_LORA], qnw_ref[...]).astype(BF16)
    qn = _dot(cqn, wuq_ref[...]).astype(BF16)
    qr = _dot(cqn, wqr_ref[...])
    qi = _dot(cqn, wqi_ref[...])
    for h in range(DSA_HEADS):
        sl = slice(h * LANES, (h + 1) * LANES)
        qcat_ref[h, :, 0:LANES] = (_dot(qn[:, sl], wuk_ref[h]) * scale).astype(BF16)
        qcat_ref[h, :, LANES:2 * LANES] = (_rope_lanes(qr[:, sl], c32, s32, DSA_ROPE // 2) * scale).astype(BF16)
        qidx_ref[h] = (_rope_lanes(qi[:, sl], c64, s64, IDX_DIM // 2) * float(IDX_DIM ** -0.5)).astype(BF16)
    kv_ref[:, 0:LANES] = _rms(pd_ref[:, 256:384], kvnw_ref[...]).astype(BF16)
    kv_ref[:, LANES:2 * LANES] = _rope_lanes(pd_ref[:, 384:512], c32, s32, DSA_ROPE // 2).astype(BF16)
    kin = _rms(pd_ref[:, 512:640], kinw_ref[...], n=IDX_DIM)
    kidx_ref[...] = _rope_lanes(kin, c64, s64, IDX_DIM // 2).astype(BF16)
    widx_ref[...] = jnp.transpose(pd_ref[:, 640:768] * float(IDX_HEADS ** -0.5))[0:IDX_HEADS, :]


def _dsa_prep(proj_d, qnw, kvnw, kinw_pad, wuq_pad, wqr_pad, wuk_pad, wqi_pad, tabs, seq, tm):
    n = proj_d.shape[0]
    nt = seq // tm
    full = lambda a: pl.BlockSpec(a.shape, lambda i: (0,) * a.ndim)
    tab = pl.BlockSpec((tm, LANES), lambda i: (i % nt, 0))
    hd = DSA_HEADS
    return pl.pallas_call(
        _dsa_prep_kernel,
        grid=(n // tm,),
        in_specs=[pl.BlockSpec((tm, DSA_GROUP), lambda i: (i, 0)),
                  full(qnw), full(kvnw), full(kinw_pad), full(wuq_pad), full(wqr_pad), full(wuk_pad),
                  full(wqi_pad), tab, tab, tab, tab],
        out_specs=[pl.BlockSpec((hd, tm, 2 * LANES), lambda i: (0, i, 0)),
                   pl.BlockSpec((hd, tm, LANES), lambda i: (0, i, 0)),
                   pl.BlockSpec((tm, 2 * LANES), lambda i: (i, 0)),
                   pl.BlockSpec((tm, LANES), lambda i: (i, 0)),
                   pl.BlockSpec((IDX_HEADS, tm), lambda i: (0, i))],
        out_shape=[jax.ShapeDtypeStruct((hd, n, 2 * LANES), BF16),
                   jax.ShapeDtypeStruct((hd, n, LANES), BF16),
                   jax.ShapeDtypeStruct((n, 2 * LANES), BF16),
                   jax.ShapeDtypeStruct((n, LANES), BF16),
                   jax.ShapeDtypeStruct((IDX_HEADS, n), F32)],
        compiler_params=_cparams(("parallel",)),
        name="dsa_prep",
    )(proj_d, qnw, kvnw, kinw_pad, wuq_pad, wqr_pad, wuk_pad, wqi_pad, *tabs)


def _dsa_kernel(qcat_ref, qidx_ref, widx_ref, kv_ref, kidx_ref, wuv_ref, o_ref,
                sc_ref, ik_ref, hi_ref, lo_ref, lg_ref, m_ref, mb_ref, acc_ref, *, topk):
    TQ, TK, H, TS = DSA_TQ, DSA_TK, DSA_HEADS, DSA_TS
    NG = TK // LANES
    NR = TS // TQ
    step = pl.program_id(1)
    nkt = (step * TS + TS + TK - 1) // TK
    key_q = lax.broadcasted_iota(I32, (TK, TQ), 0)
    qry_q = lax.broadcasted_iota(I32, (TK, TQ), 1)
    key_s = lax.broadcasted_iota(I32, (TK, TS), 0)

    def key_rows(kt):
        return pl.ds(pl.multiple_of(kt * TK, TK), TK)

    def as_i32(i):
        return jnp.asarray(i, dtype=I32)

    for r in range(NR):
        cols = slice(r * TQ, (r + 1) * TQ)
        qidx = qidx_ref[:, cols, :].reshape(H * TQ, LANES)
        wrows = [widx_ref[h:h + 1, cols] for h in range(IDX_HEADS)]
        tpos = step * TS + r * TQ + qry_q

        def score_tile(kt, carry, cols=cols, qidx=qidx, wrows=wrows, tpos=tpos):
            s_all = jnp.maximum(_dot_nt(kidx_ref[key_rows(kt), :], qidx), 0.0)
            score = s_all[:, 0:TQ] * wrows[0]
            for h in range(1, IDX_HEADS):
                score = score + s_all[:, h * TQ:(h + 1) * TQ] * wrows[h]
            bits = lax.bitcast_convert_type(score + 0.0, I32)
            key = jnp.where(bits < 0, bits ^ np.int32(0x7FFFFFFF), bits)
            key = jnp.where(kt * TK + key_q <= tpos, key, INT_MIN)
            ik_ref[kt, :, cols] = key
            hi_ref[kt, :, cols] = lax.shift_right_arithmetic(key, np.int32(16)).astype(I16)
            return carry

        lax.fori_loop(0, nkt, score_tile, 0)

    one, zero, ninf = jnp.float32(1.0), jnp.float32(0.0), jnp.float32(-jnp.inf)

    def count(pred_fn):
        def body(kt, acc):
            m = jnp.where(pred_fn(ik_ref[kt], kt * TK + key_s), one, zero)
            return acc + jnp.sum(m.reshape(TK // SUBLANES, SUBLANES, TS), axis=0)
        acc = lax.fori_loop(0, nkt, body, jnp.zeros((SUBLANES, TS), F32))
        return jnp.sum(acc, axis=0, keepdims=True)

    PK = 2 * SUBLANES

    def count16(x_ref, t16, strict=False):
        one16, zero16 = jnp.ones((), I16), jnp.zeros((), I16)

        def body(kt, acc):
            x = x_ref[kt]
            m = jnp.where((x > t16) if strict else (x >= t16), one16, zero16)
            for g in range(TK // PK):
                acc = acc + m[g * PK:(g + 1) * PK, :]
            return acc
        acc = lax.fori_loop(0, nkt, body, jnp.zeros((PK, TS), I16))
        return jnp.sum(acc.astype(I32), axis=0, keepdims=True).astype(F32)

    def search16(x_ref, kneed):
        def bit_step(i, tu):
            cand = tu | lax.shift_left(np.int32(1), np.int32(15) - as_i32(i))
            cnt = count16(x_ref, (cand - np.int32(32768)).astype(I16))
            return jnp.where(cnt >= kneed, cand, tu)
        return lax.fori_loop(0, 16, bit_step, jnp.zeros((1, TS), I32))

    kf = float(topk)
    p_hi = search16(hi_ref, kf) - np.int32(32768)
    p_hi16 = p_hi.astype(I16)
    k_lo = kf - count16(hi_ref, p_hi16, strict=True)

    def low_tile(kt, carry):
        lo = ((ik_ref[kt] & np.int32(0xFFFF)) - np.int32(32768)).astype(I16)
        lo_ref[kt] = jnp.where(hi_ref[kt] == p_hi16, lo, jnp.full((), -32768, I16))
        return carry

    lax.fori_loop(0, nkt, low_tile, 0)
    thr = p_hi * np.int32(65536) + search16(lo_ref, k_lo)
    cnt_gt = count(lambda ik, pos: ik > thr)
    cnt_ge = count(lambda ik, pos: ik >= thr)
    need = kf - cnt_gt

    has_tie = jnp.max(jnp.where((cnt_ge > kf) & (thr > INT_MIN), one, zero)) > zero
    nbits = int(np.log2(TK)) + 3

    def tie_search(_):
        def idx_step(i, x):
            cand = x | lax.shift_left(np.int32(1), np.int32(nbits - 1) - as_i32(i))
            g = count(lambda ik, pos: (ik == thr) & (pos < cand))
            return jnp.where(g < need, cand, x)
        return lax.fori_loop(0, nbits, idx_step, jnp.zeros((1, TS), I32))

    jcut = lax.cond(has_tie, tie_search, lambda _: jnp.full((1, TS), 2**30, I32), 0)
    jcut = jnp.where(thr > INT_MIN, jcut, -1)

    def bias_tile(kt, carry):
        ik = ik_ref[kt]
        sel = (ik > thr) | ((ik == thr) & (kt * TK + key_s <= jcut))
        bias_t = jnp.where(sel, zero, ninf)
        for r in range(NR):
            sc_ref[kt, r * TQ:(r + 1) * TQ, :] = jnp.transpose(bias_t[:, r * TQ:(r + 1) * TQ])
        return carry

    lax.fori_loop(0, nkt, bias_tile, 0)

    ones_v = jnp.ones((TK, LANES), BF16)

    def pass_a(r, kt):
        rows = slice(r * TQ, (r + 1) * TQ)
        qcat = qcat_ref[:, rows, :].reshape(H * TQ, 2 * LANES)
        lg = _dot_nt(qcat, kv_ref[key_rows(kt), :]).reshape(H, TQ, TK) + sc_ref[kt, rows, :][None]
        lg = lg.reshape(H * TQ, TK)
        lg_ref[r % 2, kt] = lg
        part = lg[:, 0:LANES]
        for g in range(1, NG):
            part = jnp.maximum(part, lg[:, g * LANES:(g + 1) * LANES])
        m_ref[r % 2] = jnp.maximum(m_ref[r % 2], part)

    def pass_b(r, kt):
        lg = lg_ref[r % 2, kt]
        m_b = mb_ref[...]
        p = jnp.concatenate(
            [jnp.exp((lg[:, g * LANES:(g + 1) * LANES] - m_b).astype(BF16)) for g in range(NG)], axis=1)
        kv1 = jnp.concatenate([kv_ref[key_rows(kt), 0:LANES], ones_v], axis=1)
        acc_ref[...] += _dot(p, kv1)

    def loop(fn):
        def body(kt, carry):
            fn(kt)
            return carry
        lax.fori_loop(0, nkt, body, 0)

    m_ref[0] = jnp.full((H * TQ, LANES), -jnp.inf, F32)
    loop(lambda kt: pass_a(0, kt))
    for r in range(NR):
        rows = slice(r * TQ, (r + 1) * TQ)
        mb_ref[...] = jnp.broadcast_to(jnp.max(m_ref[r % 2], axis=-1, keepdims=True), (H * TQ, LANES))
        acc_ref[...] = jnp.zeros_like(acc_ref)
        if r + 1 < NR:
            m_ref[(r + 1) % 2] = jnp.full((H * TQ, LANES), -jnp.inf, F32)
            loop(lambda kt, r=r: (pass_a(r + 1, kt), pass_b(r, kt)))
        else:
            loop(lambda kt, r=r: pass_b(r, kt))
        o_lat = (acc_ref[:, 0:LANES] / acc_ref[:, LANES:2 * LANES]).astype(BF16)
        for pr in range(H // 2):
            h0, h1 = 2 * pr, 2 * pr + 1
            o_ref[rows, pr * LANES:(pr + 1) * LANES] = (
                _dot(o_lat[h0 * TQ:(h0 + 1) * TQ], wuv_ref[h0]) + _dot(o_lat[h1 * TQ:(h1 + 1) * TQ], wuv_ref[h1])
            ).astype(o_ref.dtype)


def _dsa(qcat, qidx, widx, kv, kidx, wuv_pad, batch, seq, topk):
    n = kv.shape[0]
    ns = seq // DSA_TS
    nkt_max = seq // DSA_TK
    H = DSA_HEADS
    return pl.pallas_call(
        functools.partial(_dsa_kernel, topk=topk),
        grid=(batch, ns),
        in_specs=[pl.BlockSpec((H, DSA_TS, 2 * LANES), lambda b, s: (0, b * ns + s, 0)),
                  pl.BlockSpec((H, DSA_TS, LANES), lambda b, s: (0, b * ns + s, 0)),
                  pl.BlockSpec((IDX_HEADS, DSA_TS), lambda b, s: (0, b * ns + s)),
                  pl.BlockSpec((seq, 2 * LANES), lambda b, s: (b, 0)),
                  pl.BlockSpec((seq, LANES), lambda b, s: (b, 0)),
                  pl.BlockSpec(wuv_pad.shape, lambda b, s: (0, 0, 0))],
        out_specs=pl.BlockSpec((DSA_TS, H * DSA_VDIM), lambda b, s: (b * ns + s, 0)),
        out_shape=jax.ShapeDtypeStruct((n, H * DSA_VDIM), BF16),
        scratch_shapes=[pltpu.VMEM((nkt_max, DSA_TS, DSA_TK), F32),
                        pltpu.VMEM((nkt_max, DSA_TK, DSA_TS), I32),
                        pltpu.VMEM((nkt_max, DSA_TK, DSA_TS), I16),
                        pltpu.VMEM((nkt_max, DSA_TK, DSA_TS), I16),
                        pltpu.VMEM((2, nkt_max, H * DSA_TQ, DSA_TK), F32),
                        pltpu.VMEM((2, H * DSA_TQ, LANES), F32),
                        pltpu.VMEM((H * DSA_TQ, LANES), F32),
                        pltpu.VMEM((H * DSA_TQ, 2 * LANES), F32)],
        compiler_params=_cparams(("parallel", "arbitrary")),
        name="dsa_attn",
    )(qcat, qidx, widx, kv, kidx, wuv_pad)


def _mem_attn_kernel(q_ref, k_ref, v_ref, o_ref):
    scale = float(MEM_DH ** -0.5)
    for h in range(MEM_HEADS):
        sl = slice(h * MEM_DH, (h + 1) * MEM_DH)
        lg = _dot_nt((q_ref[:, sl] * scale).astype(BF16), k_ref[:, sl])
        p = jnp.exp(lg - jnp.max(lg, axis=-1, keepdims=True))
        o = _dot(p.astype(BF16), v_ref[:, sl]) / jnp.sum(p, axis=-1, keepdims=True)
        o_ref[:, sl] = o.astype(o_ref.dtype)


def _mem_attn(proj_m, k_m, v_m, seq, tm):
    n = proj_m.shape[0]
    nt = seq // tm
    m = k_m.shape[0] // (n // seq)
    return pl.pallas_call(
        _mem_attn_kernel,
        grid=(n // tm,),
        in_specs=[pl.BlockSpec((tm, MEM_WIDTH), lambda i: (i, 0)),
                  pl.BlockSpec((m, MEM_WIDTH), lambda i: (i // nt, 0)),
                  pl.BlockSpec((m, MEM_WIDTH), lambda i: (i // nt, 0))],
        out_specs=pl.BlockSpec((tm, MEM_WIDTH), lambda i: (i, 0)),
        out_shape=jax.ShapeDtypeStruct((n, MEM_WIDTH), BF16),
        compiler_params=_cparams(("parallel",)),
        name="mem_attn",
    )(proj_m, k_m, v_m)


def _merge_kernel(x_ref, nw_ref, oa_ref, ob_ref, oc_ref, wg_ref, wbr_ref, wo_ref, h_ref):
    x = x_ref[...]
    d = x.shape[-1]
    xn = _rms(x, nw_ref[...]).astype(BF16)
    merged = None
    for br, o_br in enumerate((oa_ref, ob_ref, oc_ref)):
        gate = _sigmoid(_dot(xn, wg_ref[:, br * d:(br + 1) * d]))
        term = gate * _dot(o_br[...], wbr_ref[br])
        merged = term if merged is None else merged + term
    h_ref[...] = x + _dot(merged.astype(BF16), wo_ref[...])


def _merge(x, nw, o_a, o_b, o_c, w_gate, w_br, w_out, tm):
    n, d = x.shape
    row = lambda wd: pl.BlockSpec((tm, wd), lambda i: (i, 0))
    full = lambda a: pl.BlockSpec(a.shape, lambda i: (0,) * a.ndim)
    return pl.pallas_call(
        _merge_kernel,
        grid=(n // tm,),
        in_specs=[row(d), full(nw), row(o_a.shape[1]), row(o_b.shape[1]), row(o_c.shape[1]),
                  full(w_gate), full(w_br), full(w_out)],
        out_specs=row(d),
        out_shape=jax.ShapeDtypeStruct((n, d), F32),
        compiler_params=_cparams(("parallel",)),
        name="merge_out",
    )(x, nw, o_a, o_b, o_c, w_gate, w_br, w_out)


def _mlp_kernel(h_ref, nw_ref, wu_ref, wd_ref, fw_ref, o_ref, *, ff_chunk):
    h = h_ref[...]
    m = _rms(h, nw_ref[...]).astype(BF16)
    acc = h
    for c0 in range(0, wu_ref.shape[1], ff_chunk):
        u = jnp.maximum(_dot(m, wu_ref[:, c0:c0 + ff_chunk]), 0.0)
        acc = acc + _dot((u * u).astype(BF16), wd_ref[c0:c0 + ff_chunk, :])
    o_ref[...] = _rms(acc, fw_ref[...])


def _mlp(h, nw, w_up, w_down, fw, tm, ff_chunk=1024):
    n, d = h.shape
    full = lambda a: pl.BlockSpec(a.shape, lambda i: (0,) * a.ndim)
    return pl.pallas_call(
        functools.partial(_mlp_kernel, ff_chunk=ff_chunk),
        grid=(n // tm,),
        in_specs=[pl.BlockSpec((tm, d), lambda i: (i, 0)), full(nw), full(w_up), full(w_down), full(fw)],
        out_specs=pl.BlockSpec((tm, d), lambda i: (i, 0)),
        out_shape=jax.ShapeDtypeStruct((n, d), F32),
        compiler_params=_cparams(("parallel",)),
        name="mlp_final",
    )(h, nw, w_up, w_down, fw)


def _pad_cols(w, width):
    return jnp.pad(w, ((0, 0), (0, width - w.shape[1])))


def _head_pad(w, width):
    return jnp.pad(w, ((0, 0), (0, 0), (0, width - w.shape[2]))).reshape(w.shape[0], -1)


def _rope_tables(seq, d):
    inv = ROPE_THETA ** (-jnp.arange(0, d, 2, dtype=F32) / d)
    ang = jnp.arange(seq, dtype=F32)[:, None] * inv[None, :]
    cos, sin = jnp.cos(ang), jnp.sin(ang)
    cosf = _pad_cols(jnp.concatenate([cos, cos], axis=1), LANES)
    sins = _pad_cols(jnp.concatenate([-sin, sin], axis=1), LANES)
    return cosf, sins


def kernel(x, mem, attn_norm_w, w_in, hgrn_lower_bounds, hgrn_out_norm_w, dsa_q_norm_w, dsa_kv_norm_w,
           dsa_kidx_norm_w, dsa_w_uq, dsa_w_qr, dsa_w_uk, dsa_w_uv, dsa_w_qidx, mem_norm_w, w_mem_kv,
           w_br_hgrn, w_br_dsa, w_br_mem, w_out, mlp_norm_w, w_mlp_up, w_mlp_down, final_norm_w):
    B, T, D = x.shape
    N = B * T
    M = mem.shape[1]
    l = 0
    topk = min(TOPK_MAX, T // 4)
    tm = 512
    x2 = x.reshape(N, D)

    wi = w_in[l]
    o = 0
    cols = {}
    for name, wd in (("hq", HG_WIDTH), ("hf", HG_WIDTH), ("hv", HG_WIDTH), ("hg", HG_WIDTH),
                     ("cq", DSA_Q_LORA), ("ckv", DSA_KV_LORA), ("kr", DSA_ROPE), ("kidx", IDX_DIM),
                     ("widx", IDX_HEADS), ("mq", MEM_WIDTH), ("gate", N_BRANCH * D)):
        cols[name] = wi[:, o:o + wd]
        o += wd
    w_proj = jnp.concatenate(
        [cols["hq"], cols["hf"], cols["hv"], cols["hg"], cols["mq"], cols["cq"], cols["ckv"],
         _pad_cols(cols["kr"], LANES), _pad_cols(cols["kidx"], LANES), _pad_cols(cols["widx"], LANES)],
        axis=1).astype(BF16)
    w_gate = cols["gate"].astype(BF16)

    proj_h, proj_m, proj_d = _norm_proj(x2, attn_norm_w[l], w_proj, (4 * HG_WIDTH, MEM_WIDTH, DSA_GROUP),
                                        (F32, F32, F32), tm)

    lb_all = jnp.cumsum(jax.nn.softmax(hgrn_lower_bounds.astype(F32), axis=0), axis=0)
    o_a = _hgrn(proj_h, lb_all[l], hgrn_out_norm_w[l], B, T)

    wuq_pad = _head_pad(dsa_w_uq[l], LANES).astype(BF16)
    wqr_pad = _head_pad(dsa_w_qr[l], LANES).astype(BF16)
    wqi_pad = _head_pad(dsa_w_qidx[l], LANES).astype(BF16)
    wuk_pad = jnp.pad(dsa_w_uk[l], ((0, 0), (0, LANES - DSA_NOPE), (0, 0))).astype(BF16)
    wuv = dsa_w_uv[l]
    wuv_pad = jnp.stack([jnp.pad(wuv[h], ((0, 0), ((h % 2) * DSA_VDIM, LANES - DSA_VDIM - (h % 2) * DSA_VDIM)))
                         for h in range(DSA_HEADS)]).astype(BF16)
    kinw_pad = _pad_cols(dsa_kidx_norm_w[l].reshape(1, IDX_DIM), LANES)
    tabs = _rope_tables(T, DSA_ROPE) + _rope_tables(T, IDX_DIM)
    qcat, qidx, kv, kidx, widx = _dsa_prep(
        proj_d, dsa_q_norm_w[l].reshape(1, -1), dsa_kv_norm_w[l].reshape(1, -1), kinw_pad,
        wuq_pad, wqr_pad, wuk_pad, wqi_pad, tabs, T, tm)
    o_b = _dsa(qcat, qidx, widx, kv, kidx, wuv_pad, B, T, topk)

    k_m, v_m = _norm_proj(mem.reshape(B * M, D), mem_norm_w[l], w_mem_kv[l].astype(BF16),
                          (MEM_WIDTH, MEM_WIDTH), (BF16, BF16), min(tm, B * M))
    o_c = _mem_attn(proj_m, k_m, v_m, T, tm)

    w_br = jnp.stack([w_br_hgrn[l], w_br_dsa[l], w_br_mem[l]]).astype(BF16)
    h = _merge(x2, attn_norm_w[l].reshape(1, D), o_a, o_b, o_c, w_gate, w_br, w_out[l].astype(BF16), tm)
    out = _mlp(h, mlp_norm_w[l].reshape(1, D), w_mlp_up[l].astype(BF16), w_mlp_down[l].astype(BF16),
               final_norm_w.reshape(1, D), tm)
    return out.reshape(B, T, D)
```

```python
import functools

import jax
import jax.numpy as jnp
import numpy as np
from jax import lax
from jax.experimental import pallas as pl
from jax.experimental.pallas import tpu as pltpu

F32 = jnp.float32
BF16 = jnp.bfloat16
I32 = jnp.int32
I16 = jnp.int16

EPS = 1e-6
ROPE_THETA = 10000.0
LANES = 128
SUBLANES = 8
HG_HEADS, HG_D, HG_CHUNK = 4, 128, 64
HG_WIDTH = HG_HEADS * HG_D
HG_SUB = 16
HG_STEP = 256
DSA_HEADS, DSA_Q_LORA, DSA_KV_LORA = 8, 256, 128
DSA_NOPE, DSA_ROPE, DSA_VDIM = 64, 32, 64
IDX_HEADS, IDX_DIM = 8, 64
TOPK_MAX = 256
MEM_HEADS, MEM_DH = 4, 128
MEM_WIDTH = MEM_HEADS * MEM_DH
N_BRANCH = 3
DSA_GROUP = 6 * LANES
VMEM_LIMIT = 56 * 1024 * 1024
INT_MIN = np.int32(-2**31)

DSA_TQ = 128
DSA_TS = 512
DSA_TK = 512


def _cparams(sem):
    return pltpu.CompilerParams(dimension_semantics=sem, vmem_limit_bytes=VMEM_LIMIT)


def _rms(x, w, n=None):
    n = x.shape[-1] if n is None else n
    ms = jnp.sum(x * x, axis=-1, keepdims=True) * (1.0 / n)
    return x * lax.rsqrt(ms + EPS) * w


def _sigmoid(x):
    return 1.0 / (1.0 + jnp.exp(-x))


def _dot(a, b):
    return jnp.dot(a, b, preferred_element_type=F32)


def _dot_nt(a, b):
    return lax.dot_general(a, b, (((1,), (1,)), ((), ())), preferred_element_type=F32)


def _dot_tn(a, b):
    return lax.dot_general(a, b, (((0,), (0,)), ((), ())), preferred_element_type=F32)


def _norm_proj_kernel(x_ref, nw_ref, w_ref, *out_refs):
    xn = _rms(x_ref[...], nw_ref[...]).astype(BF16)
    off = 0
    for o_ref in out_refs:
        wd = o_ref.shape[-1]
        o_ref[...] = _dot(xn, w_ref[:, off:off + wd]).astype(o_ref.dtype)
        off += wd


def _norm_proj(x, nw, w, widths, dtypes, tm):
    n, d = x.shape
    return pl.pallas_call(
        _norm_proj_kernel,
        grid=(n // tm,),
        in_specs=[pl.BlockSpec((tm, d), lambda i: (i, 0)),
                  pl.BlockSpec((1, d), lambda i: (0, 0)),
                  pl.BlockSpec(w.shape, lambda i: (0, 0))],
        out_specs=[pl.BlockSpec((tm, wd), lambda i: (i, 0)) for wd in widths],
        out_shape=[jax.ShapeDtypeStruct((n, wd), dt) for wd, dt in zip(widths, dtypes)],
        compiler_params=_cparams(("parallel",)),
        name="norm_proj",
    )(x, nw.reshape(1, d), w)


def _hgrn_kernel(q_ref, f_ref, v_ref, g_ref, lb_ref, onw_ref, o_ref, st_ref, b_ref, k_ref):
    @pl.when(pl.program_id(1) == 0)
    def _():
        st_ref[...] = jnp.zeros_like(st_ref)

    C, SB = HG_CHUNK, HG_SUB
    lb = lb_ref[...]
    ri = lax.broadcasted_iota(I32, (C, C), 0)
    ci = lax.broadcasted_iota(I32, (C, C), 1)
    tri = jnp.where(ri >= ci, 1.0, 0.0).astype(BF16)
    row_c = lax.broadcasted_iota(I32, (C, HG_D), 0)
    row_2 = lax.broadcasted_iota(I32, (SB, HG_D), 0)
    row_1 = lax.broadcasted_iota(I32, (SUBLANES, HG_D), 0)
    lane_2 = lax.broadcasted_iota(I32, (SB, C), 1)
    lane_1 = lax.broadcasted_iota(I32, (SUBLANES, C), 1)

    def chunk(ci_, carry):
        rows = pl.ds(pl.multiple_of(ci_ * C, C), C)
        f = lb + (1.0 - lb) * _sigmoid(f_ref[rows, :])
        lf = jnp.log(f)
        hi = lf.astype(BF16)
        r1 = lf - hi.astype(F32)
        mid = r1.astype(BF16)
        lo = (r1 - mid.astype(F32)).astype(BF16)
        b_ref[...] = _dot(tri, hi) + _dot(tri, mid) + _dot(tri, lo)
        k_ref[...] = 1.0 - f

        for h in range(HG_HEADS):
            sl = slice(h * HG_D, (h + 1) * HG_D)
            qh = q_ref[rows, sl]
            vh = v_ref[rows, sl]
            bh = b_ref[:, sl]
            kh = k_ref[:, sl]
            a_rows = [jnp.zeros((SB, C), F32)]
            for i in range(1, C // SB):
                bref = b_ref[i * SB:i * SB + 1, sl]
                qi = (qh[i * SB:(i + 1) * SB] * jnp.exp(bh[i * SB:(i + 1) * SB] - bref)).astype(BF16)
                ki = (kh * jnp.exp(jnp.where(row_c < i * SB, bref - bh, -jnp.inf))).astype(BF16)
                a_rows.append(_dot_nt(qi, ki))
            for i in range(C // SB):
                b2, q2 = bh[i * SB:(i + 1) * SB], qh[i * SB:(i + 1) * SB]
                b1, q1 = b2[SUBLANES:], q2[SUBLANES:]
                a2 = a_rows[i]
                for j in range(SB):
                    s = i * SB + j
                    brow = b_ref[s:s + 1, sl]
                    krow = k_ref[s:s + 1, sl]
                    if j < SUBLANES:
                        e = jnp.exp(jnp.where(row_2 >= j, b2 - brow, -jnp.inf))
                        r = jnp.sum(e * (q2 * krow), axis=-1, keepdims=True)
                        a2 = jnp.where(lane_2 == s, r, a2)
                    else:
                        e = jnp.exp(jnp.where(row_1 >= j - SUBLANES, b1 - brow, -jnp.inf))
                        r = jnp.sum(e * (q1 * krow), axis=-1, keepdims=True)
                        a2 = jnp.concatenate([a2[:SUBLANES], jnp.where(lane_1 == s, r, a2[SUBLANES:])], axis=0)
                a_rows[i] = a2
            a_mat = jnp.concatenate(a_rows, axis=0)
            st = st_ref[h]
            qe = (qh * jnp.exp(bh)).astype(BF16)
            vb = vh.astype(BF16)
            o = _dot_nt(qe, st.astype(BF16)) + _dot(a_mat.astype(BF16), vb)
            blast = bh[C - 1:C, :]
            kd = (kh * jnp.exp(blast - bh)).astype(BF16)
            st_ref[h] = st * jnp.exp(blast) + _dot_tn(vb, kd)
            gh = g_ref[rows, sl]
            o_ref[rows, sl] = (_rms(o, onw_ref[:, sl]) * (gh * _sigmoid(gh))).astype(o_ref.dtype)
        return carry

    lax.fori_loop(0, HG_STEP // C, chunk, 0)


def _hgrn(proj_h, lb, onw, batch, seq):
    n = proj_h.shape[0]
    nc = seq // HG_STEP
    blk = lambda col: pl.BlockSpec((HG_STEP, HG_WIDTH), lambda b, c, col=col: (b * nc + c, col))
    vec = pl.BlockSpec((1, HG_WIDTH), lambda b, c: (0, 0))
    return pl.pallas_call(
        _hgrn_kernel,
        grid=(batch, nc),
        in_specs=[blk(0), blk(1), blk(2), blk(3), vec, vec],
        out_specs=pl.BlockSpec((HG_STEP, HG_WIDTH), lambda b, c: (b * nc + c, 0)),
        out_shape=jax.ShapeDtypeStruct((n, HG_WIDTH), BF16),
        scratch_shapes=[pltpu.VMEM((HG_HEADS, HG_D, HG_D), F32),
                        pltpu.VMEM((HG_CHUNK, HG_WIDTH), F32),
                        pltpu.VMEM((HG_CHUNK, HG_WIDTH), F32)],
        compiler_params=_cparams(("parallel", "arbitrary")),
        name="hgrn2",
    )(proj_h, proj_h, proj_h, proj_h, lb.reshape(1, HG_WIDTH), onw.reshape(1, HG_WIDTH))


def _rope_lanes(x, cosf, sins, half):
    nl = x.shape[-1]
    lane = lax.broadcasted_iota(I32, x.shape, 1)
    partner = jnp.where(lane % LANES < half, pltpu.roll(x, nl - half, 1), pltpu.roll(x, half, 1))
    return x * cosf + partner * sins


def _dsa_prep_kernel(pd_ref, qnw_ref, kvnw_ref, kinw_ref, wuq_ref, wqr_ref, wuk_ref, wqi_ref,
                     c32_ref, s32_ref, c64_ref, s64_ref,
                     qcat_ref, qidx_ref, kv_ref, kidx_ref, widx_ref):
    scale = float((DSA_NOPE + DSA_ROPE) ** -0.5)
    c32, s32, c64, s64 = c32_ref[...], s32_ref[...], c64_ref[...], s64_ref[...]
    cqn = _rms(pd_ref[:, 0:DSA_Q_LORA], qnw_ref[...]).astype(BF16)
    qn = _dot(cqn, wuq_ref[...]).astype(BF16)
    qr = _dot(cqn, wqr_ref[...])
    qi = _dot(cqn, wqi_ref[...])
    for h in range(DSA_HEADS):
        sl = slice(h * LANES, (h + 1) * LANES)
        qcat_ref[h, :, 0:LANES] = (_dot(qn[:, sl], wuk_ref[h]) * scale).astype(BF16)
        qcat_ref[h, :, LANES:2 * LANES] = (_rope_lanes(qr[:, sl], c32, s32, DSA_ROPE // 2) * scale).astype(BF16)
        qidx_ref[h] = (_rope_lanes(qi[:, sl], c64, s64, IDX_DIM // 2) * float(IDX_DIM ** -0.5)).astype(BF16)
    kv_ref[:, 0:LANES] = _rms(pd_ref[:, 256:384], kvnw_ref[...]).astype(BF16)
    kv_ref[:, LANES:2 * LANES] = _rope_lanes(pd_ref[:, 384:512], c32, s32, DSA_ROPE // 2).astype(BF16)
    kin = _rms(pd_ref[:, 512:640], kinw_ref[...], n=IDX_DIM)
    kidx_ref[...] = _rope_lanes(kin, c64, s64, IDX_DIM // 2).astype(BF16)
    widx_ref[...] = jnp.transpose(pd_ref[:, 640:768] * float(IDX_HEADS ** -0.5))[0:IDX_HEADS, :]


def _dsa_prep(proj_d, qnw, kvnw, kinw_pad, wuq_pad, wqr_pad, wuk_pad, wqi_pad, tabs, seq, tm):
    n = proj_d.shape[0]
    nt = seq // tm
    full = lambda a: pl.BlockSpec(a.shape, lambda i: (0,) * a.ndim)
    tab = pl.BlockSpec((tm, LANES), lambda i: (i % nt, 0))
    hd = DSA_HEADS
    return pl.pallas_call(
        _dsa_prep_kernel,
        grid=(n // tm,),
        in_specs=[pl.BlockSpec((tm, DSA_GROUP), lambda i: (i, 0)),
                  full(qnw), full(kvnw), full(kinw_pad), full(wuq_pad), full(wqr_pad), full(wuk_pad),
                  full(wqi_pad), tab, tab, tab, tab],
        out_specs=[pl.BlockSpec((hd, tm, 2 * LANES), lambda i: (0, i, 0)),
                   pl.BlockSpec((hd, tm, LANES), lambda i: (0, i, 0)),
                   pl.BlockSpec((tm, 2 * LANES), lambda i: (i, 0)),
                   pl.BlockSpec((tm, LANES), lambda i: (i, 0)),
                   pl.BlockSpec((IDX_HEADS, tm), lambda i: (0, i))],
        out_shape=[jax.ShapeDtypeStruct((hd, n, 2 * LANES), BF16),
                   jax.ShapeDtypeStruct((hd, n, LANES), BF16),
                   jax.ShapeDtypeStruct((n, 2 * LANES), BF16),
                   jax.ShapeDtypeStruct((n, LANES), BF16),
                   jax.ShapeDtypeStruct((IDX_HEADS, n), F32)],
        compiler_params=_cparams(("parallel",)),
        name="dsa_prep",
    )(proj_d, qnw, kvnw, kinw_pad, wuq_pad, wqr_pad, wuk_pad, wqi_pad, *tabs)


def _dsa_kernel(qcat_ref, qidx_ref, widx_ref, kv_ref, kidx_ref, wuv_ref, o_ref,
                sc_ref, ik_ref, hi_ref, lo_ref, lg_ref, m_ref, mb_ref, acc_ref, *, topk):
    TQ, TK, H, TS = DSA_TQ, DSA_TK, DSA_HEADS, DSA_TS
    NG = TK // LANES
    NR = TS // TQ
    step = pl.program_id(1)
    nkt = (step * TS + TS + TK - 1) // TK
    key_q = lax.broadcasted_iota(I32, (TK, TQ), 0)
    qry_q = lax.broadcasted_iota(I32, (TK, TQ), 1)
    key_s = lax.broadcasted_iota(I32, (TK, TS), 0)

    def key_rows(kt):
        return pl.ds(pl.multiple_of(kt * TK, TK), TK)

    def as_i32(i):
        return jnp.asarray(i, dtype=I32)

    def score_tile(kt, carry):
        kidx_t = kidx_ref[key_rows(kt), :]
        for r in range(NR):
            cols = slice(r * TQ, (r + 1) * TQ)
            qidx = qidx_ref[:, cols, :].reshape(H * TQ, LANES)
            s_all = jnp.maximum(_dot_nt(kidx_t, qidx), 0.0)
            score = s_all[:, 0:TQ] * widx_ref[0:1, cols]
            for h in range(1, IDX_HEADS):
                score = score + s_all[:, h * TQ:(h + 1) * TQ] * widx_ref[h:h + 1, cols]
            bits = lax.bitcast_convert_type(score + 0.0, I32)
            key = jnp.where(bits < 0, bits ^ np.int32(0x7FFFFFFF), bits)
            key = jnp.where(kt * TK + key_q <= step * TS + r * TQ + qry_q, key, INT_MIN)
            ik_ref[kt, :, cols] = key
            hi_ref[kt, :, cols] = lax.shift_right_arithmetic(key, np.int32(16)).astype(I16)
        return carry

    lax.fori_loop(0, nkt, score_tile, 0)

    one, zero, ninf = jnp.float32(1.0), jnp.float32(0.0), jnp.float32(-jnp.inf)

    def count(pred_fn):
        def body(kt, acc):
            m = jnp.where(pred_fn(ik_ref[kt], kt * TK + key_s), one, zero)
            return acc + jnp.sum(m.reshape(TK // SUBLANES, SUBLANES, TS), axis=0)
        acc = lax.fori_loop(0, nkt, body, jnp.zeros((SUBLANES, TS), F32))
        return jnp.sum(acc, axis=0, keepdims=True)

    PK = 2 * SUBLANES

    def count16(x_ref, t16, strict=False):
        one16, zero16 = jnp.ones((), I16), jnp.zeros((), I16)

        def body(kt, acc):
            x = x_ref[kt]
            m = jnp.where((x > t16) if strict else (x >= t16), one16, zero16)
            for g in range(TK // PK):
                acc = acc + m[g * PK:(g + 1) * PK, :]
            return acc
        acc = lax.fori_loop(0, nkt, body, jnp.zeros((PK, TS), I16))
        return jnp.sum(acc.astype(I32), axis=0, keepdims=True).astype(F32)

    def search16(x_ref, kneed):
        def bit_step(i, tu):
            cand = tu | lax.shift_left(np.int32(1), np.int32(15) - as_i32(i))
            cnt = count16(x_ref, (cand - np.int32(32768)).astype(I16))
            return jnp.where(cnt >= kneed, cand, tu)
        return lax.fori_loop(0, 16, bit_step, jnp.zeros((1, TS), I32))

    kf = float(topk)
    p_hi = search16(hi_ref, kf) - np.int32(32768)
    p_hi16 = p_hi.astype(I16)
    k_lo = kf - count16(hi_ref, p_hi16, strict=True)

    def low_tile(kt, carry):
        lo = ((ik_ref[kt] & np.int32(0xFFFF)) - np.int32(32768)).astype(I16)
        lo_ref[kt] = jnp.where(hi_ref[kt] == p_hi16, lo, jnp.full((), -32768, I16))
        return carry

    lax.fori_loop(0, nkt, low_tile, 0)
    thr = p_hi * np.int32(65536) + search16(lo_ref, k_lo)
    cnt_gt = count(lambda ik, pos: ik > thr)
    cnt_ge = count(lambda ik, pos: ik >= thr)
    need = kf - cnt_gt

    has_tie = jnp.max(jnp.where((cnt_ge > kf) & (thr > INT_MIN), one, zero)) > zero
    nbits = int(np.log2(TK)) + 3

    def tie_search(_):
        def idx_step(i, x):
            cand = x | lax.shift_left(np.int32(1), np.int32(nbits - 1) - as_i32(i))
            g = count(lambda ik, pos: (ik == thr) & (pos < cand))
            return jnp.where(g < need, cand, x)
        return lax.fori_loop(0, nbits, idx_step, jnp.zeros((1, TS), I32))

    jcut = lax.cond(has_tie, tie_search, lambda _: jnp.full((1, TS), 2**30, I32), 0)
    jcut = jnp.where(thr > INT_MIN, jcut, -1)

    def bias_tile(kt, carry):
        ik = ik_ref[kt]
        sel = (ik > thr) | ((ik == thr) & (kt * TK + key_s <= jcut))
        bias_t = jnp.where(sel, zero, ninf)
        for r in range(NR):
            sc_ref[kt, r * TQ:(r + 1) * TQ, :] = jnp.transpose(bias_t[:, r * TQ:(r + 1) * TQ])
        return carry

    lax.fori_loop(0, nkt, bias_tile, 0)

    ones_v = jnp.ones((TK, LANES), BF16)

    def pass_a(r, kt):
        rows = slice(r * TQ, (r + 1) * TQ)
        qcat = qcat_ref[:, rows, :].reshape(H * TQ, 2 * LANES)
        lg = _dot_nt(qcat, kv_ref[key_rows(kt), :]).reshape(H, TQ, TK) + sc_ref[kt, rows, :][None]
        lg = lg.reshape(H * TQ, TK)
        lg_ref[r % 2, kt] = lg
        part = lg[:, 0:LANES]
        for g in range(1, NG):
            part = jnp.maximum(part, lg[:, g * LANES:(g + 1) * LANES])
        m_ref[r % 2] = jnp.maximum(m_ref[r % 2], part)

    def pass_b(r, kt):
        lg = lg_ref[r % 2, kt]
        m_b = mb_ref[...]
        p = jnp.concatenate(
            [jnp.exp((lg[:, g * LANES:(g + 1) * LANES] - m_b).astype(BF16)) for g in range(NG)], axis=1)
        kv1 = jnp.concatenate([kv_ref[key_rows(kt), 0:LANES], ones_v], axis=1)
        acc_ref[...] += _dot(p, kv1)

    def loop(fn):
        def body(kt, carry):
            fn(kt)
            return carry
        lax.fori_loop(0, nkt, body, 0)

    m_ref[0] = jnp.full((H * TQ, LANES), -jnp.inf, F32)
    loop(lambda kt: pass_a(0, kt))
    for r in range(NR):
        rows = slice(r * TQ, (r + 1) * TQ)
        mb_ref[...] = jnp.broadcast_to(jnp.max(m_ref[r % 2], axis=-1, keepdims=True), (H * TQ, LANES))
        acc_ref[...] = jnp.zeros_like(acc_ref)
        if r + 1 < NR:
            m_ref[(r + 1) % 2] = jnp.full((H * TQ, LANES), -jnp.inf, F32)
            loop(lambda kt, r=r: (pass_a(r + 1, kt), pass_b(r, kt)))
        else:
            loop(lambda kt, r=r: pass_b(r, kt))
        o_lat = (acc_ref[:, 0:LANES] / acc_ref[:, LANES:2 * LANES]).astype(BF16)
        for pr in range(H // 2):
            h0, h1 = 2 * pr, 2 * pr + 1
            o_ref[rows, pr * LANES:(pr + 1) * LANES] = (
                _dot(o_lat[h0 * TQ:(h0 + 1) * TQ], wuv_ref[h0]) + _dot(o_lat[h1 * TQ:(h1 + 1) * TQ], wuv_ref[h1])
            ).astype(o_ref.dtype)


def _dsa(qcat, qidx, widx, kv, kidx, wuv_pad, batch, seq, topk):
    n = kv.shape[0]
    ns = seq // DSA_TS
    nkt_max = seq // DSA_TK
    H = DSA_HEADS
    return pl.pallas_call(
        functools.partial(_dsa_kernel, topk=topk),
        grid=(batch, ns),
        in_specs=[pl.BlockSpec((H, DSA_TS, 2 * LANES), lambda b, s: (0, b * ns + s, 0)),
                  pl.BlockSpec((H, DSA_TS, LANES), lambda b, s: (0, b * ns + s, 0)),
                  pl.BlockSpec((IDX_HEADS, DSA_TS), lambda b, s: (0, b * ns + s)),
                  pl.BlockSpec((seq, 2 * LANES), lambda b, s: (b, 0)),
                  pl.BlockSpec((seq, LANES), lambda b, s: (b, 0)),
                  pl.BlockSpec(wuv_pad.shape, lambda b, s: (0, 0, 0))],
        out_specs=pl.BlockSpec((DSA_TS, H * DSA_VDIM), lambda b, s: (b * ns + s, 0)),
        out_shape=jax.ShapeDtypeStruct((n, H * DSA_VDIM), BF16),
        scratch_shapes=[pltpu.VMEM((nkt_max, DSA_TS, DSA_TK), F32),
                        pltpu.VMEM((nkt_max, DSA_TK, DSA_TS), I32),
                        pltpu.VMEM((nkt_max, DSA_TK, DSA_TS), I16),
                        pltpu.VMEM((nkt_max, DSA_TK, DSA_TS), I16),
                        pltpu.VMEM((2, nkt_max, H * DSA_TQ, DSA_TK), F32),
                        pltpu.VMEM((2, H * DSA_TQ, LANES), F32),
                        pltpu.VMEM((H * DSA_TQ, LANES), F32),
                        pltpu.VMEM((H * DSA_TQ, 2 * LANES), F32)],
        compiler_params=_cparams(("parallel", "arbitrary")),
        name="dsa_attn",
    )(qcat, qidx, widx, kv, kidx, wuv_pad)


def _mem_attn_kernel(q_ref, k_ref, v_ref, o_ref):
    scale = float(MEM_DH ** -0.5)
    for h in range(MEM_HEADS):
        sl = slice(h * MEM_DH, (h + 1) * MEM_DH)
        lg = _dot_nt((q_ref[:, sl] * scale).astype(BF16), k_ref[:, sl])
        p = jnp.exp(lg - jnp.max(lg, axis=-1, keepdims=True))
        o = _dot(p.astype(BF16), v_ref[:, sl]) / jnp.sum(p, axis=-1, keepdims=True)
        o_ref[:, sl] = o.astype(o_ref.dtype)


def _mem_attn(proj_m, k_m, v_m, seq, tm):
    n = proj_m.shape[0]
    nt = seq // tm
    m = k_m.shape[0] // (n // seq)
    return pl.pallas_call(
        _mem_attn_kernel,
        grid=(n // tm,),
        in_specs=[pl.BlockSpec((tm, MEM_WIDTH), lambda i: (i, 0)),
                  pl.BlockSpec((m, MEM_WIDTH), lambda i: (i // nt, 0)),
                  pl.BlockSpec((m, MEM_WIDTH), lambda i: (i // nt, 0))],
        out_specs=pl.BlockSpec((tm, MEM_WIDTH), lambda i: (i, 0)),
        out_shape=jax.ShapeDtypeStruct((n, MEM_WIDTH), BF16),
        compiler_params=_cparams(("parallel",)),
        name="mem_attn",
    )(proj_m, k_m, v_m)


def _merge_kernel(x_ref, nw_ref, oa_ref, ob_ref, oc_ref, wg_ref, wbr_ref, wo_ref, h_ref):
    x = x_ref[...]
    d = x.shape[-1]
    xn = _rms(x, nw_ref[...]).astype(BF16)
    merged = None
    for br, o_br in enumerate((oa_ref, ob_ref, oc_ref)):
        gate = _sigmoid(_dot(xn, wg_ref[:, br * d:(br + 1) * d]))
        term = gate * _dot(o_br[...], wbr_ref[br])
        merged = term if merged is None else merged + term
    h_ref[...] = x + _dot(merged.astype(BF16), wo_ref[...])


def _merge(x, nw, o_a, o_b, o_c, w_gate, w_br, w_out, tm):
    n, d = x.shape
    row = lambda wd: pl.BlockSpec((tm, wd), lambda i: (i, 0))
    full = lambda a: pl.BlockSpec(a.shape, lambda i: (0,) * a.ndim)
    return pl.pallas_call(
        _merge_kernel,
        grid=(n // tm,),
        in_specs=[row(d), full(nw), row(o_a.shape[1]), row(o_b.shape[1]), row(o_c.shape[1]),
                  full(w_gate), full(w_br), full(w_out)],
        out_specs=row(d),
        out_shape=jax.ShapeDtypeStruct((n, d), F32),
        compiler_params=_cparams(("parallel",)),
        name="merge_out",
    )(x, nw, o_a, o_b, o_c, w_gate, w_br, w_out)


def _mlp_kernel(h_ref, nw_ref, wu_ref, wd_ref, fw_ref, o_ref, *, ff_chunk):
    h = h_ref[...]
    m = _rms(h, nw_ref[...]).astype(BF16)
    acc = h
    for c0 in range(0, wu_ref.shape[1], ff_chunk):
        u = jnp.maximum(_dot(m, wu_ref[:, c0:c0 + ff_chunk]), 0.0)
        acc = acc + _dot((u * u).astype(BF16), wd_ref[c0:c0 + ff_chunk, :])
    o_ref[...] = _rms(acc, fw_ref[...])


def _mlp(h, nw, w_up, w_down, fw, tm, ff_chunk=1024):
    n, d = h.shape
    full = lambda a: pl.BlockSpec(a.shape, lambda i: (0,) * a.ndim)
    return pl.pallas_call(
        functools.partial(_mlp_kernel, ff_chunk=ff_chunk),
        grid=(n // tm,),
        in_specs=[pl.BlockSpec((tm, d), lambda i: (i, 0)), full(nw), full(w_up), full(w_down), full(fw)],
        out_specs=pl.BlockSpec((tm, d), lambda i: (i, 0)),
        out_shape=jax.ShapeDtypeStruct((n, d), F32),
        compiler_params=_cparams(("parallel",)),
        name="mlp_final",
    )(h, nw, w_up, w_down, fw)


def _pad_cols(w, width):
    return jnp.pad(w, ((0, 0), (0, width - w.shape[1])))


def _head_pad(w, width):
    return jnp.pad(w, ((0, 0), (0, 0), (0, width - w.shape[2]))).reshape(w.shape[0], -1)


def _rope_tables(seq, d):
    inv = ROPE_THETA ** (-jnp.arange(0, d, 2, dtype=F32) / d)
    ang = jnp.arange(seq, dtype=F32)[:, None] * inv[None, :]
    cos, sin = jnp.cos(ang), jnp.sin(ang)
    cosf = _pad_cols(jnp.concatenate([cos, cos], axis=1), LANES)
    sins = _pad_cols(jnp.concatenate([-sin, sin], axis=1), LANES)
    return cosf, sins


def kernel(x, mem, attn_norm_w, w_in, hgrn_lower_bounds, hgrn_out_norm_w, dsa_q_norm_w, dsa_kv_norm_w,
           dsa_kidx_norm_w, dsa_w_uq, dsa_w_qr, dsa_w_uk, dsa_w_uv, dsa_w_qidx, mem_norm_w, w_mem_kv,
           w_br_hgrn, w_br_dsa, w_br_mem, w_out, mlp_norm_w, w_mlp_up, w_mlp_down, final_norm_w):
    B, T, D = x.shape
    N = B * T
    M = mem.shape[1]
    l = 0
    topk = min(TOPK_MAX, T // 4)
    tm = 512
    x2 = x.reshape(N, D)

    wi = w_in[l]
    o = 0
    cols = {}
    for name, wd in (("hq", HG_WIDTH), ("hf", HG_WIDTH), ("hv", HG_WIDTH), ("hg", HG_WIDTH),
                     ("cq", DSA_Q_LORA), ("ckv", DSA_KV_LORA), ("kr", DSA_ROPE), ("kidx", IDX_DIM),
                     ("widx", IDX_HEADS), ("mq", MEM_WIDTH), ("gate", N_BRANCH * D)):
        cols[name] = wi[:, o:o + wd]
        o += wd
    w_proj = jnp.concatenate(
        [cols["hq"], cols["hf"], cols["hv"], cols["hg"], cols["mq"], cols["cq"], cols["ckv"],
         _pad_cols(cols["kr"], LANES), _pad_cols(cols["kidx"], LANES), _pad_cols(cols["widx"], LANES)],
        axis=1).astype(BF16)
    w_gate = cols["gate"].astype(BF16)

    proj_h, proj_m, proj_d = _norm_proj(x2, attn_norm_w[l], w_proj, (4 * HG_WIDTH, MEM_WIDTH, DSA_GROUP),
                                        (F32, F32, F32), tm)

    lb_all = jnp.cumsum(jax.nn.softmax(hgrn_lower_bounds.astype(F32), axis=0), axis=0)
    o_a = _hgrn(proj_h, lb_all[l], hgrn_out_norm_w[l], B, T)

    wuq_pad = _head_pad(dsa_w_uq[l], LANES).astype(BF16)
    wqr_pad = _head_pad(dsa_w_qr[l], LANES).astype(BF16)
    wqi_pad = _head_pad(dsa_w_qidx[l], LANES).astype(BF16)
    wuk_pad = jnp.pad(dsa_w_uk[l], ((0, 0), (0, LANES - DSA_NOPE), (0, 0))).astype(BF16)
    wuv = dsa_w_uv[l]
    wuv_pad = jnp.stack([jnp.pad(wuv[h], ((0, 0), ((h % 2) * DSA_VDIM, LANES - DSA_VDIM - (h % 2) * DSA_VDIM)))
                         for h in range(DSA_HEADS)]).astype(BF16)
    kinw_pad = _pad_cols(dsa_kidx_norm_w[l].reshape(1, IDX_DIM), LANES)
    tabs = _rope_tables(T, DSA_ROPE) + _rope_tables(T, IDX_DIM)
    qcat, qidx, kv, kidx, widx = _dsa_prep(
        proj_d, dsa_q_norm_w[l].reshape(1, -1), dsa_kv_norm_w[l].reshape(1, -1), kinw_pad,
        wuq_pad, wqr_pad, wuk_pad, wqi_pad, tabs, T, tm)
    o_b = _dsa(qcat, qidx, widx, kv, kidx, wuv_pad, B, T, topk)

    k_m, v_m = _norm_proj(mem.reshape(B * M, D), mem_norm_w[l], w_mem_kv[l].astype(BF16),
                          (MEM_WIDTH, MEM_WIDTH), (BF16, BF16), min(tm, B * M))
    o_c = _mem_attn(proj_m, k_m, v_m, T, tm)

    w_br = jnp.stack([w_br_hgrn[l], w_br_dsa[l], w_br_mem[l]]).astype(BF16)
    h = _merge(x2, attn_norm_w[l].reshape(1, D), o_a, o_b, o_c, w_gate, w_br, w_out[l].astype(BF16), tm)
    out = _mlp(h, mlp_norm_w[l].reshape(1, D), w_mlp_up[l].astype(BF16), w_mlp_down[l].astype(BF16),
               final_norm_w.reshape(1, D), tm)
    return out.reshape(B, T, D)
```

```python
import functools

import jax
import jax.numpy as jnp
import numpy as np
from jax import lax
from jax.experimental import pallas as pl
from jax.experimental.pallas import tpu as pltpu

F32 = jnp.float32
BF16 = jnp.bfloat16
I32 = jnp.int32
I16 = jnp.int16

EPS = 1e-6
ROPE_THETA = 10000.0
LANES = 128
SUBLANES = 8
HG_HEADS, HG_D, HG_CHUNK = 4, 128, 64
HG_WIDTH = HG_HEADS * HG_D
HG_SUB = 16
HG_STEP = 256
DSA_HEADS, DSA_Q_LORA, DSA_KV_LORA = 8, 256, 128
DSA_NOPE, DSA_ROPE, DSA_VDIM = 64, 32, 64
IDX_HEADS, IDX_DIM = 8, 64
TOPK_MAX = 256
MEM_HEADS, MEM_DH = 4, 128
MEM_WIDTH = MEM_HEADS * MEM_DH
N_BRANCH = 3
DSA_GROUP = 6 * LANES
VMEM_LIMIT = 56 * 1024 * 1024
INT_MIN = np.int32(-2**31)

DSA_TQ = 128
DSA_TS = 512
DSA_TK = 512


def _cparams(sem):
    return pltpu.CompilerParams(dimension_semantics=sem, vmem_limit_bytes=VMEM_LIMIT)


def _rms(x, w, n=None):
    n = x.shape[-1] if n is None else n
    ms = jnp.sum(x * x, axis=-1, keepdims=True) * (1.0 / n)
    return x * lax.rsqrt(ms + EPS) * w


def _sigmoid(x):
    return 1.0 / (1.0 + jnp.exp(-x))


def _dot(a, b):
    return jnp.dot(a, b, preferred_element_type=F32)


def _dot_nt(a, b):
    return lax.dot_general(a, b, (((1,), (1,)), ((), ())), preferred_element_type=F32)


def _dot_tn(a, b):
    return lax.dot_general(a, b, (((0,), (0,)), ((), ())), preferred_element_type=F32)


def _norm_proj_kernel(x_ref, nw_ref, w_ref, *out_refs):
    xn = _rms(x_ref[...], nw_ref[...]).astype(BF16)
    off = 0
    for o_ref in out_refs:
        wd = o_ref.shape[-1]
        o_ref[...] = _dot(xn, w_ref[:, off:off + wd]).astype(o_ref.dtype)
        off += wd


def _norm_proj(x, nw, w, widths, dtypes, tm):
    n, d = x.shape
    return pl.pallas_call(
        _norm_proj_kernel,
        grid=(n // tm,),
        in_specs=[pl.BlockSpec((tm, d), lambda i: (i, 0)),
                  pl.BlockSpec((1, d), lambda i: (0, 0)),
                  pl.BlockSpec(w.shape, lambda i: (0, 0))],
        out_specs=[pl.BlockSpec((tm, wd), lambda i: (i, 0)) for wd in widths],
        out_shape=[jax.ShapeDtypeStruct((n, wd), dt) for wd, dt in zip(widths, dtypes)],
        compiler_params=_cparams(("parallel",)),
        name="norm_proj",
    )(x, nw.reshape(1, d), w)


def _hgrn_kernel(q_ref, f_ref, v_ref, g_ref, lb_ref, onw_ref, o_ref, st_ref, b_ref, k_ref):
    @pl.when(pl.program_id(1) == 0)
    def _():
        st_ref[...] = jnp.zeros_like(st_ref)

    C, SB = HG_CHUNK, HG_SUB
    lb = lb_ref[...]
    ri = lax.broadcasted_iota(I32, (C, C), 0)
    ci = lax.broadcasted_iota(I32, (C, C), 1)
    tri = jnp.where(ri >= ci, 1.0, 0.0).astype(BF16)
    row_c = lax.broadcasted_iota(I32, (C, HG_D), 0)
    row_2 = lax.broadcasted_iota(I32, (SB, HG_D), 0)
    row_1 = lax.broadcasted_iota(I32, (SUBLANES, HG_D), 0)
    lane_2 = lax.broadcasted_iota(I32, (SB, C), 1)
    lane_1 = lax.broadcasted_iota(I32, (SUBLANES, C), 1)

    def chunk(ci_, carry):
        rows = pl.ds(pl.multiple_of(ci_ * C, C), C)
        f = lb + (1.0 - lb) * _sigmoid(f_ref[rows, :])
        lf = jnp.log(f)
        hi = lf.astype(BF16)
        r1 = lf - hi.astype(F32)
        mid = r1.astype(BF16)
        lo = (r1 - mid.astype(F32)).astype(BF16)
        b_ref[...] = _dot(tri, hi) + _dot(tri, mid) + _dot(tri, lo)
        k_ref[...] = 1.0 - f

        for h in range(HG_HEADS):
            sl = slice(h * HG_D, (h + 1) * HG_D)
            qh = q_ref[rows, sl]
            vh = v_ref[rows, sl]
            bh = b_ref[:, sl]
            kh = k_ref[:, sl]
            a_rows = [jnp.zeros((SB, C), F32)]
            for i in range(1, C // SB):
                bref = b_ref[i * SB:i * SB + 1, sl]
                qi = (qh[i * SB:(i + 1) * SB] * jnp.exp(bh[i * SB:(i + 1) * SB] - bref)).astype(BF16)
                ki = (kh * jnp.exp(jnp.where(row_c < i * SB, bref - bh, -jnp.inf))).astype(BF16)
                a_rows.append(_dot_nt(qi, ki))
            for i in range(C // SB):
                b2, q2 = bh[i * SB:(i + 1) * SB], qh[i * SB:(i + 1) * SB]
                b1, q1 = b2[SUBLANES:], q2[SUBLANES:]
                a2 = a_rows[i]
                for j in range(SB):
                    s = i * SB + j
                    brow = b_ref[s:s + 1, sl]
                    krow = k_ref[s:s + 1, sl]
                    if j < SUBLANES:
                        e = jnp.exp(jnp.where(row_2 >= j, b2 - brow, -jnp.inf))
                        r = jnp.sum(e * (q2 * krow), axis=-1, keepdims=True)
                        a2 = jnp.where(lane_2 == s, r, a2)
                    else:
                        e = jnp.exp(jnp.where(row_1 >= j - SUBLANES, b1 - brow, -jnp.inf))
                        r = jnp.sum(e * (q1 * krow), axis=-1, keepdims=True)
                        a2 = jnp.concatenate([a2[:SUBLANES], jnp.where(lane_1 == s, r, a2[SUBLANES:])], axis=0)
                a_rows[i] = a2
            a_mat = jnp.concatenate(a_rows, axis=0)
            st = st_ref[h]
            qe = (qh * jnp.exp(bh)).astype(BF16)
            vb = vh.astype(BF16)
            o = _dot_nt(qe, st.astype(BF16)) + _dot(a_mat.astype(BF16), vb)
            blast = bh[C - 1:C, :]
            kd = (kh * jnp.exp(blast - bh)).astype(BF16)
            st_ref[h] = st * jnp.exp(blast) + _dot_tn(vb, kd)
            gh = g_ref[rows, sl]
            o_ref[rows, sl] = (_rms(o, onw_ref[:, sl]) * (gh * _sigmoid(gh))).astype(o_ref.dtype)
        return carry

    lax.fori_loop(0, HG_STEP // C, chunk, 0)


def _hgrn(proj_h, lb, onw, batch, seq):
    n = proj_h.shape[0]
    nc = seq // HG_STEP
    blk = lambda col: pl.BlockSpec((HG_STEP, HG_WIDTH), lambda b, c, col=col: (b * nc + c, col))
    vec = pl.BlockSpec((1, HG_WIDTH), lambda b, c: (0, 0))
    return pl.pallas_call(
        _hgrn_kernel,
        grid=(batch, nc),
        in_specs=[blk(0), blk(1), blk(2), blk(3), vec, vec],
        out_specs=pl.BlockSpec((HG_STEP, HG_WIDTH), lambda b, c: (b * nc + c, 0)),
        out_shape=jax.ShapeDtypeStruct((n, HG_WIDTH), BF16),
        scratch_shapes=[pltpu.VMEM((HG_HEADS, HG_D, HG_D), F32),
                        pltpu.VMEM((HG_CHUNK, HG_WIDTH), F32),
                        pltpu.VMEM((HG_CHUNK, HG_WIDTH), F32)],
        compiler_params=_cparams(("parallel", "arbitrary")),
        name="hgrn2",
    )(proj_h, proj_h, proj_h, proj_h, lb.reshape(1, HG_WIDTH), onw.reshape(1, HG_WIDTH))


def _rope_lanes(x, cosf, sins, half):
    nl = x.shape[-1]
    lane = lax.broadcasted_iota(I32, x.shape, 1)
    partner = jnp.where(lane % LANES < half, pltpu.roll(x, nl - half, 1), pltpu.roll(x, half, 1))
    return x * cosf + partner * sins


def _dsa_prep_kernel(pd_ref, qnw_ref, kvnw_ref, kinw_ref, wuq_ref, wqr_ref, wuk_ref, wqi_ref,
                     c32_ref, s32_ref, c64_ref, s64_ref,
                     qcat_ref, qidx_ref, kv_ref, kidx_ref, widx_ref):
    scale = float((DSA_NOPE + DSA_ROPE) ** -0.5)
    c32, s32, c64, s64 = c32_ref[...], s32_ref[...], c64_ref[...], s64_ref[...]
    cqn = _rms(pd_ref[:, 0:DSA_Q_LORA], qnw_ref[...]).astype(BF16)
    qn = _dot(cqn, wuq_ref[...]).astype(BF16)
    qr = _dot(cqn, wqr_ref[...])
    qi = _dot(cqn, wqi_ref[...])
    for h in range(DSA_HEADS):
        sl = slice(h * LANES, (h + 1) * LANES)
        qcat_ref[h, :, 0:LANES] = (_dot(qn[:, sl], wuk_ref[h]) * scale).astype(BF16)
        qcat_ref[h, :, LANES:2 * LANES] = (_rope_lanes(qr[:, sl], c32, s32, DSA_ROPE // 2) * scale).astype(BF16)
        qidx_ref[h] = (_rope_lanes(qi[:, sl], c64, s64, IDX_DIM // 2) * float(IDX_DIM ** -0.5)).astype(BF16)
    kv_ref[:, 0:LANES] = _rms(pd_ref[:, 256:384], kvnw_ref[...]).astype(BF16)
    kv_ref[:, LANES:2 * LANES] = _rope_lanes(pd_ref[:, 384:512], c32, s32, DSA_ROPE // 2).astype(BF16)
    kin = _rms(pd_ref[:, 512:640], kinw_ref[...], n=IDX_DIM)
    kidx_ref[...] = _rope_lanes(kin, c64, s64, IDX_DIM // 2).astype(BF16)
    widx_ref[...] = jnp.transpose(pd_ref[:, 640:768] * float(IDX_HEADS ** -0.5))[0:IDX_HEADS, :]


def _dsa_prep(proj_d, qnw, kvnw, kinw_pad, wuq_pad, wqr_pad, wuk_pad, wqi_pad, tabs, seq, tm):
    n = proj_d.shape[0]
    nt = seq // tm
    full = lambda a: pl.BlockSpec(a.shape, lambda i: (0,) * a.ndim)
    tab = pl.BlockSpec((tm, LANES), lambda i: (i % nt, 0))
    hd = DSA_HEADS
    return pl.pallas_call(
        _dsa_prep_kernel,
        grid=(n // tm,),
        in_specs=[pl.BlockSpec((tm, DSA_GROUP), lambda i: (i, 0)),
                  full(qnw), full(kvnw), full(kinw_pad), full(wuq_pad), full(wqr_pad), full(wuk_pad),
                  full(wqi_pad), tab, tab, tab, tab],
        out_specs=[pl.BlockSpec((hd, tm, 2 * LANES), lambda i: (0, i, 0)),
                   pl.BlockSpec((hd, tm, LANES), lambda i: (0, i, 0)),
                   pl.BlockSpec((tm, 2 * LANES), lambda i: (i, 0)),
                   pl.BlockSpec((tm, LANES), lambda i: (i, 0)),
                   pl.BlockSpec((IDX_HEADS, tm), lambda i: (0, i))],
        out_shape=[jax.ShapeDtypeStruct((hd, n, 2 * LANES), BF16),
                   jax.ShapeDtypeStruct((hd, n, LANES), BF16),
                   jax.ShapeDtypeStruct((n, 2 * LANES), BF16),
                   jax.ShapeDtypeStruct((n, LANES), BF16),
                   jax.ShapeDtypeStruct((IDX_HEADS, n), F32)],
        compiler_params=_cparams(("parallel",)),
        name="dsa_prep",
    )(proj_d, qnw, kvnw, kinw_pad, wuq_pad, wqr_pad, wuk_pad, wqi_pad, *tabs)


def _dsa_kernel(qcat_ref, qidx_ref, widx_ref, kv_ref, kidx_ref, wuv_ref, o_ref,
                sc_ref, ik_ref, hi_ref, lo_ref, lg_ref, m_ref, mb_ref, acc_ref, *, topk):
    TQ, TK, H, TS = DSA_TQ, DSA_TK, DSA_HEADS, DSA_TS
    assert TS == TK
    NR = TS // TQ
    step = pl.program_id(1)
    FULL = (TK,) * NR
    DIAG = tuple((r + 1) * TQ for r in range(NR))

    def over_tiles(fn, init):
        carry = lax.fori_loop(0, step, lambda kt, c: fn(kt, FULL, c), init)
        return fn(step, DIAG, carry)

    def key_rows(kt, nk):
        return pl.ds(pl.multiple_of(kt * TK, TK), nk)

    def cols(r):
        return slice(r * TQ, (r + 1) * TQ)

    def as_i32(i):
        return jnp.asarray(i, dtype=I32)

    def score_tile(kt, nks, carry):
        for r, nk in enumerate(nks):
            qidx = qidx_ref[:, cols(r), :].reshape(H * TQ, LANES)
            s_all = jnp.maximum(_dot_nt(kidx_ref[key_rows(kt, nk), :], qidx), 0.0)
            score = s_all[:, 0:TQ] * widx_ref[0:1, cols(r)]
            for h in range(1, IDX_HEADS):
                score = score + s_all[:, h * TQ:(h + 1) * TQ] * widx_ref[h:h + 1, cols(r)]
            bits = lax.bitcast_convert_type(score + 0.0, I32)
            key = jnp.where(bits < 0, bits ^ np.int32(0x7FFFFFFF), bits)
            if nks is DIAG:
                key_i = lax.broadcasted_iota(I32, (nk, TQ), 0)
                qry_i = lax.broadcasted_iota(I32, (nk, TQ), 1)
                key = jnp.where(key_i <= r * TQ + qry_i, key, INT_MIN)
            ik_ref[kt, 0:nk, cols(r)] = key
            hi_ref[kt, 0:nk, cols(r)] = lax.shift_right_arithmetic(key, np.int32(16)).astype(I16)
        return carry

    over_tiles(score_tile, 0)

    one, zero, ninf = jnp.float32(1.0), jnp.float32(0.0), jnp.float32(-jnp.inf)
    PK = 2 * SUBLANES

    def count(pred_fn):
        def tile(kt, nks, accs):
            out = []
            for r, nk in enumerate(nks):
                pos = kt * TK + lax.broadcasted_iota(I32, (nk, TQ), 0)
                m = jnp.where(pred_fn(ik_ref[kt, 0:nk, cols(r)], pos, cols(r)), one, zero)
                out.append(accs[r] + jnp.sum(m.reshape(nk // SUBLANES, SUBLANES, TQ), axis=0))
            return tuple(out)
        accs = over_tiles(tile, tuple(jnp.zeros((SUBLANES, TQ), F32) for _ in range(NR)))
        return jnp.sum(jnp.concatenate(accs, axis=1), axis=0, keepdims=True)

    def count16(x_ref, t16, strict=False):
        one16, zero16 = jnp.ones((), I16), jnp.zeros((), I16)

        def block(x, t, acc):
            m = jnp.where((x > t) if strict else (x >= t), one16, zero16)
            for g in range(x.shape[0] // PK):
                acc = acc + m[g * PK:(g + 1) * PK, :]
            return acc

        def tile(kt, nks, acc):
            if nks is FULL:
                return block(x_ref[kt], t16, acc)
            return jnp.concatenate([block(x_ref[kt, 0:nk, cols(r)], t16[:, cols(r)], acc[:, cols(r)])
                                    for r, nk in enumerate(nks)], axis=1)
        acc = over_tiles(tile, jnp.zeros((PK, TS), I16))
        return jnp.sum(acc.astype(I32), axis=0, keepdims=True).astype(F32)

    def search16(x_ref, kneed):
        def bit_step(i, tu):
            cand = tu | lax.shift_left(np.int32(1), np.int32(15) - as_i32(i))
            cnt = count16(x_ref, (cand - np.int32(32768)).astype(I16))
            return jnp.where(cnt >= kneed, cand, tu)
        return lax.fori_loop(0, 16, bit_step, jnp.zeros((1, TS), I32))

    kf = float(topk)
    p_hi = search16(hi_ref, kf) - np.int32(32768)
    p_hi16 = p_hi.astype(I16)
    k_lo = kf - count16(hi_ref, p_hi16, strict=True)

    def low_tile(kt, nks, carry):
        for r, nk in enumerate(nks):
            lo = ((ik_ref[kt, 0:nk, cols(r)] & np.int32(0xFFFF)) - np.int32(32768)).astype(I16)
            lo_ref[kt, 0:nk, cols(r)] = jnp.where(hi_ref[kt, 0:nk, cols(r)] == p_hi16[:, cols(r)], lo,
                                                  jnp.full((), -32768, I16))
        return carry

    over_tiles(low_tile, 0)
    thr = p_hi * np.int32(65536) + search16(lo_ref, k_lo)
    cnt_gt = count(lambda ik, pos, c: ik > thr[:, c])
    cnt_ge = count(lambda ik, pos, c: ik >= thr[:, c])
    need = kf - cnt_gt

    has_tie = jnp.max(jnp.where((cnt_ge > kf) & (thr > INT_MIN), one, zero)) > zero
    nbits = int(np.log2(TK)) + 3

    def tie_search(_):
        def idx_step(i, x):
            cand = x | lax.shift_left(np.int32(1), np.int32(nbits - 1) - as_i32(i))
            g = count(lambda ik, pos, c: (ik == thr[:, c]) & (pos < cand[:, c]))
            return jnp.where(g < need, cand, x)
        return lax.fori_loop(0, nbits, idx_step, jnp.zeros((1, TS), I32))

    jcut = lax.cond(has_tie, tie_search, lambda _: jnp.full((1, TS), 2**30, I32), 0)
    jcut = jnp.where(thr > INT_MIN, jcut, -1)

    def bias_tile(kt, nks, carry):
        for r, nk in enumerate(nks):
            ik = ik_ref[kt, 0:nk, cols(r)]
            pos = kt * TK + lax.broadcasted_iota(I32, (nk, TQ), 0)
            sel = (ik > thr[:, cols(r)]) | ((ik == thr[:, cols(r)]) & (pos <= jcut[:, cols(r)]))
            sc_ref[kt, cols(r), 0:nk] = jnp.transpose(jnp.where(sel, zero, ninf))
        return carry

    over_tiles(bias_tile, 0)

    def pass_a(r, kt, nk):
        qcat = qcat_ref[:, cols(r), :].reshape(H * TQ, 2 * LANES)
        lg = _dot_nt(qcat, kv_ref[key_rows(kt, nk), :]).reshape(H, TQ, nk) + sc_ref[kt, cols(r), 0:nk][None]
        lg = lg.reshape(H * TQ, nk)
        lg_ref[r % 2, kt, :, 0:nk] = lg
        part = lg[:, 0:LANES]
        for g in range(1, nk // LANES):
            part = jnp.maximum(part, lg[:, g * LANES:(g + 1) * LANES])
        m_ref[r % 2] = jnp.maximum(m_ref[r % 2], part)

    def pass_b(r, kt, nk):
        m_b = mb_ref[...]
        p = jnp.concatenate(
            [jnp.exp((lg_ref[r % 2, kt, :, g * LANES:(g + 1) * LANES] - m_b).astype(BF16))
             for g in range(nk // LANES)], axis=1)
        kv1 = jnp.concatenate([kv_ref[key_rows(kt, nk), 0:LANES], jnp.ones((nk, LANES), BF16)], axis=1)
        acc_ref[...] += _dot(p, kv1)

    def run(fns):
        def tile(kt, nks, carry):
            for fn, r in fns:
                fn(r, kt, nks[r])
            return carry
        over_tiles(tile, 0)

    m_ref[0] = jnp.full((H * TQ, LANES), -jnp.inf, F32)
    run([(pass_a, 0)])
    for r in range(NR):
        rows = cols(r)
        mb_ref[...] = jnp.broadcast_to(jnp.max(m_ref[r % 2], axis=-1, keepdims=True), (H * TQ, LANES))
        acc_ref[...] = jnp.zeros_like(acc_ref)
        if r + 1 < NR:
            m_ref[(r + 1) % 2] = jnp.full((H * TQ, LANES), -jnp.inf, F32)
            run([(pass_a, r + 1), (pass_b, r)])
        else:
            run([(pass_b, r)])
        o_lat = (acc_ref[:, 0:LANES] / acc_ref[:, LANES:2 * LANES]).astype(BF16)
        for pr in range(H // 2):
            h0, h1 = 2 * pr, 2 * pr + 1
            o_ref[rows, pr * LANES:(pr + 1) * LANES] = (
                _dot(o_lat[h0 * TQ:(h0 + 1) * TQ], wuv_ref[h0]) + _dot(o_lat[h1 * TQ:(h1 + 1) * TQ], wuv_ref[h1])
            ).astype(o_ref.dtype)


def _dsa(qcat, qidx, widx, kv, kidx, wuv_pad, batch, seq, topk):
    n = kv.shape[0]
    ns = seq // DSA_TS
    nkt_max = seq // DSA_TK
    H = DSA_HEADS
    return pl.pallas_call(
        functools.partial(_dsa_kernel, topk=topk),
        grid=(batch, ns),
        in_specs=[pl.BlockSpec((H, DSA_TS, 2 * LANES), lambda b, s: (0, b * ns + s, 0)),
                  pl.BlockSpec((H, DSA_TS, LANES), lambda b, s: (0, b * ns + s, 0)),
                  pl.BlockSpec((IDX_HEADS, DSA_TS), lambda b, s: (0, b * ns + s)),
                  pl.BlockSpec((seq, 2 * LANES), lambda b, s: (b, 0)),
                  pl.BlockSpec((seq, LANES), lambda b, s: (b, 0)),
                  pl.BlockSpec(wuv_pad.shape, lambda b, s: (0, 0, 0))],
        out_specs=pl.BlockSpec((DSA_TS, H * DSA_VDIM), lambda b, s: (b * ns + s, 0)),
        out_shape=jax.ShapeDtypeStruct((n, H * DSA_VDIM), BF16),
        scratch_shapes=[pltpu.VMEM((nkt_max, DSA_TS, DSA_TK), F32),
                        pltpu.VMEM((nkt_max, DSA_TK, DSA_TS), I32),
                        pltpu.VMEM((nkt_max, DSA_TK, DSA_TS), I16),
                        pltpu.VMEM((nkt_max, DSA_TK, DSA_TS), I16),
                        pltpu.VMEM((2, nkt_max, H * DSA_TQ, DSA_TK), F32),
                        pltpu.VMEM((2, H * DSA_TQ, LANES), F32),
                        pltpu.VMEM((H * DSA_TQ, LANES), F32),
                        pltpu.VMEM((H * DSA_TQ, 2 * LANES), F32)],
        compiler_params=_cparams(("parallel", "arbitrary")),
        name="dsa_attn",
    )(qcat, qidx, widx, kv, kidx, wuv_pad)


def _mem_attn_kernel(q_ref, k_ref, v_ref, o_ref):
    scale = float(MEM_DH ** -0.5)
    for h in range(MEM_HEADS):
        sl = slice(h * MEM_DH, (h + 1) * MEM_DH)
        lg = _dot_nt((q_ref[:, sl] * scale).astype(BF16), k_ref[:, sl])
        p = jnp.exp(lg - jnp.max(lg, axis=-1, keepdims=True))
        o = _dot(p.astype(BF16), v_ref[:, sl]) / jnp.sum(p, axis=-1, keepdims=True)
        o_ref[:, sl] = o.astype(o_ref.dtype)


def _mem_attn(proj_m, k_m, v_m, seq, tm):
    n = proj_m.shape[0]
    nt = seq // tm
    m = k_m.shape[0] // (n // seq)
    return pl.pallas_call(
        _mem_attn_kernel,
        grid=(n // tm,),
        in_specs=[pl.BlockSpec((tm, MEM_WIDTH), lambda i: (i, 0)),
                  pl.BlockSpec((m, MEM_WIDTH), lambda i: (i // nt, 0)),
                  pl.BlockSpec((m, MEM_WIDTH), lambda i: (i // nt, 0))],
        out_specs=pl.BlockSpec((tm, MEM_WIDTH), lambda i: (i, 0)),
        out_shape=jax.ShapeDtypeStruct((n, MEM_WIDTH), BF16),
        compiler_params=_cparams(("parallel",)),
        name="mem_attn",
    )(proj_m, k_m, v_m)


def _merge_kernel(x_ref, nw_ref, oa_ref, ob_ref, oc_ref, wg_ref, wbr_ref, wo_ref, h_ref):
    x = x_ref[...]
    d = x.shape[-1]
    xn = _rms(x, nw_ref[...]).astype(BF16)
    merged = None
    for br, o_br in enumerate((oa_ref, ob_ref, oc_ref)):
        gate = _sigmoid(_dot(xn, wg_ref[:, br * d:(br + 1) * d]))
        term = gate * _dot(o_br[...], wbr_ref[br])
        merged = term if merged is None else merged + term
    h_ref[...] = x + _dot(merged.astype(BF16), wo_ref[...])


def _merge(x, nw, o_a, o_b, o_c, w_gate, w_br, w_out, tm):
    n, d = x.shape
    row = lambda wd: pl.BlockSpec((tm, wd), lambda i: (i, 0))
    full = lambda a: pl.BlockSpec(a.shape, lambda i: (0,) * a.ndim)
    return pl.pallas_call(
        _merge_kernel,
        grid=(n // tm,),
        in_specs=[row(d), full(nw), row(o_a.shape[1]), row(o_b.shape[1]), row(o_c.shape[1]),
                  full(w_gate), full(w_br), full(w_out)],
        out_specs=row(d),
        out_shape=jax.ShapeDtypeStruct((n, d), F32),
        compiler_params=_cparams(("parallel",)),
        name="merge_out",
    )(x, nw, o_a, o_b, o_c, w_gate, w_br, w_out)


def _mlp_kernel(h_ref, nw_ref, wu_ref, wd_ref, fw_ref, o_ref, *, ff_chunk):
    h = h_ref[...]
    m = _rms(h, nw_ref[...]).astype(BF16)
    acc = h
    for c0 in range(0, wu_ref.shape[1], ff_chunk):
        u = jnp.maximum(_dot(m, wu_ref[:, c0:c0 + ff_chunk]), 0.0)
        acc = acc + _dot((u * u).astype(BF16), wd_ref[c0:c0 + ff_chunk, :])
    o_ref[...] = _rms(acc, fw_ref[...])


def _mlp(h, nw, w_up, w_down, fw, tm, ff_chunk=1024):
    n, d = h.shape
    full = lambda a: pl.BlockSpec(a.shape, lambda i: (0,) * a.ndim)
    return pl.pallas_call(
        functools.partial(_mlp_kernel, ff_chunk=ff_chunk),
        grid=(n // tm,),
        in_specs=[pl.BlockSpec((tm, d), lambda i: (i, 0)), full(nw), full(w_up), full(w_down), full(fw)],
        out_specs=pl.BlockSpec((tm, d), lambda i: (i, 0)),
        out_shape=jax.ShapeDtypeStruct((n, d), F32),
        compiler_params=_cparams(("parallel",)),
        name="mlp_final",
    )(h, nw, w_up, w_down, fw)


def _pad_cols(w, width):
    return jnp.pad(w, ((0, 0), (0, width - w.shape[1])))


def _head_pad(w, width):
    return jnp.pad(w, ((0, 0), (0, 0), (0, width - w.shape[2]))).reshape(w.shape[0], -1)


def _rope_tables(seq, d):
    inv = ROPE_THETA ** (-jnp.arange(0, d, 2, dtype=F32) / d)
    ang = jnp.arange(seq, dtype=F32)[:, None] * inv[None, :]
    cos, sin = jnp.cos(ang), jnp.sin(ang)
    cosf = _pad_cols(jnp.concatenate([cos, cos], axis=1), LANES)
    sins = _pad_cols(jnp.concatenate([-sin, sin], axis=1), LANES)
    return cosf, sins


def kernel(x, mem, attn_norm_w, w_in, hgrn_lower_bounds, hgrn_out_norm_w, dsa_q_norm_w, dsa_kv_norm_w,
           dsa_kidx_norm_w, dsa_w_uq, dsa_w_qr, dsa_w_uk, dsa_w_uv, dsa_w_qidx, mem_norm_w, w_mem_kv,
           w_br_hgrn, w_br_dsa, w_br_mem, w_out, mlp_norm_w, w_mlp_up, w_mlp_down, final_norm_w):
    B, T, D = x.shape
    N = B * T
    M = mem.shape[1]
    l = 0
    topk = min(TOPK_MAX, T // 4)
    tm = 512
    x2 = x.reshape(N, D)

    wi = w_in[l]
    o = 0
    cols = {}
    for name, wd in (("hq", HG_WIDTH), ("hf", HG_WIDTH), ("hv", HG_WIDTH), ("hg", HG_WIDTH),
                     ("cq", DSA_Q_LORA), ("ckv", DSA_KV_LORA), ("kr", DSA_ROPE), ("kidx", IDX_DIM),
                     ("widx", IDX_HEADS), ("mq", MEM_WIDTH), ("gate", N_BRANCH * D)):
        cols[name] = wi[:, o:o + wd]
        o += wd
    w_proj = jnp.concatenate(
        [cols["hq"], cols["hf"], cols["hv"], cols["hg"], cols["mq"], cols["cq"], cols["ckv"],
         _pad_cols(cols["kr"], LANES), _pad_cols(cols["kidx"], LANES), _pad_cols(cols["widx"], LANES)],
        axis=1).astype(BF16)
    w_gate = cols["gate"].astype(BF16)

    proj_h, proj_m, proj_d = _norm_proj(x2, attn_norm_w[l], w_proj, (4 * HG_WIDTH, MEM_WIDTH, DSA_GROUP),
                                        (F32, F32, F32), tm)

    lb_all = jnp.cumsum(jax.nn.softmax(hgrn_lower_bounds.astype(F32), axis=0), axis=0)
    o_a = _hgrn(proj_h, lb_all[l], hgrn_out_norm_w[l], B, T)

    wuq_pad = _head_pad(dsa_w_uq[l], LANES).astype(BF16)
    wqr_pad = _head_pad(dsa_w_qr[l], LANES).astype(BF16)
    wqi_pad = _head_pad(dsa_w_qidx[l], LANES).astype(BF16)
    wuk_pad = jnp.pad(dsa_w_uk[l], ((0, 0), (0, LANES - DSA_NOPE), (0, 0))).astype(BF16)
    wuv = dsa_w_uv[l]
    wuv_pad = jnp.stack([jnp.pad(wuv[h], ((0, 0), ((h % 2) * DSA_VDIM, LANES - DSA_VDIM - (h % 2) * DSA_VDIM)))
                         for h in range(DSA_HEADS)]).astype(BF16)
    kinw_pad = _pad_cols(dsa_kidx_norm_w[l].reshape(1, IDX_DIM), LANES)
    tabs = _rope_tables(T, DSA_ROPE) + _rope_tables(T, IDX_DIM)
    qcat, qidx, kv, kidx, widx = _dsa_prep(
        proj_d, dsa_q_norm_w[l].reshape(1, -1), dsa_kv_norm_w[l].reshape(1, -1), kinw_pad,
        wuq_pad, wqr_pad, wuk_pad, wqi_pad, tabs, T, tm)
    o_b = _dsa(qcat, qidx, widx, kv, kidx, wuv_pad, B, T, topk)

    k_m, v_m = _norm_proj(mem.reshape(B * M, D), mem_norm_w[l], w_mem_kv[l].astype(BF16),
                          (MEM_WIDTH, MEM_WIDTH), (BF16, BF16), min(tm, B * M))
    o_c = _mem_attn(proj_m, k_m, v_m, T, tm)

    w_br = jnp.stack([w_br_hgrn[l], w_br_dsa[l], w_br_mem[l]]).astype(BF16)
    h = _merge(x2, attn_norm_w[l].reshape(1, D), o_a, o_b, o_c, w_gate, w_br, w_out[l].astype(BF16), tm)
    out = _mlp(h, mlp_norm_w[l].reshape(1, D), w_mlp_up[l].astype(BF16), w_mlp_down[l].astype(BF16),
               final_norm_w.reshape(1, D), tm)
    return out.reshape(B, T, D)
```

```python
import functools

import jax
import jax.numpy as jnp
import numpy as np
from jax import lax
from jax.experimental import pallas as pl
from jax.experimental.pallas import tpu as pltpu

F32 = jnp.float32
BF16 = jnp.bfloat16
I32 = jnp.int32

EPS = 1e-6
LOG2E = 1.4426950408889634
ROPE_THETA = 10000.0
LANES = 128
SUBLANES = 8
HG_HEADS, HG_D, HG_CHUNK = 4, 128, 64
HG_WIDTH = HG_HEADS * HG_D
HG_SUB = 16
HG_STEP = 256
DSA_HEADS, DSA_Q_LORA, DSA_KV_LORA = 8, 256, 128
DSA_NOPE, DSA_ROPE, DSA_VDIM = 64, 32, 64
IDX_HEADS, IDX_DIM = 8, 64
TOPK_MAX = 256
MEM_HEADS, MEM_DH = 4, 128
MEM_WIDTH = MEM_HEADS * MEM_DH
N_BRANCH = 3
DSA_GROUP = 6 * LANES
VMEM_LIMIT = 56 * 1024 * 1024

DSA_TQ = 128
DSA_TS = 512
DSA_TK = 512
DSA_BISECT = 30


def _cparams(sem):
    return pltpu.CompilerParams(dimension_semantics=sem, vmem_limit_bytes=VMEM_LIMIT)


def _rms(x, w, n=None):
    n = x.shape[-1] if n is None else n
    ms = jnp.sum(x * x, axis=-1, keepdims=True) * (1.0 / n)
    return x * lax.rsqrt(ms + EPS) * w


def _sigmoid(x):
    return 1.0 / (1.0 + jnp.exp(-x))


def _dot(a, b):
    return jnp.dot(a, b, preferred_element_type=F32)


def _dot_nt(a, b):
    return lax.dot_general(a, b, (((1,), (1,)), ((), ())), preferred_element_type=F32)


def _dot_tn(a, b):
    return lax.dot_general(a, b, (((0,), (0,)), ((), ())), preferred_element_type=F32)


def _norm_proj_kernel(x_ref, nw_ref, w_ref, *out_refs):
    xn = _rms(x_ref[...], nw_ref[...]).astype(BF16)
    off = 0
    for o_ref in out_refs:
        wd = o_ref.shape[-1]
        o_ref[...] = _dot(xn, w_ref[:, off:off + wd]).astype(o_ref.dtype)
        off += wd


def _norm_proj(x, nw, w, widths, dtypes, tm):
    n, d = x.shape
    return pl.pallas_call(
        _norm_proj_kernel,
        grid=(n // tm,),
        in_specs=[pl.BlockSpec((tm, d), lambda i: (i, 0)),
                  pl.BlockSpec((1, d), lambda i: (0, 0)),
                  pl.BlockSpec(w.shape, lambda i: (0, 0))],
        out_specs=[pl.BlockSpec((tm, wd), lambda i: (i, 0)) for wd in widths],
        out_shape=[jax.ShapeDtypeStruct((n, wd), dt) for wd, dt in zip(widths, dtypes)],
        compiler_params=_cparams(("parallel",)),
        name="norm_proj",
    )(x, nw.reshape(1, d), w)


def _hgrn_kernel(q_ref, f_ref, v_ref, g_ref, lb_ref, onw_ref, o_ref, st_ref, b_ref, k_ref):
    @pl.when(pl.program_id(1) == 0)
    def _():
        st_ref[...] = jnp.zeros_like(st_ref)

    C, SB = HG_CHUNK, HG_SUB
    lb = lb_ref[...]
    ri = lax.broadcasted_iota(I32, (C, C), 0)
    ci = lax.broadcasted_iota(I32, (C, C), 1)
    tri = jnp.where(ri >= ci, 1.0, 0.0).astype(BF16)
    row_c = lax.broadcasted_iota(I32, (C, HG_D), 0)
    row_1 = lax.broadcasted_iota(I32, (SUBLANES, HG_D), 0)
    lane_1 = lax.broadcasted_iota(I32, (SUBLANES, C), 1)

    def chunk(ci_, carry):
        rows = pl.ds(pl.multiple_of(ci_ * C, C), C)
        f = lb + (1.0 - lb) * _sigmoid(f_ref[rows, :])
        lf = jnp.log(f) * LOG2E
        hi = lf.astype(BF16)
        r1 = lf - hi.astype(F32)
        mid = r1.astype(BF16)
        lo = (r1 - mid.astype(F32)).astype(BF16)
        b_ref[...] = _dot(tri, hi) + _dot(tri, mid) + _dot(tri, lo)
        k_ref[...] = 1.0 - f

        for h in range(HG_HEADS):
            sl = slice(h * HG_D, (h + 1) * HG_D)
            qh = q_ref[rows, sl]
            vh = v_ref[rows, sl]
            bh = b_ref[:, sl]
            kh = k_ref[:, sl]
            a_rows = [jnp.zeros((SB, C), F32)]
            for i in range(1, C // SB):
                bref = b_ref[i * SB:i * SB + 1, sl]
                qi = (qh[i * SB:(i + 1) * SB] * jnp.exp2(bh[i * SB:(i + 1) * SB] - bref)).astype(BF16)
                ki = (kh * jnp.exp2(jnp.where(row_c < i * SB, bref - bh, -jnp.inf))).astype(BF16)
                a_rows.append(_dot_nt(qi, ki))
            for i in range(C // SB):
                b_top, q_top = bh[i * SB:i * SB + SUBLANES], qh[i * SB:i * SB + SUBLANES]
                b_bot, q_bot = bh[i * SB + SUBLANES:(i + 1) * SB], qh[i * SB + SUBLANES:(i + 1) * SB]
                a_top, a_bot = a_rows[i][:SUBLANES], a_rows[i][SUBLANES:]
                for j in range(SB):
                    s = i * SB + j
                    brow = b_ref[s:s + 1, sl]
                    krow = k_ref[s:s + 1, sl]
                    if j < SUBLANES:
                        e = jnp.exp2(jnp.where(row_1 >= j, b_top - brow, -jnp.inf))
                        r = jnp.sum(e * (q_top * krow), axis=-1, keepdims=True)
                        a_top = jnp.where(lane_1 == s, r, a_top)
                        e = jnp.exp2(b_bot - brow)
                    else:
                        e = jnp.exp2(jnp.where(row_1 >= j - SUBLANES, b_bot - brow, -jnp.inf))
                    r = jnp.sum(e * (q_bot * krow), axis=-1, keepdims=True)
                    a_bot = jnp.where(lane_1 == s, r, a_bot)
                a_rows[i] = jnp.concatenate([a_top, a_bot], axis=0)
            a_mat = jnp.concatenate(a_rows, axis=0)
            st = st_ref[h]
            qe = (qh * jnp.exp2(bh)).astype(BF16)
            vb = vh.astype(BF16)
            o = _dot_nt(qe, st.astype(BF16)) + _dot(a_mat.astype(BF16), vb)
            blast = bh[C - 1:C, :]
            kd = (kh * jnp.exp2(blast - bh)).astype(BF16)
            st_ref[h] = st * jnp.exp2(blast) + _dot_tn(vb, kd)
            gh = g_ref[rows, sl]
            o_ref[rows, sl] = (_rms(o, onw_ref[:, sl]) * (gh * _sigmoid(gh))).astype(o_ref.dtype)
        return carry

    lax.fori_loop(0, HG_STEP // C, chunk, 0, unroll=True)


def _hgrn(proj_h, lb, onw, batch, seq):
    n = proj_h.shape[0]
    nc = seq // HG_STEP
    blk = lambda col: pl.BlockSpec((HG_STEP, HG_WIDTH), lambda b, c, col=col: (b * nc + c, col))
    vec = pl.BlockSpec((1, HG_WIDTH), lambda b, c: (0, 0))
    return pl.pallas_call(
        _hgrn_kernel,
        grid=(batch, nc),
        in_specs=[blk(0), blk(1), blk(2), blk(3), vec, vec],
        out_specs=pl.BlockSpec((HG_STEP, HG_WIDTH), lambda b, c: (b * nc + c, 0)),
        out_shape=jax.ShapeDtypeStruct((n, HG_WIDTH), BF16),
        scratch_shapes=[pltpu.VMEM((HG_HEADS, HG_D, HG_D), F32),
                        pltpu.VMEM((HG_CHUNK, HG_WIDTH), F32),
                        pltpu.VMEM((HG_CHUNK, HG_WIDTH), F32)],
        compiler_params=_cparams(("parallel", "arbitrary")),
        name="hgrn2",
    )(proj_h, proj_h, proj_h, proj_h, lb.reshape(1, HG_WIDTH), onw.reshape(1, HG_WIDTH))


def _rope_lanes(x, cosf, sins, half):
    nl = x.shape[-1]
    lane = lax.broadcasted_iota(I32, x.shape, 1)
    partner = jnp.where(lane % LANES < half, pltpu.roll(x, nl - half, 1), pltpu.roll(x, half, 1))
    return x * cosf + partner * sins


def _dsa_prep_kernel(pd_ref, qnw_ref, kvnw_ref, kinw_ref, wuq_ref, wqr_ref, wuk_ref, wqi_ref,
                     c32_ref, s32_ref, c64_ref, s64_ref,
                     qcat_ref, qidx_ref, kv_ref, kidx_ref, widx_ref):
    scale = float((DSA_NOPE + DSA_ROPE) ** -0.5)
    c32, s32, c64, s64 = c32_ref[...], s32_ref[...], c64_ref[...], s64_ref[...]
    cqn = _rms(pd_ref[:, 0:DSA_Q_LORA], qnw_ref[...]).astype(BF16)
    qn = _dot(cqn, wuq_ref[...]).astype(BF16)
    qr = _dot(cqn, wqr_ref[...])
    qi = _dot(cqn, wqi_ref[...])
    for h in range(DSA_HEADS):
        sl = slice(h * LANES, (h + 1) * LANES)
        qcat_ref[h, :, 0:LANES] = (_dot(qn[:, sl], wuk_ref[h]) * scale).astype(BF16)
        qcat_ref[h, :, LANES:2 * LANES] = (_rope_lanes(qr[:, sl], c32, s32, DSA_ROPE // 2) * scale).astype(BF16)
        qidx_ref[h] = (_rope_lanes(qi[:, sl], c64, s64, IDX_DIM // 2) * float(IDX_DIM ** -0.5)).astype(BF16)
    kv_ref[:, 0:LANES] = _rms(pd_ref[:, 256:384], kvnw_ref[...]).astype(BF16)
    kv_ref[:, LANES:2 * LANES] = _rope_lanes(pd_ref[:, 384:512], c32, s32, DSA_ROPE // 2).astype(BF16)
    kin = _rms(pd_ref[:, 512:640], kinw_ref[...], n=IDX_DIM)
    kidx_ref[...] = _rope_lanes(kin, c64, s64, IDX_DIM // 2).astype(BF16)
    widx_ref[...] = jnp.transpose(pd_ref[:, 640:768] * float(IDX_HEADS ** -0.5))[0:IDX_HEADS, :]


def _dsa_prep(proj_d, qnw, kvnw, kinw_pad, wuq_pad, wqr_pad, wuk_pad, wqi_pad, tabs, seq, tm):
    n = proj_d.shape[0]
    nt = seq // tm
    full = lambda a: pl.BlockSpec(a.shape, lambda i: (0,) * a.ndim)
    tab = pl.BlockSpec((tm, LANES), lambda i: (i % nt, 0))
    hd = DSA_HEADS
    return pl.pallas_call(
        _dsa_prep_kernel,
        grid=(n // tm,),
        in_specs=[pl.BlockSpec((tm, DSA_GROUP), lambda i: (i, 0)),
                  full(qnw), full(kvnw), full(kinw_pad), full(wuq_pad), full(wqr_pad), full(wuk_pad),
                  full(wqi_pad), tab, tab, tab, tab],
        out_specs=[pl.BlockSpec((hd, tm, 2 * LANES), lambda i: (0, i, 0)),
                   pl.BlockSpec((hd, tm, LANES), lambda i: (0, i, 0)),
                   pl.BlockSpec((tm, 2 * LANES), lambda i: (i, 0)),
                   pl.BlockSpec((tm, LANES), lambda i: (i, 0)),
                   pl.BlockSpec((IDX_HEADS, tm), lambda i: (0, i))],
        out_shape=[jax.ShapeDtypeStruct((hd, n, 2 * LANES), BF16),
                   jax.ShapeDtypeStruct((hd, n, LANES), BF16),
                   jax.ShapeDtypeStruct((n, 2 * LANES), BF16),
                   jax.ShapeDtypeStruct((n, LANES), BF16),
                   jax.ShapeDtypeStruct((IDX_HEADS, n), F32)],
        compiler_params=_cparams(("parallel",)),
        name="dsa_prep",
    )(proj_d, qnw, kvnw, kinw_pad, wuq_pad, wqr_pad, wuk_pad, wqi_pad, *tabs)


def _dsa_kernel(qcat_ref, qidx_ref, widx_ref, kv_ref, kidx_ref, wuv_ref, o_ref,
                sc_ref, x_ref, lg_ref, m_ref, mb_ref, acc_ref, *, topk):
    TQ, TK, H, TS = DSA_TQ, DSA_TK, DSA_HEADS, DSA_TS
    assert TS == TK
    NR = TS // TQ
    step = pl.program_id(1)
    FULL = (TK,) * NR
    DIAG = tuple((r + 1) * TQ for r in range(NR))

    def over_tiles(fn, init):
        carry = lax.fori_loop(0, step, lambda kt, c: fn(kt, FULL, c), init)
        return fn(step, DIAG, carry)

    def key_rows(kt, nk):
        return pl.ds(pl.multiple_of(kt * TK, TK), nk)

    def cols(r):
        return slice(r * TQ, (r + 1) * TQ)

    def as_i32(i):
        return jnp.asarray(i, dtype=I32)

    one, zero = jnp.float32(1.0), jnp.float32(0.0)
    ninf, pinf = jnp.float32(-jnp.inf), jnp.float32(jnp.inf)

    def fold(x, op):
        return op(x.reshape(x.shape[0] // SUBLANES, SUBLANES, x.shape[1]), axis=0)

    def score_tile(kt, nks, carry):
        mx, mn = list(carry[0]), list(carry[1])
        for r, nk in enumerate(nks):
            qidx = qidx_ref[:, cols(r), :].reshape(H * TQ, LANES)
            s_all = jnp.maximum(_dot_nt(kidx_ref[key_rows(kt, nk), :], qidx), 0.0)
            score = s_all[:, 0:TQ] * widx_ref[0:1, cols(r)]
            for h in range(1, IDX_HEADS):
                score = score + s_all[:, h * TQ:(h + 1) * TQ] * widx_ref[h:h + 1, cols(r)]
            lo_src = score
            if nks is DIAG:
                causal = (lax.broadcasted_iota(I32, (nk, TQ), 0)
                          <= r * TQ + lax.broadcasted_iota(I32, (nk, TQ), 1))
                score = jnp.where(causal, score, ninf)
                lo_src = jnp.where(causal, score, pinf)
            x_ref[kt, 0:nk, cols(r)] = score
            mx[r] = jnp.maximum(mx[r], fold(score, jnp.max))
            mn[r] = jnp.minimum(mn[r], fold(lo_src, jnp.min))
        return tuple(mx), tuple(mn)

    mx, mn = over_tiles(score_tile, (tuple(jnp.full((SUBLANES, TQ), ninf, F32) for _ in range(NR)),
                                     tuple(jnp.full((SUBLANES, TQ), pinf, F32) for _ in range(NR))))
    x_max = jnp.max(jnp.concatenate(mx, axis=1), axis=0, keepdims=True)
    x_min = jnp.min(jnp.concatenate(mn, axis=1), axis=0, keepdims=True)

    def count(pred_fn):
        def block(kt, nk, c, acc):
            pos = kt * TK + lax.broadcasted_iota(I32, (nk, acc.shape[1]), 0)
            return acc + fold(jnp.where(pred_fn(x_ref[kt, 0:nk, c], pos, c), one, zero), jnp.sum)

        def tile(kt, nks, acc):
            if nks is FULL:
                return block(kt, TK, slice(0, TS), acc)
            return jnp.concatenate([block(kt, nk, cols(r), acc[:, cols(r)]) for r, nk in enumerate(nks)], axis=1)
        return jnp.sum(over_tiles(tile, jnp.zeros((SUBLANES, TS), F32)), axis=0, keepdims=True)

    kf = float(topk)
    n_causal = (step * TS + 1 + lax.broadcasted_iota(I32, (1, TS), 1)).astype(F32)

    def bisect(_, carry):
        lo, hi, c_lo, c_hi = carry
        mid = lo + (hi - lo) * 0.5
        c = count(lambda x, pos, cs: x >= mid[:, cs])
        up = c >= kf
        return (jnp.where(up, mid, lo), jnp.where(up, hi, mid), jnp.where(up, c, c_lo), jnp.where(up, c_hi, c))

    lo, hi, c_lo, c_hi = lax.fori_loop(
        0, DSA_BISECT, bisect, (x_min, x_max + (x_max - x_min) + 1.0, n_causal, jnp.zeros((1, TS), F32)))
    need = kf - c_hi

    has_tie = jnp.max(jnp.where(c_lo > kf, one, zero)) > zero
    nbits = int(np.log2(TK)) + 3

    def tie_search(_):
        def idx_step(i, j):
            cand = j | lax.shift_left(np.int32(1), np.int32(nbits - 1) - as_i32(i))
            g = count(lambda x, pos, cs: (x >= lo[:, cs]) & (x < hi[:, cs]) & (pos < cand[:, cs]))
            return jnp.where(g < need, cand, j)
        return lax.fori_loop(0, nbits, idx_step, jnp.zeros((1, TS), I32))

    jcut = lax.cond(has_tie, tie_search, lambda _: jnp.full((1, TS), 2**30, I32), 0)

    def bias_tile(kt, nks, carry):
        for r, nk in enumerate(nks):
            x = x_ref[kt, 0:nk, cols(r)]
            pos = kt * TK + lax.broadcasted_iota(I32, (nk, TQ), 0)
            sel = (x >= lo[:, cols(r)]) & ((x >= hi[:, cols(r)]) | (pos <= jcut[:, cols(r)]))
            sc_ref[kt, cols(r), 0:nk] = jnp.transpose(jnp.where(sel, zero, ninf))
        return carry

    over_tiles(bias_tile, 0)

    def pass_a(r, kt, nk):
        qcat = qcat_ref[:, cols(r), :].reshape(H * TQ, 2 * LANES)
        lg = _dot_nt(qcat, kv_ref[key_rows(kt, nk), :]).reshape(H, TQ, nk) + sc_ref[kt, cols(r), 0:nk][None]
        lg = lg.reshape(H * TQ, nk)
        lg_ref[r % 2, kt, :, 0:nk] = lg
        part = lg[:, 0:LANES]
        for g in range(1, nk // LANES):
            part = jnp.maximum(part, lg[:, g * LANES:(g + 1) * LANES])
        m_ref[r % 2] = jnp.maximum(m_ref[r % 2], part)

    def pass_b(r, kt, nk):
        m_b = mb_ref[...]
        p = jnp.concatenate(
            [jnp.exp((lg_ref[r % 2, kt, :, g * LANES:(g + 1) * LANES] - m_b).astype(BF16))
             for g in range(nk // LANES)], axis=1)
        kv1 = jnp.concatenate([kv_ref[key_rows(kt, nk), 0:LANES], jnp.ones((nk, LANES), BF16)], axis=1)
        acc_ref[...] += _dot(p, kv1)

    def run(fns):
        def tile(kt, nks, carry):
            for fn, r in fns:
                fn(r, kt, nks[r])
            return carry
        over_tiles(tile, 0)

    m_ref[0] = jnp.full((H * TQ, LANES), -jnp.inf, F32)
    run([(pass_a, 0)])
    for r in range(NR):
        rows = cols(r)
        mb_ref[...] = jnp.broadcast_to(jnp.max(m_ref[r % 2], axis=-1, keepdims=True), (H * TQ, LANES))
        acc_ref[...] = jnp.zeros_like(acc_ref)
        if r + 1 < NR:
            m_ref[(r + 1) % 2] = jnp.full((H * TQ, LANES), -jnp.inf, F32)
            run([(pass_a, r + 1), (pass_b, r)])
        else:
            run([(pass_b, r)])
        o_lat = (acc_ref[:, 0:LANES] / acc_ref[:, LANES:2 * LANES]).astype(BF16)
        for pr in range(H // 2):
            h0, h1 = 2 * pr, 2 * pr + 1
            o_ref[rows, pr * LANES:(pr + 1) * LANES] = (
                _dot(o_lat[h0 * TQ:(h0 + 1) * TQ], wuv_ref[h0]) + _dot(o_lat[h1 * TQ:(h1 + 1) * TQ], wuv_ref[h1])
            ).astype(o_ref.dtype)


def _dsa(qcat, qidx, widx, kv, kidx, wuv_pad, batch, seq, topk):
    n = kv.shape[0]
    ns = seq // DSA_TS
    nkt_max = seq // DSA_TK
    H = DSA_HEADS
    return pl.pallas_call(
        functools.partial(_dsa_kernel, topk=topk),
        grid=(batch, ns),
        in_specs=[pl.BlockSpec((H, DSA_TS, 2 * LANES), lambda b, s: (0, b * ns + s, 0)),
                  pl.BlockSpec((H, DSA_TS, LANES), lambda b, s: (0, b * ns + s, 0)),
                  pl.BlockSpec((IDX_HEADS, DSA_TS), lambda b, s: (0, b * ns + s)),
                  pl.BlockSpec((seq, 2 * LANES), lambda b, s: (b, 0)),
                  pl.BlockSpec((seq, LANES), lambda b, s: (b, 0)),
                  pl.BlockSpec(wuv_pad.shape, lambda b, s: (0, 0, 0))],
        out_specs=pl.BlockSpec((DSA_TS, H * DSA_VDIM), lambda b, s: (b * ns + s, 0)),
        out_shape=jax.ShapeDtypeStruct((n, H * DSA_VDIM), BF16),
        scratch_shapes=[pltpu.VMEM((nkt_max, DSA_TS, DSA_TK), F32),
                        pltpu.VMEM((nkt_max, DSA_TK, DSA_TS), F32),
                        pltpu.VMEM((2, nkt_max, H * DSA_TQ, DSA_TK), F32),
                        pltpu.VMEM((2, H * DSA_TQ, LANES), F32),
                        pltpu.VMEM((H * DSA_TQ, LANES), F32),
                        pltpu.VMEM((H * DSA_TQ, 2 * LANES), F32)],
        compiler_params=_cparams(("parallel", "arbitrary")),
        name="dsa_attn",
    )(qcat, qidx, widx, kv, kidx, wuv_pad)


def _mem_attn_kernel(q_ref, k_ref, v_ref, o_ref):
    scale = float(MEM_DH ** -0.5)
    for h in range(MEM_HEADS):
        sl = slice(h * MEM_DH, (h + 1) * MEM_DH)
        lg = _dot_nt((q_ref[:, sl] * scale).astype(BF16), k_ref[:, sl])
        p = jnp.exp(lg - jnp.max(lg, axis=-1, keepdims=True))
        o = _dot(p.astype(BF16), v_ref[:, sl]) / jnp.sum(p, axis=-1, keepdims=True)
        o_ref[:, sl] = o.astype(o_ref.dtype)


def _mem_attn(proj_m, k_m, v_m, seq, tm):
    n = proj_m.shape[0]
    nt = seq // tm
    m = k_m.shape[0] // (n // seq)
    return pl.pallas_call(
        _mem_attn_kernel,
        grid=(n // tm,),
        in_specs=[pl.BlockSpec((tm, MEM_WIDTH), lambda i: (i, 0)),
                  pl.BlockSpec((m, MEM_WIDTH), lambda i: (i // nt, 0)),
                  pl.BlockSpec((m, MEM_WIDTH), lambda i: (i // nt, 0))],
        out_specs=pl.BlockSpec((tm, MEM_WIDTH), lambda i: (i, 0)),
        out_shape=jax.ShapeDtypeStruct((n, MEM_WIDTH), BF16),
        compiler_params=_cparams(("parallel",)),
        name="mem_attn",
    )(proj_m, k_m, v_m)


def _merge_kernel(x_ref, nw_ref, oa_ref, ob_ref, oc_ref, wg_ref, wbr_ref, wo_ref, h_ref):
    x = x_ref[...]
    d = x.shape[-1]
    xn = _rms(x, nw_ref[...]).astype(BF16)
    merged = None
    for br, o_br in enumerate((oa_ref, ob_ref, oc_ref)):
        gate = _sigmoid(_dot(xn, wg_ref[:, br * d:(br + 1) * d]))
        term = gate * _dot(o_br[...], wbr_ref[br])
        merged = term if merged is None else merged + term
    h_ref[...] = x + _dot(merged.astype(BF16), wo_ref[...])


def _merge(x, nw, o_a, o_b, o_c, w_gate, w_br, w_out, tm):
    n, d = x.shape
    row = lambda wd: pl.BlockSpec((tm, wd), lambda i: (i, 0))
    full = lambda a: pl.BlockSpec(a.shape, lambda i: (0,) * a.ndim)
    return pl.pallas_call(
        _merge_kernel,
        grid=(n // tm,),
        in_specs=[row(d), full(nw), row(o_a.shape[1]), row(o_b.shape[1]), row(o_c.shape[1]),
                  full(w_gate), full(w_br), full(w_out)],
        out_specs=row(d),
        out_shape=jax.ShapeDtypeStruct((n, d), F32),
        compiler_params=_cparams(("parallel",)),
        name="merge_out",
    )(x, nw, o_a, o_b, o_c, w_gate, w_br, w_out)


def _mlp_kernel(h_ref, nw_ref, wu_ref, wd_ref, fw_ref, o_ref, *, ff_chunk):
    h = h_ref[...]
    m = _rms(h, nw_ref[...]).astype(BF16)
    acc = h
    for c0 in range(0, wu_ref.shape[1], ff_chunk):
        u = jnp.maximum(_dot(m, wu_ref[:, c0:c0 + ff_chunk]), 0.0)
        acc = acc + _dot((u * u).astype(BF16), wd_ref[c0:c0 + ff_chunk, :])
    o_ref[...] = _rms(acc, fw_ref[...])


def _mlp(h, nw, w_up, w_down, fw, tm, ff_chunk=1024):
    n, d = h.shape
    full = lambda a: pl.BlockSpec(a.shape, lambda i: (0,) * a.ndim)
    return pl.pallas_call(
        functools.partial(_mlp_kernel, ff_chunk=ff_chunk),
        grid=(n // tm,),
        in_specs=[pl.BlockSpec((tm, d), lambda i: (i, 0)), full(nw), full(w_up), full(w_down), full(fw)],
        out_specs=pl.BlockSpec((tm, d), lambda i: (i, 0)),
        out_shape=jax.ShapeDtypeStruct((n, d), F32),
        compiler_params=_cparams(("parallel",)),
        name="mlp_final",
    )(h, nw, w_up, w_down, fw)


def _pad_cols(w, width):
    return jnp.pad(w, ((0, 0), (0, width - w.shape[1])))


def _head_pad(w, width):
    return jnp.pad(w, ((0, 0), (0, 0), (0, width - w.shape[2]))).reshape(w.shape[0], -1)


def _rope_tables(seq, d):
    inv = ROPE_THETA ** (-jnp.arange(0, d, 2, dtype=F32) / d)
    ang = jnp.arange(seq, dtype=F32)[:, None] * inv[None, :]
    cos, sin = jnp.cos(ang), jnp.sin(ang)
    cosf = _pad_cols(jnp.concatenate([cos, cos], axis=1), LANES)
    sins = _pad_cols(jnp.concatenate([-sin, sin], axis=1), LANES)
    return cosf, sins


def kernel(x, mem, attn_norm_w, w_in, hgrn_lower_bounds, hgrn_out_norm_w, dsa_q_norm_w, dsa_kv_norm_w,
           dsa_kidx_norm_w, dsa_w_uq, dsa_w_qr, dsa_w_uk, dsa_w_uv, dsa_w_qidx, mem_norm_w, w_mem_kv,
           w_br_hgrn, w_br_dsa, w_br_mem, w_out, mlp_norm_w, w_mlp_up, w_mlp_down, final_norm_w):
    B, T, D = x.shape
    N = B * T
    M = mem.shape[1]
    l = 0
    topk = min(TOPK_MAX, T // 4)
    tm = 512
    x2 = x.reshape(N, D)

    wi = w_in[l]
    o = 0
    cols = {}
    for name, wd in (("hq", HG_WIDTH), ("hf", HG_WIDTH), ("hv", HG_WIDTH), ("hg", HG_WIDTH),
                     ("cq", DSA_Q_LORA), ("ckv", DSA_KV_LORA), ("kr", DSA_ROPE), ("kidx", IDX_DIM),
                     ("widx", IDX_HEADS), ("mq", MEM_WIDTH), ("gate", N_BRANCH * D)):
        cols[name] = wi[:, o:o + wd]
        o += wd
    w_proj = jnp.concatenate(
        [cols["hq"], cols["hf"], cols["hv"], cols["hg"], cols["mq"], cols["cq"], cols["ckv"],
         _pad_cols(cols["kr"], LANES), _pad_cols(cols["kidx"], LANES), _pad_cols(cols["widx"], LANES)],
        axis=1).astype(BF16)
    w_gate = cols["gate"].astype(BF16)

    proj_h, proj_m, proj_d = _norm_proj(x2, attn_norm_w[l], w_proj, (4 * HG_WIDTH, MEM_WIDTH, DSA_GROUP),
                                        (F32, F32, F32), tm)

    lb_all = jnp.cumsum(jax.nn.softmax(hgrn_lower_bounds.astype(F32), axis=0), axis=0)
    o_a = _hgrn(proj_h, lb_all[l], hgrn_out_norm_w[l], B, T)

    wuq_pad = _head_pad(dsa_w_uq[l], LANES).astype(BF16)
    wqr_pad = _head_pad(dsa_w_qr[l], LANES).astype(BF16)
    wqi_pad = _head_pad(dsa_w_qidx[l], LANES).astype(BF16)
    wuk_pad = jnp.pad(dsa_w_uk[l], ((0, 0), (0, LANES - DSA_NOPE), (0, 0))).astype(BF16)
    wuv = dsa_w_uv[l]
    wuv_pad = jnp.stack([jnp.pad(wuv[h], ((0, 0), ((h % 2) * DSA_VDIM, LANES - DSA_VDIM - (h % 2) * DSA_VDIM)))
                         for h in range(DSA_HEADS)]).astype(BF16)
    kinw_pad = _pad_cols(dsa_kidx_norm_w[l].reshape(1, IDX_DIM), LANES)
    tabs = _rope_tables(T, DSA_ROPE) + _rope_tables(T, IDX_DIM)
    qcat, qidx, kv, kidx, widx = _dsa_prep(
        proj_d, dsa_q_norm_w[l].reshape(1, -1), dsa_kv_norm_w[l].reshape(1, -1), kinw_pad,
        wuq_pad, wqr_pad, wuk_pad, wqi_pad, tabs, T, tm)
    o_b = _dsa(qcat, qidx, widx, kv, kidx, wuv_pad, B, T, topk)

    k_m, v_m = _norm_proj(mem.reshape(B * M, D), mem_norm_w[l], w_mem_kv[l].astype(BF16),
                          (MEM_WIDTH, MEM_WIDTH), (BF16, BF16), min(tm, B * M))
    o_c = _mem_attn(proj_m, k_m, v_m, T, tm)

    w_br = jnp.stack([w_br_hgrn[l], w_br_dsa[l], w_br_mem[l]]).astype(BF16)
    h = _merge(x2, attn_norm_w[l].reshape(1, D), o_a, o_b, o_c, w_gate, w_br, w_out[l].astype(BF16), tm)
    out = _mlp(h, mlp_norm_w[l].reshape(1, D), w_mlp_up[l].astype(BF16), w_mlp_down[l].astype(BF16),
               final_norm_w.reshape(1, D), tm)
    return out.reshape(B, T, D)
```

```python
import functools

import jax
import jax.numpy as jnp
import numpy as np
from jax import lax
from jax.experimental import pallas as pl
from jax.experimental.pallas import tpu as pltpu

F32 = jnp.float32
BF16 = jnp.bfloat16
I32 = jnp.int32

EPS = 1e-6
LOG2E = 1.4426950408889634
ROPE_THETA = 10000.0
LANES = 128
SUBLANES = 8
HG_HEADS, HG_D, HG_CHUNK = 4, 128, 64
HG_WIDTH = HG_HEADS * HG_D
HG_SUB = 16
HG_STEP = 256
DSA_HEADS, DSA_Q_LORA, DSA_KV_LORA = 8, 256, 128
DSA_NOPE, DSA_ROPE, DSA_VDIM = 64, 32, 64
IDX_HEADS, IDX_DIM = 8, 64
TOPK_MAX = 256
MEM_HEADS, MEM_DH = 4, 128
MEM_WIDTH = MEM_HEADS * MEM_DH
N_BRANCH = 3
DSA_GROUP = 6 * LANES
VMEM_LIMIT = 56 * 1024 * 1024

DSA_TQ = 128
DSA_TS = 512
DSA_TK = 512
DSA_BISECT = 30


def _cparams(sem):
    return pltpu.CompilerParams(dimension_semantics=sem, vmem_limit_bytes=VMEM_LIMIT)


def _rms(x, w, n=None):
    n = x.shape[-1] if n is None else n
    ms = jnp.sum(x * x, axis=-1, keepdims=True) * (1.0 / n)
    return x * lax.rsqrt(ms + EPS) * w


def _sigmoid(x):
    return 1.0 / (1.0 + jnp.exp(-x))


def _dot(a, b):
    return jnp.dot(a, b, preferred_element_type=F32)


def _dot_nt(a, b):
    return lax.dot_general(a, b, (((1,), (1,)), ((), ())), preferred_element_type=F32)


def _dot_tn(a, b):
    return lax.dot_general(a, b, (((0,), (0,)), ((), ())), preferred_element_type=F32)


def _norm_proj_kernel(x_ref, nw_ref, w_ref, *out_refs):
    xn = _rms(x_ref[...], nw_ref[...]).astype(BF16)
    off = 0
    for o_ref in out_refs:
        wd = o_ref.shape[-1]
        o_ref[...] = _dot(xn, w_ref[:, off:off + wd]).astype(o_ref.dtype)
        off += wd


def _norm_proj(x, nw, w, widths, dtypes, tm):
    n, d = x.shape
    return pl.pallas_call(
        _norm_proj_kernel,
        grid=(n // tm,),
        in_specs=[pl.BlockSpec((tm, d), lambda i: (i, 0)),
                  pl.BlockSpec((1, d), lambda i: (0, 0)),
                  pl.BlockSpec(w.shape, lambda i: (0, 0))],
        out_specs=[pl.BlockSpec((tm, wd), lambda i: (i, 0)) for wd in widths],
        out_shape=[jax.ShapeDtypeStruct((n, wd), dt) for wd, dt in zip(widths, dtypes)],
        compiler_params=_cparams(("parallel",)),
        name="norm_proj",
    )(x, nw.reshape(1, d), w)


def _hgrn_kernel(q_ref, f_ref, v_ref, g_ref, lb_ref, onw_ref, o_ref, st_ref, b_ref, k_ref):
    @pl.when(pl.program_id(1) == 0)
    def _():
        st_ref[...] = jnp.zeros_like(st_ref)

    C, SB = HG_CHUNK, HG_SUB
    lb = lb_ref[...]
    ri = lax.broadcasted_iota(I32, (C, C), 0)
    ci = lax.broadcasted_iota(I32, (C, C), 1)
    tri = jnp.where(ri >= ci, 1.0, 0.0).astype(BF16)
    row_c = lax.broadcasted_iota(I32, (C, HG_D), 0)
    row_1 = lax.broadcasted_iota(I32, (SUBLANES, HG_D), 0)
    lane_1 = lax.broadcasted_iota(I32, (SUBLANES, C), 1)

    def chunk(ci_, carry):
        rows = pl.ds(pl.multiple_of(ci_ * C, C), C)
        f = lb + (1.0 - lb) * _sigmoid(f_ref[rows, :])
        lf = jnp.log(f) * LOG2E
        hi = lf.astype(BF16)
        r1 = lf - hi.astype(F32)
        mid = r1.astype(BF16)
        lo = (r1 - mid.astype(F32)).astype(BF16)
        b_ref[...] = _dot(tri, hi) + _dot(tri, mid) + _dot(tri, lo)
        k_ref[...] = 1.0 - f

        for h in range(HG_HEADS):
            sl = slice(h * HG_D, (h + 1) * HG_D)
            qh = q_ref[rows, sl]
            vh = v_ref[rows, sl]
            bh = b_ref[:, sl]
            kh = k_ref[:, sl]
            a_rows = [jnp.zeros((SB, C), F32)]
            for i in range(1, C // SB):
                bref = b_ref[i * SB:i * SB + 1, sl]
                qi = (qh[i * SB:(i + 1) * SB] * jnp.exp2(bh[i * SB:(i + 1) * SB] - bref)).astype(BF16)
                ki = (kh * jnp.exp2(jnp.where(row_c < i * SB, bref - bh, -jnp.inf))).astype(BF16)
                a_rows.append(_dot_nt(qi, ki))
            for i in range(C // SB):
                b_top, q_top = bh[i * SB:i * SB + SUBLANES], qh[i * SB:i * SB + SUBLANES]
                b_bot, q_bot = bh[i * SB + SUBLANES:(i + 1) * SB], qh[i * SB + SUBLANES:(i + 1) * SB]
                a_top, a_bot = a_rows[i][:SUBLANES], a_rows[i][SUBLANES:]
                for j in range(SB):
                    s = i * SB + j
                    brow = b_ref[s:s + 1, sl]
                    krow = k_ref[s:s + 1, sl]
                    if j < SUBLANES:
                        e = jnp.exp2(jnp.where(row_1 >= j, b_top - brow, -jnp.inf))
                        r = jnp.sum(e * (q_top * krow), axis=-1, keepdims=True)
                        a_top = jnp.where(lane_1 == s, r, a_top)
                        e = jnp.exp2(b_bot - brow)
                    else:
                        e = jnp.exp2(jnp.where(row_1 >= j - SUBLANES, b_bot - brow, -jnp.inf))
                    r = jnp.sum(e * (q_bot * krow), axis=-1, keepdims=True)
                    a_bot = jnp.where(lane_1 == s, r, a_bot)
                a_rows[i] = jnp.concatenate([a_top, a_bot], axis=0)
            a_mat = jnp.concatenate(a_rows, axis=0)
            st = st_ref[h]
            qe = (qh * jnp.exp2(bh)).astype(BF16)
            vb = vh.astype(BF16)
            o = _dot_nt(qe, st.astype(BF16)) + _dot(a_mat.astype(BF16), vb)
            blast = bh[C - 1:C, :]
            kd = (kh * jnp.exp2(blast - bh)).astype(BF16)
            st_ref[h] = st * jnp.exp2(blast) + _dot_tn(vb, kd)
            gh = g_ref[rows, sl]
            o_ref[rows, sl] = (_rms(o, onw_ref[:, sl]) * (gh * _sigmoid(gh))).astype(o_ref.dtype)
        return carry

    lax.fori_loop(0, HG_STEP // C, chunk, 0, unroll=True)


def _hgrn(proj_h, lb, onw, batch, seq):
    n = proj_h.shape[0]
    nc = seq // HG_STEP
    blk = lambda col: pl.BlockSpec((HG_STEP, HG_WIDTH), lambda b, c, col=col: (b * nc + c, col))
    vec = pl.BlockSpec((1, HG_WIDTH), lambda b, c: (0, 0))
    return pl.pallas_call(
        _hgrn_kernel,
        grid=(batch, nc),
        in_specs=[blk(0), blk(1), blk(2), blk(3), vec, vec],
        out_specs=pl.BlockSpec((HG_STEP, HG_WIDTH), lambda b, c: (b * nc + c, 0)),
        out_shape=jax.ShapeDtypeStruct((n, HG_WIDTH), BF16),
        scratch_shapes=[pltpu.VMEM((HG_HEADS, HG_D, HG_D), F32),
                        pltpu.VMEM((HG_CHUNK, HG_WIDTH), F32),
                        pltpu.VMEM((HG_CHUNK, HG_WIDTH), F32)],
        compiler_params=_cparams(("parallel", "arbitrary")),
        name="hgrn2",
    )(proj_h, proj_h, proj_h, proj_h, lb.reshape(1, HG_WIDTH), onw.reshape(1, HG_WIDTH))


def _rope_lanes(x, cosf, sins, half):
    nl = x.shape[-1]
    lane = lax.broadcasted_iota(I32, x.shape, 1)
    partner = jnp.where(lane % LANES < half, pltpu.roll(x, nl - half, 1), pltpu.roll(x, half, 1))
    return x * cosf + partner * sins


def _dsa_prep_kernel(pd_ref, qnw_ref, kvnw_ref, kinw_ref, wuq_ref, wqr_ref, wuk_ref, wqi_ref,
                     c32_ref, s32_ref, c64_ref, s64_ref,
                     qcat_ref, qidx_ref, kv_ref, kidx_ref, widx_ref):
    scale = float((DSA_NOPE + DSA_ROPE) ** -0.5)
    c32, s32, c64, s64 = c32_ref[...], s32_ref[...], c64_ref[...], s64_ref[...]
    cqn = _rms(pd_ref[:, 0:DSA_Q_LORA], qnw_ref[...]).astype(BF16)
    qn = _dot(cqn, wuq_ref[...]).astype(BF16)
    qr = _dot(cqn, wqr_ref[...])
    qi = _dot(cqn, wqi_ref[...])
    hw = DSA_HEADS * LANES
    for h in range(DSA_HEADS):
        sl = slice(h * LANES, (h + 1) * LANES)
        sp = slice(hw + h * LANES, hw + (h + 1) * LANES)
        qcat_ref[h, :, 0:LANES] = (_dot(qn[:, sl], wuk_ref[h]) * scale).astype(BF16)
        qcat_ref[h, :, LANES:2 * LANES] = ((qr[:, sl] * c32 + qr[:, sp] * s32) * scale).astype(BF16)
        qidx_ref[h] = ((qi[:, sl] * c64 + qi[:, sp] * s64) * float(IDX_DIM ** -0.5)).astype(BF16)
    kv_ref[:, 0:LANES] = _rms(pd_ref[:, 256:384], kvnw_ref[...]).astype(BF16)
    kv_ref[:, LANES:2 * LANES] = _rope_lanes(pd_ref[:, 384:512], c32, s32, DSA_ROPE // 2).astype(BF16)
    kin = _rms(pd_ref[:, 512:640], kinw_ref[...], n=IDX_DIM)
    kidx_ref[...] = _rope_lanes(kin, c64, s64, IDX_DIM // 2).astype(BF16)
    widx_ref[...] = jnp.transpose(pd_ref[:, 640:768] * float(IDX_HEADS ** -0.5))[0:IDX_HEADS, :]


def _dsa_prep(proj_d, qnw, kvnw, kinw_pad, wuq_pad, wqr_pad, wuk_pad, wqi_pad, tabs, seq, tm):
    n = proj_d.shape[0]
    nt = seq // tm
    full = lambda a: pl.BlockSpec(a.shape, lambda i: (0,) * a.ndim)
    tab = pl.BlockSpec((tm, LANES), lambda i: (i % nt, 0))
    hd = DSA_HEADS
    return pl.pallas_call(
        _dsa_prep_kernel,
        grid=(n // tm,),
        in_specs=[pl.BlockSpec((tm, DSA_GROUP), lambda i: (i, 0)),
                  full(qnw), full(kvnw), full(kinw_pad), full(wuq_pad), full(wqr_pad), full(wuk_pad),
                  full(wqi_pad), tab, tab, tab, tab],
        out_specs=[pl.BlockSpec((hd, tm, 2 * LANES), lambda i: (0, i, 0)),
                   pl.BlockSpec((hd, tm, LANES), lambda i: (0, i, 0)),
                   pl.BlockSpec((tm, 2 * LANES), lambda i: (i, 0)),
                   pl.BlockSpec((tm, LANES), lambda i: (i, 0)),
                   pl.BlockSpec((IDX_HEADS, tm), lambda i: (0, i))],
        out_shape=[jax.ShapeDtypeStruct((hd, n, 2 * LANES), BF16),
                   jax.ShapeDtypeStruct((hd, n, LANES), BF16),
                   jax.ShapeDtypeStruct((n, 2 * LANES), BF16),
                   jax.ShapeDtypeStruct((n, LANES), BF16),
                   jax.ShapeDtypeStruct((IDX_HEADS, n), F32)],
        compiler_params=_cparams(("parallel",)),
        name="dsa_prep",
    )(proj_d, qnw, kvnw, kinw_pad, wuq_pad, wqr_pad, wuk_pad, wqi_pad, *tabs)


def _dsa_kernel(qcat_ref, qidx_ref, widx_ref, kv_ref, kidx_ref, wuv_ref, o_ref,
                sc_ref, x_ref, lg_ref, m_ref, mb_ref, acc_ref, *, topk):
    TQ, TK, H, TS = DSA_TQ, DSA_TK, DSA_HEADS, DSA_TS
    assert TS == TK
    NR = TS // TQ
    step = pl.program_id(1)
    FULL = (TK,) * NR
    DIAG = tuple((r + 1) * TQ for r in range(NR))

    def over_tiles(fn, init):
        carry = lax.fori_loop(0, step, lambda kt, c: fn(kt, FULL, c), init)
        return fn(step, DIAG, carry)

    def key_rows(kt, nk):
        return pl.ds(pl.multiple_of(kt * TK, TK), nk)

    def cols(r):
        return slice(r * TQ, (r + 1) * TQ)

    def as_i32(i):
        return jnp.asarray(i, dtype=I32)

    one, zero = jnp.float32(1.0), jnp.float32(0.0)
    ninf, pinf = jnp.float32(-jnp.inf), jnp.float32(jnp.inf)

    def fold(x, op):
        return op(x.reshape(x.shape[0] // SUBLANES, SUBLANES, x.shape[1]), axis=0)

    def score_tile(kt, nks, carry):
        mx, mn = list(carry[0]), list(carry[1])
        for r, nk in enumerate(nks):
            qidx = qidx_ref[:, cols(r), :].reshape(H * TQ, LANES)
            s_all = jnp.maximum(_dot_nt(kidx_ref[key_rows(kt, nk), :], qidx), 0.0)
            score = s_all[:, 0:TQ] * widx_ref[0:1, cols(r)]
            for h in range(1, IDX_HEADS):
                score = score + s_all[:, h * TQ:(h + 1) * TQ] * widx_ref[h:h + 1, cols(r)]
            lo_src = score
            if nks is DIAG:
                causal = (lax.broadcasted_iota(I32, (nk, TQ), 0)
                          <= r * TQ + lax.broadcasted_iota(I32, (nk, TQ), 1))
                score = jnp.where(causal, score, ninf)
                lo_src = jnp.where(causal, score, pinf)
            x_ref[kt, 0:nk, cols(r)] = score
            mx[r] = jnp.maximum(mx[r], fold(score, jnp.max))
            mn[r] = jnp.minimum(mn[r], fold(lo_src, jnp.min))
        return tuple(mx), tuple(mn)

    mx, mn = over_tiles(score_tile, (tuple(jnp.full((SUBLANES, TQ), ninf, F32) for _ in range(NR)),
                                     tuple(jnp.full((SUBLANES, TQ), pinf, F32) for _ in range(NR))))
    x_max = jnp.max(jnp.concatenate(mx, axis=1), axis=0, keepdims=True)
    x_min = jnp.min(jnp.concatenate(mn, axis=1), axis=0, keepdims=True)

    def count(pred_fn):
        def block(kt, nk, c, acc):
            pos = kt * TK + lax.broadcasted_iota(I32, (nk, acc.shape[1]), 0)
            return acc + fold(jnp.where(pred_fn(x_ref[kt, 0:nk, c], pos, c), one, zero), jnp.sum)

        def tile(kt, nks, acc):
            if nks is FULL:
                return block(kt, TK, slice(0, TS), acc)
            return jnp.concatenate([block(kt, nk, cols(r), acc[:, cols(r)]) for r, nk in enumerate(nks)], axis=1)
        return jnp.sum(over_tiles(tile, jnp.zeros((SUBLANES, TS), F32)), axis=0, keepdims=True)

    kf = float(topk)
    n_causal = (step * TS + 1 + lax.broadcasted_iota(I32, (1, TS), 1)).astype(F32)

    def bisect(_, carry):
        lo, hi, c_lo, c_hi = carry
        mid = lo + (hi - lo) * 0.5
        c = count(lambda x, pos, cs: x >= mid[:, cs])
        up = c >= kf
        return (jnp.where(up, mid, lo), jnp.where(up, hi, mid), jnp.where(up, c, c_lo), jnp.where(up, c_hi, c))

    lo, hi, c_lo, c_hi = lax.fori_loop(
        0, DSA_BISECT, bisect, (x_min, x_max + (x_max - x_min) + 1.0, n_causal, jnp.zeros((1, TS), F32)))
    need = kf - c_hi

    has_tie = jnp.max(jnp.where(c_lo > kf, one, zero)) > zero
    nbits = int(np.log2(TK)) + 3

    def tie_search(_):
        def idx_step(i, j):
            cand = j | lax.shift_left(np.int32(1), np.int32(nbits - 1) - as_i32(i))
            g = count(lambda x, pos, cs: (x >= lo[:, cs]) & (x < hi[:, cs]) & (pos < cand[:, cs]))
            return jnp.where(g < need, cand, j)
        return lax.fori_loop(0, nbits, idx_step, jnp.zeros((1, TS), I32))

    jcut = lax.cond(has_tie, tie_search, lambda _: jnp.full((1, TS), 2**30, I32), 0)

    def bias_tile(kt, nks, carry):
        for r, nk in enumerate(nks):
            x = x_ref[kt, 0:nk, cols(r)]
            pos = kt * TK + lax.broadcasted_iota(I32, (nk, TQ), 0)
            sel = (x >= lo[:, cols(r)]) & ((x >= hi[:, cols(r)]) | (pos <= jcut[:, cols(r)]))
            sc_ref[kt, cols(r), 0:nk] = jnp.transpose(jnp.where(sel, zero, ninf))
        return carry

    over_tiles(bias_tile, 0)

    def pass_a(r, kt, nk):
        qcat = qcat_ref[:, cols(r), :].reshape(H * TQ, 2 * LANES)
        lg = _dot_nt(qcat, kv_ref[key_rows(kt, nk), :]).reshape(H, TQ, nk) + sc_ref[kt, cols(r), 0:nk][None]
        lg = lg.reshape(H * TQ, nk)
        lg_ref[r % 2, kt, :, 0:nk] = lg
        part = lg[:, 0:LANES]
        for g in range(1, nk // LANES):
            part = jnp.maximum(part, lg[:, g * LANES:(g + 1) * LANES])
        m_ref[r % 2] = jnp.maximum(m_ref[r % 2], part)

    def pass_b(r, kt, nk):
        m_b = mb_ref[...]
        p = jnp.concatenate(
            [jnp.exp((lg_ref[r % 2, kt, :, g * LANES:(g + 1) * LANES] - m_b).astype(BF16))
             for g in range(nk // LANES)], axis=1)
        kv1 = jnp.concatenate([kv_ref[key_rows(kt, nk), 0:LANES], jnp.ones((nk, LANES), BF16)], axis=1)
        acc_ref[...] += _dot(p, kv1)

    def run(fns):
        def tile(kt, nks, carry):
            for fn, r in fns:
                fn(r, kt, nks[r])
            return carry
        over_tiles(tile, 0)

    m_ref[0] = jnp.full((H * TQ, LANES), -jnp.inf, F32)
    run([(pass_a, 0)])
    for r in range(NR):
        rows = cols(r)
        mb_ref[...] = jnp.broadcast_to(jnp.max(m_ref[r % 2], axis=-1, keepdims=True), (H * TQ, LANES))
        acc_ref[...] = jnp.zeros_like(acc_ref)
        if r + 1 < NR:
            m_ref[(r + 1) % 2] = jnp.full((H * TQ, LANES), -jnp.inf, F32)
            run([(pass_a, r + 1), (pass_b, r)])
        else:
            run([(pass_b, r)])
        o_lat = (acc_ref[:, 0:LANES] / acc_ref[:, LANES:2 * LANES]).astype(BF16)
        for pr in range(H // 2):
            h0, h1 = 2 * pr, 2 * pr + 1
            o_ref[rows, pr * LANES:(pr + 1) * LANES] = (
                _dot(o_lat[h0 * TQ:(h0 + 1) * TQ], wuv_ref[h0]) + _dot(o_lat[h1 * TQ:(h1 + 1) * TQ], wuv_ref[h1])
            ).astype(o_ref.dtype)


def _dsa(qcat, qidx, widx, kv, kidx, wuv_pad, batch, seq, topk):
    n = kv.shape[0]
    ns = seq // DSA_TS
    nkt_max = seq // DSA_TK
    H = DSA_HEADS
    return pl.pallas_call(
        functools.partial(_dsa_kernel, topk=topk),
        grid=(batch, ns),
        in_specs=[pl.BlockSpec((H, DSA_TS, 2 * LANES), lambda b, s: (0, b * ns + s, 0)),
                  pl.BlockSpec((H, DSA_TS, LANES), lambda b, s: (0, b * ns + s, 0)),
                  pl.BlockSpec((IDX_HEADS, DSA_TS), lambda b, s: (0, b * ns + s)),
                  pl.BlockSpec((seq, 2 * LANES), lambda b, s: (b, 0)),
                  pl.BlockSpec((seq, LANES), lambda b, s: (b, 0)),
                  pl.BlockSpec(wuv_pad.shape, lambda b, s: (0, 0, 0))],
        out_specs=pl.BlockSpec((DSA_TS, H * DSA_VDIM), lambda b, s: (b * ns + s, 0)),
        out_shape=jax.ShapeDtypeStruct((n, H * DSA_VDIM), BF16),
        scratch_shapes=[pltpu.VMEM((nkt_max, DSA_TS, DSA_TK), F32),
                        pltpu.VMEM((nkt_max, DSA_TK, DSA_TS), F32),
                        pltpu.VMEM((2, nkt_max, H * DSA_TQ, DSA_TK), F32),
                        pltpu.VMEM((2, H * DSA_TQ, LANES), F32),
                        pltpu.VMEM((H * DSA_TQ, LANES), F32),
                        pltpu.VMEM((H * DSA_TQ, 2 * LANES), F32)],
        compiler_params=_cparams(("parallel", "arbitrary")),
        name="dsa_attn",
    )(qcat, qidx, widx, kv, kidx, wuv_pad)


def _mem_attn_kernel(q_ref, k_ref, v_ref, o_ref):
    scale = float(MEM_DH ** -0.5)
    for h in range(MEM_HEADS):
        sl = slice(h * MEM_DH, (h + 1) * MEM_DH)
        lg = _dot_nt((q_ref[:, sl] * scale).astype(BF16), k_ref[:, sl])
        p = jnp.exp(lg - jnp.max(lg, axis=-1, keepdims=True))
        o = _dot(p.astype(BF16), v_ref[:, sl]) / jnp.sum(p, axis=-1, keepdims=True)
        o_ref[:, sl] = o.astype(o_ref.dtype)


def _mem_attn(proj_m, k_m, v_m, seq, tm):
    n = proj_m.shape[0]
    nt = seq // tm
    m = k_m.shape[0] // (n // seq)
    return pl.pallas_call(
        _mem_attn_kernel,
        grid=(n // tm,),
        in_specs=[pl.BlockSpec((tm, MEM_WIDTH), lambda i: (i, 0)),
                  pl.BlockSpec((m, MEM_WIDTH), lambda i: (i // nt, 0)),
                  pl.BlockSpec((m, MEM_WIDTH), lambda i: (i // nt, 0))],
        out_specs=pl.BlockSpec((tm, MEM_WIDTH), lambda i: (i, 0)),
        out_shape=jax.ShapeDtypeStruct((n, MEM_WIDTH), BF16),
        compiler_params=_cparams(("parallel",)),
        name="mem_attn",
    )(proj_m, k_m, v_m)


def _merge_kernel(x_ref, nw_ref, oa_ref, ob_ref, oc_ref, wg_ref, wbr_ref, wo_ref, h_ref):
    x = x_ref[...]
    d = x.shape[-1]
    xn = _rms(x, nw_ref[...]).astype(BF16)
    merged = None
    for br, o_br in enumerate((oa_ref, ob_ref, oc_ref)):
        gate = _sigmoid(_dot(xn, wg_ref[:, br * d:(br + 1) * d]))
        term = gate * _dot(o_br[...], wbr_ref[br])
        merged = term if merged is None else merged + term
    h_ref[...] = x + _dot(merged.astype(BF16), wo_ref[...])


def _merge(x, nw, o_a, o_b, o_c, w_gate, w_br, w_out, tm):
    n, d = x.shape
    row = lambda wd: pl.BlockSpec((tm, wd), lambda i: (i, 0))
    full = lambda a: pl.BlockSpec(a.shape, lambda i: (0,) * a.ndim)
    return pl.pallas_call(
        _merge_kernel,
        grid=(n // tm,),
        in_specs=[row(d), full(nw), row(o_a.shape[1]), row(o_b.shape[1]), row(o_c.shape[1]),
                  full(w_gate), full(w_br), full(w_out)],
        out_specs=row(d),
        out_shape=jax.ShapeDtypeStruct((n, d), F32),
        compiler_params=_cparams(("parallel",)),
        name="merge_out",
    )(x, nw, o_a, o_b, o_c, w_gate, w_br, w_out)


def _mlp_kernel(h_ref, nw_ref, wu_ref, wd_ref, fw_ref, o_ref, *, ff_chunk):
    h = h_ref[...]
    m = _rms(h, nw_ref[...]).astype(BF16)
    acc = h
    for c0 in range(0, wu_ref.shape[1], ff_chunk):
        u = jnp.maximum(_dot(m, wu_ref[:, c0:c0 + ff_chunk]), 0.0)
        acc = acc + _dot((u * u).astype(BF16), wd_ref[c0:c0 + ff_chunk, :])
    o_ref[...] = _rms(acc, fw_ref[...])


def _mlp(h, nw, w_up, w_down, fw, tm, ff_chunk=1024):
    n, d = h.shape
    full = lambda a: pl.BlockSpec(a.shape, lambda i: (0,) * a.ndim)
    return pl.pallas_call(
        functools.partial(_mlp_kernel, ff_chunk=ff_chunk),
        grid=(n // tm,),
        in_specs=[pl.BlockSpec((tm, d), lambda i: (i, 0)), full(nw), full(w_up), full(w_down), full(fw)],
        out_specs=pl.BlockSpec((tm, d), lambda i: (i, 0)),
        out_shape=jax.ShapeDtypeStruct((n, d), F32),
        compiler_params=_cparams(("parallel",)),
        name="mlp_final",
    )(h, nw, w_up, w_down, fw)


def _pad_cols(w, width):
    return jnp.pad(w, ((0, 0), (0, width - w.shape[1])))


def _head_pad(w, width):
    return jnp.pad(w, ((0, 0), (0, 0), (0, width - w.shape[2]))).reshape(w.shape[0], -1)


def _with_rope_partner(w):
    half = w.shape[2] // 2
    swapped = jnp.concatenate([w[..., half:], w[..., :half]], axis=-1)
    return jnp.concatenate([_head_pad(w, LANES), _head_pad(swapped, LANES)], axis=1)


def _rope_tables(seq, d):
    inv = ROPE_THETA ** (-jnp.arange(0, d, 2, dtype=F32) / d)
    ang = jnp.arange(seq, dtype=F32)[:, None] * inv[None, :]
    cos, sin = jnp.cos(ang), jnp.sin(ang)
    cosf = _pad_cols(jnp.concatenate([cos, cos], axis=1), LANES)
    sins = _pad_cols(jnp.concatenate([-sin, sin], axis=1), LANES)
    return cosf, sins


def kernel(x, mem, attn_norm_w, w_in, hgrn_lower_bounds, hgrn_out_norm_w, dsa_q_norm_w, dsa_kv_norm_w,
           dsa_kidx_norm_w, dsa_w_uq, dsa_w_qr, dsa_w_uk, dsa_w_uv, dsa_w_qidx, mem_norm_w, w_mem_kv,
           w_br_hgrn, w_br_dsa, w_br_mem, w_out, mlp_norm_w, w_mlp_up, w_mlp_down, final_norm_w):
    B, T, D = x.shape
    N = B * T
    M = mem.shape[1]
    l = 0
    topk = min(TOPK_MAX, T // 4)
    tm = 512
    x2 = x.reshape(N, D)

    wi = w_in[l]
    o = 0
    cols = {}
    for name, wd in (("hq", HG_WIDTH), ("hf", HG_WIDTH), ("hv", HG_WIDTH), ("hg", HG_WIDTH),
                     ("cq", DSA_Q_LORA), ("ckv", DSA_KV_LORA), ("kr", DSA_ROPE), ("kidx", IDX_DIM),
                     ("widx", IDX_HEADS), ("mq", MEM_WIDTH), ("gate", N_BRANCH * D)):
        cols[name] = wi[:, o:o + wd]
        o += wd
    w_proj = jnp.concatenate(
        [cols["hq"], cols["hf"], cols["hv"], cols["hg"], cols["mq"], cols["cq"], cols["ckv"],
         _pad_cols(cols["kr"], LANES), _pad_cols(cols["kidx"], LANES), _pad_cols(cols["widx"], LANES)],
        axis=1).astype(BF16)
    w_gate = cols["gate"].astype(BF16)

    proj_h, proj_m, proj_d = _norm_proj(x2, attn_norm_w[l], w_proj, (4 * HG_WIDTH, MEM_WIDTH, DSA_GROUP),
                                        (F32, F32, F32), tm)

    lb_all = jnp.cumsum(jax.nn.softmax(hgrn_lower_bounds.astype(F32), axis=0), axis=0)
    o_a = _hgrn(proj_h, lb_all[l], hgrn_out_norm_w[l], B, T)

    wuq_pad = _head_pad(dsa_w_uq[l], LANES).astype(BF16)
    wqr_pad = _with_rope_partner(dsa_w_qr[l]).astype(BF16)
    wqi_pad = _with_rope_partner(dsa_w_qidx[l]).astype(BF16)
    wuk_pad = jnp.pad(dsa_w_uk[l], ((0, 0), (0, LANES - DSA_NOPE), (0, 0))).astype(BF16)
    wuv = dsa_w_uv[l]
    wuv_pad = jnp.stack([jnp.pad(wuv[h], ((0, 0), ((h % 2) * DSA_VDIM, LANES - DSA_VDIM - (h % 2) * DSA_VDIM)))
                         for h in range(DSA_HEADS)]).astype(BF16)
    kinw_pad = _pad_cols(dsa_kidx_norm_w[l].reshape(1, IDX_DIM), LANES)
    tabs = _rope_tables(T, DSA_ROPE) + _rope_tables(T, IDX_DIM)
    qcat, qidx, kv, kidx, widx = _dsa_prep(
        proj_d, dsa_q_norm_w[l].reshape(1, -1), dsa_kv_norm_w[l].reshape(1, -1), kinw_pad,
        wuq_pad, wqr_pad, wuk_pad, wqi_pad, tabs, T, tm)
    o_b = _dsa(qcat, qidx, widx, kv, kidx, wuv_pad, B, T, topk)

    k_m, v_m = _norm_proj(mem.reshape(B * M, D), mem_norm_w[l], w_mem_kv[l].astype(BF16),
                          (MEM_WIDTH, MEM_WIDTH), (BF16, BF16), min(tm, B * M))
    o_c = _mem_attn(proj_m, k_m, v_m, T, tm)

    w_br = jnp.stack([w_br_hgrn[l], w_br_dsa[l], w_br_mem[l]]).astype(BF16)
    h = _merge(x2, attn_norm_w[l].reshape(1, D), o_a, o_b, o_c, w_gate, w_br, w_out[l].astype(BF16), tm)
    out = _mlp(h, mlp_norm_w[l].reshape(1, D), w_mlp_up[l].astype(BF16), w_mlp_down[l].astype(BF16),
               final_norm_w.reshape(1, D), tm)
    return out.reshape(B, T, D)
```

```python
import functools

import jax
import jax.numpy as jnp
import numpy as np
from jax import lax
from jax.experimental import pallas as pl
from jax.experimental.pallas import tpu as pltpu

F32 = jnp.float32
BF16 = jnp.bfloat16
I32 = jnp.int32

EPS = 1e-6
LOG2E = 1.4426950408889634
ROPE_THETA = 10000.0
LANES = 128
SUBLANES = 8
HG_HEADS, HG_D, HG_CHUNK = 4, 128, 64
HG_WIDTH = HG_HEADS * HG_D
HG_SUB = 16
HG_STEP = 256
DSA_HEADS, DSA_Q_LORA, DSA_KV_LORA = 8, 256, 128
DSA_NOPE, DSA_ROPE, DSA_VDIM = 64, 32, 64
IDX_HEADS, IDX_DIM = 8, 64
TOPK_MAX = 256
MEM_HEADS, MEM_DH = 4, 128
MEM_WIDTH = MEM_HEADS * MEM_DH
N_BRANCH = 3
DSA_GROUP = 6 * LANES
VMEM_LIMIT = 56 * 1024 * 1024

DSA_TQ = 128
DSA_TS = 512
DSA_TK = 512
DSA_BISECT = 30


def _cparams(sem):
    return pltpu.CompilerParams(dimension_semantics=sem, vmem_limit_bytes=VMEM_LIMIT)


def _rms(x, w, n=None):
    n = x.shape[-1] if n is None else n
    ms = jnp.sum(x * x, axis=-1, keepdims=True) * (1.0 / n)
    return x * lax.rsqrt(ms + EPS) * w


def _sigmoid(x):
    return 1.0 / (1.0 + jnp.exp(-x))


def _dot(a, b):
    return jnp.dot(a, b, preferred_element_type=F32)


def _dot_nt(a, b):
    return lax.dot_general(a, b, (((1,), (1,)), ((), ())), preferred_element_type=F32)


def _dot_tn(a, b):
    return lax.dot_general(a, b, (((0,), (0,)), ((), ())), preferred_element_type=F32)


def _norm_proj_kernel(x_ref, nw_ref, w_ref, *out_refs):
    xn = _rms(x_ref[...], nw_ref[...]).astype(BF16)
    off = 0
    for o_ref in out_refs:
        wd = o_ref.shape[-1]
        o_ref[...] = _dot(xn, w_ref[:, off:off + wd]).astype(o_ref.dtype)
        off += wd


def _norm_proj(x, nw, w, widths, dtypes, tm):
    n, d = x.shape
    return pl.pallas_call(
        _norm_proj_kernel,
        grid=(n // tm,),
        in_specs=[pl.BlockSpec((tm, d), lambda i: (i, 0)),
                  pl.BlockSpec((1, d), lambda i: (0, 0)),
                  pl.BlockSpec(w.shape, lambda i: (0, 0))],
        out_specs=[pl.BlockSpec((tm, wd), lambda i: (i, 0)) for wd in widths],
        out_shape=[jax.ShapeDtypeStruct((n, wd), dt) for wd, dt in zip(widths, dtypes)],
        compiler_params=_cparams(("parallel",)),
        name="norm_proj",
    )(x, nw.reshape(1, d), w)


def _hgrn_kernel(q_ref, f_ref, v_ref, g_ref, lb_ref, onw_ref, o_ref, st_ref, b_ref, k_ref):
    @pl.when(pl.program_id(1) == 0)
    def _():
        st_ref[...] = jnp.zeros_like(st_ref)

    C, SB = HG_CHUNK, HG_SUB
    lb = lb_ref[...]
    ri = lax.broadcasted_iota(I32, (C, C), 0)
    ci = lax.broadcasted_iota(I32, (C, C), 1)
    tri = jnp.where(ri >= ci, 1.0, 0.0).astype(BF16)
    row_c = lax.broadcasted_iota(I32, (C, HG_D), 0)
    row_1 = lax.broadcasted_iota(I32, (SUBLANES, HG_D), 0)
    lane_1 = lax.broadcasted_iota(I32, (SUBLANES, C), 1)

    def chunk(ci_, carry):
        rows = pl.ds(pl.multiple_of(ci_ * C, C), C)
        f = lb + (1.0 - lb) * _sigmoid(f_ref[rows, :])
        lf = jnp.log(f) * LOG2E
        hi = lf.astype(BF16)
        r1 = lf - hi.astype(F32)
        mid = r1.astype(BF16)
        lo = (r1 - mid.astype(F32)).astype(BF16)
        b_ref[...] = _dot(tri, hi) + _dot(tri, mid) + _dot(tri, lo)
        k_ref[...] = 1.0 - f

        for h in range(HG_HEADS):
            sl = slice(h * HG_D, (h + 1) * HG_D)
            qh = q_ref[rows, sl]
            vh = v_ref[rows, sl]
            bh = b_ref[:, sl]
            kh = k_ref[:, sl]
            a_rows = [jnp.zeros((SB, C), F32)]
            for i in range(1, C // SB):
                bref = b_ref[i * SB:i * SB + 1, sl]
                qi = (qh[i * SB:(i + 1) * SB] * jnp.exp2(bh[i * SB:(i + 1) * SB] - bref)).astype(BF16)
                ki = (kh * jnp.exp2(jnp.where(row_c < i * SB, bref - bh, -jnp.inf))).astype(BF16)
                a_rows.append(_dot_nt(qi, ki))
            for i in range(C // SB):
                b_top, q_top = bh[i * SB:i * SB + SUBLANES], qh[i * SB:i * SB + SUBLANES]
                b_bot, q_bot = bh[i * SB + SUBLANES:(i + 1) * SB], qh[i * SB + SUBLANES:(i + 1) * SB]
                a_top, a_bot = a_rows[i][:SUBLANES], a_rows[i][SUBLANES:]
                for j in range(SB):
                    s = i * SB + j
                    brow = b_ref[s:s + 1, sl]
                    krow = k_ref[s:s + 1, sl]
                    if j < SUBLANES:
                        e = jnp.exp2(jnp.where(row_1 >= j, b_top - brow, -jnp.inf))
                        r = jnp.sum(e * (q_top * krow), axis=-1, keepdims=True)
                        a_top = jnp.where(lane_1 == s, r, a_top)
                        e = jnp.exp2(b_bot - brow)
                    else:
                        e = jnp.exp2(jnp.where(row_1 >= j - SUBLANES, b_bot - brow, -jnp.inf))
                    r = jnp.sum(e * (q_bot * krow), axis=-1, keepdims=True)
                    a_bot = jnp.where(lane_1 == s, r, a_bot)
                a_rows[i] = jnp.concatenate([a_top, a_bot], axis=0)
            a_mat = jnp.concatenate(a_rows, axis=0)
            st = st_ref[h]
            qe = (qh * jnp.exp2(bh)).astype(BF16)
            vb = vh.astype(BF16)
            o = _dot_nt(qe, st.astype(BF16)) + _dot(a_mat.astype(BF16), vb)
            blast = bh[C - 1:C, :]
            kd = (kh * jnp.exp2(blast - bh)).astype(BF16)
            st_ref[h] = st * jnp.exp2(blast) + _dot_tn(vb, kd)
            gh = g_ref[rows, sl]
            o_ref[rows, sl] = (_rms(o, onw_ref[:, sl]) * (gh * _sigmoid(gh))).astype(o_ref.dtype)
        return carry

    lax.fori_loop(0, HG_STEP // C, chunk, 0, unroll=True)


def _hgrn(proj_h, lb, onw, batch, seq):
    n = proj_h.shape[0]
    nc = seq // HG_STEP
    blk = lambda col: pl.BlockSpec((HG_STEP, HG_WIDTH), lambda b, c, col=col: (b * nc + c, col))
    vec = pl.BlockSpec((1, HG_WIDTH), lambda b, c: (0, 0))
    return pl.pallas_call(
        _hgrn_kernel,
        grid=(batch, nc),
        in_specs=[blk(0), blk(1), blk(2), blk(3), vec, vec],
        out_specs=pl.BlockSpec((HG_STEP, HG_WIDTH), lambda b, c: (b * nc + c, 0)),
        out_shape=jax.ShapeDtypeStruct((n, HG_WIDTH), BF16),
        scratch_shapes=[pltpu.VMEM((HG_HEADS, HG_D, HG_D), F32),
                        pltpu.VMEM((HG_CHUNK, HG_WIDTH), F32),
                        pltpu.VMEM((HG_CHUNK, HG_WIDTH), F32)],
        compiler_params=_cparams(("parallel", "arbitrary")),
        name="hgrn2",
    )(proj_h, proj_h, proj_h, proj_h, lb.reshape(1, HG_WIDTH), onw.reshape(1, HG_WIDTH))


def _rope_lanes(x, cosf, sins, half):
    nl = x.shape[-1]
    lane = lax.broadcasted_iota(I32, x.shape, 1)
    partner = jnp.where(lane % LANES < half, pltpu.roll(x, nl - half, 1), pltpu.roll(x, half, 1))
    return x * cosf + partner * sins


def _dsa_prep_kernel(pd_ref, qnw_ref, kvnw_ref, kinw_ref, wuq_ref, wqr_ref, wuk_ref, wqi_ref,
                     c32_ref, s32_ref, c64_ref, s64_ref,
                     qcat_ref, qidx_ref, kv_ref, kidx_ref, widx_ref):
    scale = float((DSA_NOPE + DSA_ROPE) ** -0.5)
    c32, s32, c64, s64 = c32_ref[...], s32_ref[...], c64_ref[...], s64_ref[...]
    cqn = _rms(pd_ref[:, 0:DSA_Q_LORA], qnw_ref[...]).astype(BF16)
    qn = _dot(cqn, wuq_ref[...]).astype(BF16)
    qr = _dot(cqn, wqr_ref[...])
    qi = _dot(cqn, wqi_ref[...])
    hw = DSA_HEADS * LANES
    for h in range(DSA_HEADS):
        sl = slice(h * LANES, (h + 1) * LANES)
        sp = slice(hw + h * LANES, hw + (h + 1) * LANES)
        qcat_ref[h, :, 0:LANES] = (_dot(qn[:, sl], wuk_ref[h]) * scale).astype(BF16)
        qcat_ref[h, :, LANES:2 * LANES] = ((qr[:, sl] * c32 + qr[:, sp] * s32) * scale).astype(BF16)
        qidx_ref[h] = ((qi[:, sl] * c64 + qi[:, sp] * s64) * float(IDX_DIM ** -0.5)).astype(BF16)
    kv_ref[:, 0:LANES] = _rms(pd_ref[:, 256:384], kvnw_ref[...]).astype(BF16)
    kv_ref[:, LANES:2 * LANES] = _rope_lanes(pd_ref[:, 384:512], c32, s32, DSA_ROPE // 2).astype(BF16)
    kin = _rms(pd_ref[:, 512:640], kinw_ref[...], n=IDX_DIM)
    kidx_ref[...] = _rope_lanes(kin, c64, s64, IDX_DIM // 2).astype(BF16)
    widx_ref[...] = jnp.transpose(pd_ref[:, 640:768] * float(IDX_HEADS ** -0.5))[0:IDX_HEADS, :]


def _dsa_prep(proj_d, qnw, kvnw, kinw_pad, wuq_pad, wqr_pad, wuk_pad, wqi_pad, tabs, seq, tm):
    n = proj_d.shape[0]
    nt = seq // tm
    full = lambda a: pl.BlockSpec(a.shape, lambda i: (0,) * a.ndim)
    tab = pl.BlockSpec((tm, LANES), lambda i: (i % nt, 0))
    hd = DSA_HEADS
    return pl.pallas_call(
        _dsa_prep_kernel,
        grid=(n // tm,),
        in_specs=[pl.BlockSpec((tm, DSA_GROUP), lambda i: (i, 0)),
                  full(qnw), full(kvnw), full(kinw_pad), full(wuq_pad), full(wqr_pad), full(wuk_pad),
                  full(wqi_pad), tab, tab, tab, tab],
        out_specs=[pl.BlockSpec((hd, tm, 2 * LANES), lambda i: (0, i, 0)),
                   pl.BlockSpec((hd, tm, LANES), lambda i: (0, i, 0)),
                   pl.BlockSpec((tm, 2 * LANES), lambda i: (i, 0)),
                   pl.BlockSpec((tm, LANES), lambda i: (i, 0)),
                   pl.BlockSpec((IDX_HEADS, tm), lambda i: (0, i))],
        out_shape=[jax.ShapeDtypeStruct((hd, n, 2 * LANES), BF16),
                   jax.ShapeDtypeStruct((hd, n, LANES), BF16),
                   jax.ShapeDtypeStruct((n, 2 * LANES), BF16),
                   jax.ShapeDtypeStruct((n, LANES), BF16),
                   jax.ShapeDtypeStruct((IDX_HEADS, n), F32)],
        compiler_params=_cparams(("parallel",)),
        name="dsa_prep",
    )(proj_d, qnw, kvnw, kinw_pad, wuq_pad, wqr_pad, wuk_pad, wqi_pad, *tabs)


def _dsa_kernel(qcat_ref, qidx_ref, widx_ref, kv_ref, kidx_ref, wuv_ref, o_ref,
                sc_ref, x_ref, lg_ref, m_ref, mb_ref, acc_ref, *, topk):
    TQ, TK, H, TS = DSA_TQ, DSA_TK, DSA_HEADS, DSA_TS
    assert TS == TK
    NR = TS // TQ
    step = pl.program_id(1)
    FULL = (TK,) * NR
    DIAG = tuple((r + 1) * TQ for r in range(NR))

    def over_tiles(fn, init):
        carry = lax.fori_loop(0, step, lambda kt, c: fn(kt, FULL, c), init)
        return fn(step, DIAG, carry)

    def key_rows(kt, nk):
        return pl.ds(pl.multiple_of(kt * TK, TK), nk)

    def cols(r):
        return slice(r * TQ, (r + 1) * TQ)

    def as_i32(i):
        return jnp.asarray(i, dtype=I32)

    one, zero = jnp.float32(1.0), jnp.float32(0.0)
    ninf, pinf = jnp.float32(-jnp.inf), jnp.float32(jnp.inf)

    def fold(x, op):
        return op(x.reshape(x.shape[0] // SUBLANES, SUBLANES, x.shape[1]), axis=0)

    def score_tile(kt, nks, carry):
        mx, mn = list(carry[0]), list(carry[1])
        for r, nk in enumerate(nks):
            qidx = qidx_ref[:, cols(r), :].reshape(H * TQ, LANES)
            s_all = jnp.maximum(_dot_nt(kidx_ref[key_rows(kt, nk), :], qidx), 0.0)
            score = s_all[:, 0:TQ] * widx_ref[0:1, cols(r)]
            for h in range(1, IDX_HEADS):
                score = score + s_all[:, h * TQ:(h + 1) * TQ] * widx_ref[h:h + 1, cols(r)]
            lo_src = score
            if nks is DIAG:
                causal = (lax.broadcasted_iota(I32, (nk, TQ), 0)
                          <= r * TQ + lax.broadcasted_iota(I32, (nk, TQ), 1))
                score = jnp.where(causal, score, ninf)
                lo_src = jnp.where(causal, score, pinf)
            x_ref[kt, 0:nk, cols(r)] = score
            mx[r] = jnp.maximum(mx[r], fold(score, jnp.max))
            mn[r] = jnp.minimum(mn[r], fold(lo_src, jnp.min))
        return tuple(mx), tuple(mn)

    mx, mn = over_tiles(score_tile, (tuple(jnp.full((SUBLANES, TQ), ninf, F32) for _ in range(NR)),
                                     tuple(jnp.full((SUBLANES, TQ), pinf, F32) for _ in range(NR))))
    x_max = jnp.max(jnp.concatenate(mx, axis=1), axis=0, keepdims=True)
    x_min = jnp.min(jnp.concatenate(mn, axis=1), axis=0, keepdims=True)

    def count(pred_fn):
        def block(kt, nk, c, acc):
            pos = kt * TK + lax.broadcasted_iota(I32, (nk, acc.shape[1]), 0)
            return acc + fold(jnp.where(pred_fn(x_ref[kt, 0:nk, c], pos, c), one, zero), jnp.sum)

        def tile(kt, nks, acc):
            if nks is FULL:
                return block(kt, TK, slice(0, TS), acc)
            return jnp.concatenate([block(kt, nk, cols(r), acc[:, cols(r)]) for r, nk in enumerate(nks)], axis=1)
        return jnp.sum(over_tiles(tile, jnp.zeros((SUBLANES, TS), F32)), axis=0, keepdims=True)

    kf = float(topk)
    n_causal = (step * TS + 1 + lax.broadcasted_iota(I32, (1, TS), 1)).astype(F32)

    def bisect(_, carry):
        lo, hi, c_lo, c_hi = carry
        mid = lo + (hi - lo) * 0.5
        c = count(lambda x, pos, cs: x >= mid[:, cs])
        up = c >= kf
        return (jnp.where(up, mid, lo), jnp.where(up, hi, mid), jnp.where(up, c, c_lo), jnp.where(up, c_hi, c))

    lo, hi, c_lo, c_hi = lax.fori_loop(
        0, DSA_BISECT, bisect, (x_min, x_max + (x_max - x_min) + 1.0, n_causal, jnp.zeros((1, TS), F32)))
    need = kf - c_hi

    has_tie = jnp.max(jnp.where(c_lo > kf, one, zero)) > zero
    nbits = int(np.log2(TK)) + 3

    def tie_search(_):
        def idx_step(i, j):
            cand = j | lax.shift_left(np.int32(1), np.int32(nbits - 1) - as_i32(i))
            g = count(lambda x, pos, cs: (x >= lo[:, cs]) & (x < hi[:, cs]) & (pos < cand[:, cs]))
            return jnp.where(g < need, cand, j)
        return lax.fori_loop(0, nbits, idx_step, jnp.zeros((1, TS), I32))

    jcut = lax.cond(has_tie, tie_search, lambda _: jnp.full((1, TS), 2**30, I32), 0)

    def bias_tile(kt, nks, carry):
        for r, nk in enumerate(nks):
            x = x_ref[kt, 0:nk, cols(r)]
            pos = kt * TK + lax.broadcasted_iota(I32, (nk, TQ), 0)
            sel = (x >= lo[:, cols(r)]) & ((x >= hi[:, cols(r)]) | (pos <= jcut[:, cols(r)]))
            sc_ref[kt, cols(r), 0:nk] = jnp.transpose(jnp.where(sel, zero, ninf))
        return carry

    over_tiles(bias_tile, 0)

    def pass_a(r, kt, nk):
        qcat = qcat_ref[:, cols(r), :].reshape(H * TQ, 2 * LANES)
        lg = _dot_nt(qcat, kv_ref[key_rows(kt, nk), :]).reshape(H, TQ, nk) + sc_ref[kt, cols(r), 0:nk][None]
        lg = lg.reshape(H * TQ, nk)
        lg_ref[r % 2, kt, :, 0:nk] = lg
        part = lg[:, 0:LANES]
        for g in range(1, nk // LANES):
            part = jnp.maximum(part, lg[:, g * LANES:(g + 1) * LANES])
        m_ref[r % 2] = jnp.maximum(m_ref[r % 2], part)

    def pass_b(r, kt, nk):
        m_b = mb_ref[...]
        p = jnp.concatenate(
            [jnp.exp((lg_ref[r % 2, kt, :, g * LANES:(g + 1) * LANES] - m_b).astype(BF16))
             for g in range(nk // LANES)], axis=1)
        kv1 = jnp.concatenate([kv_ref[key_rows(kt, nk), 0:LANES], jnp.ones((nk, LANES), BF16)], axis=1)
        acc_ref[...] += _dot(p, kv1)

    def run(fns):
        def tile(kt, nks, carry):
            for fn, r in fns:
                fn(r, kt, nks[r])
            return carry
        over_tiles(tile, 0)

    m_ref[0] = jnp.full((H * TQ, LANES), -jnp.inf, F32)
    run([(pass_a, 0)])
    for r in range(NR):
        rows = cols(r)
        mb_ref[...] = jnp.broadcast_to(jnp.max(m_ref[r % 2], axis=-1, keepdims=True), (H * TQ, LANES))
        acc_ref[...] = jnp.zeros_like(acc_ref)
        if r + 1 < NR:
            m_ref[(r + 1) % 2] = jnp.full((H * TQ, LANES), -jnp.inf, F32)
            run([(pass_a, r + 1), (pass_b, r)])
        else:
            run([(pass_b, r)])
        o_lat = (acc_ref[:, 0:LANES] / acc_ref[:, LANES:2 * LANES]).astype(BF16)
        for pr in range(H // 2):
            h0, h1 = 2 * pr, 2 * pr + 1
            o_ref[rows, pr * LANES:(pr + 1) * LANES] = (
                _dot(o_lat[h0 * TQ:(h0 + 1) * TQ], wuv_ref[h0]) + _dot(o_lat[h1 * TQ:(h1 + 1) * TQ], wuv_ref[h1])
            ).astype(o_ref.dtype)


def _dsa(qcat, qidx, widx, kv, kidx, wuv_pad, batch, seq, topk):
    n = kv.shape[0]
    ns = seq // DSA_TS
    nkt_max = seq // DSA_TK
    H = DSA_HEADS
    return pl.pallas_call(
        functools.partial(_dsa_kernel, topk=topk),
        grid=(batch, ns),
        in_specs=[pl.BlockSpec((H, DSA_TS, 2 * LANES), lambda b, s: (0, b * ns + s, 0)),
                  pl.BlockSpec((H, DSA_TS, LANES), lambda b, s: (0, b * ns + s, 0)),
                  pl.BlockSpec((IDX_HEADS, DSA_TS), lambda b, s: (0, b * ns + s)),
                  pl.BlockSpec((seq, 2 * LANES), lambda b, s: (b, 0)),
                  pl.BlockSpec((seq, LANES), lambda b, s: (b, 0)),
                  pl.BlockSpec(wuv_pad.shape, lambda b, s: (0, 0, 0))],
        out_specs=pl.BlockSpec((DSA_TS, H * DSA_VDIM), lambda b, s: (b * ns + s, 0)),
        out_shape=jax.ShapeDtypeStruct((n, H * DSA_VDIM), BF16),
        scratch_shapes=[pltpu.VMEM((nkt_max, DSA_TS, DSA_TK), F32),
                        pltpu.VMEM((nkt_max, DSA_TK, DSA_TS), F32),
                        pltpu.VMEM((2, nkt_max, H * DSA_TQ, DSA_TK), F32),
                        pltpu.VMEM((2, H * DSA_TQ, LANES), F32),
                        pltpu.VMEM((H * DSA_TQ, LANES), F32),
                        pltpu.VMEM((H * DSA_TQ, 2 * LANES), F32)],
        compiler_params=_cparams(("parallel", "arbitrary")),
        name="dsa_attn",
    )(qcat, qidx, widx, kv, kidx, wuv_pad)


def _mem_attn_kernel(q_ref, k_ref, v_ref, o_ref):
    scale = float(MEM_DH ** -0.5)
    for h in range(MEM_HEADS):
        sl = slice(h * MEM_DH, (h + 1) * MEM_DH)
        lg = _dot_nt((q_ref[:, sl] * scale).astype(BF16), k_ref[:, sl])
        p = jnp.exp(lg - jnp.max(lg, axis=-1, keepdims=True))
        o = _dot(p.astype(BF16), v_ref[:, sl]) / jnp.sum(p, axis=-1, keepdims=True)
        o_ref[:, sl] = o.astype(o_ref.dtype)


def _mem_attn(proj_m, k_m, v_m, seq, tm):
    n = proj_m.shape[0]
    nt = seq // tm
    m = k_m.shape[0] // (n // seq)
    return pl.pallas_call(
        _mem_attn_kernel,
        grid=(n // tm,),
        in_specs=[pl.BlockSpec((tm, MEM_WIDTH), lambda i: (i, 0)),
                  pl.BlockSpec((m, MEM_WIDTH), lambda i: (i // nt, 0)),
                  pl.BlockSpec((m, MEM_WIDTH), lambda i: (i // nt, 0))],
        out_specs=pl.BlockSpec((tm, MEM_WIDTH), lambda i: (i, 0)),
        out_shape=jax.ShapeDtypeStruct((n, MEM_WIDTH), BF16),
        compiler_params=_cparams(("parallel",)),
        name="mem_attn",
    )(proj_m, k_m, v_m)


def _merge_mlp_kernel(x_ref, anw_ref, oa_ref, ob_ref, oc_ref, wg_ref, wbr_ref, wo_ref,
                      mnw_ref, wu_ref, wd_ref, fw_ref, o_ref, *, ff_chunk):
    x = x_ref[...]
    d = x.shape[-1]
    xn = _rms(x, anw_ref[...]).astype(BF16)
    merged = None
    for br, o_br in enumerate((oa_ref, ob_ref, oc_ref)):
        gate = _sigmoid(_dot(xn, wg_ref[:, br * d:(br + 1) * d]))
        term = gate * _dot(o_br[...], wbr_ref[br])
        merged = term if merged is None else merged + term
    h = x + _dot(merged.astype(BF16), wo_ref[...])
    m = _rms(h, mnw_ref[...]).astype(BF16)
    acc = h
    for c0 in range(0, wu_ref.shape[1], ff_chunk):
        u = jnp.maximum(_dot(m, wu_ref[:, c0:c0 + ff_chunk]), 0.0)
        acc = acc + _dot((u * u).astype(BF16), wd_ref[c0:c0 + ff_chunk, :])
    o_ref[...] = _rms(acc, fw_ref[...])


def _merge_mlp(x, anw, o_a, o_b, o_c, w_gate, w_br, w_out, mnw, w_up, w_down, fw, tm, ff_chunk=1024):
    n, d = x.shape
    row = lambda wd: pl.BlockSpec((tm, wd), lambda i: (i, 0))
    full = lambda a: pl.BlockSpec(a.shape, lambda i: (0,) * a.ndim, pipeline_mode=pl.Buffered(1))
    return pl.pallas_call(
        functools.partial(_merge_mlp_kernel, ff_chunk=ff_chunk),
        grid=(n // tm,),
        in_specs=[row(d), full(anw), row(o_a.shape[1]), row(o_b.shape[1]), row(o_c.shape[1]),
                  full(w_gate), full(w_br), full(w_out), full(mnw), full(w_up), full(w_down), full(fw)],
        out_specs=row(d),
        out_shape=jax.ShapeDtypeStruct((n, d), F32),
        compiler_params=_cparams(("parallel",)),
        name="merge_mlp",
    )(x, anw, o_a, o_b, o_c, w_gate, w_br, w_out, mnw, w_up, w_down, fw)


def _pad_cols(w, width):
    return jnp.pad(w, ((0, 0), (0, width - w.shape[1])))


def _head_pad(w, width):
    return jnp.pad(w, ((0, 0), (0, 0), (0, width - w.shape[2]))).reshape(w.shape[0], -1)


def _with_rope_partner(w):
    half = w.shape[2] // 2
    swapped = jnp.concatenate([w[..., half:], w[..., :half]], axis=-1)
    return jnp.concatenate([_head_pad(w, LANES), _head_pad(swapped, LANES)], axis=1)


def _rope_tables(seq, d):
    inv = ROPE_THETA ** (-jnp.arange(0, d, 2, dtype=F32) / d)
    ang = jnp.arange(seq, dtype=F32)[:, None] * inv[None, :]
    cos, sin = jnp.cos(ang), jnp.sin(ang)
    cosf = _pad_cols(jnp.concatenate([cos, cos], axis=1), LANES)
    sins = _pad_cols(jnp.concatenate([-sin, sin], axis=1), LANES)
    return cosf, sins


def kernel(x, mem, attn_norm_w, w_in, hgrn_lower_bounds, hgrn_out_norm_w, dsa_q_norm_w, dsa_kv_norm_w,
           dsa_kidx_norm_w, dsa_w_uq, dsa_w_qr, dsa_w_uk, dsa_w_uv, dsa_w_qidx, mem_norm_w, w_mem_kv,
           w_br_hgrn, w_br_dsa, w_br_mem, w_out, mlp_norm_w, w_mlp_up, w_mlp_down, final_norm_w):
    B, T, D = x.shape
    N = B * T
    M = mem.shape[1]
    l = 0
    topk = min(TOPK_MAX, T // 4)
    tm = 512
    x2 = x.reshape(N, D)

    wi = w_in[l]
    o = 0
    cols = {}
    for name, wd in (("hq", HG_WIDTH), ("hf", HG_WIDTH), ("hv", HG_WIDTH), ("hg", HG_WIDTH),
                     ("cq", DSA_Q_LORA), ("ckv", DSA_KV_LORA), ("kr", DSA_ROPE), ("kidx", IDX_DIM),
                     ("widx", IDX_HEADS), ("mq", MEM_WIDTH), ("gate", N_BRANCH * D)):
        cols[name] = wi[:, o:o + wd]
        o += wd
    w_proj = jnp.concatenate(
        [cols["hq"], cols["hf"], cols["hv"], cols["hg"], cols["mq"], cols["cq"], cols["ckv"],
         _pad_cols(cols["kr"], LANES), _pad_cols(cols["kidx"], LANES), _pad_cols(cols["widx"], LANES)],
        axis=1).astype(BF16)
    w_gate = cols["gate"].astype(BF16)

    proj_h, proj_m, proj_d = _norm_proj(x2, attn_norm_w[l], w_proj, (4 * HG_WIDTH, MEM_WIDTH, DSA_GROUP),
                                        (F32, F32, F32), tm)

    lb_all = jnp.cumsum(jax.nn.softmax(hgrn_lower_bounds.astype(F32), axis=0), axis=0)
    o_a = _hgrn(proj_h, lb_all[l], hgrn_out_norm_w[l], B, T)

    wuq_pad = _head_pad(dsa_w_uq[l], LANES).astype(BF16)
    wqr_pad = _with_rope_partner(dsa_w_qr[l]).astype(BF16)
    wqi_pad = _with_rope_partner(dsa_w_qidx[l]).astype(BF16)
    wuk_pad = jnp.pad(dsa_w_uk[l], ((0, 0), (0, LANES - DSA_NOPE), (0, 0))).astype(BF16)
    wuv = dsa_w_uv[l]
    wuv_pad = jnp.stack([jnp.pad(wuv[h], ((0, 0), ((h % 2) * DSA_VDIM, LANES - DSA_VDIM - (h % 2) * DSA_VDIM)))
                         for h in range(DSA_HEADS)]).astype(BF16)
    kinw_pad = _pad_cols(dsa_kidx_norm_w[l].reshape(1, IDX_DIM), LANES)
    tabs = _rope_tables(T, DSA_ROPE) + _rope_tables(T, IDX_DIM)
    qcat, qidx, kv, kidx, widx = _dsa_prep(
        proj_d, dsa_q_norm_w[l].reshape(1, -1), dsa_kv_norm_w[l].reshape(1, -1), kinw_pad,
        wuq_pad, wqr_pad, wuk_pad, wqi_pad, tabs, T, tm)
    o_b = _dsa(qcat, qidx, widx, kv, kidx, wuv_pad, B, T, topk)

    k_m, v_m = _norm_proj(mem.reshape(B * M, D), mem_norm_w[l], w_mem_kv[l].astype(BF16),
                          (MEM_WIDTH, MEM_WIDTH), (BF16, BF16), min(tm, B * M))
    o_c = _mem_attn(proj_m, k_m, v_m, T, tm)

    w_br = jnp.stack([w_br_hgrn[l], w_br_dsa[l], w_br_mem[l]]).astype(BF16)
    out = _merge_mlp(x2, attn_norm_w[l].reshape(1, D), o_a, o_b, o_c, w_gate, w_br, w_out[l].astype(BF16),
                     mlp_norm_w[l].reshape(1, D), w_mlp_up[l].astype(BF16), w_mlp_down[l].astype(BF16),
                     final_norm_w.reshape(1, D), tm)
    return out.reshape(B, T, D)
```

```python
import functools

import jax
import jax.numpy as jnp
import numpy as np
from jax import lax
from jax.experimental import pallas as pl
from jax.experimental.pallas import tpu as pltpu

F32 = jnp.float32
BF16 = jnp.bfloat16
I32 = jnp.int32

EPS = 1e-6
LOG2E = 1.4426950408889634
ROPE_THETA = 10000.0
LANES = 128
SUBLANES = 8
HG_HEADS, HG_D, HG_CHUNK = 4, 128, 64
HG_WIDTH = HG_HEADS * HG_D
HG_SUB = 16
HG_STEP = 256
DSA_HEADS, DSA_Q_LORA, DSA_KV_LORA = 8, 256, 128
DSA_NOPE, DSA_ROPE, DSA_VDIM = 64, 32, 64
IDX_HEADS, IDX_DIM = 8, 64
TOPK_MAX = 256
MEM_HEADS, MEM_DH = 4, 128
MEM_WIDTH = MEM_HEADS * MEM_DH
N_BRANCH = 3
DSA_GROUP = 6 * LANES
VMEM_LIMIT = 56 * 1024 * 1024

DSA_TQ = 128
DSA_TS = 512
DSA_TK = 512
DSA_BISECT = 30


def _cparams(sem):
    return pltpu.CompilerParams(dimension_semantics=sem, vmem_limit_bytes=VMEM_LIMIT)


def _rms(x, w, n=None):
    n = x.shape[-1] if n is None else n
    ms = jnp.sum(x * x, axis=-1, keepdims=True) * (1.0 / n)
    return x * lax.rsqrt(ms + EPS) * w


def _sigmoid(x):
    return 1.0 / (1.0 + jnp.exp(-x))


def _dot(a, b):
    return jnp.dot(a, b, preferred_element_type=F32)


def _dot_nt(a, b):
    return lax.dot_general(a, b, (((1,), (1,)), ((), ())), preferred_element_type=F32)


def _dot_tn(a, b):
    return lax.dot_general(a, b, (((0,), (0,)), ((), ())), preferred_element_type=F32)


def _norm_proj_kernel(x_ref, nw_ref, w_ref, *out_refs):
    xn = _rms(x_ref[...], nw_ref[...]).astype(BF16)
    off = 0
    for o_ref in out_refs:
        wd = o_ref.shape[-1]
        o_ref[...] = _dot(xn, w_ref[:, off:off + wd]).astype(o_ref.dtype)
        off += wd


def _norm_proj(x, nw, w, widths, dtypes, tm):
    n, d = x.shape
    return pl.pallas_call(
        _norm_proj_kernel,
        grid=(n // tm,),
        in_specs=[pl.BlockSpec((tm, d), lambda i: (i, 0)),
                  pl.BlockSpec((1, d), lambda i: (0, 0)),
                  pl.BlockSpec(w.shape, lambda i: (0, 0))],
        out_specs=[pl.BlockSpec((tm, wd), lambda i: (i, 0)) for wd in widths],
        out_shape=[jax.ShapeDtypeStruct((n, wd), dt) for wd, dt in zip(widths, dtypes)],
        compiler_params=_cparams(("parallel",)),
        name="norm_proj",
    )(x, nw.reshape(1, d), w)


def _hgrn_kernel(q_ref, f_ref, v_ref, g_ref, lb_ref, onw_ref, o_ref, st_ref, b_ref, k_ref):
    @pl.when(pl.program_id(1) == 0)
    def _():
        st_ref[...] = jnp.zeros_like(st_ref)

    C, SB = HG_CHUNK, HG_SUB
    lb = lb_ref[...]
    ri = lax.broadcasted_iota(I32, (C, C), 0)
    ci = lax.broadcasted_iota(I32, (C, C), 1)
    tri = jnp.where(ri >= ci, 1.0, 0.0).astype(BF16)
    row_c = lax.broadcasted_iota(I32, (C, HG_D), 0)
    row_1 = lax.broadcasted_iota(I32, (SUBLANES, HG_D), 0)
    lane_1 = lax.broadcasted_iota(I32, (SUBLANES, C), 1)

    def chunk(ci_, carry):
        rows = pl.ds(pl.multiple_of(ci_ * C, C), C)
        f = lb + (1.0 - lb) * _sigmoid(f_ref[rows, :])
        lf = jnp.log(f) * LOG2E
        hi = lf.astype(BF16)
        r1 = lf - hi.astype(F32)
        mid = r1.astype(BF16)
        lo = (r1 - mid.astype(F32)).astype(BF16)
        b_ref[...] = _dot(tri, hi) + _dot(tri, mid) + _dot(tri, lo)
        k_ref[...] = 1.0 - f

        for h in range(HG_HEADS):
            sl = slice(h * HG_D, (h + 1) * HG_D)
            qh = q_ref[rows, sl]
            vh = v_ref[rows, sl]
            bh = b_ref[:, sl]
            kh = k_ref[:, sl]
            a_rows = [jnp.zeros((SB, C), F32)]
            for i in range(1, C // SB):
                bref = b_ref[i * SB:i * SB + 1, sl]
                qi = (qh[i * SB:(i + 1) * SB] * jnp.exp2(bh[i * SB:(i + 1) * SB] - bref)).astype(BF16)
                ki = (kh * jnp.exp2(jnp.where(row_c < i * SB, bref - bh, -jnp.inf))).astype(BF16)
                a_rows.append(_dot_nt(qi, ki))
            for i in range(C // SB):
                b_top, q_top = bh[i * SB:i * SB + SUBLANES], qh[i * SB:i * SB + SUBLANES]
                b_bot, q_bot = bh[i * SB + SUBLANES:(i + 1) * SB], qh[i * SB + SUBLANES:(i + 1) * SB]
                a_top, a_bot = a_rows[i][:SUBLANES], a_rows[i][SUBLANES:]
                for j in range(SB):
                    s = i * SB + j
                    brow = b_ref[s:s + 1, sl]
                    krow = k_ref[s:s + 1, sl]
                    if j < SUBLANES:
                        e = jnp.exp2(jnp.where(row_1 >= j, b_top - brow, -jnp.inf))
                        r = jnp.sum(e * (q_top * krow), axis=-1, keepdims=True)
                        a_top = jnp.where(lane_1 == s, r, a_top)
                        e = jnp.exp2(b_bot - brow)
                    else:
                        e = jnp.exp2(jnp.where(row_1 >= j - SUBLANES, b_bot - brow, -jnp.inf))
                    r = jnp.sum(e * (q_bot * krow), axis=-1, keepdims=True)
                    a_bot = jnp.where(lane_1 == s, r, a_bot)
                a_rows[i] = jnp.concatenate([a_top, a_bot], axis=0)
            a_mat = jnp.concatenate(a_rows, axis=0)
            st = st_ref[h]
            qe = (qh * jnp.exp2(bh)).astype(BF16)
            vb = vh.astype(BF16)
            o = _dot_nt(qe, st.astype(BF16)) + _dot(a_mat.astype(BF16), vb)
            blast = bh[C - 1:C, :]
            kd = (kh * jnp.exp2(blast - bh)).astype(BF16)
            st_ref[h] = st * jnp.exp2(blast) + _dot_tn(vb, kd)
            gh = g_ref[rows, sl]
            o_ref[rows, sl] = (_rms(o, onw_ref[:, sl]) * (gh * _sigmoid(gh))).astype(o_ref.dtype)
        return carry

    lax.fori_loop(0, HG_STEP // C, chunk, 0, unroll=True)


def _hgrn(proj_h, lb, onw, batch, seq):
    n = proj_h.shape[0]
    nc = seq // HG_STEP
    blk = lambda col: pl.BlockSpec((HG_STEP, HG_WIDTH), lambda b, c, col=col: (b * nc + c, col))
    vec = pl.BlockSpec((1, HG_WIDTH), lambda b, c: (0, 0))
    return pl.pallas_call(
        _hgrn_kernel,
        grid=(batch, nc),
        in_specs=[blk(0), blk(1), blk(2), blk(3), vec, vec],
        out_specs=pl.BlockSpec((HG_STEP, HG_WIDTH), lambda b, c: (b * nc + c, 0)),
        out_shape=jax.ShapeDtypeStruct((n, HG_WIDTH), BF16),
        scratch_shapes=[pltpu.VMEM((HG_HEADS, HG_D, HG_D), F32),
                        pltpu.VMEM((HG_CHUNK, HG_WIDTH), F32),
                        pltpu.VMEM((HG_CHUNK, HG_WIDTH), F32)],
        compiler_params=_cparams(("parallel", "arbitrary")),
        name="hgrn2",
    )(proj_h, proj_h, proj_h, proj_h, lb.reshape(1, HG_WIDTH), onw.reshape(1, HG_WIDTH))


def _rope_lanes(x, cosf, sins, half):
    nl = x.shape[-1]
    lane = lax.broadcasted_iota(I32, x.shape, 1)
    partner = jnp.where(lane % LANES < half, pltpu.roll(x, nl - half, 1), pltpu.roll(x, half, 1))
    return x * cosf + partner * sins


def _dsa_prep_kernel(pd_ref, qnw_ref, kvnw_ref, kinw_ref, wuq_ref, wqr_ref, wuk_ref, wqi_ref,
                     c32_ref, s32_ref, c64_ref, s64_ref,
                     qcat_ref, qidxt_ref, kv_ref, kvt_ref, kidx_ref, widx_ref):
    scale = float((DSA_NOPE + DSA_ROPE) ** -0.5)
    tq = DSA_TQ
    c32, s32, c64, s64 = c32_ref[...], s32_ref[...], c64_ref[...], s64_ref[...]
    cqn = _rms(pd_ref[:, 0:DSA_Q_LORA], qnw_ref[...]).astype(BF16)
    qn = _dot(cqn, wuq_ref[...]).astype(BF16)
    qr = _dot(cqn, wqr_ref[...])
    qi = _dot(cqn, wqi_ref[...])
    hw = DSA_HEADS * LANES
    for h in range(DSA_HEADS):
        sl = slice(h * LANES, (h + 1) * LANES)
        sp = slice(hw + h * LANES, hw + (h + 1) * LANES)
        qcat_ref[h, :, 0:LANES] = (_dot(qn[:, sl], wuk_ref[h]) * scale).astype(BF16)
        qcat_ref[h, :, LANES:2 * LANES] = ((qr[:, sl] * c32 + qr[:, sp] * s32) * scale).astype(BF16)
        qih = (qi[:, sl] * c64 + qi[:, sp] * s64) * float(IDX_DIM ** -0.5)
        for r in range(qih.shape[0] // tq):
            qidxt_ref[r, :, h * tq:(h + 1) * tq] = jnp.transpose(qih[r * tq:(r + 1) * tq]).astype(BF16)
    ckv = _rms(pd_ref[:, 256:384], kvnw_ref[...])
    krope = _rope_lanes(pd_ref[:, 384:512], c32, s32, DSA_ROPE // 2)
    kv_ref[:, 0:LANES] = ckv.astype(BF16)
    kv_ref[:, LANES:2 * LANES] = krope.astype(BF16)
    kvt_ref[0, 0:LANES, :] = jnp.transpose(ckv).astype(BF16)
    kvt_ref[0, LANES:2 * LANES, :] = jnp.transpose(krope).astype(BF16)
    kin = _rms(pd_ref[:, 512:640], kinw_ref[...], n=IDX_DIM)
    kidx_ref[...] = _rope_lanes(kin, c64, s64, IDX_DIM // 2).astype(BF16)
    widx_ref[...] = jnp.transpose(pd_ref[:, 640:768] * float(IDX_HEADS ** -0.5))[0:IDX_HEADS, :]


def _dsa_prep(proj_d, qnw, kvnw, kinw_pad, wuq_pad, wqr_pad, wuk_pad, wqi_pad, tabs, seq, tm):
    n = proj_d.shape[0]
    nt = seq // tm
    full = lambda a: pl.BlockSpec(a.shape, lambda i: (0,) * a.ndim)
    tab = pl.BlockSpec((tm, LANES), lambda i: (i % nt, 0))
    hd = DSA_HEADS
    return pl.pallas_call(
        _dsa_prep_kernel,
        grid=(n // tm,),
        in_specs=[pl.BlockSpec((tm, DSA_GROUP), lambda i: (i, 0)),
                  full(qnw), full(kvnw), full(kinw_pad), full(wuq_pad), full(wqr_pad), full(wuk_pad),
                  full(wqi_pad), tab, tab, tab, tab],
        out_specs=[pl.BlockSpec((hd, tm, 2 * LANES), lambda i: (0, i, 0)),
                   pl.BlockSpec((tm // DSA_TQ, LANES, hd * DSA_TQ), lambda i: (i, 0, 0)),
                   pl.BlockSpec((tm, 2 * LANES), lambda i: (i, 0)),
                   pl.BlockSpec((1, 2 * LANES, tm), lambda i: (i, 0, 0)),
                   pl.BlockSpec((tm, LANES), lambda i: (i, 0)),
                   pl.BlockSpec((IDX_HEADS, tm), lambda i: (0, i))],
        out_shape=[jax.ShapeDtypeStruct((hd, n, 2 * LANES), BF16),
                   jax.ShapeDtypeStruct((n // DSA_TQ, LANES, hd * DSA_TQ), BF16),
                   jax.ShapeDtypeStruct((n, 2 * LANES), BF16),
                   jax.ShapeDtypeStruct((n // tm, 2 * LANES, tm), BF16),
                   jax.ShapeDtypeStruct((n, LANES), BF16),
                   jax.ShapeDtypeStruct((IDX_HEADS, n), F32)],
        compiler_params=_cparams(("parallel",)),
        name="dsa_prep",
    )(proj_d, qnw, kvnw, kinw_pad, wuq_pad, wqr_pad, wuk_pad, wqi_pad, *tabs)


def _dsa_kernel(qcat_ref, qidxt_ref, widx_ref, kv_ref, kvt_ref, kidx_ref, wuv_ref, o_ref,
                sc_ref, x_ref, lg_ref, m_ref, mb_ref, acc_ref, *, topk):
    TQ, TK, H, TS = DSA_TQ, DSA_TK, DSA_HEADS, DSA_TS
    assert TS == TK
    NR = TS // TQ
    step = pl.program_id(1)
    FULL = (TK,) * NR
    DIAG = tuple((r + 1) * TQ for r in range(NR))

    def over_tiles(fn, init):
        carry = lax.fori_loop(0, step, lambda kt, c: fn(kt, FULL, c), init)
        return fn(step, DIAG, carry)

    def key_rows(kt, nk):
        return pl.ds(pl.multiple_of(kt * TK, TK), nk)

    def cols(r):
        return slice(r * TQ, (r + 1) * TQ)

    def as_i32(i):
        return jnp.asarray(i, dtype=I32)

    one, zero = jnp.float32(1.0), jnp.float32(0.0)
    ninf, pinf = jnp.float32(-jnp.inf), jnp.float32(jnp.inf)

    def fold(x, op):
        return op(x.reshape(x.shape[0] // SUBLANES, SUBLANES, x.shape[1]), axis=0)

    def score_tile(kt, nks, carry):
        mx, mn = list(carry[0]), list(carry[1])
        for r, nk in enumerate(nks):
            s_all = jnp.maximum(_dot(kidx_ref[key_rows(kt, nk), :], qidxt_ref[r]), 0.0)
            score = s_all[:, 0:TQ] * widx_ref[0:1, cols(r)]
            for h in range(1, IDX_HEADS):
                score = score + s_all[:, h * TQ:(h + 1) * TQ] * widx_ref[h:h + 1, cols(r)]
            lo_src = score
            if nks is DIAG:
                causal = (lax.broadcasted_iota(I32, (nk, TQ), 0)
                          <= r * TQ + lax.broadcasted_iota(I32, (nk, TQ), 1))
                score = jnp.where(causal, score, ninf)
                lo_src = jnp.where(causal, score, pinf)
            x_ref[kt, 0:nk, cols(r)] = score
            mx[r] = jnp.maximum(mx[r], fold(score, jnp.max))
            mn[r] = jnp.minimum(mn[r], fold(lo_src, jnp.min))
        return tuple(mx), tuple(mn)

    mx, mn = over_tiles(score_tile, (tuple(jnp.full((SUBLANES, TQ), ninf, F32) for _ in range(NR)),
                                     tuple(jnp.full((SUBLANES, TQ), pinf, F32) for _ in range(NR))))
    x_max = jnp.max(jnp.concatenate(mx, axis=1), axis=0, keepdims=True)
    x_min = jnp.min(jnp.concatenate(mn, axis=1), axis=0, keepdims=True)

    def count(pred_fn):
        def block(kt, nk, c, acc):
            pos = kt * TK + lax.broadcasted_iota(I32, (nk, acc.shape[1]), 0)
            return acc + fold(jnp.where(pred_fn(x_ref[kt, 0:nk, c], pos, c), one, zero), jnp.sum)

        def tile(kt, nks, acc):
            if nks is FULL:
                return block(kt, TK, slice(0, TS), acc)
            return jnp.concatenate([block(kt, nk, cols(r), acc[:, cols(r)]) for r, nk in enumerate(nks)], axis=1)
        return jnp.sum(over_tiles(tile, jnp.zeros((SUBLANES, TS), F32)), axis=0, keepdims=True)

    kf = float(topk)
    n_causal = (step * TS + 1 + lax.broadcasted_iota(I32, (1, TS), 1)).astype(F32)

    def bisect(_, carry):
        lo, hi, c_lo, c_hi = carry
        mid = lo + (hi - lo) * 0.5
        c = count(lambda x, pos, cs: x >= mid[:, cs])
        up = c >= kf
        return (jnp.where(up, mid, lo), jnp.where(up, hi, mid), jnp.where(up, c, c_lo), jnp.where(up, c_hi, c))

    lo, hi, c_lo, c_hi = lax.fori_loop(
        0, DSA_BISECT, bisect, (x_min, x_max + (x_max - x_min) + 1.0, n_causal, jnp.zeros((1, TS), F32)))
    need = kf - c_hi

    has_tie = jnp.max(jnp.where(c_lo > kf, one, zero)) > zero
    nbits = int(np.log2(TK)) + 3

    def tie_search(_):
        def idx_step(i, j):
            cand = j | lax.shift_left(np.int32(1), np.int32(nbits - 1) - as_i32(i))
            g = count(lambda x, pos, cs: (x >= lo[:, cs]) & (x < hi[:, cs]) & (pos < cand[:, cs]))
            return jnp.where(g < need, cand, j)
        return lax.fori_loop(0, nbits, idx_step, jnp.zeros((1, TS), I32))

    jcut = lax.cond(has_tie, tie_search, lambda _: jnp.full((1, TS), 2**30, I32), 0)

    def bias_tile(kt, nks, carry):
        for r, nk in enumerate(nks):
            x = x_ref[kt, 0:nk, cols(r)]
            pos = kt * TK + lax.broadcasted_iota(I32, (nk, TQ), 0)
            sel = (x >= lo[:, cols(r)]) & ((x >= hi[:, cols(r)]) | (pos <= jcut[:, cols(r)]))
            sc_ref[kt, cols(r), 0:nk] = jnp.transpose(jnp.where(sel, zero, ninf))
        return carry

    over_tiles(bias_tile, 0)

    def pass_a(r, kt, nk):
        qcat = qcat_ref[:, cols(r), :].reshape(H * TQ, 2 * LANES)
        lg = _dot(qcat, kvt_ref[kt, :, 0:nk]).reshape(H, TQ, nk) + sc_ref[kt, cols(r), 0:nk][None]
        lg = lg.reshape(H * TQ, nk)
        lg_ref[r % 2, kt, :, 0:nk] = lg
        part = lg[:, 0:LANES]
        for g in range(1, nk // LANES):
            part = jnp.maximum(part, lg[:, g * LANES:(g + 1) * LANES])
        m_ref[r % 2] = jnp.maximum(m_ref[r % 2], part)

    def pass_b(r, kt, nk):
        m_b = mb_ref[...]
        p = jnp.concatenate(
            [jnp.exp((lg_ref[r % 2, kt, :, g * LANES:(g + 1) * LANES] - m_b).astype(BF16))
             for g in range(nk // LANES)], axis=1)
        kv1 = jnp.concatenate([kv_ref[key_rows(kt, nk), 0:LANES], jnp.ones((nk, LANES), BF16)], axis=1)
        acc_ref[...] += _dot(p, kv1)

    def run(fns):
        def tile(kt, nks, carry):
            for fn, r in fns:
                fn(r, kt, nks[r])
            return carry
        over_tiles(tile, 0)

    m_ref[0] = jnp.full((H * TQ, LANES), -jnp.inf, F32)
    run([(pass_a, 0)])
    for r in range(NR):
        rows = cols(r)
        mb_ref[...] = jnp.broadcast_to(jnp.max(m_ref[r % 2], axis=-1, keepdims=True), (H * TQ, LANES))
        acc_ref[...] = jnp.zeros_like(acc_ref)
        if r + 1 < NR:
            m_ref[(r + 1) % 2] = jnp.full((H * TQ, LANES), -jnp.inf, F32)
            run([(pass_a, r + 1), (pass_b, r)])
        else:
            run([(pass_b, r)])
        o_lat = (acc_ref[:, 0:LANES] / acc_ref[:, LANES:2 * LANES]).astype(BF16)
        for pr in range(H // 2):
            h0, h1 = 2 * pr, 2 * pr + 1
            o_ref[rows, pr * LANES:(pr + 1) * LANES] = (
                _dot(o_lat[h0 * TQ:(h0 + 1) * TQ], wuv_ref[h0]) + _dot(o_lat[h1 * TQ:(h1 + 1) * TQ], wuv_ref[h1])
            ).astype(o_ref.dtype)


def _dsa(qcat, qidxt, widx, kv, kvt, kidx, wuv_pad, batch, seq, topk):
    n = kv.shape[0]
    ns = seq // DSA_TS
    nkt_max = seq // DSA_TK
    nr = DSA_TS // DSA_TQ
    H = DSA_HEADS
    return pl.pallas_call(
        functools.partial(_dsa_kernel, topk=topk),
        grid=(batch, ns),
        in_specs=[pl.BlockSpec((H, DSA_TS, 2 * LANES), lambda b, s: (0, b * ns + s, 0)),
                  pl.BlockSpec((nr, LANES, H * DSA_TQ), lambda b, s: (b * ns + s, 0, 0)),
                  pl.BlockSpec((IDX_HEADS, DSA_TS), lambda b, s: (0, b * ns + s)),
                  pl.BlockSpec((seq, 2 * LANES), lambda b, s: (b, 0)),
                  pl.BlockSpec((nkt_max, 2 * LANES, DSA_TK), lambda b, s: (b, 0, 0)),
                  pl.BlockSpec((seq, LANES), lambda b, s: (b, 0)),
                  pl.BlockSpec(wuv_pad.shape, lambda b, s: (0, 0, 0))],
        out_specs=pl.BlockSpec((DSA_TS, H * DSA_VDIM), lambda b, s: (b * ns + s, 0)),
        out_shape=jax.ShapeDtypeStruct((n, H * DSA_VDIM), BF16),
        scratch_shapes=[pltpu.VMEM((nkt_max, DSA_TS, DSA_TK), F32),
                        pltpu.VMEM((nkt_max, DSA_TK, DSA_TS), F32),
                        pltpu.VMEM((2, nkt_max, H * DSA_TQ, DSA_TK), F32),
                        pltpu.VMEM((2, H * DSA_TQ, LANES), F32),
                        pltpu.VMEM((H * DSA_TQ, LANES), F32),
                        pltpu.VMEM((H * DSA_TQ, 2 * LANES), F32)],
        compiler_params=_cparams(("parallel", "arbitrary")),
        name="dsa_attn",
    )(qcat, qidxt, widx, kv, kvt, kidx, wuv_pad)


def _mem_attn_kernel(q_ref, k_ref, v_ref, o_ref):
    scale = float(MEM_DH ** -0.5)
    for h in range(MEM_HEADS):
        sl = slice(h * MEM_DH, (h + 1) * MEM_DH)
        lg = _dot_nt((q_ref[:, sl] * scale).astype(BF16), k_ref[:, sl])
        p = jnp.exp(lg - jnp.max(lg, axis=-1, keepdims=True))
        o = _dot(p.astype(BF16), v_ref[:, sl]) / jnp.sum(p, axis=-1, keepdims=True)
        o_ref[:, sl] = o.astype(o_ref.dtype)


def _mem_attn(proj_m, k_m, v_m, seq, tm):
    n = proj_m.shape[0]
    nt = seq // tm
    m = k_m.shape[0] // (n // seq)
    return pl.pallas_call(
        _mem_attn_kernel,
        grid=(n // tm,),
        in_specs=[pl.BlockSpec((tm, MEM_WIDTH), lambda i: (i, 0)),
                  pl.BlockSpec((m, MEM_WIDTH), lambda i: (i // nt, 0)),
                  pl.BlockSpec((m, MEM_WIDTH), lambda i: (i // nt, 0))],
        out_specs=pl.BlockSpec((tm, MEM_WIDTH), lambda i: (i, 0)),
        out_shape=jax.ShapeDtypeStruct((n, MEM_WIDTH), BF16),
        compiler_params=_cparams(("parallel",)),
        name="mem_attn",
    )(proj_m, k_m, v_m)


def _merge_mlp_kernel(x_ref, anw_ref, oa_ref, ob_ref, oc_ref, wg_ref, wbr_ref, wo_ref,
                      mnw_ref, wu_ref, wd_ref, fw_ref, o_ref, *, ff_chunk):
    x = x_ref[...]
    d = x.shape[-1]
    xn = _rms(x, anw_ref[...]).astype(BF16)
    merged = None
    for br, o_br in enumerate((oa_ref, ob_ref, oc_ref)):
        gate = _sigmoid(_dot(xn, wg_ref[:, br * d:(br + 1) * d]))
        term = gate * _dot(o_br[...], wbr_ref[br])
        merged = term if merged is None else merged + term
    h = x + _dot(merged.astype(BF16), wo_ref[...])
    m = _rms(h, mnw_ref[...]).astype(BF16)
    acc = h
    for c0 in range(0, wu_ref.shape[1], ff_chunk):
        u = jnp.maximum(_dot(m, wu_ref[:, c0:c0 + ff_chunk]), 0.0)
        acc = acc + _dot((u * u).astype(BF16), wd_ref[c0:c0 + ff_chunk, :])
    o_ref[...] = _rms(acc, fw_ref[...])


def _merge_mlp(x, anw, o_a, o_b, o_c, w_gate, w_br, w_out, mnw, w_up, w_down, fw, tm, ff_chunk=1024):
    n, d = x.shape
    row = lambda wd: pl.BlockSpec((tm, wd), lambda i: (i, 0))
    full = lambda a: pl.BlockSpec(a.shape, lambda i: (0,) * a.ndim, pipeline_mode=pl.Buffered(1))
    return pl.pallas_call(
        functools.partial(_merge_mlp_kernel, ff_chunk=ff_chunk),
        grid=(n // tm,),
        in_specs=[row(d), full(anw), row(o_a.shape[1]), row(o_b.shape[1]), row(o_c.shape[1]),
                  full(w_gate), full(w_br), full(w_out), full(mnw), full(w_up), full(w_down), full(fw)],
        out_specs=row(d),
        out_shape=jax.ShapeDtypeStruct((n, d), F32),
        compiler_params=_cparams(("parallel",)),
        name="merge_mlp",
    )(x, anw, o_a, o_b, o_c, w_gate, w_br, w_out, mnw, w_up, w_down, fw)


def _pad_cols(w, width):
    return jnp.pad(w, ((0, 0), (0, width - w.shape[1])))


def _head_pad(w, width):
    return jnp.pad(w, ((0, 0), (0, 0), (0, width - w.shape[2]))).reshape(w.shape[0], -1)


def _with_rope_partner(w):
    half = w.shape[2] // 2
    swapped = jnp.concatenate([w[..., half:], w[..., :half]], axis=-1)
    return jnp.concatenate([_head_pad(w, LANES), _head_pad(swapped, LANES)], axis=1)


def _rope_tables(seq, d):
    inv = ROPE_THETA ** (-jnp.arange(0, d, 2, dtype=F32) / d)
    ang = jnp.arange(seq, dtype=F32)[:, None] * inv[None, :]
    cos, sin = jnp.cos(ang), jnp.sin(ang)
    cosf = _pad_cols(jnp.concatenate([cos, cos], axis=1), LANES)
    sins = _pad_cols(jnp.concatenate([-sin, sin], axis=1), LANES)
    return cosf, sins


def kernel(x, mem, attn_norm_w, w_in, hgrn_lower_bounds, hgrn_out_norm_w, dsa_q_norm_w, dsa_kv_norm_w,
           dsa_kidx_norm_w, dsa_w_uq, dsa_w_qr, dsa_w_uk, dsa_w_uv, dsa_w_qidx, mem_norm_w, w_mem_kv,
           w_br_hgrn, w_br_dsa, w_br_mem, w_out, mlp_norm_w, w_mlp_up, w_mlp_down, final_norm_w):
    B, T, D = x.shape
    N = B * T
    M = mem.shape[1]
    l = 0
    topk = min(TOPK_MAX, T // 4)
    tm = 512
    x2 = x.reshape(N, D)

    wi = w_in[l]
    o = 0
    cols = {}
    for name, wd in (("hq", HG_WIDTH), ("hf", HG_WIDTH), ("hv", HG_WIDTH), ("hg", HG_WIDTH),
                     ("cq", DSA_Q_LORA), ("ckv", DSA_KV_LORA), ("kr", DSA_ROPE), ("kidx", IDX_DIM),
                     ("widx", IDX_HEADS), ("mq", MEM_WIDTH), ("gate", N_BRANCH * D)):
        cols[name] = wi[:, o:o + wd]
        o += wd
    w_proj = jnp.concatenate(
        [cols["hq"], cols["hf"], cols["hv"], cols["hg"], cols["mq"], cols["cq"], cols["ckv"],
         _pad_cols(cols["kr"], LANES), _pad_cols(cols["kidx"], LANES), _pad_cols(cols["widx"], LANES)],
        axis=1).astype(BF16)
    w_gate = cols["gate"].astype(BF16)

    proj_h, proj_m, proj_d = _norm_proj(x2, attn_norm_w[l], w_proj, (4 * HG_WIDTH, MEM_WIDTH, DSA_GROUP),
                                        (F32, F32, F32), tm)

    lb_all = jnp.cumsum(jax.nn.softmax(hgrn_lower_bounds.astype(F32), axis=0), axis=0)
    o_a = _hgrn(proj_h, lb_all[l], hgrn_out_norm_w[l], B, T)

    wuq_pad = _head_pad(dsa_w_uq[l], LANES).astype(BF16)
    wqr_pad = _with_rope_partner(dsa_w_qr[l]).astype(BF16)
    wqi_pad = _with_rope_partner(dsa_w_qidx[l]).astype(BF16)
    wuk_pad = jnp.pad(dsa_w_uk[l], ((0, 0), (0, LANES - DSA_NOPE), (0, 0))).astype(BF16)
    wuv = dsa_w_uv[l]
    wuv_pad = jnp.stack([jnp.pad(wuv[h], ((0, 0), ((h % 2) * DSA_VDIM, LANES - DSA_VDIM - (h % 2) * DSA_VDIM)))
                         for h in range(DSA_HEADS)]).astype(BF16)
    kinw_pad = _pad_cols(dsa_kidx_norm_w[l].reshape(1, IDX_DIM), LANES)
    tabs = _rope_tables(T, DSA_ROPE) + _rope_tables(T, IDX_DIM)
    assert tm == DSA_TK and tm % DSA_TQ == 0
    qcat, qidxt, kv, kvt, kidx, widx = _dsa_prep(
        proj_d, dsa_q_norm_w[l].reshape(1, -1), dsa_kv_norm_w[l].reshape(1, -1), kinw_pad,
        wuq_pad, wqr_pad, wuk_pad, wqi_pad, tabs, T, tm)
    o_b = _dsa(qcat, qidxt, widx, kv, kvt, kidx, wuv_pad, B, T, topk)

    k_m, v_m = _norm_proj(mem.reshape(B * M, D), mem_norm_w[l], w_mem_kv[l].astype(BF16),
                          (MEM_WIDTH, MEM_WIDTH), (BF16, BF16), min(tm, B * M))
    o_c = _mem_attn(proj_m, k_m, v_m, T, tm)

    w_br = jnp.stack([w_br_hgrn[l], w_br_dsa[l], w_br_mem[l]]).astype(BF16)
    out = _merge_mlp(x2, attn_norm_w[l].reshape(1, D), o_a, o_b, o_c, w_gate, w_br, w_out[l].astype(BF16),
                     mlp_norm_w[l].reshape(1, D), w_mlp_up[l].astype(BF16), w_mlp_down[l].astype(BF16),
                     final_norm_w.reshape(1, D), tm)
    return out.reshape(B, T, D)
```

```python
import functools

import jax
import jax.numpy as jnp
import numpy as np
from jax import lax
from jax.experimental import pallas as pl
from jax.experimental.pallas import tpu as pltpu

F32 = jnp.float32
BF16 = jnp.bfloat16
I32 = jnp.int32

EPS = 1e-6
LOG2E = 1.4426950408889634
ROPE_THETA = 10000.0
LANES = 128
SUBLANES = 8
HG_HEADS, HG_D, HG_CHUNK = 4, 128, 64
HG_WIDTH = HG_HEADS * HG_D
HG_SUB = 16
HG_STEP = 256
DSA_HEADS, DSA_Q_LORA, DSA_KV_LORA = 8, 256, 128
DSA_NOPE, DSA_ROPE, DSA_VDIM = 64, 32, 64
IDX_HEADS, IDX_DIM = 8, 64
TOPK_MAX = 256
MEM_HEADS, MEM_DH = 4, 128
MEM_WIDTH = MEM_HEADS * MEM_DH
N_BRANCH = 3
DSA_GROUP = 6 * LANES
VMEM_LIMIT = 56 * 1024 * 1024

DSA_TQ = 128
DSA_TS = 512
DSA_TK = 512
DSA_BISECT = 30


def _cparams(sem):
    return pltpu.CompilerParams(dimension_semantics=sem, vmem_limit_bytes=VMEM_LIMIT)


def _rms(x, w, n=None):
    n = x.shape[-1] if n is None else n
    ms = jnp.sum(x * x, axis=-1, keepdims=True) * (1.0 / n)
    return x * lax.rsqrt(ms + EPS) * w


def _sigmoid(x):
    return 1.0 / (1.0 + jnp.exp(-x))


def _dot(a, b):
    return jnp.dot(a, b, preferred_element_type=F32)


def _dot_nt(a, b):
    return lax.dot_general(a, b, (((1,), (1,)), ((), ())), preferred_element_type=F32)


def _dot_tn(a, b):
    return lax.dot_general(a, b, (((0,), (0,)), ((), ())), preferred_element_type=F32)


def _norm_proj_kernel(x_ref, nw_ref, w_ref, *out_refs):
    xn = _rms(x_ref[...], nw_ref[...]).astype(BF16)
    off = 0
    for o_ref in out_refs:
        wd = o_ref.shape[-1]
        o_ref[...] = _dot(xn, w_ref[:, off:off + wd]).astype(o_ref.dtype)
        off += wd


def _norm_proj(x, nw, w, widths, dtypes, tm):
    n, d = x.shape
    return pl.pallas_call(
        _norm_proj_kernel,
        grid=(n // tm,),
        in_specs=[pl.BlockSpec((tm, d), lambda i: (i, 0)),
                  pl.BlockSpec((1, d), lambda i: (0, 0)),
                  pl.BlockSpec(w.shape, lambda i: (0, 0))],
        out_specs=[pl.BlockSpec((tm, wd), lambda i: (i, 0)) for wd in widths],
        out_shape=[jax.ShapeDtypeStruct((n, wd), dt) for wd, dt in zip(widths, dtypes)],
        compiler_params=_cparams(("parallel",)),
        name="norm_proj",
    )(x, nw.reshape(1, d), w)


def _hgrn_kernel(q_ref, f_ref, v_ref, g_ref, lb_ref, onw_ref, o_ref, st_ref, b_ref, k_ref):
    @pl.when(pl.program_id(1) == 0)
    def _():
        st_ref[...] = jnp.zeros_like(st_ref)

    C, SB = HG_CHUNK, HG_SUB
    lb = lb_ref[...]
    ri = lax.broadcasted_iota(I32, (C, C), 0)
    ci = lax.broadcasted_iota(I32, (C, C), 1)
    tri = jnp.where(ri >= ci, 1.0, 0.0).astype(BF16)
    row_c = lax.broadcasted_iota(I32, (C, HG_D), 0)
    row_1 = lax.broadcasted_iota(I32, (SUBLANES, HG_D), 0)
    lane_1 = lax.broadcasted_iota(I32, (SUBLANES, C), 1)

    def chunk(ci_, carry):
        rows = pl.ds(pl.multiple_of(ci_ * C, C), C)
        f = lb + (1.0 - lb) * _sigmoid(f_ref[rows, :])
        lf = jnp.log(f) * LOG2E
        hi = lf.astype(BF16)
        r1 = lf - hi.astype(F32)
        mid = r1.astype(BF16)
        lo = (r1 - mid.astype(F32)).astype(BF16)
        b_ref[...] = _dot(tri, hi) + _dot(tri, mid) + _dot(tri, lo)
        k_ref[...] = 1.0 - f

        for h in range(HG_HEADS):
            sl = slice(h * HG_D, (h + 1) * HG_D)
            qh = q_ref[rows, sl]
            vh = v_ref[rows, sl]
            bh = b_ref[:, sl]
            kh = k_ref[:, sl]
            a_rows = [jnp.zeros((SB, C), F32)]
            for i in range(1, C // SB):
                bref = b_ref[i * SB:i * SB + 1, sl]
                qi = (qh[i * SB:(i + 1) * SB] * jnp.exp2(bh[i * SB:(i + 1) * SB] - bref)).astype(BF16)
                ki = (kh * jnp.exp2(jnp.where(row_c < i * SB, bref - bh, -jnp.inf))).astype(BF16)
                a_rows.append(_dot_nt(qi, ki))
            for i in range(C // SB):
                b_top, q_top = bh[i * SB:i * SB + SUBLANES], qh[i * SB:i * SB + SUBLANES]
                b_bot, q_bot = bh[i * SB + SUBLANES:(i + 1) * SB], qh[i * SB + SUBLANES:(i + 1) * SB]
                a_top, a_bot = a_rows[i][:SUBLANES], a_rows[i][SUBLANES:]
                for j in range(SB):
                    s = i * SB + j
                    brow = b_ref[s:s + 1, sl]
                    krow = k_ref[s:s + 1, sl]
                    if j < SUBLANES:
                        e = jnp.exp2(jnp.where(row_1 >= j, b_top - brow, -jnp.inf))
                        r = jnp.sum(e * (q_top * krow), axis=-1, keepdims=True)
                        a_top = jnp.where(lane_1 == s, r, a_top)
                        e = jnp.exp2(b_bot - brow)
                    else:
                        e = jnp.exp2(jnp.where(row_1 >= j - SUBLANES, b_bot - brow, -jnp.inf))
                    r = jnp.sum(e * (q_bot * krow), axis=-1, keepdims=True)
                    a_bot = jnp.where(lane_1 == s, r, a_bot)
                a_rows[i] = jnp.concatenate([a_top, a_bot], axis=0)
            a_mat = jnp.concatenate(a_rows, axis=0)
            st = st_ref[h]
            qe = (qh * jnp.exp2(bh)).astype(BF16)
            vb = vh.astype(BF16)
            o = _dot_nt(qe, st.astype(BF16)) + _dot(a_mat.astype(BF16), vb)
            blast = bh[C - 1:C, :]
            kd = (kh * jnp.exp2(blast - bh)).astype(BF16)
            st_ref[h] = st * jnp.exp2(blast) + _dot_tn(vb, kd)
            gh = g_ref[rows, sl]
            o_ref[rows, sl] = (_rms(o, onw_ref[:, sl]) * (gh * _sigmoid(gh))).astype(o_ref.dtype)
        return carry

    lax.fori_loop(0, HG_STEP // C, chunk, 0, unroll=True)


def _hgrn(proj_h, lb, onw, batch, seq):
    n = proj_h.shape[0]
    nc = seq // HG_STEP
    blk = lambda col: pl.BlockSpec((HG_STEP, HG_WIDTH), lambda b, c, col=col: (b * nc + c, col))
    vec = pl.BlockSpec((1, HG_WIDTH), lambda b, c: (0, 0))
    return pl.pallas_call(
        _hgrn_kernel,
        grid=(batch, nc),
        in_specs=[blk(0), blk(1), blk(2), blk(3), vec, vec],
        out_specs=pl.BlockSpec((HG_STEP, HG_WIDTH), lambda b, c: (b * nc + c, 0)),
        out_shape=jax.ShapeDtypeStruct((n, HG_WIDTH), BF16),
        scratch_shapes=[pltpu.VMEM((HG_HEADS, HG_D, HG_D), F32),
                        pltpu.VMEM((HG_CHUNK, HG_WIDTH), F32),
                        pltpu.VMEM((HG_CHUNK, HG_WIDTH), F32)],
        compiler_params=_cparams(("parallel", "arbitrary")),
        name="hgrn2",
    )(proj_h, proj_h, proj_h, proj_h, lb.reshape(1, HG_WIDTH), onw.reshape(1, HG_WIDTH))


def _rope_lanes(x, cosf, sins, half):
    nl = x.shape[-1]
    lane = lax.broadcasted_iota(I32, x.shape, 1)
    partner = jnp.where(lane % LANES < half, pltpu.roll(x, nl - half, 1), pltpu.roll(x, half, 1))
    return x * cosf + partner * sins


def _dsa_prep_kernel(pd_ref, qnw_ref, kvnw_ref, kinw_ref, wuq_ref, wqr_ref, wuk_ref, wqi_ref,
                     c32_ref, s32_ref, c64_ref, s64_ref,
                     qcat_ref, qidxt_ref, kv_ref, kvt_ref, kidx_ref, widx_ref):
    scale = float((DSA_NOPE + DSA_ROPE) ** -0.5)
    tq = DSA_TQ
    c32, s32, c64, s64 = c32_ref[...], s32_ref[...], c64_ref[...], s64_ref[...]
    cqn = _rms(pd_ref[:, 0:DSA_Q_LORA], qnw_ref[...]).astype(BF16)
    qn = _dot(cqn, wuq_ref[...]).astype(BF16)
    qr = _dot(cqn, wqr_ref[...])
    qi = _dot(cqn, wqi_ref[...])
    hw = DSA_HEADS * LANES
    for h in range(DSA_HEADS):
        sl = slice(h * LANES, (h + 1) * LANES)
        sp = slice(hw + h * LANES, hw + (h + 1) * LANES)
        qcat_ref[h, :, 0:LANES] = (_dot(qn[:, sl], wuk_ref[h]) * scale).astype(BF16)
        qcat_ref[h, :, LANES:2 * LANES] = ((qr[:, sl] * c32 + qr[:, sp] * s32) * scale).astype(BF16)
        qih = (qi[:, sl] * c64 + qi[:, sp] * s64) * float(IDX_DIM ** -0.5)
        for r in range(qih.shape[0] // tq):
            qidxt_ref[r, :, h * tq:(h + 1) * tq] = jnp.transpose(qih[r * tq:(r + 1) * tq]).astype(BF16)
    ckv = _rms(pd_ref[:, 256:384], kvnw_ref[...])
    krope = _rope_lanes(pd_ref[:, 384:512], c32, s32, DSA_ROPE // 2)
    kv_ref[:, 0:LANES] = ckv.astype(BF16)
    kv_ref[:, LANES:2 * LANES] = krope.astype(BF16)
    kvt_ref[0, 0:LANES, :] = jnp.transpose(ckv).astype(BF16)
    kvt_ref[0, LANES:2 * LANES, :] = jnp.transpose(krope).astype(BF16)
    kin = _rms(pd_ref[:, 512:640], kinw_ref[...], n=IDX_DIM)
    kidx_ref[...] = _rope_lanes(kin, c64, s64, IDX_DIM // 2).astype(BF16)
    widx_ref[...] = jnp.transpose(pd_ref[:, 640:768] * float(IDX_HEADS ** -0.5))[0:IDX_HEADS, :]


def _dsa_prep(proj_d, qnw, kvnw, kinw_pad, wuq_pad, wqr_pad, wuk_pad, wqi_pad, tabs, seq, tm):
    n = proj_d.shape[0]
    nt = seq // tm
    full = lambda a: pl.BlockSpec(a.shape, lambda i: (0,) * a.ndim)
    tab = pl.BlockSpec((tm, LANES), lambda i: (i % nt, 0))
    hd = DSA_HEADS
    return pl.pallas_call(
        _dsa_prep_kernel,
        grid=(n // tm,),
        in_specs=[pl.BlockSpec((tm, DSA_GROUP), lambda i: (i, 0)),
                  full(qnw), full(kvnw), full(kinw_pad), full(wuq_pad), full(wqr_pad), full(wuk_pad),
                  full(wqi_pad), tab, tab, tab, tab],
        out_specs=[pl.BlockSpec((hd, tm, 2 * LANES), lambda i: (0, i, 0)),
                   pl.BlockSpec((tm // DSA_TQ, LANES, hd * DSA_TQ), lambda i: (i, 0, 0)),
                   pl.BlockSpec((tm, 2 * LANES), lambda i: (i, 0)),
                   pl.BlockSpec((1, 2 * LANES, tm), lambda i: (i, 0, 0)),
                   pl.BlockSpec((tm, LANES), lambda i: (i, 0)),
                   pl.BlockSpec((IDX_HEADS, tm), lambda i: (0, i))],
        out_shape=[jax.ShapeDtypeStruct((hd, n, 2 * LANES), BF16),
                   jax.ShapeDtypeStruct((n // DSA_TQ, LANES, hd * DSA_TQ), BF16),
                   jax.ShapeDtypeStruct((n, 2 * LANES), BF16),
                   jax.ShapeDtypeStruct((n // tm, 2 * LANES, tm), BF16),
                   jax.ShapeDtypeStruct((n, LANES), BF16),
                   jax.ShapeDtypeStruct((IDX_HEADS, n), F32)],
        compiler_params=_cparams(("parallel",)),
        name="dsa_prep",
    )(proj_d, qnw, kvnw, kinw_pad, wuq_pad, wqr_pad, wuk_pad, wqi_pad, *tabs)


def _dsa_kernel(qcat_ref, qidxt_ref, widx_ref, kv_ref, kvt_ref, kidx_ref, wuv_ref, tril_ref, o_ref,
                sc_ref, x_ref, lg_ref, m_ref, mb_ref, acc_ref, *, topk):
    TQ, TK, H, TS = DSA_TQ, DSA_TK, DSA_HEADS, DSA_TS
    assert TS == TK
    NR = TS // TQ
    step = pl.program_id(1)
    FULL = (TK,) * NR
    DIAG = tuple((r + 1) * TQ for r in range(NR))

    def over_tiles(fn, init):
        carry = lax.fori_loop(0, step, lambda kt, c: fn(kt, FULL, c), init)
        return fn(step, DIAG, carry)

    def key_rows(kt, nk):
        return pl.ds(pl.multiple_of(kt * TK, TK), nk)

    def cols(r):
        return slice(r * TQ, (r + 1) * TQ)

    one, zero = jnp.float32(1.0), jnp.float32(0.0)
    ninf, pinf = jnp.float32(-jnp.inf), jnp.float32(jnp.inf)

    def fold(x, op):
        return op(x.reshape(x.shape[0] // SUBLANES, SUBLANES, x.shape[1]), axis=0)

    def score_tile(kt, nks, carry):
        mx, mn = list(carry[0]), list(carry[1])
        for r, nk in enumerate(nks):
            s_all = jnp.maximum(_dot(kidx_ref[key_rows(kt, nk), :], qidxt_ref[r]), 0.0)
            score = s_all[:, 0:TQ] * widx_ref[0:1, cols(r)]
            for h in range(1, IDX_HEADS):
                score = score + s_all[:, h * TQ:(h + 1) * TQ] * widx_ref[h:h + 1, cols(r)]
            lo_src = score
            if nks is DIAG:
                causal = (lax.broadcasted_iota(I32, (nk, TQ), 0)
                          <= r * TQ + lax.broadcasted_iota(I32, (nk, TQ), 1))
                score = jnp.where(causal, score, ninf)
                lo_src = jnp.where(causal, score, pinf)
            x_ref[kt, 0:nk, cols(r)] = score
            mx[r] = jnp.maximum(mx[r], fold(score, jnp.max))
            mn[r] = jnp.minimum(mn[r], fold(lo_src, jnp.min))
        return tuple(mx), tuple(mn)

    mx, mn = over_tiles(score_tile, (tuple(jnp.full((SUBLANES, TQ), ninf, F32) for _ in range(NR)),
                                     tuple(jnp.full((SUBLANES, TQ), pinf, F32) for _ in range(NR))))
    x_max = jnp.max(jnp.concatenate(mx, axis=1), axis=0, keepdims=True)
    x_min = jnp.min(jnp.concatenate(mn, axis=1), axis=0, keepdims=True)

    def count(pred_fn):
        def block(kt, nk, c, acc):
            return acc + fold(jnp.where(pred_fn(x_ref[kt, 0:nk, c], c), one, zero), jnp.sum)

        def tile(kt, nks, acc):
            if nks is FULL:
                return block(kt, TK, slice(0, TS), acc)
            return jnp.concatenate([block(kt, nk, cols(r), acc[:, cols(r)]) for r, nk in enumerate(nks)], axis=1)
        return jnp.sum(over_tiles(tile, jnp.zeros((SUBLANES, TS), F32)), axis=0, keepdims=True)

    kf = float(topk)

    def bisect(_, carry):
        lo, hi, c_hi = carry
        mid = lo + (hi - lo) * 0.5
        c = count(lambda x, cs: x >= mid[:, cs])
        up = c >= kf
        return jnp.where(up, mid, lo), jnp.where(up, hi, mid), jnp.where(up, c_hi, c)

    lo, hi, c_hi = lax.fori_loop(
        0, DSA_BISECT, bisect, (x_min, x_max + (x_max - x_min) + 1.0, jnp.zeros((1, TS), F32)))
    need = kf - c_hi

    def bias_block(kt, nk, c, seen):
        x = x_ref[kt, 0:nk, c]
        tie = (x >= lo[:, c]) & (x < hi[:, c])
        tie_f = jnp.where(tie, one, zero)
        rank = _dot(tril_ref[0:nk, 0:nk], tie_f.astype(BF16)) + seen
        bias_t = jnp.where((x >= hi[:, c]) | (tie & (rank < need[:, c])), zero, ninf)
        return bias_t, rank[nk - 1:nk, :] + tie_f[nk - 1:nk, :]

    def bias_tile(kt, nks, seen):
        if nks is FULL:
            bias_t, seen = bias_block(kt, TK, slice(0, TS), seen)
            for r in range(NR):
                sc_ref[kt, cols(r), :] = jnp.transpose(bias_t[:, cols(r)])
            return seen
        out = []
        for r, nk in enumerate(nks):
            bias_t, s = bias_block(kt, nk, cols(r), seen[:, cols(r)])
            sc_ref[kt, cols(r), 0:nk] = jnp.transpose(bias_t)
            out.append(s)
        return jnp.concatenate(out, axis=1)

    over_tiles(bias_tile, jnp.zeros((1, TS), F32))

    def pass_a(r, kt, nk):
        qcat = qcat_ref[:, cols(r), :].reshape(H * TQ, 2 * LANES)
        lg = _dot(qcat, kvt_ref[kt, :, 0:nk]).reshape(H, TQ, nk) + sc_ref[kt, cols(r), 0:nk][None]
        lg = lg.reshape(H * TQ, nk)
        lg_ref[r % 2, kt, :, 0:nk] = lg
        part = lg[:, 0:LANES]
        for g in range(1, nk // LANES):
            part = jnp.maximum(part, lg[:, g * LANES:(g + 1) * LANES])
        m_ref[r % 2] = jnp.maximum(m_ref[r % 2], part)

    def pass_b(r, kt, nk):
        m_b = mb_ref[...]
        p = jnp.concatenate(
            [jnp.exp((lg_ref[r % 2, kt, :, g * LANES:(g + 1) * LANES] - m_b).astype(BF16))
             for g in range(nk // LANES)], axis=1)
        kv1 = jnp.concatenate([kv_ref[key_rows(kt, nk), 0:LANES], jnp.ones((nk, LANES), BF16)], axis=1)
        acc_ref[...] += _dot(p, kv1)

    def run(fns):
        def tile(kt, nks, carry):
            for fn, r in fns:
                fn(r, kt, nks[r])
            return carry
        over_tiles(tile, 0)

    m_ref[0] = jnp.full((H * TQ, LANES), -jnp.inf, F32)
    run([(pass_a, 0)])
    for r in range(NR):
        rows = cols(r)
        mb_ref[...] = jnp.broadcast_to(jnp.max(m_ref[r % 2], axis=-1, keepdims=True), (H * TQ, LANES))
        acc_ref[...] = jnp.zeros_like(acc_ref)
        if r + 1 < NR:
            m_ref[(r + 1) % 2] = jnp.full((H * TQ, LANES), -jnp.inf, F32)
            run([(pass_a, r + 1), (pass_b, r)])
        else:
            run([(pass_b, r)])
        o_lat = (acc_ref[:, 0:LANES] / acc_ref[:, LANES:2 * LANES]).astype(BF16)
        for pr in range(H // 2):
            h0, h1 = 2 * pr, 2 * pr + 1
            o_ref[rows, pr * LANES:(pr + 1) * LANES] = (
                _dot(o_lat[h0 * TQ:(h0 + 1) * TQ], wuv_ref[h0]) + _dot(o_lat[h1 * TQ:(h1 + 1) * TQ], wuv_ref[h1])
            ).astype(o_ref.dtype)


def _dsa(qcat, qidxt, widx, kv, kvt, kidx, wuv_pad, batch, seq, topk):
    n = kv.shape[0]
    ns = seq // DSA_TS
    nkt_max = seq // DSA_TK
    nr = DSA_TS // DSA_TQ
    H = DSA_HEADS
    return pl.pallas_call(
        functools.partial(_dsa_kernel, topk=topk),
        grid=(batch, ns),
        in_specs=[pl.BlockSpec((H, DSA_TS, 2 * LANES), lambda b, s: (0, b * ns + s, 0)),
                  pl.BlockSpec((nr, LANES, H * DSA_TQ), lambda b, s: (b * ns + s, 0, 0)),
                  pl.BlockSpec((IDX_HEADS, DSA_TS), lambda b, s: (0, b * ns + s)),
                  pl.BlockSpec((seq, 2 * LANES), lambda b, s: (b, 0)),
                  pl.BlockSpec((nkt_max, 2 * LANES, DSA_TK), lambda b, s: (b, 0, 0)),
                  pl.BlockSpec((seq, LANES), lambda b, s: (b, 0)),
                  pl.BlockSpec(wuv_pad.shape, lambda b, s: (0, 0, 0)),
                  pl.BlockSpec((DSA_TK, DSA_TK), lambda b, s: (0, 0))],
        out_specs=pl.BlockSpec((DSA_TS, H * DSA_VDIM), lambda b, s: (b * ns + s, 0)),
        out_shape=jax.ShapeDtypeStruct((n, H * DSA_VDIM), BF16),
        scratch_shapes=[pltpu.VMEM((nkt_max, DSA_TS, DSA_TK), F32),
                        pltpu.VMEM((nkt_max, DSA_TK, DSA_TS), F32),
                        pltpu.VMEM((2, nkt_max, H * DSA_TQ, DSA_TK), F32),
                        pltpu.VMEM((2, H * DSA_TQ, LANES), F32),
                        pltpu.VMEM((H * DSA_TQ, LANES), F32),
                        pltpu.VMEM((H * DSA_TQ, 2 * LANES), F32)],
        compiler_params=_cparams(("parallel", "arbitrary")),
        name="dsa_attn",
    )(qcat, qidxt, widx, kv, kvt, kidx, wuv_pad, jnp.tri(DSA_TK, k=-1, dtype=BF16))


def _mem_attn_kernel(q_ref, k_ref, v_ref, o_ref):
    scale = float(MEM_DH ** -0.5)
    for h in range(MEM_HEADS):
        sl = slice(h * MEM_DH, (h + 1) * MEM_DH)
        lg = _dot_nt((q_ref[:, sl] * scale).astype(BF16), k_ref[:, sl])
        p = jnp.exp(lg - jnp.max(lg, axis=-1, keepdims=True))
        o = _dot(p.astype(BF16), v_ref[:, sl]) / jnp.sum(p, axis=-1, keepdims=True)
        o_ref[:, sl] = o.astype(o_ref.dtype)


def _mem_attn(proj_m, k_m, v_m, seq, tm):
    n = proj_m.shape[0]
    nt = seq // tm
    m = k_m.shape[0] // (n // seq)
    return pl.pallas_call(
        _mem_attn_kernel,
        grid=(n // tm,),
        in_specs=[pl.BlockSpec((tm, MEM_WIDTH), lambda i: (i, 0)),
                  pl.BlockSpec((m, MEM_WIDTH), lambda i: (i // nt, 0)),
                  pl.BlockSpec((m, MEM_WIDTH), lambda i: (i // nt, 0))],
        out_specs=pl.BlockSpec((tm, MEM_WIDTH), lambda i: (i, 0)),
        out_shape=jax.ShapeDtypeStruct((n, MEM_WIDTH), BF16),
        compiler_params=_cparams(("parallel",)),
        name="mem_attn",
    )(proj_m, k_m, v_m)


def _merge_mlp_kernel(x_ref, anw_ref, oa_ref, ob_ref, oc_ref, wg_ref, wbr_ref, wo_ref,
                      mnw_ref, wu_ref, wd_ref, fw_ref, o_ref, *, ff_chunk):
    x = x_ref[...]
    d = x.shape[-1]
    xn = _rms(x, anw_ref[...]).astype(BF16)
    merged = None
    for br, o_br in enumerate((oa_ref, ob_ref, oc_ref)):
        gate = _sigmoid(_dot(xn, wg_ref[:, br * d:(br + 1) * d]))
        term = gate * _dot(o_br[...], wbr_ref[br])
        merged = term if merged is None else merged + term
    h = x + _dot(merged.astype(BF16), wo_ref[...])
    m = _rms(h, mnw_ref[...]).astype(BF16)
    acc = h
    for c0 in range(0, wu_ref.shape[1], ff_chunk):
        u = jnp.maximum(_dot(m, wu_ref[:, c0:c0 + ff_chunk]), 0.0)
        acc = acc + _dot((u * u).astype(BF16), wd_ref[c0:c0 + ff_chunk, :])
    o_ref[...] = _rms(acc, fw_ref[...])


def _merge_mlp(x, anw, o_a, o_b, o_c, w_gate, w_br, w_out, mnw, w_up, w_down, fw, tm, ff_chunk=1024):
    n, d = x.shape
    row = lambda wd: pl.BlockSpec((tm, wd), lambda i: (i, 0))
    full = lambda a: pl.BlockSpec(a.shape, lambda i: (0,) * a.ndim, pipeline_mode=pl.Buffered(1))
    return pl.pallas_call(
        functools.partial(_merge_mlp_kernel, ff_chunk=ff_chunk),
        grid=(n // tm,),
        in_specs=[row(d), full(anw), row(o_a.shape[1]), row(o_b.shape[1]), row(o_c.shape[1]),
                  full(w_gate), full(w_br), full(w_out), full(mnw), full(w_up), full(w_down), full(fw)],
        out_specs=row(d),
        out_shape=jax.ShapeDtypeStruct((n, d), F32),
        compiler_params=_cparams(("parallel",)),
        name="merge_mlp",
    )(x, anw, o_a, o_b, o_c, w_gate, w_br, w_out, mnw, w_up, w_down, fw)


def _pad_cols(w, width):
    return jnp.pad(w, ((0, 0), (0, width - w.shape[1])))


def _head_pad(w, width):
    return jnp.pad(w, ((0, 0), (0, 0), (0, width - w.shape[2]))).reshape(w.shape[0], -1)


def _with_rope_partner(w):
    half = w.shape[2] // 2
    swapped = jnp.concatenate([w[..., half:], w[..., :half]], axis=-1)
    return jnp.concatenate([_head_pad(w, LANES), _head_pad(swapped, LANES)], axis=1)


def _rope_tables(seq, d):
    inv = ROPE_THETA ** (-jnp.arange(0, d, 2, dtype=F32) / d)
    ang = jnp.arange(seq, dtype=F32)[:, None] * inv[None, :]
    cos, sin = jnp.cos(ang), jnp.sin(ang)
    cosf = _pad_cols(jnp.concatenate([cos, cos], axis=1), LANES)
    sins = _pad_cols(jnp.concatenate([-sin, sin], axis=1), LANES)
    return cosf, sins


def kernel(x, mem, attn_norm_w, w_in, hgrn_lower_bounds, hgrn_out_norm_w, dsa_q_norm_w, dsa_kv_norm_w,
           dsa_kidx_norm_w, dsa_w_uq, dsa_w_qr, dsa_w_uk, dsa_w_uv, dsa_w_qidx, mem_norm_w, w_mem_kv,
           w_br_hgrn, w_br_dsa, w_br_mem, w_out, mlp_norm_w, w_mlp_up, w_mlp_down, final_norm_w):
    B, T, D = x.shape
    N = B * T
    M = mem.shape[1]
    l = 0
    topk = min(TOPK_MAX, T // 4)
    tm = 512
    x2 = x.reshape(N, D)

    wi = w_in[l]
    o = 0
    cols = {}
    for name, wd in (("hq", HG_WIDTH), ("hf", HG_WIDTH), ("hv", HG_WIDTH), ("hg", HG_WIDTH),
                     ("cq", DSA_Q_LORA), ("ckv", DSA_KV_LORA), ("kr", DSA_ROPE), ("kidx", IDX_DIM),
                     ("widx", IDX_HEADS), ("mq", MEM_WIDTH), ("gate", N_BRANCH * D)):
        cols[name] = wi[:, o:o + wd]
        o += wd
    w_proj = jnp.concatenate(
        [cols["hq"], cols["hf"], cols["hv"], cols["hg"], cols["mq"], cols["cq"], cols["ckv"],
         _pad_cols(cols["kr"], LANES), _pad_cols(cols["kidx"], LANES), _pad_cols(cols["widx"], LANES)],
        axis=1).astype(BF16)
    w_gate = cols["gate"].astype(BF16)

    proj_h, proj_m, proj_d = _norm_proj(x2, attn_norm_w[l], w_proj, (4 * HG_WIDTH, MEM_WIDTH, DSA_GROUP),
                                        (F32, F32, F32), tm)

    lb_all = jnp.cumsum(jax.nn.softmax(hgrn_lower_bounds.astype(F32), axis=0), axis=0)
    o_a = _hgrn(proj_h, lb_all[l], hgrn_out_norm_w[l], B, T)

    wuq_pad = _head_pad(dsa_w_uq[l], LANES).astype(BF16)
    wqr_pad = _with_rope_partner(dsa_w_qr[l]).astype(BF16)
    wqi_pad = _with_rope_partner(dsa_w_qidx[l]).astype(BF16)
    wuk_pad = jnp.pad(dsa_w_uk[l], ((0, 0), (0, LANES - DSA_NOPE), (0, 0))).astype(BF16)
    wuv = dsa_w_uv[l]
    wuv_pad = jnp.stack([jnp.pad(wuv[h], ((0, 0), ((h % 2) * DSA_VDIM, LANES - DSA_VDIM - (h % 2) * DSA_VDIM)))
                         for h in range(DSA_HEADS)]).astype(BF16)
    kinw_pad = _pad_cols(dsa_kidx_norm_w[l].reshape(1, IDX_DIM), LANES)
    tabs = _rope_tables(T, DSA_ROPE) + _rope_tables(T, IDX_DIM)
    assert tm == DSA_TK and tm % DSA_TQ == 0
    qcat, qidxt, kv, kvt, kidx, widx = _dsa_prep(
        proj_d, dsa_q_norm_w[l].reshape(1, -1), dsa_kv_norm_w[l].reshape(1, -1), kinw_pad,
        wuq_pad, wqr_pad, wuk_pad, wqi_pad, tabs, T, tm)
    o_b = _dsa(qcat, qidxt, widx, kv, kvt, kidx, wuv_pad, B, T, topk)

    k_m, v_m = _norm_proj(mem.reshape(B * M, D), mem_norm_w[l], w_mem_kv[l].astype(BF16),
                          (MEM_WIDTH, MEM_WIDTH), (BF16, BF16), min(tm, B * M))
    o_c = _mem_attn(proj_m, k_m, v_m, T, tm)

    w_br = jnp.stack([w_br_hgrn[l], w_br_dsa[l], w_br_mem[l]]).astype(BF16)
    out = _merge_mlp(x2, attn_norm_w[l].reshape(1, D), o_a, o_b, o_c, w_gate, w_br, w_out[l].astype(BF16),
                     mlp_norm_w[l].reshape(1, D), w_mlp_up[l].astype(BF16), w_mlp_down[l].astype(BF16),
                     final_norm_w.reshape(1, D), tm)
    return out.reshape(B, T, D)
```

```python
import functools

import jax
import jax.numpy as jnp
import numpy as np
from jax import lax
from jax.experimental import pallas as pl
from jax.experimental.pallas import tpu as pltpu

F32 = jnp.float32
BF16 = jnp.bfloat16
I32 = jnp.int32

EPS = 1e-6
LOG2E = 1.4426950408889634
ROPE_THETA = 10000.0
LANES = 128
SUBLANES = 8
HG_HEADS, HG_D, HG_CHUNK = 4, 128, 64
HG_WIDTH = HG_HEADS * HG_D
HG_SUB = 16
HG_STEP = 256
DSA_HEADS, DSA_Q_LORA, DSA_KV_LORA = 8, 256, 128
DSA_NOPE, DSA_ROPE, DSA_VDIM = 64, 32, 64
IDX_HEADS, IDX_DIM = 8, 64
TOPK_MAX = 256
MEM_HEADS, MEM_DH = 4, 128
MEM_WIDTH = MEM_HEADS * MEM_DH
N_BRANCH = 3
DSA_GROUP = 6 * LANES
VMEM_LIMIT = 56 * 1024 * 1024

DSA_TQ = 128
DSA_TS = 512
DSA_TK = 512
DSA_BISECT = 28


def _cparams(sem):
    return pltpu.CompilerParams(dimension_semantics=sem, vmem_limit_bytes=VMEM_LIMIT)


def _rms(x, w, n=None):
    n = x.shape[-1] if n is None else n
    ms = jnp.sum(x * x, axis=-1, keepdims=True) * (1.0 / n)
    return x * lax.rsqrt(ms + EPS) * w


def _sigmoid(x):
    return 1.0 / (1.0 + jnp.exp(-x))


def _dot(a, b):
    return jnp.dot(a, b, preferred_element_type=F32)


def _dot_nt(a, b):
    return lax.dot_general(a, b, (((1,), (1,)), ((), ())), preferred_element_type=F32)


def _dot_tn(a, b):
    return lax.dot_general(a, b, (((0,), (0,)), ((), ())), preferred_element_type=F32)


def _norm_proj_kernel(x_ref, nw_ref, w_ref, *out_refs):
    xn = _rms(x_ref[...], nw_ref[...]).astype(BF16)
    off = 0
    for o_ref in out_refs:
        wd = o_ref.shape[-1]
        o_ref[...] = _dot(xn, w_ref[:, off:off + wd]).astype(o_ref.dtype)
        off += wd


def _norm_proj(x, nw, w, widths, dtypes, tm):
    n, d = x.shape
    return pl.pallas_call(
        _norm_proj_kernel,
        grid=(n // tm,),
        in_specs=[pl.BlockSpec((tm, d), lambda i: (i, 0)),
                  pl.BlockSpec((1, d), lambda i: (0, 0)),
                  pl.BlockSpec(w.shape, lambda i: (0, 0))],
        out_specs=[pl.BlockSpec((tm, wd), lambda i: (i, 0)) for wd in widths],
        out_shape=[jax.ShapeDtypeStruct((n, wd), dt) for wd, dt in zip(widths, dtypes)],
        compiler_params=_cparams(("parallel",)),
        name="norm_proj",
    )(x, nw.reshape(1, d), w)


def _hgrn_kernel(q_ref, f_ref, v_ref, g_ref, lb_ref, onw_ref, o_ref, st_ref, b_ref, k_ref):
    @pl.when(pl.program_id(1) == 0)
    def _():
        st_ref[...] = jnp.zeros_like(st_ref)

    C, SB = HG_CHUNK, HG_SUB
    lb = lb_ref[...]
    ri = lax.broadcasted_iota(I32, (C, C), 0)
    ci = lax.broadcasted_iota(I32, (C, C), 1)
    tri = jnp.where(ri >= ci, 1.0, 0.0).astype(BF16)
    row_c = lax.broadcasted_iota(I32, (C, HG_D), 0)
    row_1 = lax.broadcasted_iota(I32, (SUBLANES, HG_D), 0)
    lane_1 = lax.broadcasted_iota(I32, (SUBLANES, C), 1)

    def chunk(ci_, carry):
        rows = pl.ds(pl.multiple_of(ci_ * C, C), C)
        f = lb + (1.0 - lb) * _sigmoid(f_ref[rows, :])
        lf = jnp.log(f) * LOG2E
        hi = lf.astype(BF16)
        r1 = lf - hi.astype(F32)
        mid = r1.astype(BF16)
        lo = (r1 - mid.astype(F32)).astype(BF16)
        b_ref[...] = _dot(tri, hi) + _dot(tri, mid) + _dot(tri, lo)
        k_ref[...] = 1.0 - f

        for h in range(HG_HEADS):
            sl = slice(h * HG_D, (h + 1) * HG_D)
            qh = q_ref[rows, sl]
            vh = v_ref[rows, sl]
            bh = b_ref[:, sl]
            kh = k_ref[:, sl]
            a_rows = [jnp.zeros((SB, C), F32)]
            for i in range(1, C // SB):
                bref = b_ref[i * SB:i * SB + 1, sl]
                qi = (qh[i * SB:(i + 1) * SB] * jnp.exp2(bh[i * SB:(i + 1) * SB] - bref)).astype(BF16)
                ki = (kh * jnp.exp2(jnp.where(row_c < i * SB, bref - bh, -jnp.inf))).astype(BF16)
                a_rows.append(_dot_nt(qi, ki))
            for i in range(C // SB):
                b_top, q_top = bh[i * SB:i * SB + SUBLANES], qh[i * SB:i * SB + SUBLANES]
                b_bot, q_bot = bh[i * SB + SUBLANES:(i + 1) * SB], qh[i * SB + SUBLANES:(i + 1) * SB]
                a_top, a_bot = a_rows[i][:SUBLANES], a_rows[i][SUBLANES:]
                for j in range(SB):
                    s = i * SB + j
                    brow = b_ref[s:s + 1, sl]
                    krow = k_ref[s:s + 1, sl]
                    if j < SUBLANES:
                        e = jnp.exp2(jnp.where(row_1 >= j, b_top - brow, -jnp.inf))
                        r = jnp.sum(e * (q_top * krow), axis=-1, keepdims=True)
                        a_top = jnp.where(lane_1 == s, r, a_top)
                        e = jnp.exp2(b_bot - brow)
                    else:
                        e = jnp.exp2(jnp.where(row_1 >= j - SUBLANES, b_bot - brow, -jnp.inf))
                    r = jnp.sum(e * (q_bot * krow), axis=-1, keepdims=True)
                    a_bot = jnp.where(lane_1 == s, r, a_bot)
                a_rows[i] = jnp.concatenate([a_top, a_bot], axis=0)
            a_mat = jnp.concatenate(a_rows, axis=0)
            st = st_ref[h]
            qe = (qh * jnp.exp2(bh)).astype(BF16)
            vb = vh.astype(BF16)
            o = _dot_nt(qe, st.astype(BF16)) + _dot(a_mat.astype(BF16), vb)
            blast = bh[C - 1:C, :]
            kd = (kh * jnp.exp2(blast - bh)).astype(BF16)
            st_ref[h] = st * jnp.exp2(blast) + _dot_tn(vb, kd)
            gh = g_ref[rows, sl]
            o_ref[rows, sl] = (_rms(o, onw_ref[:, sl]) * (gh * _sigmoid(gh))).astype(o_ref.dtype)
        return carry

    lax.fori_loop(0, HG_STEP // C, chunk, 0, unroll=True)


def _hgrn(proj_h, lb, onw, batch, seq):
    n = proj_h.shape[0]
    nc = seq // HG_STEP
    blk = lambda col: pl.BlockSpec((HG_STEP, HG_WIDTH), lambda b, c, col=col: (b * nc + c, col))
    vec = pl.BlockSpec((1, HG_WIDTH), lambda b, c: (0, 0))
    return pl.pallas_call(
        _hgrn_kernel,
        grid=(batch, nc),
        in_specs=[blk(0), blk(1), blk(2), blk(3), vec, vec],
        out_specs=pl.BlockSpec((HG_STEP, HG_WIDTH), lambda b, c: (b * nc + c, 0)),
        out_shape=jax.ShapeDtypeStruct((n, HG_WIDTH), BF16),
        scratch_shapes=[pltpu.VMEM((HG_HEADS, HG_D, HG_D), F32),
                        pltpu.VMEM((HG_CHUNK, HG_WIDTH), F32),
                        pltpu.VMEM((HG_CHUNK, HG_WIDTH), F32)],
        compiler_params=_cparams(("parallel", "arbitrary")),
        name="hgrn2",
    )(proj_h, proj_h, proj_h, proj_h, lb.reshape(1, HG_WIDTH), onw.reshape(1, HG_WIDTH))


def _rope_lanes(x, cosf, sins, half):
    nl = x.shape[-1]
    lane = lax.broadcasted_iota(I32, x.shape, 1)
    partner = jnp.where(lane % LANES < half, pltpu.roll(x, nl - half, 1), pltpu.roll(x, half, 1))
    return x * cosf + partner * sins


def _dsa_prep_kernel(pd_ref, qnw_ref, kvnw_ref, kinw_ref, wuq_ref, wqr_ref, wuk_ref, wqi_ref,
                     c32_ref, s32_ref, c64_ref, s64_ref,
                     qcat_ref, qidxt_ref, kv_ref, kvt_ref, kidx_ref, widx_ref):
    scale = float((DSA_NOPE + DSA_ROPE) ** -0.5)
    tq = DSA_TQ
    c32, s32, c64, s64 = c32_ref[...], s32_ref[...], c64_ref[...], s64_ref[...]
    cqn = _rms(pd_ref[:, 0:DSA_Q_LORA], qnw_ref[...]).astype(BF16)
    qn = _dot(cqn, wuq_ref[...]).astype(BF16)
    qr = _dot(cqn, wqr_ref[...])
    qi = _dot(cqn, wqi_ref[...])
    hw = DSA_HEADS * LANES
    for h in range(DSA_HEADS):
        sl = slice(h * LANES, (h + 1) * LANES)
        sp = slice(hw + h * LANES, hw + (h + 1) * LANES)
        qcat_ref[h, :, 0:LANES] = (_dot(qn[:, sl], wuk_ref[h]) * scale).astype(BF16)
        qcat_ref[h, :, LANES:2 * LANES] = ((qr[:, sl] * c32 + qr[:, sp] * s32) * scale).astype(BF16)
        qih = (qi[:, sl] * c64 + qi[:, sp] * s64) * float(IDX_DIM ** -0.5)
        for r in range(qih.shape[0] // tq):
            qidxt_ref[r, :, h * tq:(h + 1) * tq] = jnp.transpose(qih[r * tq:(r + 1) * tq]).astype(BF16)
    ckv = _rms(pd_ref[:, 256:384], kvnw_ref[...])
    krope = _rope_lanes(pd_ref[:, 384:512], c32, s32, DSA_ROPE // 2)
    kv_ref[:, 0:LANES] = ckv.astype(BF16)
    kv_ref[:, LANES:2 * LANES] = krope.astype(BF16)
    kvt_ref[0, 0:LANES, :] = jnp.transpose(ckv).astype(BF16)
    kvt_ref[0, LANES:2 * LANES, :] = jnp.transpose(krope).astype(BF16)
    kin = _rms(pd_ref[:, 512:640], kinw_ref[...], n=IDX_DIM)
    kidx_ref[...] = _rope_lanes(kin, c64, s64, IDX_DIM // 2).astype(BF16)
    widx_ref[...] = jnp.transpose(pd_ref[:, 640:768] * float(IDX_HEADS ** -0.5))[0:IDX_HEADS, :]


def _dsa_prep(proj_d, qnw, kvnw, kinw_pad, wuq_pad, wqr_pad, wuk_pad, wqi_pad, tabs, seq, tm):
    n = proj_d.shape[0]
    nt = seq // tm
    full = lambda a: pl.BlockSpec(a.shape, lambda i: (0,) * a.ndim)
    tab = pl.BlockSpec((tm, LANES), lambda i: (i % nt, 0))
    hd = DSA_HEADS
    return pl.pallas_call(
        _dsa_prep_kernel,
        grid=(n // tm,),
        in_specs=[pl.BlockSpec((tm, DSA_GROUP), lambda i: (i, 0)),
                  full(qnw), full(kvnw), full(kinw_pad), full(wuq_pad), full(wqr_pad), full(wuk_pad),
                  full(wqi_pad), tab, tab, tab, tab],
        out_specs=[pl.BlockSpec((hd, tm, 2 * LANES), lambda i: (0, i, 0)),
                   pl.BlockSpec((tm // DSA_TQ, LANES, hd * DSA_TQ), lambda i: (i, 0, 0)),
                   pl.BlockSpec((tm, 2 * LANES), lambda i: (i, 0)),
                   pl.BlockSpec((1, 2 * LANES, tm), lambda i: (i, 0, 0)),
                   pl.BlockSpec((tm, LANES), lambda i: (i, 0)),
                   pl.BlockSpec((IDX_HEADS, tm), lambda i: (0, i))],
        out_shape=[jax.ShapeDtypeStruct((hd, n, 2 * LANES), BF16),
                   jax.ShapeDtypeStruct((n // DSA_TQ, LANES, hd * DSA_TQ), BF16),
                   jax.ShapeDtypeStruct((n, 2 * LANES), BF16),
                   jax.ShapeDtypeStruct((n // tm, 2 * LANES, tm), BF16),
                   jax.ShapeDtypeStruct((n, LANES), BF16),
                   jax.ShapeDtypeStruct((IDX_HEADS, n), F32)],
        compiler_params=_cparams(("parallel",)),
        name="dsa_prep",
    )(proj_d, qnw, kvnw, kinw_pad, wuq_pad, wqr_pad, wuk_pad, wqi_pad, *tabs)


def _dsa_kernel(qcat_ref, qidxt_ref, widx_ref, kv_ref, kvt_ref, kidx_ref, wuv_ref, tril_ref, o_ref,
                sc_ref, x_ref, lg_ref, m_ref, mb_ref, acc_ref, *, topk):
    TQ, TK, H, TS = DSA_TQ, DSA_TK, DSA_HEADS, DSA_TS
    assert TS == TK
    NR = TS // TQ
    step = pl.program_id(1)
    FULL = (TK,) * NR
    DIAG = tuple((r + 1) * TQ for r in range(NR))

    def over_tiles(fn, init):
        carry = lax.fori_loop(0, step, lambda kt, c: fn(kt, FULL, c), init)
        return fn(step, DIAG, carry)

    def key_rows(kt, nk):
        return pl.ds(pl.multiple_of(kt * TK, TK), nk)

    def cols(r):
        return slice(r * TQ, (r + 1) * TQ)

    one, zero = jnp.float32(1.0), jnp.float32(0.0)
    ninf, pinf = jnp.float32(-jnp.inf), jnp.float32(jnp.inf)

    def fold(x, op):
        return op(x.reshape(x.shape[0] // SUBLANES, SUBLANES, x.shape[1]), axis=0)

    def score_tile(kt, nks, carry):
        mx, mn = list(carry[0]), list(carry[1])
        for r, nk in enumerate(nks):
            s_all = jnp.maximum(_dot(kidx_ref[key_rows(kt, nk), :], qidxt_ref[r]), 0.0)
            score = s_all[:, 0:TQ] * widx_ref[0:1, cols(r)]
            for h in range(1, IDX_HEADS):
                score = score + s_all[:, h * TQ:(h + 1) * TQ] * widx_ref[h:h + 1, cols(r)]
            lo_src = score
            if nks is DIAG:
                causal = (lax.broadcasted_iota(I32, (nk, TQ), 0)
                          <= r * TQ + lax.broadcasted_iota(I32, (nk, TQ), 1))
                score = jnp.where(causal, score, ninf)
                lo_src = jnp.where(causal, score, pinf)
            x_ref[kt, 0:nk, cols(r)] = score
            mx[r] = jnp.maximum(mx[r], fold(score, jnp.max))
            mn[r] = jnp.minimum(mn[r], fold(lo_src, jnp.min))
        return tuple(mx), tuple(mn)

    mx, mn = over_tiles(score_tile, (tuple(jnp.full((SUBLANES, TQ), ninf, F32) for _ in range(NR)),
                                     tuple(jnp.full((SUBLANES, TQ), pinf, F32) for _ in range(NR))))
    x_max = jnp.max(jnp.concatenate(mx, axis=1), axis=0, keepdims=True)
    x_min = jnp.min(jnp.concatenate(mn, axis=1), axis=0, keepdims=True)

    def count(pred_fn):
        def block(kt, nk, c, acc):
            return acc + fold(jnp.where(pred_fn(x_ref[kt, 0:nk, c], c), one, zero), jnp.sum)

        def tile(kt, nks, acc):
            if nks is FULL:
                return block(kt, TK, slice(0, TS), acc)
            return jnp.concatenate([block(kt, nk, cols(r), acc[:, cols(r)]) for r, nk in enumerate(nks)], axis=1)
        return jnp.sum(over_tiles(tile, jnp.zeros((SUBLANES, TS), F32)), axis=0, keepdims=True)

    kf = float(topk)

    def bisect(_, carry):
        lo, hi, c_hi = carry
        mid = lo + (hi - lo) * 0.5
        c = count(lambda x, cs: x >= mid[:, cs])
        up = c >= kf
        return jnp.where(up, mid, lo), jnp.where(up, hi, mid), jnp.where(up, c_hi, c)

    hi0 = x_max + (jnp.abs(x_max) + 1.0) * float(2.0 ** -20)
    lo, hi, c_hi = lax.fori_loop(0, DSA_BISECT, bisect, (x_min, hi0, jnp.zeros((1, TS), F32)))
    need = kf - c_hi

    def bias_block(kt, nk, c, seen):
        x = x_ref[kt, 0:nk, c]
        tie = (x >= lo[:, c]) & (x < hi[:, c])
        tie_f = jnp.where(tie, one, zero)
        rank = _dot(tril_ref[0:nk, 0:nk], tie_f.astype(BF16)) + seen
        bias_t = jnp.where((x >= hi[:, c]) | (tie & (rank < need[:, c])), zero, ninf)
        return bias_t, rank[nk - 1:nk, :] + tie_f[nk - 1:nk, :]

    def bias_tile(kt, nks, seen):
        if nks is FULL:
            out = []
            for r0 in range(0, NR, 2):
                c = slice(r0 * TQ, (r0 + 2) * TQ)
                bias_t, s = bias_block(kt, TK, c, seen[:, c])
                sc_ref[kt, cols(r0), :] = jnp.transpose(bias_t[:, 0:TQ])
                sc_ref[kt, cols(r0 + 1), :] = jnp.transpose(bias_t[:, TQ:2 * TQ])
                out.append(s)
            return jnp.concatenate(out, axis=1)
        out = []
        for r, nk in enumerate(nks):
            bias_t, s = bias_block(kt, nk, cols(r), seen[:, cols(r)])
            sc_ref[kt, cols(r), 0:nk] = jnp.transpose(bias_t)
            out.append(s)
        return jnp.concatenate(out, axis=1)

    over_tiles(bias_tile, jnp.zeros((1, TS), F32))

    def pass_a(r, kt, nk):
        qcat = qcat_ref[:, cols(r), :].reshape(H * TQ, 2 * LANES)
        lg = _dot(qcat, kvt_ref[kt, :, 0:nk]).reshape(H, TQ, nk) + sc_ref[kt, cols(r), 0:nk][None]
        lg = lg.reshape(H * TQ, nk)
        lg_ref[r % 2, kt, :, 0:nk] = lg
        part = lg[:, 0:LANES]
        for g in range(1, nk // LANES):
            part = jnp.maximum(part, lg[:, g * LANES:(g + 1) * LANES])
        m_ref[r % 2] = jnp.maximum(m_ref[r % 2], part)

    def pass_b(r, kt, nk):
        m_b = mb_ref[...]
        p = jnp.concatenate(
            [jnp.exp((lg_ref[r % 2, kt, :, g * LANES:(g + 1) * LANES] - m_b).astype(BF16))
             for g in range(nk // LANES)], axis=1)
        kv1 = jnp.concatenate([kv_ref[key_rows(kt, nk), 0:LANES], jnp.ones((nk, LANES), BF16)], axis=1)
        acc_ref[...] += _dot(p, kv1)

    def run(fns):
        def tile(kt, nks, carry):
            for fn, r in fns:
                fn(r, kt, nks[r])
            return carry
        over_tiles(tile, 0)

    m_ref[0] = jnp.full((H * TQ, LANES), -jnp.inf, F32)
    run([(pass_a, 0)])
    for r in range(NR):
        rows = cols(r)
        mb_ref[...] = jnp.broadcast_to(jnp.max(m_ref[r % 2], axis=-1, keepdims=True), (H * TQ, LANES))
        acc_ref[...] = jnp.zeros_like(acc_ref)
        if r + 1 < NR:
            m_ref[(r + 1) % 2] = jnp.full((H * TQ, LANES), -jnp.inf, F32)
            run([(pass_a, r + 1), (pass_b, r)])
        else:
            run([(pass_b, r)])
        o_lat = (acc_ref[:, 0:LANES] / acc_ref[:, LANES:2 * LANES]).astype(BF16)
        for pr in range(H // 2):
            h0, h1 = 2 * pr, 2 * pr + 1
            o_ref[rows, pr * LANES:(pr + 1) * LANES] = (
                _dot(o_lat[h0 * TQ:(h0 + 1) * TQ], wuv_ref[h0]) + _dot(o_lat[h1 * TQ:(h1 + 1) * TQ], wuv_ref[h1])
            ).astype(o_ref.dtype)


def _dsa(qcat, qidxt, widx, kv, kvt, kidx, wuv_pad, batch, seq, topk):
    n = kv.shape[0]
    ns = seq // DSA_TS
    nkt_max = seq // DSA_TK
    nr = DSA_TS // DSA_TQ
    H = DSA_HEADS
    return pl.pallas_call(
        functools.partial(_dsa_kernel, topk=topk),
        grid=(batch, ns),
        in_specs=[pl.BlockSpec((H, DSA_TS, 2 * LANES), lambda b, s: (0, b * ns + s, 0)),
                  pl.BlockSpec((nr, LANES, H * DSA_TQ), lambda b, s: (b * ns + s, 0, 0)),
                  pl.BlockSpec((IDX_HEADS, DSA_TS), lambda b, s: (0, b * ns + s)),
                  pl.BlockSpec((seq, 2 * LANES), lambda b, s: (b, 0)),
                  pl.BlockSpec((nkt_max, 2 * LANES, DSA_TK), lambda b, s: (b, 0, 0)),
                  pl.BlockSpec((seq, LANES), lambda b, s: (b, 0)),
                  pl.BlockSpec(wuv_pad.shape, lambda b, s: (0, 0, 0)),
                  pl.BlockSpec((DSA_TK, DSA_TK), lambda b, s: (0, 0))],
        out_specs=pl.BlockSpec((DSA_TS, H * DSA_VDIM), lambda b, s: (b * ns + s, 0)),
        out_shape=jax.ShapeDtypeStruct((n, H * DSA_VDIM), BF16),
        scratch_shapes=[pltpu.VMEM((nkt_max, DSA_TS, DSA_TK), F32),
                        pltpu.VMEM((nkt_max, DSA_TK, DSA_TS), F32),
                        pltpu.VMEM((2, nkt_max, H * DSA_TQ, DSA_TK), F32),
                        pltpu.VMEM((2, H * DSA_TQ, LANES), F32),
                        pltpu.VMEM((H * DSA_TQ, LANES), F32),
                        pltpu.VMEM((H * DSA_TQ, 2 * LANES), F32)],
        compiler_params=_cparams(("parallel", "arbitrary")),
        name="dsa_attn",
    )(qcat, qidxt, widx, kv, kvt, kidx, wuv_pad, jnp.tri(DSA_TK, k=-1, dtype=BF16))


def _mem_attn_kernel(q_ref, k_ref, v_ref, o_ref):
    scale = float(MEM_DH ** -0.5)
    for h in range(MEM_HEADS):
        sl = slice(h * MEM_DH, (h + 1) * MEM_DH)
        lg = _dot_nt((q_ref[:, sl] * scale).astype(BF16), k_ref[:, sl])
        p = jnp.exp(lg - jnp.max(lg, axis=-1, keepdims=True))
        o = _dot(p.astype(BF16), v_ref[:, sl]) / jnp.sum(p, axis=-1, keepdims=True)
        o_ref[:, sl] = o.astype(o_ref.dtype)


def _mem_attn(proj_m, k_m, v_m, seq, tm):
    n = proj_m.shape[0]
    nt = seq // tm
    m = k_m.shape[0] // (n // seq)
    return pl.pallas_call(
        _mem_attn_kernel,
        grid=(n // tm,),
        in_specs=[pl.BlockSpec((tm, MEM_WIDTH), lambda i: (i, 0)),
                  pl.BlockSpec((m, MEM_WIDTH), lambda i: (i // nt, 0)),
                  pl.BlockSpec((m, MEM_WIDTH), lambda i: (i // nt, 0))],
        out_specs=pl.BlockSpec((tm, MEM_WIDTH), lambda i: (i, 0)),
        out_shape=jax.ShapeDtypeStruct((n, MEM_WIDTH), BF16),
        compiler_params=_cparams(("parallel",)),
        name="mem_attn",
    )(proj_m, k_m, v_m)


def _merge_mlp_kernel(x_ref, anw_ref, oa_ref, ob_ref, oc_ref, wg_ref, wbr_ref, wo_ref,
                      mnw_ref, wu_ref, wd_ref, fw_ref, o_ref, *, ff_chunk):
    x = x_ref[...]
    d = x.shape[-1]
    xn = _rms(x, anw_ref[...]).astype(BF16)
    merged = None
    for br, o_br in enumerate((oa_ref, ob_ref, oc_ref)):
        gate = _sigmoid(_dot(xn, wg_ref[:, br * d:(br + 1) * d]))
        term = gate * _dot(o_br[...], wbr_ref[br])
        merged = term if merged is None else merged + term
    h = x + _dot(merged.astype(BF16), wo_ref[...])
    m = _rms(h, mnw_ref[...]).astype(BF16)
    acc = h
    for c0 in range(0, wu_ref.shape[1], ff_chunk):
        u = jnp.maximum(_dot(m, wu_ref[:, c0:c0 + ff_chunk]), 0.0)
        acc = acc + _dot((u * u).astype(BF16), wd_ref[c0:c0 + ff_chunk, :])
    o_ref[...] = _rms(acc, fw_ref[...])


def _merge_mlp(x, anw, o_a, o_b, o_c, w_gate, w_br, w_out, mnw, w_up, w_down, fw, tm, ff_chunk=1024):
    n, d = x.shape
    row = lambda wd: pl.BlockSpec((tm, wd), lambda i: (i, 0))
    full = lambda a: pl.BlockSpec(a.shape, lambda i: (0,) * a.ndim, pipeline_mode=pl.Buffered(1))
    return pl.pallas_call(
        functools.partial(_merge_mlp_kernel, ff_chunk=ff_chunk),
        grid=(n // tm,),
        in_specs=[row(d), full(anw), row(o_a.shape[1]), row(o_b.shape[1]), row(o_c.shape[1]),
                  full(w_gate), full(w_br), full(w_out), full(mnw), full(w_up), full(w_down), full(fw)],
        out_specs=row(d),
        out_shape=jax.ShapeDtypeStruct((n, d), F32),
        compiler_params=_cparams(("parallel",)),
        name="merge_mlp",
    )(x, anw, o_a, o_b, o_c, w_gate, w_br, w_out, mnw, w_up, w_down, fw)


def _pad_cols(w, width):
    return jnp.pad(w, ((0, 0), (0, width - w.shape[1])))


def _head_pad(w, width):
    return jnp.pad(w, ((0, 0), (0, 0), (0, width - w.shape[2]))).reshape(w.shape[0], -1)


def _with_rope_partner(w):
    half = w.shape[2] // 2
    swapped = jnp.concatenate([w[..., half:], w[..., :half]], axis=-1)
    return jnp.concatenate([_head_pad(w, LANES), _head_pad(swapped, LANES)], axis=1)


def _rope_tables(seq, d):
    inv = ROPE_THETA ** (-jnp.arange(0, d, 2, dtype=F32) / d)
    ang = jnp.arange(seq, dtype=F32)[:, None] * inv[None, :]
    cos, sin = jnp.cos(ang), jnp.sin(ang)
    cosf = _pad_cols(jnp.concatenate([cos, cos], axis=1), LANES)
    sins = _pad_cols(jnp.concatenate([-sin, sin], axis=1), LANES)
    return cosf, sins


def kernel(x, mem, attn_norm_w, w_in, hgrn_lower_bounds, hgrn_out_norm_w, dsa_q_norm_w, dsa_kv_norm_w,
           dsa_kidx_norm_w, dsa_w_uq, dsa_w_qr, dsa_w_uk, dsa_w_uv, dsa_w_qidx, mem_norm_w, w_mem_kv,
           w_br_hgrn, w_br_dsa, w_br_mem, w_out, mlp_norm_w, w_mlp_up, w_mlp_down, final_norm_w):
    B, T, D = x.shape
    N = B * T
    M = mem.shape[1]
    l = 0
    topk = min(TOPK_MAX, T // 4)
    tm = 512
    x2 = x.reshape(N, D)

    wi = w_in[l]
    o = 0
    cols = {}
    for name, wd in (("hq", HG_WIDTH), ("hf", HG_WIDTH), ("hv", HG_WIDTH), ("hg", HG_WIDTH),
                     ("cq", DSA_Q_LORA), ("ckv", DSA_KV_LORA), ("kr", DSA_ROPE), ("kidx", IDX_DIM),
                     ("widx", IDX_HEADS), ("mq", MEM_WIDTH), ("gate", N_BRANCH * D)):
        cols[name] = wi[:, o:o + wd]
        o += wd
    w_proj = jnp.concatenate(
        [cols["hq"], cols["hf"], cols["hv"], cols["hg"], cols["mq"], cols["cq"], cols["ckv"],
         _pad_cols(cols["kr"], LANES), _pad_cols(cols["kidx"], LANES), _pad_cols(cols["widx"], LANES)],
        axis=1).astype(BF16)
    w_gate = cols["gate"].astype(BF16)

    proj_h, proj_m, proj_d = _norm_proj(x2, attn_norm_w[l], w_proj, (4 * HG_WIDTH, MEM_WIDTH, DSA_GROUP),
                                        (F32, F32, F32), tm)

    lb_all = jnp.cumsum(jax.nn.softmax(hgrn_lower_bounds.astype(F32), axis=0), axis=0)
    o_a = _hgrn(proj_h, lb_all[l], hgrn_out_norm_w[l], B, T)

    wuq_pad = _head_pad(dsa_w_uq[l], LANES).astype(BF16)
    wqr_pad = _with_rope_partner(dsa_w_qr[l]).astype(BF16)
    wqi_pad = _with_rope_partner(dsa_w_qidx[l]).astype(BF16)
    wuk_pad = jnp.pad(dsa_w_uk[l], ((0, 0), (0, LANES - DSA_NOPE), (0, 0))).astype(BF16)
    wuv = dsa_w_uv[l]
    wuv_pad = jnp.stack([jnp.pad(wuv[h], ((0, 0), ((h % 2) * DSA_VDIM, LANES - DSA_VDIM - (h % 2) * DSA_VDIM)))
                         for h in range(DSA_HEADS)]).astype(BF16)
    kinw_pad = _pad_cols(dsa_kidx_norm_w[l].reshape(1, IDX_DIM), LANES)
    tabs = _rope_tables(T, DSA_ROPE) + _rope_tables(T, IDX_DIM)
    assert tm == DSA_TK and tm % DSA_TQ == 0
    qcat, qidxt, kv, kvt, kidx, widx = _dsa_prep(
        proj_d, dsa_q_norm_w[l].reshape(1, -1), dsa_kv_norm_w[l].reshape(1, -1), kinw_pad,
        wuq_pad, wqr_pad, wuk_pad, wqi_pad, tabs, T, tm)
    o_b = _dsa(qcat, qidxt, widx, kv, kvt, kidx, wuv_pad, B, T, topk)

    k_m, v_m = _norm_proj(mem.reshape(B * M, D), mem_norm_w[l], w_mem_kv[l].astype(BF16),
                          (MEM_WIDTH, MEM_WIDTH), (BF16, BF16), min(tm, B * M))
    o_c = _mem_attn(proj_m, k_m, v_m, T, tm)

    w_br = jnp.stack([w_br_hgrn[l], w_br_dsa[l], w_br_mem[l]]).astype(BF16)
    out = _merge_mlp(x2, attn_norm_w[l].reshape(1, D), o_a, o_b, o_c, w_gate, w_br, w_out[l].astype(BF16),
                     mlp_norm_w[l].reshape(1, D), w_mlp_up[l].astype(BF16), w_mlp_down[l].astype(BF16),
                     final_norm_w.reshape(1, D), tm)
    return out.reshape(B, T, D)
```

```python
import functools

import jax
import jax.numpy as jnp
from jax import lax
from jax.experimental import pallas as pl
from jax.experimental.pallas import tpu as pltpu

F32 = jnp.float32
BF16 = jnp.bfloat16
I32 = jnp.int32

EPS = 1e-6
LOG2E = 1.4426950408889634
ROPE_THETA = 10000.0
LANES = 128
SUBLANES = 8
HG_HEADS, HG_D, HG_CHUNK = 4, 128, 64
HG_WIDTH = HG_HEADS * HG_D
HG_SUB = 16
HG_STEP = 256
DSA_HEADS, DSA_Q_LORA, DSA_KV_LORA = 8, 256, 128
DSA_NOPE, DSA_ROPE, DSA_VDIM = 64, 32, 64
IDX_HEADS, IDX_DIM = 8, 64
TOPK_MAX = 256
MEM_HEADS, MEM_DH = 4, 128
MEM_WIDTH = MEM_HEADS * MEM_DH
N_BRANCH = 3
DSA_GROUP = 6 * LANES
VMEM_LIMIT = 56 * 1024 * 1024

DSA_TQ = 128
DSA_TS = 512
DSA_TK = 512
DSA_BISECT = 28


def _cparams(sem):
    return pltpu.CompilerParams(dimension_semantics=sem, vmem_limit_bytes=VMEM_LIMIT)


def _rms(x, w, n=None):
    n = x.shape[-1] if n is None else n
    ms = jnp.sum(x * x, axis=-1, keepdims=True) * (1.0 / n)
    return x * lax.rsqrt(ms + EPS) * w


def _sigmoid(x):
    return 1.0 / (1.0 + jnp.exp(-x))


def _dot(a, b):
    return jnp.dot(a, b, preferred_element_type=F32)


def _dot_nt(a, b):
    return lax.dot_general(a, b, (((1,), (1,)), ((), ())), preferred_element_type=F32)


def _dot_tn(a, b):
    return lax.dot_general(a, b, (((0,), (0,)), ((), ())), preferred_element_type=F32)


def _norm_proj_kernel(x_ref, nw_ref, w_ref, *out_refs):
    xn = _rms(x_ref[...], nw_ref[...]).astype(BF16)
    off = 0
    for o_ref in out_refs:
        wd = o_ref.shape[-1]
        o_ref[...] = _dot(xn, w_ref[:, off:off + wd]).astype(o_ref.dtype)
        off += wd


def _norm_proj(x, nw, w, widths, dtypes, tm):
    n, d = x.shape
    return pl.pallas_call(
        _norm_proj_kernel,
        grid=(n // tm,),
        in_specs=[pl.BlockSpec((tm, d), lambda i: (i, 0)),
                  pl.BlockSpec((1, d), lambda i: (0, 0)),
                  pl.BlockSpec(w.shape, lambda i: (0, 0))],
        out_specs=[pl.BlockSpec((tm, wd), lambda i: (i, 0)) for wd in widths],
        out_shape=[jax.ShapeDtypeStruct((n, wd), dt) for wd, dt in zip(widths, dtypes)],
        compiler_params=_cparams(("parallel",)),
        name="norm_proj",
    )(x, nw.reshape(1, d), w)


def _hgrn_kernel(q_ref, f_ref, v_ref, g_ref, lb_ref, onw_ref, o_ref, st_ref, b_ref, k_ref):
    @pl.when(pl.program_id(1) == 0)
    def _():
        st_ref[...] = jnp.zeros_like(st_ref)

    C, SB = HG_CHUNK, HG_SUB
    lb = lb_ref[...]
    ri = lax.broadcasted_iota(I32, (C, C), 0)
    ci = lax.broadcasted_iota(I32, (C, C), 1)
    tri = jnp.where(ri >= ci, 1.0, 0.0).astype(BF16)
    row_c = lax.broadcasted_iota(I32, (C, HG_D), 0)
    row_1 = lax.broadcasted_iota(I32, (SUBLANES, HG_D), 0)
    lane_1 = lax.broadcasted_iota(I32, (SUBLANES, C), 1)

    def chunk(ci_, carry):
        rows = pl.ds(pl.multiple_of(ci_ * C, C), C)
        f = lb + (1.0 - lb) * _sigmoid(f_ref[rows, :])
        lf = jnp.log(f) * LOG2E
        hi = lf.astype(BF16)
        r1 = lf - hi.astype(F32)
        mid = r1.astype(BF16)
        lo = (r1 - mid.astype(F32)).astype(BF16)
        b_ref[...] = _dot(tri, hi) + _dot(tri, mid) + _dot(tri, lo)
        k_ref[...] = 1.0 - f

        for h in range(HG_HEADS):
            sl = slice(h * HG_D, (h + 1) * HG_D)
            qh = q_ref[rows, sl]
            vh = v_ref[rows, sl]
            bh = b_ref[:, sl]
            kh = k_ref[:, sl]
            a_rows = [jnp.zeros((SB, C), F32)]
            for i in range(1, C // SB):
                bref = b_ref[i * SB:i * SB + 1, sl]
                qi = (qh[i * SB:(i + 1) * SB] * jnp.exp2(bh[i * SB:(i + 1) * SB] - bref)).astype(BF16)
                ki = (kh * jnp.exp2(jnp.where(row_c < i * SB, bref - bh, -jnp.inf))).astype(BF16)
                a_rows.append(_dot_nt(qi, ki))
            for i in range(C // SB):
                b_top, q_top = bh[i * SB:i * SB + SUBLANES], qh[i * SB:i * SB + SUBLANES]
                b_bot, q_bot = bh[i * SB + SUBLANES:(i + 1) * SB], qh[i * SB + SUBLANES:(i + 1) * SB]
                a_top, a_bot = a_rows[i][:SUBLANES], a_rows[i][SUBLANES:]
                for j in range(SB):
                    s = i * SB + j
                    brow = b_ref[s:s + 1, sl]
                    krow = k_ref[s:s + 1, sl]
                    if j < SUBLANES:
                        e = jnp.exp2(jnp.where(row_1 >= j, b_top - brow, -jnp.inf))
                        r = jnp.sum(e * (q_top * krow), axis=-1, keepdims=True)
                        a_top = jnp.where(lane_1 == s, r, a_top)
                        e = jnp.exp2(b_bot - brow)
                    else:
                        e = jnp.exp2(jnp.where(row_1 >= j - SUBLANES, b_bot - brow, -jnp.inf))
                    r = jnp.sum(e * (q_bot * krow), axis=-1, keepdims=True)
                    a_bot = jnp.where(lane_1 == s, r, a_bot)
                a_rows[i] = jnp.concatenate([a_top, a_bot], axis=0)
            a_mat = jnp.concatenate(a_rows, axis=0)
            st = st_ref[h]
            qe = (qh * jnp.exp2(bh)).astype(BF16)
            vb = vh.astype(BF16)
            o = _dot_nt(qe, st.astype(BF16)) + _dot(a_mat.astype(BF16), vb)
            blast = bh[C - 1:C, :]
            kd = (kh * jnp.exp2(blast - bh)).astype(BF16)
            st_ref[h] = st * jnp.exp2(blast) + _dot_tn(vb, kd)
            gh = g_ref[rows, sl]
            o_ref[rows, sl] = (_rms(o, onw_ref[:, sl]) * (gh * _sigmoid(gh))).astype(o_ref.dtype)
        return carry

    lax.fori_loop(0, HG_STEP // C, chunk, 0, unroll=True)


def _hgrn(proj_h, lb, onw, batch, seq):
    n = proj_h.shape[0]
    nc = seq // HG_STEP
    blk = lambda col: pl.BlockSpec((HG_STEP, HG_WIDTH), lambda b, c, col=col: (b * nc + c, col))
    vec = pl.BlockSpec((1, HG_WIDTH), lambda b, c: (0, 0))
    return pl.pallas_call(
        _hgrn_kernel,
        grid=(batch, nc),
        in_specs=[blk(0), blk(1), blk(2), blk(3), vec, vec],
        out_specs=pl.BlockSpec((HG_STEP, HG_WIDTH), lambda b, c: (b * nc + c, 0)),
        out_shape=jax.ShapeDtypeStruct((n, HG_WIDTH), BF16),
        scratch_shapes=[pltpu.VMEM((HG_HEADS, HG_D, HG_D), F32),
                        pltpu.VMEM((HG_CHUNK, HG_WIDTH), F32),
                        pltpu.VMEM((HG_CHUNK, HG_WIDTH), F32)],
        compiler_params=_cparams(("parallel", "arbitrary")),
        name="hgrn2",
    )(proj_h, proj_h, proj_h, proj_h, lb.reshape(1, HG_WIDTH), onw.reshape(1, HG_WIDTH))


def _rope_lanes(x, cosf, sins, half):
    nl = x.shape[-1]
    lane = lax.broadcasted_iota(I32, x.shape, 1)
    partner = jnp.where(lane % LANES < half, pltpu.roll(x, nl - half, 1), pltpu.roll(x, half, 1))
    return x * cosf + partner * sins


def _dsa_prep_kernel(pd_ref, qnw_ref, kvnw_ref, kinw_ref, wuq_ref, wqr_ref, wuk_ref, wqi_ref,
                     c32_ref, s32_ref, c64_ref, s64_ref,
                     qcat_ref, qidxt_ref, kv_ref, kvt_ref, kidx_ref, widx_ref):
    scale = float((DSA_NOPE + DSA_ROPE) ** -0.5)
    tq = DSA_TQ
    c32, s32, c64, s64 = c32_ref[...], s32_ref[...], c64_ref[...], s64_ref[...]
    cqn = _rms(pd_ref[:, 0:DSA_Q_LORA], qnw_ref[...]).astype(BF16)
    qn = _dot(cqn, wuq_ref[...]).astype(BF16)
    qr = _dot(cqn, wqr_ref[...])
    qi = _dot(cqn, wqi_ref[...])
    hw = DSA_HEADS * LANES
    for h in range(DSA_HEADS):
        sl = slice(h * LANES, (h + 1) * LANES)
        sp = slice(hw + h * LANES, hw + (h + 1) * LANES)
        qcat_ref[h, :, 0:LANES] = (_dot(qn[:, sl], wuk_ref[h]) * scale).astype(BF16)
        qcat_ref[h, :, LANES:2 * LANES] = ((qr[:, sl] * c32 + qr[:, sp] * s32) * scale).astype(BF16)
        qih = (qi[:, sl] * c64 + qi[:, sp] * s64) * float(IDX_DIM ** -0.5)
        for r in range(qih.shape[0] // tq):
            qidxt_ref[r, :, h * tq:(h + 1) * tq] = jnp.transpose(qih[r * tq:(r + 1) * tq]).astype(BF16)
    ckv = _rms(pd_ref[:, 256:384], kvnw_ref[...])
    krope = _rope_lanes(pd_ref[:, 384:512], c32, s32, DSA_ROPE // 2)
    kv_ref[:, 0:LANES] = ckv.astype(BF16)
    kv_ref[:, LANES:2 * LANES] = krope.astype(BF16)
    kvt_ref[0, 0:LANES, :] = jnp.transpose(ckv).astype(BF16)
    kvt_ref[0, LANES:2 * LANES, :] = jnp.transpose(krope).astype(BF16)
    kin = _rms(pd_ref[:, 512:640], kinw_ref[...], n=IDX_DIM)
    kidx_ref[...] = _rope_lanes(kin, c64, s64, IDX_DIM // 2).astype(BF16)
    widx_ref[...] = jnp.transpose(pd_ref[:, 640:768] * float(IDX_HEADS ** -0.5))[0:IDX_HEADS, :]


def _dsa_prep(proj_d, qnw, kvnw, kinw_pad, wuq_pad, wqr_pad, wuk_pad, wqi_pad, tabs, seq, tm):
    n = proj_d.shape[0]
    nt = seq // tm
    full = lambda a: pl.BlockSpec(a.shape, lambda i: (0,) * a.ndim)
    tab = pl.BlockSpec((tm, LANES), lambda i: (i % nt, 0))
    hd = DSA_HEADS
    return pl.pallas_call(
        _dsa_prep_kernel,
        grid=(n // tm,),
        in_specs=[pl.BlockSpec((tm, DSA_GROUP), lambda i: (i, 0)),
                  full(qnw), full(kvnw), full(kinw_pad), full(wuq_pad), full(wqr_pad), full(wuk_pad),
                  full(wqi_pad), tab, tab, tab, tab],
        out_specs=[pl.BlockSpec((hd, tm, 2 * LANES), lambda i: (0, i, 0)),
                   pl.BlockSpec((tm // DSA_TQ, LANES, hd * DSA_TQ), lambda i: (i, 0, 0)),
                   pl.BlockSpec((tm, 2 * LANES), lambda i: (i, 0)),
                   pl.BlockSpec((1, 2 * LANES, tm), lambda i: (i, 0, 0)),
                   pl.BlockSpec((tm, LANES), lambda i: (i, 0)),
                   pl.BlockSpec((IDX_HEADS, tm), lambda i: (0, i))],
        out_shape=[jax.ShapeDtypeStruct((hd, n, 2 * LANES), BF16),
                   jax.ShapeDtypeStruct((n // DSA_TQ, LANES, hd * DSA_TQ), BF16),
                   jax.ShapeDtypeStruct((n, 2 * LANES), BF16),
                   jax.ShapeDtypeStruct((n // tm, 2 * LANES, tm), BF16),
                   jax.ShapeDtypeStruct((n, LANES), BF16),
                   jax.ShapeDtypeStruct((IDX_HEADS, n), F32)],
        compiler_params=_cparams(("parallel",)),
        name="dsa_prep",
    )(proj_d, qnw, kvnw, kinw_pad, wuq_pad, wqr_pad, wuk_pad, wqi_pad, *tabs)


def _dsa_kernel(qcat_ref, qidxt_ref, widx_ref, kv_ref, kvt_ref, kidx_ref, wuv_ref, tril_ref, o_ref,
                sc_ref, x_ref, lg_ref, m_ref, mb_ref, acc_ref, *, topk):
    TQ, TK, H, TS = DSA_TQ, DSA_TK, DSA_HEADS, DSA_TS
    assert TS == TK
    NR = TS // TQ
    step = pl.program_id(1)
    FULL = (TK,) * NR
    DIAG = tuple((r + 1) * TQ for r in range(NR))

    def over_tiles(fn, init):
        carry = lax.fori_loop(0, step, lambda kt, c: fn(kt, FULL, c), init)
        return fn(step, DIAG, carry)

    def key_rows(kt, nk):
        return pl.ds(pl.multiple_of(kt * TK, TK), nk)

    def cols(r):
        return slice(r * TQ, (r + 1) * TQ)

    one, zero = jnp.float32(1.0), jnp.float32(0.0)
    ninf, pinf = jnp.float32(-jnp.inf), jnp.float32(jnp.inf)

    def fold(x, op):
        return op(x.reshape(x.shape[0] // SUBLANES, SUBLANES, x.shape[1]), axis=0)

    def score_tile(kt, nks, carry):
        mx, mn = list(carry[0]), list(carry[1])
        for r, nk in enumerate(nks):
            s_all = jnp.maximum(_dot(kidx_ref[key_rows(kt, nk), :], qidxt_ref[r]), 0.0)
            score = s_all[:, 0:TQ] * widx_ref[0:1, cols(r)]
            for h in range(1, IDX_HEADS):
                score = score + s_all[:, h * TQ:(h + 1) * TQ] * widx_ref[h:h + 1, cols(r)]
            lo_src = score
            if nks is DIAG:
                causal = (lax.broadcasted_iota(I32, (nk, TQ), 0)
                          <= r * TQ + lax.broadcasted_iota(I32, (nk, TQ), 1))
                score = jnp.where(causal, score, ninf)
                lo_src = jnp.where(causal, score, pinf)
            x_ref[kt, 0:nk, cols(r)] = score
            mx[r] = jnp.maximum(mx[r], fold(score, jnp.max))
            mn[r] = jnp.minimum(mn[r], fold(lo_src, jnp.min))
        return tuple(mx), tuple(mn)

    mx, mn = over_tiles(score_tile, (tuple(jnp.full((SUBLANES, TQ), ninf, F32) for _ in range(NR)),
                                     tuple(jnp.full((SUBLANES, TQ), pinf, F32) for _ in range(NR))))
    x_max = jnp.max(jnp.concatenate(mx, axis=1), axis=0, keepdims=True)
    x_min = jnp.min(jnp.concatenate(mn, axis=1), axis=0, keepdims=True)

    def count(pred_fn):
        def block(kt, nk, c, acc):
            return acc + fold(jnp.where(pred_fn(x_ref[kt, 0:nk, c], c), one, zero), jnp.sum)

        def tile(kt, nks, acc):
            if nks is FULL:
                return block(kt, TK, slice(0, TS), acc)
            return jnp.concatenate([block(kt, nk, cols(r), acc[:, cols(r)]) for r, nk in enumerate(nks)], axis=1)
        return jnp.sum(over_tiles(tile, jnp.zeros((SUBLANES, TS), F32)), axis=0, keepdims=True)

    kf = float(topk)

    def bisect(_, carry):
        lo, hi, c_hi = carry
        mid = lo + (hi - lo) * 0.5
        c = count(lambda x, cs: x >= mid[:, cs])
        up = c >= kf
        return jnp.where(up, mid, lo), jnp.where(up, hi, mid), jnp.where(up, c_hi, c)

    hi0 = x_max + (jnp.abs(x_max) + 1.0) * float(2.0 ** -20)
    lo, hi, c_hi = lax.fori_loop(0, DSA_BISECT, bisect, (x_min, hi0, jnp.zeros((1, TS), F32)))
    need = kf - c_hi

    def bias_block(kt, nk, c, seen):
        x = x_ref[kt, 0:nk, c]
        tie = (x >= lo[:, c]) & (x < hi[:, c])
        tie_f = jnp.where(tie, one, zero)
        rank = _dot(tril_ref[0:nk, 0:nk], tie_f.astype(BF16)) + seen
        bias_t = jnp.where((x >= hi[:, c]) | (tie & (rank < need[:, c])), zero, ninf)
        return bias_t, rank[nk - 1:nk, :] + tie_f[nk - 1:nk, :]

    def bias_tile(kt, nks, seen):
        if nks is FULL:
            out = []
            for r0 in range(0, NR, 2):
                c = slice(r0 * TQ, (r0 + 2) * TQ)
                bias_t, s = bias_block(kt, TK, c, seen[:, c])
                sc_ref[kt, cols(r0), :] = jnp.transpose(bias_t[:, 0:TQ])
                sc_ref[kt, cols(r0 + 1), :] = jnp.transpose(bias_t[:, TQ:2 * TQ])
                out.append(s)
            return jnp.concatenate(out, axis=1)
        out = []
        for r, nk in enumerate(nks):
            bias_t, s = bias_block(kt, nk, cols(r), seen[:, cols(r)])
            sc_ref[kt, cols(r), 0:nk] = jnp.transpose(bias_t)
            out.append(s)
        return jnp.concatenate(out, axis=1)

    over_tiles(bias_tile, jnp.zeros((1, TS), F32))

    def pass_a(r, kt, nk):
        qcat = qcat_ref[:, cols(r), :].reshape(H * TQ, 2 * LANES)
        lg = _dot(qcat, kvt_ref[kt, :, 0:nk]).reshape(H, TQ, nk) + sc_ref[kt, cols(r), 0:nk][None]
        lg = lg.reshape(H * TQ, nk)
        lg_ref[r % 2, kt, :, 0:nk] = lg
        part = lg[:, 0:LANES]
        for g in range(1, nk // LANES):
            part = jnp.maximum(part, lg[:, g * LANES:(g + 1) * LANES])
        m_ref[r % 2] = jnp.maximum(m_ref[r % 2], part)

    def pass_b(r, kt, nk):
        m_b = mb_ref[...]
        p = jnp.concatenate(
            [jnp.exp((lg_ref[r % 2, kt, :, g * LANES:(g + 1) * LANES] - m_b).astype(BF16))
             for g in range(nk // LANES)], axis=1)
        kv1 = jnp.concatenate([kv_ref[key_rows(kt, nk), 0:LANES], jnp.ones((nk, LANES), BF16)], axis=1)
        acc_ref[...] += _dot(p, kv1)

    def run(fns):
        def tile(kt, nks, carry):
            for fn, r in fns:
                fn(r, kt, nks[r])
            return carry
        over_tiles(tile, 0)

    m_ref[0] = jnp.full((H * TQ, LANES), -jnp.inf, F32)
    run([(pass_a, 0)])
    for r in range(NR):
        rows = cols(r)
        mb_ref[...] = jnp.broadcast_to(jnp.max(m_ref[r % 2], axis=-1, keepdims=True), (H * TQ, LANES))
        acc_ref[...] = jnp.zeros_like(acc_ref)
        if r + 1 < NR:
            m_ref[(r + 1) % 2] = jnp.full((H * TQ, LANES), -jnp.inf, F32)
            run([(pass_a, r + 1), (pass_b, r)])
        else:
            run([(pass_b, r)])
        o_lat = (acc_ref[:, 0:LANES] / acc_ref[:, LANES:2 * LANES]).astype(BF16)
        for pr in range(H // 2):
            h0, h1 = 2 * pr, 2 * pr + 1
            o_ref[rows, pr * LANES:(pr + 1) * LANES] = (
                _dot(o_lat[h0 * TQ:(h0 + 1) * TQ], wuv_ref[h0]) + _dot(o_lat[h1 * TQ:(h1 + 1) * TQ], wuv_ref[h1])
            ).astype(o_ref.dtype)


def _dsa(qcat, qidxt, widx, kv, kvt, kidx, wuv_pad, batch, seq, topk):
    n = kv.shape[0]
    ns = seq // DSA_TS
    nkt_max = seq // DSA_TK
    nr = DSA_TS // DSA_TQ
    H = DSA_HEADS
    return pl.pallas_call(
        functools.partial(_dsa_kernel, topk=topk),
        grid=(batch, ns),
        in_specs=[pl.BlockSpec((H, DSA_TS, 2 * LANES), lambda b, s: (0, b * ns + s, 0)),
                  pl.BlockSpec((nr, LANES, H * DSA_TQ), lambda b, s: (b * ns + s, 0, 0)),
                  pl.BlockSpec((IDX_HEADS, DSA_TS), lambda b, s: (0, b * ns + s)),
                  pl.BlockSpec((seq, 2 * LANES), lambda b, s: (b, 0)),
                  pl.BlockSpec((nkt_max, 2 * LANES, DSA_TK), lambda b, s: (b, 0, 0)),
                  pl.BlockSpec((seq, LANES), lambda b, s: (b, 0)),
                  pl.BlockSpec(wuv_pad.shape, lambda b, s: (0, 0, 0)),
                  pl.BlockSpec((DSA_TK, DSA_TK), lambda b, s: (0, 0))],
        out_specs=pl.BlockSpec((DSA_TS, H * DSA_VDIM), lambda b, s: (b * ns + s, 0)),
        out_shape=jax.ShapeDtypeStruct((n, H * DSA_VDIM), BF16),
        scratch_shapes=[pltpu.VMEM((nkt_max, DSA_TS, DSA_TK), F32),
                        pltpu.VMEM((nkt_max, DSA_TK, DSA_TS), F32),
                        pltpu.VMEM((2, nkt_max, H * DSA_TQ, DSA_TK), F32),
                        pltpu.VMEM((2, H * DSA_TQ, LANES), F32),
                        pltpu.VMEM((H * DSA_TQ, LANES), F32),
                        pltpu.VMEM((H * DSA_TQ, 2 * LANES), F32)],
        compiler_params=_cparams(("parallel", "arbitrary")),
        name="dsa_attn",
    )(qcat, qidxt, widx, kv, kvt, kidx, wuv_pad, jnp.tri(DSA_TK, k=-1, dtype=BF16))


def _mem_attn_kernel(q_ref, k_ref, v_ref, o_ref):
    scale = float(MEM_DH ** -0.5)
    for h in range(MEM_HEADS):
        sl = slice(h * MEM_DH, (h + 1) * MEM_DH)
        lg = _dot_nt((q_ref[:, sl] * scale).astype(BF16), k_ref[:, sl])
        p = jnp.exp(lg - jnp.max(lg, axis=-1, keepdims=True))
        o = _dot(p.astype(BF16), v_ref[:, sl]) / jnp.sum(p, axis=-1, keepdims=True)
        o_ref[:, sl] = o.astype(o_ref.dtype)


def _mem_attn(proj_m, k_m, v_m, seq, tm):
    n = proj_m.shape[0]
    nt = seq // tm
    m = k_m.shape[0] // (n // seq)
    return pl.pallas_call(
        _mem_attn_kernel,
        grid=(n // tm,),
        in_specs=[pl.BlockSpec((tm, MEM_WIDTH), lambda i: (i, 0)),
                  pl.BlockSpec((m, MEM_WIDTH), lambda i: (i // nt, 0)),
                  pl.BlockSpec((m, MEM_WIDTH), lambda i: (i // nt, 0))],
        out_specs=pl.BlockSpec((tm, MEM_WIDTH), lambda i: (i, 0)),
        out_shape=jax.ShapeDtypeStruct((n, MEM_WIDTH), BF16),
        compiler_params=_cparams(("parallel",)),
        name="mem_attn",
    )(proj_m, k_m, v_m)


def _merge_mlp_kernel(x_ref, anw_ref, oa_ref, ob_ref, oc_ref, wg_ref, wbr_ref, wo_ref,
                      mnw_ref, wu_ref, wd_ref, fw_ref, o_ref, *, ff_chunk):
    x = x_ref[...]
    d = x.shape[-1]
    xn = _rms(x, anw_ref[...]).astype(BF16)
    merged = None
    for br, o_br in enumerate((oa_ref, ob_ref, oc_ref)):
        gate = _sigmoid(_dot(xn, wg_ref[:, br * d:(br + 1) * d]))
        term = gate * _dot(o_br[...], wbr_ref[br])
        merged = term if merged is None else merged + term
    h = x + _dot(merged.astype(BF16), wo_ref[...])
    m = _rms(h, mnw_ref[...]).astype(BF16)
    acc = h
    for c0 in range(0, wu_ref.shape[1], ff_chunk):
        u = jnp.maximum(_dot(m, wu_ref[:, c0:c0 + ff_chunk]), 0.0)
        acc = acc + _dot((u * u).astype(BF16), wd_ref[c0:c0 + ff_chunk, :])
    o_ref[...] = _rms(acc, fw_ref[...])


def _merge_mlp(x, anw, o_a, o_b, o_c, w_gate, w_br, w_out, mnw, w_up, w_down, fw, tm, ff_chunk=1024):
    n, d = x.shape
    row = lambda wd: pl.BlockSpec((tm, wd), lambda i: (i, 0))
    full = lambda a: pl.BlockSpec(a.shape, lambda i: (0,) * a.ndim, pipeline_mode=pl.Buffered(1))
    return pl.pallas_call(
        functools.partial(_merge_mlp_kernel, ff_chunk=ff_chunk),
        grid=(n // tm,),
        in_specs=[row(d), full(anw), row(o_a.shape[1]), row(o_b.shape[1]), row(o_c.shape[1]),
                  full(w_gate), full(w_br), full(w_out), full(mnw), full(w_up), full(w_down), full(fw)],
        out_specs=row(d),
        out_shape=jax.ShapeDtypeStruct((n, d), F32),
        compiler_params=_cparams(("parallel",)),
        name="merge_mlp",
    )(x, anw, o_a, o_b, o_c, w_gate, w_br, w_out, mnw, w_up, w_down, fw)


def _pad_cols(w, width):
    return jnp.pad(w, ((0, 0), (0, width - w.shape[1])))


def _head_pad(w, width):
    return jnp.pad(w, ((0, 0), (0, 0), (0, width - w.shape[2]))).reshape(w.shape[0], -1)


def _with_rope_partner(w):
    half = w.shape[2] // 2
    swapped = jnp.concatenate([w[..., half:], w[..., :half]], axis=-1)
    return jnp.concatenate([_head_pad(w, LANES), _head_pad(swapped, LANES)], axis=1)


def _rope_tables(seq, d):
    inv = ROPE_THETA ** (-jnp.arange(0, d, 2, dtype=F32) / d)
    ang = jnp.arange(seq, dtype=F32)[:, None] * inv[None, :]
    cos, sin = jnp.cos(ang), jnp.sin(ang)
    cosf = _pad_cols(jnp.concatenate([cos, cos], axis=1), LANES)
    sins = _pad_cols(jnp.concatenate([-sin, sin], axis=1), LANES)
    return cosf, sins


def kernel(x, mem, attn_norm_w, w_in, hgrn_lower_bounds, hgrn_out_norm_w, dsa_q_norm_w, dsa_kv_norm_w,
           dsa_kidx_norm_w, dsa_w_uq, dsa_w_qr, dsa_w_uk, dsa_w_uv, dsa_w_qidx, mem_norm_w, w_mem_kv,
           w_br_hgrn, w_br_dsa, w_br_mem, w_out, mlp_norm_w, w_mlp_up, w_mlp_down, final_norm_w):
    B, T, D = x.shape
    N = B * T
    M = mem.shape[1]
    l = 0
    topk = min(TOPK_MAX, T // 4)
    tm = 512
    x2 = x.reshape(N, D)

    wi = w_in[l]
    o = 0
    cols = {}
    for name, wd in (("hq", HG_WIDTH), ("hf", HG_WIDTH), ("hv", HG_WIDTH), ("hg", HG_WIDTH),
                     ("cq", DSA_Q_LORA), ("ckv", DSA_KV_LORA), ("kr", DSA_ROPE), ("kidx", IDX_DIM),
                     ("widx", IDX_HEADS), ("mq", MEM_WIDTH), ("gate", N_BRANCH * D)):
        cols[name] = wi[:, o:o + wd]
        o += wd
    w_proj = jnp.concatenate(
        [cols["hq"], cols["hf"], cols["hv"], cols["hg"], cols["mq"], cols["cq"], cols["ckv"],
         _pad_cols(cols["kr"], LANES), _pad_cols(cols["kidx"], LANES), _pad_cols(cols["widx"], LANES)],
        axis=1).astype(BF16)
    w_gate = cols["gate"].astype(BF16)

    proj_h, proj_m, proj_d = _norm_proj(x2, attn_norm_w[l], w_proj, (4 * HG_WIDTH, MEM_WIDTH, DSA_GROUP),
                                        (F32, F32, F32), tm)

    lb_all = jnp.cumsum(jax.nn.softmax(hgrn_lower_bounds.astype(F32), axis=0), axis=0)
    o_a = _hgrn(proj_h, lb_all[l], hgrn_out_norm_w[l], B, T)

    wuq_pad = _head_pad(dsa_w_uq[l], LANES).astype(BF16)
    wqr_pad = _with_rope_partner(dsa_w_qr[l]).astype(BF16)
    wqi_pad = _with_rope_partner(dsa_w_qidx[l]).astype(BF16)
    wuk_pad = jnp.pad(dsa_w_uk[l], ((0, 0), (0, LANES - DSA_NOPE), (0, 0))).astype(BF16)
    wuv = dsa_w_uv[l]
    wuv_pad = jnp.stack([jnp.pad(wuv[h], ((0, 0), ((h % 2) * DSA_VDIM, LANES - DSA_VDIM - (h % 2) * DSA_VDIM)))
                         for h in range(DSA_HEADS)]).astype(BF16)
    kinw_pad = _pad_cols(dsa_kidx_norm_w[l].reshape(1, IDX_DIM), LANES)
    tabs = _rope_tables(T, DSA_ROPE) + _rope_tables(T, IDX_DIM)
    assert tm == DSA_TK and tm % DSA_TQ == 0
    qcat, qidxt, kv, kvt, kidx, widx = _dsa_prep(
        proj_d, dsa_q_norm_w[l].reshape(1, -1), dsa_kv_norm_w[l].reshape(1, -1), kinw_pad,
        wuq_pad, wqr_pad, wuk_pad, wqi_pad, tabs, T, tm)
    o_b = _dsa(qcat, qidxt, widx, kv, kvt, kidx, wuv_pad, B, T, topk)

    k_m, v_m = _norm_proj(mem.reshape(B * M, D), mem_norm_w[l], w_mem_kv[l].astype(BF16),
                          (MEM_WIDTH, MEM_WIDTH), (BF16, BF16), min(tm, B * M))
    o_c = _mem_attn(proj_m, k_m, v_m, T, tm)

    w_br = jnp.stack([w_br_hgrn[l], w_br_dsa[l], w_br_mem[l]]).astype(BF16)
    out = _merge_mlp(x2, attn_norm_w[l].reshape(1, D), o_a, o_b, o_c, w_gate, w_br, w_out[l].astype(BF16),
                     mlp_norm_w[l].reshape(1, D), w_mlp_up[l].astype(BF16), w_mlp_down[l].astype(BF16),
                     final_norm_w.reshape(1, D), tm)
    return out.reshape(B, T, D)
```

```python
import functools

import jax
import jax.numpy as jnp
from jax import lax
from jax.experimental import pallas as pl
from jax.experimental.pallas import tpu as pltpu

F32 = jnp.float32
BF16 = jnp.bfloat16
I32 = jnp.int32

EPS = 1e-6
LOG2E = 1.4426950408889634
ROPE_THETA = 10000.0
LANES = 128
SUBLANES = 8
HG_HEADS, HG_D, HG_CHUNK = 4, 128, 64
HG_WIDTH = HG_HEADS * HG_D
HG_SUB = 16
HG_STEP = 512
DSA_HEADS, DSA_Q_LORA, DSA_KV_LORA = 8, 256, 128
DSA_NOPE, DSA_ROPE, DSA_VDIM = 64, 32, 64
IDX_HEADS, IDX_DIM = 8, 64
TOPK_MAX = 256
MEM_HEADS, MEM_DH = 4, 128
MEM_WIDTH = MEM_HEADS * MEM_DH
N_BRANCH = 3
DSA_GROUP = 6 * LANES
VMEM_LIMIT = 56 * 1024 * 1024

DSA_TQ = 128
DSA_TS = 512
DSA_TK = 512
DSA_BISECT = 28


def _cparams(sem):
    return pltpu.CompilerParams(dimension_semantics=sem, vmem_limit_bytes=VMEM_LIMIT)


def _rms(x, w, n=None):
    n = x.shape[-1] if n is None else n
    ms = jnp.sum(x * x, axis=-1, keepdims=True) * (1.0 / n)
    return x * lax.rsqrt(ms + EPS) * w


def _sigmoid(x):
    return 1.0 / (1.0 + jnp.exp(-x))


def _dot(a, b):
    return jnp.dot(a, b, preferred_element_type=F32)


def _dot_nt(a, b):
    return lax.dot_general(a, b, (((1,), (1,)), ((), ())), preferred_element_type=F32)


def _dot_tn(a, b):
    return lax.dot_general(a, b, (((0,), (0,)), ((), ())), preferred_element_type=F32)


def _norm_proj_kernel(x_ref, nw_ref, w_ref, *out_refs):
    xn = _rms(x_ref[...], nw_ref[...]).astype(BF16)
    off = 0
    for o_ref in out_refs:
        wd = o_ref.shape[-1]
        o_ref[...] = _dot(xn, w_ref[:, off:off + wd]).astype(o_ref.dtype)
        off += wd


def _norm_proj(x, nw, w, widths, dtypes, tm):
    n, d = x.shape
    return pl.pallas_call(
        _norm_proj_kernel,
        grid=(n // tm,),
        in_specs=[pl.BlockSpec((tm, d), lambda i: (i, 0)),
                  pl.BlockSpec((1, d), lambda i: (0, 0)),
                  pl.BlockSpec(w.shape, lambda i: (0, 0))],
        out_specs=[pl.BlockSpec((tm, wd), lambda i: (i, 0)) for wd in widths],
        out_shape=[jax.ShapeDtypeStruct((n, wd), dt) for wd, dt in zip(widths, dtypes)],
        compiler_params=_cparams(("parallel",)),
        name="norm_proj",
    )(x, nw.reshape(1, d), w)


def _hgrn_kernel(q_ref, f_ref, v_ref, g_ref, lb_ref, onw_ref, o_ref, st_ref, b_ref, k_ref):
    @pl.when(pl.program_id(1) == 0)
    def _():
        st_ref[...] = jnp.zeros_like(st_ref)

    C, SB = HG_CHUNK, HG_SUB
    lb = lb_ref[...]
    ri = lax.broadcasted_iota(I32, (C, C), 0)
    ci = lax.broadcasted_iota(I32, (C, C), 1)
    tri = jnp.where(ri >= ci, 1.0, 0.0).astype(BF16)
    row_c = lax.broadcasted_iota(I32, (C, HG_D), 0)
    row_1 = lax.broadcasted_iota(I32, (SUBLANES, HG_D), 0)
    lane_1 = lax.broadcasted_iota(I32, (SUBLANES, C), 1)

    def chunk(ci_, carry):
        rows = pl.ds(pl.multiple_of(ci_ * C, C), C)
        f = lb + (1.0 - lb) * _sigmoid(f_ref[rows, :])
        lf = jnp.log(f) * LOG2E
        hi = lf.astype(BF16)
        r1 = lf - hi.astype(F32)
        mid = r1.astype(BF16)
        lo = (r1 - mid.astype(F32)).astype(BF16)
        b_ref[...] = _dot(tri, hi) + _dot(tri, mid) + _dot(tri, lo)
        k_ref[...] = 1.0 - f

        for h in range(HG_HEADS):
            sl = slice(h * HG_D, (h + 1) * HG_D)
            qh = q_ref[rows, sl]
            vh = v_ref[rows, sl]
            bh = b_ref[:, sl]
            kh = k_ref[:, sl]
            a_rows = [jnp.zeros((SB, C), F32)]
            for i in range(1, C // SB):
                bref = b_ref[i * SB:i * SB + 1, sl]
                qi = (qh[i * SB:(i + 1) * SB] * jnp.exp2(bh[i * SB:(i + 1) * SB] - bref)).astype(BF16)
                ki = (kh * jnp.exp2(jnp.where(row_c < i * SB, bref - bh, -jnp.inf))).astype(BF16)
                a_rows.append(_dot_nt(qi, ki))
            for i in range(C // SB):
                b_top, q_top = bh[i * SB:i * SB + SUBLANES], qh[i * SB:i * SB + SUBLANES]
                b_bot, q_bot = bh[i * SB + SUBLANES:(i + 1) * SB], qh[i * SB + SUBLANES:(i + 1) * SB]
                a_top, a_bot = a_rows[i][:SUBLANES], a_rows[i][SUBLANES:]
                for j in range(SB):
                    s = i * SB + j
                    brow = b_ref[s:s + 1, sl]
                    krow = k_ref[s:s + 1, sl]
                    if j < SUBLANES:
                        e = jnp.exp2(jnp.where(row_1 >= j, b_top - brow, -jnp.inf))
                        r = jnp.sum(e * (q_top * krow), axis=-1, keepdims=True)
                        a_top = jnp.where(lane_1 == s, r, a_top)
                        e = jnp.exp2(b_bot - brow)
                    else:
                        e = jnp.exp2(jnp.where(row_1 >= j - SUBLANES, b_bot - brow, -jnp.inf))
                    r = jnp.sum(e * (q_bot * krow), axis=-1, keepdims=True)
                    a_bot = jnp.where(lane_1 == s, r, a_bot)
                a_rows[i] = jnp.concatenate([a_top, a_bot], axis=0)
            a_mat = jnp.concatenate(a_rows, axis=0)
            st = st_ref[h]
            qe = (qh * jnp.exp2(bh)).astype(BF16)
            vb = vh.astype(BF16)
            o = _dot_nt(qe, st.astype(BF16)) + _dot(a_mat.astype(BF16), vb)
            blast = bh[C - 1:C, :]
            kd = (kh * jnp.exp2(blast - bh)).astype(BF16)
            st_ref[h] = st * jnp.exp2(blast) + _dot_tn(vb, kd)
            gh = g_ref[rows, sl]
            o_ref[rows, sl] = (_rms(o, onw_ref[:, sl]) * (gh * _sigmoid(gh))).astype(o_ref.dtype)
        return carry

    lax.fori_loop(0, HG_STEP // C, chunk, 0, unroll=True)


def _hgrn(proj_h, lb, onw, batch, seq):
    n = proj_h.shape[0]
    nc = seq // HG_STEP
    blk = lambda col: pl.BlockSpec((HG_STEP, HG_WIDTH), lambda b, c, col=col: (b * nc + c, col))
    vec = pl.BlockSpec((1, HG_WIDTH), lambda b, c: (0, 0))
    return pl.pallas_call(
        _hgrn_kernel,
        grid=(batch, nc),
        in_specs=[blk(0), blk(1), blk(2), blk(3), vec, vec],
        out_specs=pl.BlockSpec((HG_STEP, HG_WIDTH), lambda b, c: (b * nc + c, 0)),
        out_shape=jax.ShapeDtypeStruct((n, HG_WIDTH), BF16),
        scratch_shapes=[pltpu.VMEM((HG_HEADS, HG_D, HG_D), F32),
                        pltpu.VMEM((HG_CHUNK, HG_WIDTH), F32),
                        pltpu.VMEM((HG_CHUNK, HG_WIDTH), F32)],
        compiler_params=_cparams(("parallel", "arbitrary")),
        name="hgrn2",
    )(proj_h, proj_h, proj_h, proj_h, lb.reshape(1, HG_WIDTH), onw.reshape(1, HG_WIDTH))


def _rope_lanes(x, cosf, sins, half):
    nl = x.shape[-1]
    lane = lax.broadcasted_iota(I32, x.shape, 1)
    partner = jnp.where(lane % LANES < half, pltpu.roll(x, nl - half, 1), pltpu.roll(x, half, 1))
    return x * cosf + partner * sins


def _dsa_prep_kernel(pd_ref, qnw_ref, kvnw_ref, kinw_ref, wuq_ref, wqr_ref, wuk_ref, wqi_ref,
                     c32_ref, s32_ref, c64_ref, s64_ref,
                     qcat_ref, qidxt_ref, kv_ref, kvt_ref, kidx_ref, widx_ref):
    scale = float((DSA_NOPE + DSA_ROPE) ** -0.5)
    tq = DSA_TQ
    c32, s32, c64, s64 = c32_ref[...], s32_ref[...], c64_ref[...], s64_ref[...]
    cqn = _rms(pd_ref[:, 0:DSA_Q_LORA], qnw_ref[...]).astype(BF16)
    qn = _dot(cqn, wuq_ref[...]).astype(BF16)
    qr = _dot(cqn, wqr_ref[...])
    qi = _dot(cqn, wqi_ref[...])
    hw = DSA_HEADS * LANES
    for h in range(DSA_HEADS):
        sl = slice(h * LANES, (h + 1) * LANES)
        sp = slice(hw + h * LANES, hw + (h + 1) * LANES)
        qcat_ref[h, :, 0:LANES] = (_dot(qn[:, sl], wuk_ref[h]) * scale).astype(BF16)
        qcat_ref[h, :, LANES:2 * LANES] = ((qr[:, sl] * c32 + qr[:, sp] * s32) * scale).astype(BF16)
        qih = (qi[:, sl] * c64 + qi[:, sp] * s64) * float(IDX_DIM ** -0.5)
        for r in range(qih.shape[0] // tq):
            qidxt_ref[r, :, h * tq:(h + 1) * tq] = jnp.transpose(qih[r * tq:(r + 1) * tq]).astype(BF16)
    ckv = _rms(pd_ref[:, 256:384], kvnw_ref[...])
    krope = _rope_lanes(pd_ref[:, 384:512], c32, s32, DSA_ROPE // 2)
    kv_ref[:, 0:LANES] = ckv.astype(BF16)
    kv_ref[:, LANES:2 * LANES] = krope.astype(BF16)
    kvt_ref[0, 0:LANES, :] = jnp.transpose(ckv).astype(BF16)
    kvt_ref[0, LANES:2 * LANES, :] = jnp.transpose(krope).astype(BF16)
    kin = _rms(pd_ref[:, 512:640], kinw_ref[...], n=IDX_DIM)
    kidx_ref[...] = _rope_lanes(kin, c64, s64, IDX_DIM // 2).astype(BF16)
    widx_ref[...] = jnp.transpose(pd_ref[:, 640:768] * float(IDX_HEADS ** -0.5))[0:IDX_HEADS, :]


def _dsa_prep(proj_d, qnw, kvnw, kinw_pad, wuq_pad, wqr_pad, wuk_pad, wqi_pad, tabs, seq, tm):
    n = proj_d.shape[0]
    nt = seq // tm
    full = lambda a: pl.BlockSpec(a.shape, lambda i: (0,) * a.ndim)
    tab = pl.BlockSpec((tm, LANES), lambda i: (i % nt, 0))
    hd = DSA_HEADS
    return pl.pallas_call(
        _dsa_prep_kernel,
        grid=(n // tm,),
        in_specs=[pl.BlockSpec((tm, DSA_GROUP), lambda i: (i, 0)),
                  full(qnw), full(kvnw), full(kinw_pad), full(wuq_pad), full(wqr_pad), full(wuk_pad),
                  full(wqi_pad), tab, tab, tab, tab],
        out_specs=[pl.BlockSpec((hd, tm, 2 * LANES), lambda i: (0, i, 0)),
                   pl.BlockSpec((tm // DSA_TQ, LANES, hd * DSA_TQ), lambda i: (i, 0, 0)),
                   pl.BlockSpec((tm, 2 * LANES), lambda i: (i, 0)),
                   pl.BlockSpec((1, 2 * LANES, tm), lambda i: (i, 0, 0)),
                   pl.BlockSpec((tm, LANES), lambda i: (i, 0)),
                   pl.BlockSpec((IDX_HEADS, tm), lambda i: (0, i))],
        out_shape=[jax.ShapeDtypeStruct((hd, n, 2 * LANES), BF16),
                   jax.ShapeDtypeStruct((n // DSA_TQ, LANES, hd * DSA_TQ), BF16),
                   jax.ShapeDtypeStruct((n, 2 * LANES), BF16),
                   jax.ShapeDtypeStruct((n // tm, 2 * LANES, tm), BF16),
                   jax.ShapeDtypeStruct((n, LANES), BF16),
                   jax.ShapeDtypeStruct((IDX_HEADS, n), F32)],
        compiler_params=_cparams(("parallel",)),
        name="dsa_prep",
    )(proj_d, qnw, kvnw, kinw_pad, wuq_pad, wqr_pad, wuk_pad, wqi_pad, *tabs)


def _dsa_kernel(qcat_ref, qidxt_ref, widx_ref, kv_ref, kvt_ref, kidx_ref, wuv_ref, tril_ref, o_ref,
                sc_ref, x_ref, lg_ref, m_ref, mb_ref, acc_ref, *, topk):
    TQ, TK, H, TS = DSA_TQ, DSA_TK, DSA_HEADS, DSA_TS
    assert TS == TK
    NR = TS // TQ
    step = pl.program_id(1)
    FULL = (TK,) * NR
    DIAG = tuple((r + 1) * TQ for r in range(NR))

    def over_tiles(fn, init):
        carry = lax.fori_loop(0, step, lambda kt, c: fn(kt, FULL, c), init)
        return fn(step, DIAG, carry)

    def key_rows(kt, nk):
        return pl.ds(pl.multiple_of(kt * TK, TK), nk)

    def cols(r):
        return slice(r * TQ, (r + 1) * TQ)

    one, zero = jnp.float32(1.0), jnp.float32(0.0)
    ninf, pinf = jnp.float32(-jnp.inf), jnp.float32(jnp.inf)

    def fold(x, op):
        return op(x.reshape(x.shape[0] // SUBLANES, SUBLANES, x.shape[1]), axis=0)

    def score_tile(kt, nks, carry):
        mx, mn = list(carry[0]), list(carry[1])
        for r, nk in enumerate(nks):
            s_all = jnp.maximum(_dot(kidx_ref[key_rows(kt, nk), :], qidxt_ref[r]), 0.0)
            score = s_all[:, 0:TQ] * widx_ref[0:1, cols(r)]
            for h in range(1, IDX_HEADS):
                score = score + s_all[:, h * TQ:(h + 1) * TQ] * widx_ref[h:h + 1, cols(r)]
            lo_src = score
            if nks is DIAG:
                causal = (lax.broadcasted_iota(I32, (nk, TQ), 0)
                          <= r * TQ + lax.broadcasted_iota(I32, (nk, TQ), 1))
                score = jnp.where(causal, score, ninf)
                lo_src = jnp.where(causal, score, pinf)
            x_ref[kt, 0:nk, cols(r)] = score
            mx[r] = jnp.maximum(mx[r], fold(score, jnp.max))
            mn[r] = jnp.minimum(mn[r], fold(lo_src, jnp.min))
        return tuple(mx), tuple(mn)

    mx, mn = over_tiles(score_tile, (tuple(jnp.full((SUBLANES, TQ), ninf, F32) for _ in range(NR)),
                                     tuple(jnp.full((SUBLANES, TQ), pinf, F32) for _ in range(NR))))
    x_max = jnp.max(jnp.concatenate(mx, axis=1), axis=0, keepdims=True)
    x_min = jnp.min(jnp.concatenate(mn, axis=1), axis=0, keepdims=True)

    def count(pred_fn):
        def block(kt, nk, c, acc):
            return acc + fold(jnp.where(pred_fn(x_ref[kt, 0:nk, c], c), one, zero), jnp.sum)

        def tile(kt, nks, acc):
            if nks is FULL:
                return block(kt, TK, slice(0, TS), acc)
            return jnp.concatenate([block(kt, nk, cols(r), acc[:, cols(r)]) for r, nk in enumerate(nks)], axis=1)
        return jnp.sum(over_tiles(tile, jnp.zeros((SUBLANES, TS), F32)), axis=0, keepdims=True)

    kf = float(topk)

    def bisect(_, carry):
        lo, hi, c_hi = carry
        mid = lo + (hi - lo) * 0.5
        c = count(lambda x, cs: x >= mid[:, cs])
        up = c >= kf
        return jnp.where(up, mid, lo), jnp.where(up, hi, mid), jnp.where(up, c_hi, c)

    hi0 = x_max + (jnp.abs(x_max) + 1.0) * float(2.0 ** -20)
    lo, hi, c_hi = lax.fori_loop(0, DSA_BISECT, bisect, (x_min, hi0, jnp.zeros((1, TS), F32)))
    need = kf - c_hi

    def bias_block(kt, nk, c, seen):
        x = x_ref[kt, 0:nk, c]
        tie = (x >= lo[:, c]) & (x < hi[:, c])
        tie_f = jnp.where(tie, one, zero)
        rank = _dot(tril_ref[0:nk, 0:nk], tie_f.astype(BF16)) + seen
        bias_t = jnp.where((x >= hi[:, c]) | (tie & (rank < need[:, c])), zero, ninf)
        return bias_t, rank[nk - 1:nk, :] + tie_f[nk - 1:nk, :]

    def bias_tile(kt, nks, seen):
        if nks is FULL:
            out = []
            for r0 in range(0, NR, 2):
                c = slice(r0 * TQ, (r0 + 2) * TQ)
                bias_t, s = bias_block(kt, TK, c, seen[:, c])
                sc_ref[kt, cols(r0), :] = jnp.transpose(bias_t[:, 0:TQ])
                sc_ref[kt, cols(r0 + 1), :] = jnp.transpose(bias_t[:, TQ:2 * TQ])
                out.append(s)
            return jnp.concatenate(out, axis=1)
        out = []
        for r, nk in enumerate(nks):
            bias_t, s = bias_block(kt, nk, cols(r), seen[:, cols(r)])
            sc_ref[kt, cols(r), 0:nk] = jnp.transpose(bias_t)
            out.append(s)
        return jnp.concatenate(out, axis=1)

    over_tiles(bias_tile, jnp.zeros((1, TS), F32))

    def pass_a(r, kt, nk):
        qcat = qcat_ref[:, cols(r), :].reshape(H * TQ, 2 * LANES)
        lg = _dot(qcat, kvt_ref[kt, :, 0:nk]).reshape(H, TQ, nk) + sc_ref[kt, cols(r), 0:nk][None]
        lg = lg.reshape(H * TQ, nk)
        lg_ref[r % 2, kt, :, 0:nk] = lg
        part = lg[:, 0:LANES]
        for g in range(1, nk // LANES):
            part = jnp.maximum(part, lg[:, g * LANES:(g + 1) * LANES])
        m_ref[r % 2] = jnp.maximum(m_ref[r % 2], part)

    def pass_b(r, kt, nk):
        m_b = mb_ref[...]
        p = jnp.concatenate(
            [jnp.exp((lg_ref[r % 2, kt, :, g * LANES:(g + 1) * LANES] - m_b).astype(BF16))
             for g in range(nk // LANES)], axis=1)
        kv1 = jnp.concatenate([kv_ref[key_rows(kt, nk), 0:LANES], jnp.ones((nk, LANES), BF16)], axis=1)
        acc_ref[...] += _dot(p, kv1)

    def run(fns):
        def tile(kt, nks, carry):
            for fn, r in fns:
                fn(r, kt, nks[r])
            return carry
        over_tiles(tile, 0)

    m_ref[0] = jnp.full((H * TQ, LANES), -jnp.inf, F32)
    run([(pass_a, 0)])
    for r in range(NR):
        rows = cols(r)
        mb_ref[...] = jnp.broadcast_to(jnp.max(m_ref[r % 2], axis=-1, keepdims=True), (H * TQ, LANES))
        acc_ref[...] = jnp.zeros_like(acc_ref)
        if r + 1 < NR:
            m_ref[(r + 1) % 2] = jnp.full((H * TQ, LANES), -jnp.inf, F32)
            run([(pass_a, r + 1), (pass_b, r)])
        else:
            run([(pass_b, r)])
        o_lat = (acc_ref[:, 0:LANES] / acc_ref[:, LANES:2 * LANES]).astype(BF16)
        for pr in range(H // 2):
            h0, h1 = 2 * pr, 2 * pr + 1
            o_ref[rows, pr * LANES:(pr + 1) * LANES] = (
                _dot(o_lat[h0 * TQ:(h0 + 1) * TQ], wuv_ref[h0]) + _dot(o_lat[h1 * TQ:(h1 + 1) * TQ], wuv_ref[h1])
            ).astype(o_ref.dtype)


def _dsa(qcat, qidxt, widx, kv, kvt, kidx, wuv_pad, batch, seq, topk):
    n = kv.shape[0]
    ns = seq // DSA_TS
    nkt_max = seq // DSA_TK
    nr = DSA_TS // DSA_TQ
    H = DSA_HEADS
    return pl.pallas_call(
        functools.partial(_dsa_kernel, topk=topk),
        grid=(batch, ns),
        in_specs=[pl.BlockSpec((H, DSA_TS, 2 * LANES), lambda b, s: (0, b * ns + s, 0)),
                  pl.BlockSpec((nr, LANES, H * DSA_TQ), lambda b, s: (b * ns + s, 0, 0)),
                  pl.BlockSpec((IDX_HEADS, DSA_TS), lambda b, s: (0, b * ns + s)),
                  pl.BlockSpec((seq, 2 * LANES), lambda b, s: (b, 0)),
                  pl.BlockSpec((nkt_max, 2 * LANES, DSA_TK), lambda b, s: (b, 0, 0)),
                  pl.BlockSpec((seq, LANES), lambda b, s: (b, 0)),
                  pl.BlockSpec(wuv_pad.shape, lambda b, s: (0, 0, 0)),
                  pl.BlockSpec((DSA_TK, DSA_TK), lambda b, s: (0, 0))],
        out_specs=pl.BlockSpec((DSA_TS, H * DSA_VDIM), lambda b, s: (b * ns + s, 0)),
        out_shape=jax.ShapeDtypeStruct((n, H * DSA_VDIM), BF16),
        scratch_shapes=[pltpu.VMEM((nkt_max, DSA_TS, DSA_TK), F32),
                        pltpu.VMEM((nkt_max, DSA_TK, DSA_TS), F32),
                        pltpu.VMEM((2, nkt_max, H * DSA_TQ, DSA_TK), F32),
                        pltpu.VMEM((2, H * DSA_TQ, LANES), F32),
                        pltpu.VMEM((H * DSA_TQ, LANES), F32),
                        pltpu.VMEM((H * DSA_TQ, 2 * LANES), F32)],
        compiler_params=_cparams(("parallel", "arbitrary")),
        name="dsa_attn",
    )(qcat, qidxt, widx, kv, kvt, kidx, wuv_pad, jnp.tri(DSA_TK, k=-1, dtype=BF16))


def _mem_attn_kernel(q_ref, k_ref, v_ref, o_ref):
    scale = float(MEM_DH ** -0.5)
    for h in range(MEM_HEADS):
        sl = slice(h * MEM_DH, (h + 1) * MEM_DH)
        lg = _dot_nt((q_ref[:, sl] * scale).astype(BF16), k_ref[:, sl])
        p = jnp.exp((lg - jnp.max(lg, axis=-1, keepdims=True)).astype(BF16))
        v1 = jnp.concatenate([v_ref[:, sl], jnp.ones((v_ref.shape[0], MEM_DH), BF16)], axis=1)
        acc = _dot(p, v1)
        o_ref[:, sl] = (acc[:, 0:MEM_DH] / acc[:, MEM_DH:2 * MEM_DH]).astype(o_ref.dtype)


def _mem_attn(proj_m, k_m, v_m, seq, tm):
    n = proj_m.shape[0]
    nt = seq // tm
    m = k_m.shape[0] // (n // seq)
    return pl.pallas_call(
        _mem_attn_kernel,
        grid=(n // tm,),
        in_specs=[pl.BlockSpec((tm, MEM_WIDTH), lambda i: (i, 0)),
                  pl.BlockSpec((m, MEM_WIDTH), lambda i: (i // nt, 0)),
                  pl.BlockSpec((m, MEM_WIDTH), lambda i: (i // nt, 0))],
        out_specs=pl.BlockSpec((tm, MEM_WIDTH), lambda i: (i, 0)),
        out_shape=jax.ShapeDtypeStruct((n, MEM_WIDTH), BF16),
        compiler_params=_cparams(("parallel",)),
        name="mem_attn",
    )(proj_m, k_m, v_m)


def _merge_mlp_kernel(x_ref, anw_ref, oa_ref, ob_ref, oc_ref, wg_ref, wbr_ref, wo_ref,
                      mnw_ref, wu_ref, wd_ref, fw_ref, o_ref, *, ff_chunk):
    x = x_ref[...]
    d = x.shape[-1]
    xn = _rms(x, anw_ref[...]).astype(BF16)
    merged = None
    for br, o_br in enumerate((oa_ref, ob_ref, oc_ref)):
        gate = _sigmoid(_dot(xn, wg_ref[:, br * d:(br + 1) * d]))
        term = gate * _dot(o_br[...], wbr_ref[br])
        merged = term if merged is None else merged + term
    h = x + _dot(merged.astype(BF16), wo_ref[...])
    m = _rms(h, mnw_ref[...]).astype(BF16)
    acc = h
    for c0 in range(0, wu_ref.shape[1], ff_chunk):
        u = jnp.maximum(_dot(m, wu_ref[:, c0:c0 + ff_chunk]), 0.0)
        acc = acc + _dot((u * u).astype(BF16), wd_ref[c0:c0 + ff_chunk, :])
    o_ref[...] = _rms(acc, fw_ref[...])


def _merge_mlp(x, anw, o_a, o_b, o_c, w_gate, w_br, w_out, mnw, w_up, w_down, fw, tm, ff_chunk=1024):
    n, d = x.shape
    row = lambda wd: pl.BlockSpec((tm, wd), lambda i: (i, 0))
    full = lambda a: pl.BlockSpec(a.shape, lambda i: (0,) * a.ndim, pipeline_mode=pl.Buffered(1))
    return pl.pallas_call(
        functools.partial(_merge_mlp_kernel, ff_chunk=ff_chunk),
        grid=(n // tm,),
        in_specs=[row(d), full(anw), row(o_a.shape[1]), row(o_b.shape[1]), row(o_c.shape[1]),
                  full(w_gate), full(w_br), full(w_out), full(mnw), full(w_up), full(w_down), full(fw)],
        out_specs=row(d),
        out_shape=jax.ShapeDtypeStruct((n, d), F32),
        compiler_params=_cparams(("parallel",)),
        name="merge_mlp",
    )(x, anw, o_a, o_b, o_c, w_gate, w_br, w_out, mnw, w_up, w_down, fw)


def _pad_cols(w, width):
    return jnp.pad(w, ((0, 0), (0, width - w.shape[1])))


def _head_pad(w, width):
    return jnp.pad(w, ((0, 0), (0, 0), (0, width - w.shape[2]))).reshape(w.shape[0], -1)


def _with_rope_partner(w):
    half = w.shape[2] // 2
    swapped = jnp.concatenate([w[..., half:], w[..., :half]], axis=-1)
    return jnp.concatenate([_head_pad(w, LANES), _head_pad(swapped, LANES)], axis=1)


def _rope_tables(seq, d):
    inv = ROPE_THETA ** (-jnp.arange(0, d, 2, dtype=F32) / d)
    ang = jnp.arange(seq, dtype=F32)[:, None] * inv[None, :]
    cos, sin = jnp.cos(ang), jnp.sin(ang)
    cosf = _pad_cols(jnp.concatenate([cos, cos], axis=1), LANES)
    sins = _pad_cols(jnp.concatenate([-sin, sin], axis=1), LANES)
    return cosf, sins


def kernel(x, mem, attn_norm_w, w_in, hgrn_lower_bounds, hgrn_out_norm_w, dsa_q_norm_w, dsa_kv_norm_w,
           dsa_kidx_norm_w, dsa_w_uq, dsa_w_qr, dsa_w_uk, dsa_w_uv, dsa_w_qidx, mem_norm_w, w_mem_kv,
           w_br_hgrn, w_br_dsa, w_br_mem, w_out, mlp_norm_w, w_mlp_up, w_mlp_down, final_norm_w):
    B, T, D = x.shape
    N = B * T
    M = mem.shape[1]
    l = 0
    topk = min(TOPK_MAX, T // 4)
    tm = 512
    x2 = x.reshape(N, D)

    wi = w_in[l]
    o = 0
    cols = {}
    for name, wd in (("hq", HG_WIDTH), ("hf", HG_WIDTH), ("hv", HG_WIDTH), ("hg", HG_WIDTH),
                     ("cq", DSA_Q_LORA), ("ckv", DSA_KV_LORA), ("kr", DSA_ROPE), ("kidx", IDX_DIM),
                     ("widx", IDX_HEADS), ("mq", MEM_WIDTH), ("gate", N_BRANCH * D)):
        cols[name] = wi[:, o:o + wd]
        o += wd
    w_proj = jnp.concatenate(
        [cols["hq"], cols["hf"], cols["hv"], cols["hg"], cols["mq"], cols["cq"], cols["ckv"],
         _pad_cols(cols["kr"], LANES), _pad_cols(cols["kidx"], LANES), _pad_cols(cols["widx"], LANES)],
        axis=1).astype(BF16)
    w_gate = cols["gate"].astype(BF16)

    proj_h, proj_m, proj_d = _norm_proj(x2, attn_norm_w[l], w_proj, (4 * HG_WIDTH, MEM_WIDTH, DSA_GROUP),
                                        (F32, F32, F32), tm)

    lb_all = jnp.cumsum(jax.nn.softmax(hgrn_lower_bounds.astype(F32), axis=0), axis=0)
    o_a = _hgrn(proj_h, lb_all[l], hgrn_out_norm_w[l], B, T)

    wuq_pad = _head_pad(dsa_w_uq[l], LANES).astype(BF16)
    wqr_pad = _with_rope_partner(dsa_w_qr[l]).astype(BF16)
    wqi_pad = _with_rope_partner(dsa_w_qidx[l]).astype(BF16)
    wuk_pad = jnp.pad(dsa_w_uk[l], ((0, 0), (0, LANES - DSA_NOPE), (0, 0))).astype(BF16)
    wuv = dsa_w_uv[l]
    wuv_pad = jnp.stack([jnp.pad(wuv[h], ((0, 0), ((h % 2) * DSA_VDIM, LANES - DSA_VDIM - (h % 2) * DSA_VDIM)))
                         for h in range(DSA_HEADS)]).astype(BF16)
    kinw_pad = _pad_cols(dsa_kidx_norm_w[l].reshape(1, IDX_DIM), LANES)
    tabs = _rope_tables(T, DSA_ROPE) + _rope_tables(T, IDX_DIM)
    assert tm == DSA_TK and tm % DSA_TQ == 0
    qcat, qidxt, kv, kvt, kidx, widx = _dsa_prep(
        proj_d, dsa_q_norm_w[l].reshape(1, -1), dsa_kv_norm_w[l].reshape(1, -1), kinw_pad,
        wuq_pad, wqr_pad, wuk_pad, wqi_pad, tabs, T, tm)
    o_b = _dsa(qcat, qidxt, widx, kv, kvt, kidx, wuv_pad, B, T, topk)

    k_m, v_m = _norm_proj(mem.reshape(B * M, D), mem_norm_w[l], w_mem_kv[l].astype(BF16),
                          (MEM_WIDTH, MEM_WIDTH), (BF16, BF16), min(tm, B * M))
    o_c = _mem_attn(proj_m, k_m, v_m, T, tm)

    w_br = jnp.stack([w_br_hgrn[l], w_br_dsa[l], w_br_mem[l]]).astype(BF16)
    out = _merge_mlp(x2, attn_norm_w[l].reshape(1, D), o_a, o_b, o_c, w_gate, w_br, w_out[l].astype(BF16),
                     mlp_norm_w[l].reshape(1, D), w_mlp_up[l].astype(BF16), w_mlp_down[l].astype(BF16),
                     final_norm_w.reshape(1, D), tm)
    return out.reshape(B, T, D)
```

```python
import functools

import jax
import jax.numpy as jnp
from jax import lax
from jax.experimental import pallas as pl
from jax.experimental.pallas import tpu as pltpu

F32 = jnp.float32
BF16 = jnp.bfloat16
I32 = jnp.int32

EPS = 1e-6
LOG2E = 1.4426950408889634
ROPE_THETA = 10000.0
LANES = 128
SUBLANES = 8
HG_HEADS, HG_D, HG_CHUNK = 4, 128, 64
HG_WIDTH = HG_HEADS * HG_D
HG_SUB = 16
HG_STEP = 512
DSA_HEADS, DSA_Q_LORA, DSA_KV_LORA = 8, 256, 128
DSA_NOPE, DSA_ROPE, DSA_VDIM = 64, 32, 64
IDX_HEADS, IDX_DIM = 8, 64
TOPK_MAX = 256
MEM_HEADS, MEM_DH = 4, 128
MEM_WIDTH = MEM_HEADS * MEM_DH
N_BRANCH = 3
DSA_GROUP = 6 * LANES
VMEM_LIMIT = 56 * 1024 * 1024

DSA_TQ = 128
DSA_TS = 512
DSA_TK = 512
DSA_BISECT = 28


def _cparams(sem):
    return pltpu.CompilerParams(dimension_semantics=sem, vmem_limit_bytes=VMEM_LIMIT)


def _rms(x, w, n=None):
    n = x.shape[-1] if n is None else n
    ms = jnp.sum(x * x, axis=-1, keepdims=True) * (1.0 / n)
    return x * lax.rsqrt(ms + EPS) * w


def _sigmoid(x):
    return 1.0 / (1.0 + jnp.exp(-x))


def _dot(a, b):
    return jnp.dot(a, b, preferred_element_type=F32)


def _dot_nt(a, b):
    return lax.dot_general(a, b, (((1,), (1,)), ((), ())), preferred_element_type=F32)


def _dot_tn(a, b):
    return lax.dot_general(a, b, (((0,), (0,)), ((), ())), preferred_element_type=F32)


def _norm_proj_kernel(x_ref, nw_ref, w_ref, *out_refs):
    xn = _rms(x_ref[...], nw_ref[...]).astype(BF16)
    off = 0
    for o_ref in out_refs:
        wd = o_ref.shape[-1]
        o_ref[...] = _dot(xn, w_ref[:, off:off + wd]).astype(o_ref.dtype)
        off += wd


def _norm_proj(x, nw, w, widths, dtypes, tm):
    n, d = x.shape
    return pl.pallas_call(
        _norm_proj_kernel,
        grid=(n // tm,),
        in_specs=[pl.BlockSpec((tm, d), lambda i: (i, 0)),
                  pl.BlockSpec((1, d), lambda i: (0, 0)),
                  pl.BlockSpec(w.shape, lambda i: (0, 0))],
        out_specs=[pl.BlockSpec((tm, wd), lambda i: (i, 0)) for wd in widths],
        out_shape=[jax.ShapeDtypeStruct((n, wd), dt) for wd, dt in zip(widths, dtypes)],
        compiler_params=_cparams(("parallel",)),
        name="norm_proj",
    )(x, nw.reshape(1, d), w)


def _hgrn_kernel(q_ref, f_ref, v_ref, g_ref, lb_ref, onw_ref, o_ref, st_ref, b_ref, k_ref):
    @pl.when(pl.program_id(1) == 0)
    def _():
        st_ref[...] = jnp.zeros_like(st_ref)

    C, SB = HG_CHUNK, HG_SUB
    lb = lb_ref[...]
    ri = lax.broadcasted_iota(I32, (C, C), 0)
    ci = lax.broadcasted_iota(I32, (C, C), 1)
    tri = jnp.where(ri >= ci, 1.0, 0.0).astype(BF16)
    row_c = lax.broadcasted_iota(I32, (C, HG_D), 0)
    row_1 = lax.broadcasted_iota(I32, (SUBLANES, HG_D), 0)
    lane_1 = lax.broadcasted_iota(I32, (SUBLANES, C), 1)

    def chunk(ci_, carry):
        rows = pl.ds(pl.multiple_of(ci_ * C, C), C)
        f = lb + (1.0 - lb) * _sigmoid(f_ref[rows, :])
        lf = jnp.log(f) * LOG2E
        hi = lf.astype(BF16)
        r1 = lf - hi.astype(F32)
        mid = r1.astype(BF16)
        lo = (r1 - mid.astype(F32)).astype(BF16)
        b_ref[...] = _dot(tri, hi) + _dot(tri, mid) + _dot(tri, lo)
        k_ref[...] = 1.0 - f

        for h in range(HG_HEADS):
            sl = slice(h * HG_D, (h + 1) * HG_D)
            qh = q_ref[rows, sl]
            vh = v_ref[rows, sl]
            bh = b_ref[:, sl]
            kh = k_ref[:, sl]
            a_rows = [jnp.zeros((SB, C), F32)]
            for i in range(1, C // SB):
                bref = b_ref[i * SB:i * SB + 1, sl]
                qi = (qh[i * SB:(i + 1) * SB] * jnp.exp2(bh[i * SB:(i + 1) * SB] - bref)).astype(BF16)
                ki = (kh * jnp.exp2(jnp.where(row_c < i * SB, bref - bh, -jnp.inf))).astype(BF16)
                a_rows.append(_dot_nt(qi, ki))
            for i in range(C // SB):
                b_top, q_top = bh[i * SB:i * SB + SUBLANES], qh[i * SB:i * SB + SUBLANES]
                b_bot, q_bot = bh[i * SB + SUBLANES:(i + 1) * SB], qh[i * SB + SUBLANES:(i + 1) * SB]
                a_top, a_bot = a_rows[i][:SUBLANES], a_rows[i][SUBLANES:]
                for j in range(SB):
                    s = i * SB + j
                    brow = b_ref[s:s + 1, sl]
                    krow = k_ref[s:s + 1, sl]
                    if j < SUBLANES:
                        e = jnp.exp2(jnp.where(row_1 >= j, b_top - brow, -jnp.inf))
                        r = jnp.sum(e * (q_top * krow), axis=-1, keepdims=True)
                        a_top = jnp.where(lane_1 == s, r, a_top)
                        e = jnp.exp2(b_bot - brow)
                    else:
                        e = jnp.exp2(jnp.where(row_1 >= j - SUBLANES, b_bot - brow, -jnp.inf))
                    r = jnp.sum(e * (q_bot * krow), axis=-1, keepdims=True)
                    a_bot = jnp.where(lane_1 == s, r, a_bot)
                a_rows[i] = jnp.concatenate([a_top, a_bot], axis=0)
            a_mat = jnp.concatenate(a_rows, axis=0)
            st = st_ref[h]
            qe = (qh * jnp.exp2(bh)).astype(BF16)
            vb = vh.astype(BF16)
            o = _dot_nt(qe, st.astype(BF16)) + _dot(a_mat.astype(BF16), vb)
            blast = bh[C - 1:C, :]
            kd = (kh * jnp.exp2(blast - bh)).astype(BF16)
            st_ref[h] = st * jnp.exp2(blast) + _dot_tn(vb, kd)
            gh = g_ref[rows, sl]
            o_ref[rows, sl] = (_rms(o, onw_ref[:, sl]) * (gh * _sigmoid(gh))).astype(o_ref.dtype)
        return carry

    lax.fori_loop(0, HG_STEP // C, chunk, 0, unroll=True)


def _hgrn(proj_h, lb, onw, batch, seq):
    n = proj_h.shape[0]
    nc = seq // HG_STEP
    blk = lambda col: pl.BlockSpec((HG_STEP, HG_WIDTH), lambda b, c, col=col: (b * nc + c, col))
    vec = pl.BlockSpec((1, HG_WIDTH), lambda b, c: (0, 0))
    return pl.pallas_call(
        _hgrn_kernel,
        grid=(batch, nc),
        in_specs=[blk(0), blk(1), blk(2), blk(3), vec, vec],
        out_specs=pl.BlockSpec((HG_STEP, HG_WIDTH), lambda b, c: (b * nc + c, 0)),
        out_shape=jax.ShapeDtypeStruct((n, HG_WIDTH), BF16),
        scratch_shapes=[pltpu.VMEM((HG_HEADS, HG_D, HG_D), F32),
                        pltpu.VMEM((HG_CHUNK, HG_WIDTH), F32),
                        pltpu.VMEM((HG_CHUNK, HG_WIDTH), F32)],
        compiler_params=_cparams(("parallel", "arbitrary")),
        name="hgrn2",
    )(proj_h, proj_h, proj_h, proj_h, lb.reshape(1, HG_WIDTH), onw.reshape(1, HG_WIDTH))


def _rope_lanes(x, cosf, sins, half):
    nl = x.shape[-1]
    lane = lax.broadcasted_iota(I32, x.shape, 1)
    partner = jnp.where(lane % LANES < half, pltpu.roll(x, nl - half, 1), pltpu.roll(x, half, 1))
    return x * cosf + partner * sins


def _dsa_prep_kernel(pd_ref, qnw_ref, kvnw_ref, kinw_ref, wuq_ref, wqr_ref, wuk_ref, wqi_ref,
                     c32_ref, s32_ref, c64_ref, s64_ref,
                     qcat_ref, qidxt_ref, kv_ref, kvt_ref, kidx_ref, widx_ref):
    scale = float((DSA_NOPE + DSA_ROPE) ** -0.5)
    tq = DSA_TQ
    c32, s32, c64, s64 = c32_ref[...], s32_ref[...], c64_ref[...], s64_ref[...]
    cqn = _rms(pd_ref[:, 0:DSA_Q_LORA], qnw_ref[...]).astype(BF16)
    qn = _dot(cqn, wuq_ref[...]).astype(BF16)
    qr = _dot(cqn, wqr_ref[...])
    qi = _dot(cqn, wqi_ref[...])
    hw = DSA_HEADS * LANES
    for h in range(DSA_HEADS):
        sl = slice(h * LANES, (h + 1) * LANES)
        sp = slice(hw + h * LANES, hw + (h + 1) * LANES)
        qcat_ref[h, :, 0:LANES] = (_dot(qn[:, sl], wuk_ref[h]) * scale).astype(BF16)
        qcat_ref[h, :, LANES:2 * LANES] = ((qr[:, sl] * c32 + qr[:, sp] * s32) * scale).astype(BF16)
        qih = (qi[:, sl] * c64 + qi[:, sp] * s64) * float(IDX_DIM ** -0.5)
        for r in range(qih.shape[0] // tq):
            qidxt_ref[r, :, h * tq:(h + 1) * tq] = jnp.transpose(qih[r * tq:(r + 1) * tq]).astype(BF16)
    ckv = _rms(pd_ref[:, 256:384], kvnw_ref[...])
    krope = _rope_lanes(pd_ref[:, 384:512], c32, s32, DSA_ROPE // 2)
    kv_ref[:, 0:LANES] = ckv.astype(BF16)
    kv_ref[:, LANES:2 * LANES] = krope.astype(BF16)
    kvt_ref[0, 0:LANES, :] = jnp.transpose(ckv).astype(BF16)
    kvt_ref[0, LANES:2 * LANES, :] = jnp.transpose(krope).astype(BF16)
    kin = _rms(pd_ref[:, 512:640], kinw_ref[...], n=IDX_DIM)
    kidx_ref[...] = _rope_lanes(kin, c64, s64, IDX_DIM // 2).astype(BF16)
    widx_ref[...] = jnp.transpose(pd_ref[:, 640:768] * float(IDX_HEADS ** -0.5))[0:IDX_HEADS, :]


def _dsa_prep(proj_d, qnw, kvnw, kinw_pad, wuq_pad, wqr_pad, wuk_pad, wqi_pad, tabs, seq, tm):
    n = proj_d.shape[0]
    nt = seq // tm
    full = lambda a: pl.BlockSpec(a.shape, lambda i: (0,) * a.ndim)
    tab = pl.BlockSpec((tm, LANES), lambda i: (i % nt, 0))
    hd = DSA_HEADS
    return pl.pallas_call(
        _dsa_prep_kernel,
        grid=(n // tm,),
        in_specs=[pl.BlockSpec((tm, DSA_GROUP), lambda i: (i, 0)),
                  full(qnw), full(kvnw), full(kinw_pad), full(wuq_pad), full(wqr_pad), full(wuk_pad),
                  full(wqi_pad), tab, tab, tab, tab],
        out_specs=[pl.BlockSpec((hd, tm, 2 * LANES), lambda i: (0, i, 0)),
                   pl.BlockSpec((tm // DSA_TQ, LANES, hd * DSA_TQ), lambda i: (i, 0, 0)),
                   pl.BlockSpec((tm, 2 * LANES), lambda i: (i, 0)),
                   pl.BlockSpec((1, 2 * LANES, tm), lambda i: (i, 0, 0)),
                   pl.BlockSpec((tm, LANES), lambda i: (i, 0)),
                   pl.BlockSpec((IDX_HEADS, tm), lambda i: (0, i))],
        out_shape=[jax.ShapeDtypeStruct((hd, n, 2 * LANES), BF16),
                   jax.ShapeDtypeStruct((n // DSA_TQ, LANES, hd * DSA_TQ), BF16),
                   jax.ShapeDtypeStruct((n, 2 * LANES), BF16),
                   jax.ShapeDtypeStruct((n // tm, 2 * LANES, tm), BF16),
                   jax.ShapeDtypeStruct((n, LANES), BF16),
                   jax.ShapeDtypeStruct((IDX_HEADS, n), F32)],
        compiler_params=_cparams(("parallel",)),
        name="dsa_prep",
    )(proj_d, qnw, kvnw, kinw_pad, wuq_pad, wqr_pad, wuk_pad, wqi_pad, *tabs)


def _dsa_kernel(qcat_ref, qidxt_ref, widx_ref, kv_ref, kvt_ref, kidx_ref, wuv_ref, tril_ref, o_ref,
                sc_ref, x_ref, lg_ref, m_ref, mb_ref, acc_ref, *, topk):
    TQ, TK, H, TS = DSA_TQ, DSA_TK, DSA_HEADS, DSA_TS
    assert TS == TK
    NR = TS // TQ
    step = pl.program_id(1)
    FULL = (TK,) * NR
    DIAG = tuple((r + 1) * TQ for r in range(NR))

    def over_tiles(fn, init):
        carry = lax.fori_loop(0, step, lambda kt, c: fn(kt, FULL, c), init)
        return fn(step, DIAG, carry)

    def key_rows(kt, nk):
        return pl.ds(pl.multiple_of(kt * TK, TK), nk)

    def cols(r):
        return slice(r * TQ, (r + 1) * TQ)

    one, zero = jnp.float32(1.0), jnp.float32(0.0)
    ninf, pinf = jnp.float32(-jnp.inf), jnp.float32(jnp.inf)

    def fold(x, op):
        return op(x.reshape(x.shape[0] // SUBLANES, SUBLANES, x.shape[1]), axis=0)

    def score_tile(kt, nks, carry):
        mx, mn = list(carry[0]), list(carry[1])
        for r, nk in enumerate(nks):
            s_all = jnp.maximum(_dot(kidx_ref[key_rows(kt, nk), :], qidxt_ref[r]), 0.0)
            score = s_all[:, 0:TQ] * widx_ref[0:1, cols(r)]
            for h in range(1, IDX_HEADS):
                score = score + s_all[:, h * TQ:(h + 1) * TQ] * widx_ref[h:h + 1, cols(r)]
            lo_src = score
            if nks is DIAG:
                causal = (lax.broadcasted_iota(I32, (nk, TQ), 0)
                          <= r * TQ + lax.broadcasted_iota(I32, (nk, TQ), 1))
                score = jnp.where(causal, score, ninf)
                lo_src = jnp.where(causal, score, pinf)
            x_ref[kt, 0:nk, cols(r)] = score
            mx[r] = jnp.maximum(mx[r], fold(score, jnp.max))
            mn[r] = jnp.minimum(mn[r], fold(lo_src, jnp.min))
        return tuple(mx), tuple(mn)

    mx, mn = over_tiles(score_tile, (tuple(jnp.full((SUBLANES, TQ), ninf, F32) for _ in range(NR)),
                                     tuple(jnp.full((SUBLANES, TQ), pinf, F32) for _ in range(NR))))
    x_max = jnp.max(jnp.concatenate(mx, axis=1), axis=0, keepdims=True)
    x_min = jnp.min(jnp.concatenate(mn, axis=1), axis=0, keepdims=True)

    def count(pred_fn):
        def block(kt, nk, c, acc):
            return acc + fold(jnp.where(pred_fn(x_ref[kt, 0:nk, c], c), one, zero), jnp.sum)

        def tile(kt, nks, acc):
            if nks is FULL:
                return block(kt, TK, slice(0, TS), acc)
            return jnp.concatenate([block(kt, nk, cols(r), acc[:, cols(r)]) for r, nk in enumerate(nks)], axis=1)
        return jnp.sum(over_tiles(tile, jnp.zeros((SUBLANES, TS), F32)), axis=0, keepdims=True)

    kf = float(topk)

    def bisect(_, carry):
        lo, hi, c_hi = carry
        mid = lo + (hi - lo) * 0.5
        c = count(lambda x, cs: x >= mid[:, cs])
        up = c >= kf
        return jnp.where(up, mid, lo), jnp.where(up, hi, mid), jnp.where(up, c_hi, c)

    hi0 = x_max + (jnp.abs(x_max) + 1.0) * float(2.0 ** -20)
    lo, hi, c_hi = lax.fori_loop(0, DSA_BISECT, bisect, (x_min, hi0, jnp.zeros((1, TS), F32)))
    need = kf - c_hi

    def bias_block(kt, nk, c, seen):
        x = x_ref[kt, 0:nk, c]
        tie = (x >= lo[:, c]) & (x < hi[:, c])
        tie_f = jnp.where(tie, one, zero)
        rank = _dot(tril_ref[0:nk, 0:nk], tie_f.astype(BF16)) + seen
        bias_t = jnp.where((x >= hi[:, c]) | (tie & (rank < need[:, c])), zero, ninf)
        return bias_t, rank[nk - 1:nk, :] + tie_f[nk - 1:nk, :]

    def bias_tile(kt, nks, seen):
        if nks is FULL:
            out = []
            for r0 in range(0, NR, 2):
                c = slice(r0 * TQ, (r0 + 2) * TQ)
                bias_t, s = bias_block(kt, TK, c, seen[:, c])
                sc_ref[kt, cols(r0), :] = jnp.transpose(bias_t[:, 0:TQ])
                sc_ref[kt, cols(r0 + 1), :] = jnp.transpose(bias_t[:, TQ:2 * TQ])
                out.append(s)
            return jnp.concatenate(out, axis=1)
        out = []
        for r, nk in enumerate(nks):
            bias_t, s = bias_block(kt, nk, cols(r), seen[:, cols(r)])
            sc_ref[kt, cols(r), 0:nk] = jnp.transpose(bias_t)
            out.append(s)
        return jnp.concatenate(out, axis=1)

    over_tiles(bias_tile, jnp.zeros((1, TS), F32))

    def pass_a(r, kt, nk):
        qcat = qcat_ref[:, cols(r), :].reshape(H * TQ, 2 * LANES)
        lg = _dot(qcat, kvt_ref[kt, :, 0:nk]).reshape(H, TQ, nk) + sc_ref[kt, cols(r), 0:nk][None]
        lg = lg.reshape(H * TQ, nk)
        lg_ref[r % 2, kt, :, 0:nk] = lg
        part = lg[:, 0:LANES]
        for g in range(1, nk // LANES):
            part = jnp.maximum(part, lg[:, g * LANES:(g + 1) * LANES])
        m_ref[r % 2] = jnp.maximum(m_ref[r % 2], part)

    def pass_b(r, kt, nk):
        m_b = mb_ref[...]
        p = jnp.concatenate(
            [jnp.exp((lg_ref[r % 2, kt, :, g * LANES:(g + 1) * LANES] - m_b).astype(BF16))
             for g in range(nk // LANES)], axis=1)
        kv1 = jnp.concatenate([kv_ref[key_rows(kt, nk), 0:LANES], jnp.ones((nk, LANES), BF16)], axis=1)
        acc_ref[...] += _dot(p, kv1)

    def run(fns):
        def tile(kt, nks, carry):
            for fn, r in fns:
                fn(r, kt, nks[r])
            return carry
        over_tiles(tile, 0)

    m_ref[0] = jnp.full((H * TQ, LANES), -jnp.inf, F32)
    run([(pass_a, 0)])
    for r in range(NR):
        rows = cols(r)
        mb_ref[...] = jnp.broadcast_to(jnp.max(m_ref[r % 2], axis=-1, keepdims=True), (H * TQ, LANES))
        acc_ref[...] = jnp.zeros_like(acc_ref)
        if r + 1 < NR:
            m_ref[(r + 1) % 2] = jnp.full((H * TQ, LANES), -jnp.inf, F32)
            run([(pass_a, r + 1), (pass_b, r)])
        else:
            run([(pass_b, r)])
        o_lat = (acc_ref[:, 0:LANES] / acc_ref[:, LANES:2 * LANES]).astype(BF16)
        for pr in range(H // 2):
            h0, h1 = 2 * pr, 2 * pr + 1
            o_ref[rows, pr * LANES:(pr + 1) * LANES] = (
                _dot(o_lat[h0 * TQ:(h0 + 1) * TQ], wuv_ref[h0]) + _dot(o_lat[h1 * TQ:(h1 + 1) * TQ], wuv_ref[h1])
            ).astype(o_ref.dtype)


def _dsa(qcat, qidxt, widx, kv, kvt, kidx, wuv_pad, batch, seq, topk):
    n = kv.shape[0]
    ns = seq // DSA_TS
    nkt_max = seq // DSA_TK
    nr = DSA_TS // DSA_TQ
    H = DSA_HEADS
    return pl.pallas_call(
        functools.partial(_dsa_kernel, topk=topk),
        grid=(batch, ns),
        in_specs=[pl.BlockSpec((H, DSA_TS, 2 * LANES), lambda b, s: (0, b * ns + s, 0)),
                  pl.BlockSpec((nr, LANES, H * DSA_TQ), lambda b, s: (b * ns + s, 0, 0)),
                  pl.BlockSpec((IDX_HEADS, DSA_TS), lambda b, s: (0, b * ns + s)),
                  pl.BlockSpec((seq, 2 * LANES), lambda b, s: (b, 0)),
                  pl.BlockSpec((nkt_max, 2 * LANES, DSA_TK), lambda b, s: (b, 0, 0)),
                  pl.BlockSpec((seq, LANES), lambda b, s: (b, 0)),
                  pl.BlockSpec(wuv_pad.shape, lambda b, s: (0, 0, 0)),
                  pl.BlockSpec((DSA_TK, DSA_TK), lambda b, s: (0, 0))],
        out_specs=pl.BlockSpec((DSA_TS, H * DSA_VDIM), lambda b, s: (b * ns + s, 0)),
        out_shape=jax.ShapeDtypeStruct((n, H * DSA_VDIM), BF16),
        scratch_shapes=[pltpu.VMEM((nkt_max, DSA_TS, DSA_TK), F32),
                        pltpu.VMEM((nkt_max, DSA_TK, DSA_TS), F32),
                        pltpu.VMEM((2, nkt_max, H * DSA_TQ, DSA_TK), F32),
                        pltpu.VMEM((2, H * DSA_TQ, LANES), F32),
                        pltpu.VMEM((H * DSA_TQ, LANES), F32),
                        pltpu.VMEM((H * DSA_TQ, 2 * LANES), F32)],
        compiler_params=_cparams(("parallel", "arbitrary")),
        name="dsa_attn",
    )(qcat, qidxt, widx, kv, kvt, kidx, wuv_pad, jnp.tri(DSA_TK, k=-1, dtype=BF16))


def _mem_attn(q_ref, k_ref, v_ref):
    scale = float(MEM_DH ** -0.5)
    outs = []
    for h in range(MEM_HEADS):
        sl = slice(h * MEM_DH, (h + 1) * MEM_DH)
        lg = _dot_nt((q_ref[:, sl] * scale).astype(BF16), k_ref[:, sl])
        p = jnp.exp((lg - jnp.max(lg, axis=-1, keepdims=True)).astype(BF16))
        v1 = jnp.concatenate([v_ref[:, sl], jnp.ones((v_ref.shape[0], MEM_DH), BF16)], axis=1)
        acc = _dot(p, v1)
        outs.append((acc[:, 0:MEM_DH] / acc[:, MEM_DH:2 * MEM_DH]).astype(BF16))
    return jnp.concatenate(outs, axis=1)


def _merge_mlp_kernel(x_ref, anw_ref, oa_ref, ob_ref, mq_ref, mk_ref, mv_ref, wg_ref, wbr_ref, wo_ref,
                      mnw_ref, wu_ref, wd_ref, fw_ref, o_ref, *, ff_chunk):
    x = x_ref[...]
    d = x.shape[-1]
    xn = _rms(x, anw_ref[...]).astype(BF16)
    merged = None
    for br, o_br in enumerate((oa_ref[...], ob_ref[...], _mem_attn(mq_ref, mk_ref, mv_ref))):
        gate = _sigmoid(_dot(xn, wg_ref[:, br * d:(br + 1) * d]))
        term = gate * _dot(o_br, wbr_ref[br])
        merged = term if merged is None else merged + term
    h = x + _dot(merged.astype(BF16), wo_ref[...])
    m = _rms(h, mnw_ref[...]).astype(BF16)
    acc = h
    for c0 in range(0, wu_ref.shape[1], ff_chunk):
        u = jnp.maximum(_dot(m, wu_ref[:, c0:c0 + ff_chunk]), 0.0)
        acc = acc + _dot((u * u).astype(BF16), wd_ref[c0:c0 + ff_chunk, :])
    o_ref[...] = _rms(acc, fw_ref[...])


def _merge_mlp(x, anw, o_a, o_b, proj_m, k_m, v_m, w_gate, w_br, w_out, mnw, w_up, w_down, fw, seq, tm,
               ff_chunk=1024):
    n, d = x.shape
    nt = seq // tm
    m = k_m.shape[0] // (n // seq)
    row = lambda wd: pl.BlockSpec((tm, wd), lambda i: (i, 0))
    mem = pl.BlockSpec((m, MEM_WIDTH), lambda i: (i // nt, 0))
    full = lambda a: pl.BlockSpec(a.shape, lambda i: (0,) * a.ndim, pipeline_mode=pl.Buffered(1))
    return pl.pallas_call(
        functools.partial(_merge_mlp_kernel, ff_chunk=ff_chunk),
        grid=(n // tm,),
        in_specs=[row(d), full(anw), row(o_a.shape[1]), row(o_b.shape[1]), row(MEM_WIDTH), mem, mem,
                  full(w_gate), full(w_br), full(w_out), full(mnw), full(w_up), full(w_down), full(fw)],
        out_specs=row(d),
        out_shape=jax.ShapeDtypeStruct((n, d), F32),
        compiler_params=_cparams(("parallel",)),
        name="merge_mlp",
    )(x, anw, o_a, o_b, proj_m, k_m, v_m, w_gate, w_br, w_out, mnw, w_up, w_down, fw)


def _pad_cols(w, width):
    return jnp.pad(w, ((0, 0), (0, width - w.shape[1])))


def _head_pad(w, width):
    return jnp.pad(w, ((0, 0), (0, 0), (0, width - w.shape[2]))).reshape(w.shape[0], -1)


def _with_rope_partner(w):
    half = w.shape[2] // 2
    swapped = jnp.concatenate([w[..., half:], w[..., :half]], axis=-1)
    return jnp.concatenate([_head_pad(w, LANES), _head_pad(swapped, LANES)], axis=1)


def _rope_tables(seq, d):
    inv = ROPE_THETA ** (-jnp.arange(0, d, 2, dtype=F32) / d)
    ang = jnp.arange(seq, dtype=F32)[:, None] * inv[None, :]
    cos, sin = jnp.cos(ang), jnp.sin(ang)
    cosf = _pad_cols(jnp.concatenate([cos, cos], axis=1), LANES)
    sins = _pad_cols(jnp.concatenate([-sin, sin], axis=1), LANES)
    return cosf, sins


def kernel(x, mem, attn_norm_w, w_in, hgrn_lower_bounds, hgrn_out_norm_w, dsa_q_norm_w, dsa_kv_norm_w,
           dsa_kidx_norm_w, dsa_w_uq, dsa_w_qr, dsa_w_uk, dsa_w_uv, dsa_w_qidx, mem_norm_w, w_mem_kv,
           w_br_hgrn, w_br_dsa, w_br_mem, w_out, mlp_norm_w, w_mlp_up, w_mlp_down, final_norm_w):
    B, T, D = x.shape
    N = B * T
    M = mem.shape[1]
    l = 0
    topk = min(TOPK_MAX, T // 4)
    tm = 512
    x2 = x.reshape(N, D)

    wi = w_in[l]
    o = 0
    cols = {}
    for name, wd in (("hq", HG_WIDTH), ("hf", HG_WIDTH), ("hv", HG_WIDTH), ("hg", HG_WIDTH),
                     ("cq", DSA_Q_LORA), ("ckv", DSA_KV_LORA), ("kr", DSA_ROPE), ("kidx", IDX_DIM),
                     ("widx", IDX_HEADS), ("mq", MEM_WIDTH), ("gate", N_BRANCH * D)):
        cols[name] = wi[:, o:o + wd]
        o += wd
    w_proj = jnp.concatenate(
        [cols["hq"], cols["hf"], cols["hv"], cols["hg"], cols["mq"], cols["cq"], cols["ckv"],
         _pad_cols(cols["kr"], LANES), _pad_cols(cols["kidx"], LANES), _pad_cols(cols["widx"], LANES)],
        axis=1).astype(BF16)
    w_gate = cols["gate"].astype(BF16)

    proj_h, proj_m, proj_d = _norm_proj(x2, attn_norm_w[l], w_proj, (4 * HG_WIDTH, MEM_WIDTH, DSA_GROUP),
                                        (F32, F32, F32), tm)

    lb_all = jnp.cumsum(jax.nn.softmax(hgrn_lower_bounds.astype(F32), axis=0), axis=0)
    o_a = _hgrn(proj_h, lb_all[l], hgrn_out_norm_w[l], B, T)

    wuq_pad = _head_pad(dsa_w_uq[l], LANES).astype(BF16)
    wqr_pad = _with_rope_partner(dsa_w_qr[l]).astype(BF16)
    wqi_pad = _with_rope_partner(dsa_w_qidx[l]).astype(BF16)
    wuk_pad = jnp.pad(dsa_w_uk[l], ((0, 0), (0, LANES - DSA_NOPE), (0, 0))).astype(BF16)
    wuv = dsa_w_uv[l]
    wuv_pad = jnp.stack([jnp.pad(wuv[h], ((0, 0), ((h % 2) * DSA_VDIM, LANES - DSA_VDIM - (h % 2) * DSA_VDIM)))
                         for h in range(DSA_HEADS)]).astype(BF16)
    kinw_pad = _pad_cols(dsa_kidx_norm_w[l].reshape(1, IDX_DIM), LANES)
    tabs = _rope_tables(T, DSA_ROPE) + _rope_tables(T, IDX_DIM)
    assert tm == DSA_TK and tm % DSA_TQ == 0
    qcat, qidxt, kv, kvt, kidx, widx = _dsa_prep(
        proj_d, dsa_q_norm_w[l].reshape(1, -1), dsa_kv_norm_w[l].reshape(1, -1), kinw_pad,
        wuq_pad, wqr_pad, wuk_pad, wqi_pad, tabs, T, tm)
    o_b = _dsa(qcat, qidxt, widx, kv, kvt, kidx, wuv_pad, B, T, topk)

    k_m, v_m = _norm_proj(mem.reshape(B * M, D), mem_norm_w[l], w_mem_kv[l].astype(BF16),
                          (MEM_WIDTH, MEM_WIDTH), (BF16, BF16), min(tm, B * M))

    w_br = jnp.stack([w_br_hgrn[l], w_br_dsa[l], w_br_mem[l]]).astype(BF16)
    out = _merge_mlp(x2, attn_norm_w[l].reshape(1, D), o_a, o_b, proj_m, k_m, v_m, w_gate, w_br,
                     w_out[l].astype(BF16), mlp_norm_w[l].reshape(1, D), w_mlp_up[l].astype(BF16),
                     w_mlp_down[l].astype(BF16), final_norm_w.reshape(1, D), T, tm)
    return out.reshape(B, T, D)
```

```python
import functools

import jax
import jax.numpy as jnp
from jax import lax
from jax.experimental import pallas as pl
from jax.experimental.pallas import tpu as pltpu

F32 = jnp.float32
BF16 = jnp.bfloat16
I32 = jnp.int32

EPS = 1e-6
LOG2E = 1.4426950408889634
ROPE_THETA = 10000.0
LANES = 128
SUBLANES = 8
HG_HEADS, HG_D, HG_CHUNK = 4, 128, 64
HG_WIDTH = HG_HEADS * HG_D
HG_SUB = 16
HG_STEP = 512
DSA_HEADS, DSA_Q_LORA, DSA_KV_LORA = 8, 256, 128
DSA_NOPE, DSA_ROPE, DSA_VDIM = 64, 32, 64
IDX_HEADS, IDX_DIM = 8, 64
TOPK_MAX = 256
MEM_HEADS, MEM_DH = 4, 128
MEM_WIDTH = MEM_HEADS * MEM_DH
N_BRANCH = 3
DSA_GROUP = 6 * LANES
VMEM_LIMIT = 56 * 1024 * 1024

DSA_TQ = 128
DSA_TS = 512
DSA_TK = 512
DSA_BISECT = 28


def _cparams(sem):
    return pltpu.CompilerParams(dimension_semantics=sem, vmem_limit_bytes=VMEM_LIMIT)


def _rms(x, w, n=None):
    n = x.shape[-1] if n is None else n
    ms = jnp.sum(x * x, axis=-1, keepdims=True) * (1.0 / n)
    return x * lax.rsqrt(ms + EPS) * w


def _sigmoid(x):
    return 1.0 / (1.0 + jnp.exp(-x))


def _dot(a, b):
    return jnp.dot(a, b, preferred_element_type=F32)


def _dot_nt(a, b):
    return lax.dot_general(a, b, (((1,), (1,)), ((), ())), preferred_element_type=F32)


def _dot_tn(a, b):
    return lax.dot_general(a, b, (((0,), (0,)), ((), ())), preferred_element_type=F32)


def _norm_proj_kernel(x_ref, nw_ref, w_ref, *out_refs):
    xn = _rms(x_ref[...], nw_ref[...]).astype(BF16)
    off = 0
    for o_ref in out_refs:
        wd = o_ref.shape[-1]
        o_ref[...] = _dot(xn, w_ref[:, off:off + wd]).astype(o_ref.dtype)
        off += wd


def _norm_proj(x, nw, w, widths, dtypes, tm):
    n, d = x.shape
    return pl.pallas_call(
        _norm_proj_kernel,
        grid=(n // tm,),
        in_specs=[pl.BlockSpec((tm, d), lambda i: (i, 0)),
                  pl.BlockSpec((1, d), lambda i: (0, 0)),
                  pl.BlockSpec(w.shape, lambda i: (0, 0))],
        out_specs=[pl.BlockSpec((tm, wd), lambda i: (i, 0)) for wd in widths],
        out_shape=[jax.ShapeDtypeStruct((n, wd), dt) for wd, dt in zip(widths, dtypes)],
        compiler_params=_cparams(("parallel",)),
        name="norm_proj",
    )(x, nw.reshape(1, d), w)


def _hgrn_kernel(q_ref, f_ref, v_ref, g_ref, lb_ref, onw_ref, o_ref, st_ref, b_ref, k_ref):
    @pl.when(pl.program_id(1) == 0)
    def _():
        st_ref[...] = jnp.zeros_like(st_ref)

    C, SB = HG_CHUNK, HG_SUB
    lb = lb_ref[...]
    ri = lax.broadcasted_iota(I32, (C, C), 0)
    ci = lax.broadcasted_iota(I32, (C, C), 1)
    tri = jnp.where(ri >= ci, 1.0, 0.0).astype(BF16)
    row_c = lax.broadcasted_iota(I32, (C, HG_D), 0)
    row_1 = lax.broadcasted_iota(I32, (SUBLANES, HG_D), 0)
    lane_1 = lax.broadcasted_iota(I32, (SUBLANES, C), 1)

    def chunk(ci_, carry):
        rows = pl.ds(pl.multiple_of(ci_ * C, C), C)
        f = lb + (1.0 - lb) * _sigmoid(f_ref[rows, :])
        lf = jnp.log(f) * LOG2E
        hi = lf.astype(BF16)
        r1 = lf - hi.astype(F32)
        mid = r1.astype(BF16)
        lo = (r1 - mid.astype(F32)).astype(BF16)
        b_ref[...] = _dot(tri, hi) + _dot(tri, mid) + _dot(tri, lo)
        k_ref[...] = 1.0 - f

        for h in range(HG_HEADS):
            sl = slice(h * HG_D, (h + 1) * HG_D)
            qh = q_ref[rows, sl]
            vh = v_ref[rows, sl]
            bh = b_ref[:, sl]
            kh = k_ref[:, sl]
            a_rows = [jnp.zeros((SB, C), F32)]
            for i in range(1, C // SB):
                bref = b_ref[i * SB:i * SB + 1, sl]
                qi = (qh[i * SB:(i + 1) * SB] * jnp.exp2(bh[i * SB:(i + 1) * SB] - bref)).astype(BF16)
                ki = (kh * jnp.exp2(jnp.where(row_c < i * SB, bref - bh, -jnp.inf))).astype(BF16)
                a_rows.append(_dot_nt(qi, ki))
            for i in range(C // SB):
                b_top, q_top = bh[i * SB:i * SB + SUBLANES], qh[i * SB:i * SB + SUBLANES]
                b_bot, q_bot = bh[i * SB + SUBLANES:(i + 1) * SB], qh[i * SB + SUBLANES:(i + 1) * SB]
                a_top, a_bot = a_rows[i][:SUBLANES], a_rows[i][SUBLANES:]
                for j in range(SB):
                    s = i * SB + j
                    brow = b_ref[s:s + 1, sl]
                    krow = k_ref[s:s + 1, sl]
                    if j < SUBLANES:
                        e = jnp.exp2(jnp.where(row_1 >= j, b_top - brow, -jnp.inf))
                        r = jnp.sum(e * (q_top * krow), axis=-1, keepdims=True)
                        a_top = jnp.where(lane_1 == s, r, a_top)
                        e = jnp.exp2(b_bot - brow)
                    else:
                        e = jnp.exp2(jnp.where(row_1 >= j - SUBLANES, b_bot - brow, -jnp.inf))
                    r = jnp.sum(e * (q_bot * krow), axis=-1, keepdims=True)
                    a_bot = jnp.where(lane_1 == s, r, a_bot)
                a_rows[i] = jnp.concatenate([a_top, a_bot], axis=0)
            a_mat = jnp.concatenate(a_rows, axis=0)
            st = st_ref[h]
            qe = (qh * jnp.exp2(bh)).astype(BF16)
            vb = vh.astype(BF16)
            o = _dot_nt(qe, st.astype(BF16)) + _dot(a_mat.astype(BF16), vb)
            blast = bh[C - 1:C, :]
            kd = (kh * jnp.exp2(blast - bh)).astype(BF16)
            st_ref[h] = st * jnp.exp2(blast) + _dot_tn(vb, kd)
            gh = g_ref[rows, sl]
            o_ref[rows, sl] = (_rms(o, onw_ref[:, sl]) * (gh * _sigmoid(gh))).astype(o_ref.dtype)
        return carry

    lax.fori_loop(0, HG_STEP // C, chunk, 0, unroll=True)


def _hgrn(proj_h, lb, onw, batch, seq):
    n = proj_h.shape[0]
    nc = seq // HG_STEP
    blk = lambda col: pl.BlockSpec((HG_STEP, HG_WIDTH), lambda b, c, col=col: (b * nc + c, col))
    vec = pl.BlockSpec((1, HG_WIDTH), lambda b, c: (0, 0))
    return pl.pallas_call(
        _hgrn_kernel,
        grid=(batch, nc),
        in_specs=[blk(0), blk(1), blk(2), blk(3), vec, vec],
        out_specs=pl.BlockSpec((HG_STEP, HG_WIDTH), lambda b, c: (b * nc + c, 0)),
        out_shape=jax.ShapeDtypeStruct((n, HG_WIDTH), BF16),
        scratch_shapes=[pltpu.VMEM((HG_HEADS, HG_D, HG_D), F32),
                        pltpu.VMEM((HG_CHUNK, HG_WIDTH), F32),
                        pltpu.VMEM((HG_CHUNK, HG_WIDTH), F32)],
        compiler_params=_cparams(("parallel", "arbitrary")),
        name="hgrn2",
    )(proj_h, proj_h, proj_h, proj_h, lb.reshape(1, HG_WIDTH), onw.reshape(1, HG_WIDTH))


def _rope_lanes(x, cosf, sins, half):
    nl = x.shape[-1]
    lane = lax.broadcasted_iota(I32, x.shape, 1)
    partner = jnp.where(lane % LANES < half, pltpu.roll(x, nl - half, 1), pltpu.roll(x, half, 1))
    return x * cosf + partner * sins


def _dsa_prep_kernel(pd_ref, qnw_ref, kvnw_ref, kinw_ref, wuq_ref, wqr_ref, wuk_ref, wqi_ref,
                     c32_ref, s32_ref, c64_ref, s64_ref,
                     qcat_ref, qidxt_ref, kv_ref, kvt_ref, kidx_ref, widx_ref):
    scale = float((DSA_NOPE + DSA_ROPE) ** -0.5)
    tq = DSA_TQ
    c32, s32, c64, s64 = c32_ref[...], s32_ref[...], c64_ref[...], s64_ref[...]
    cqn = _rms(pd_ref[:, 0:DSA_Q_LORA], qnw_ref[...]).astype(BF16)
    qn = _dot(cqn, wuq_ref[...]).astype(BF16)
    qr = _dot(cqn, wqr_ref[...])
    qi = _dot(cqn, wqi_ref[...])
    hw = DSA_HEADS * LANES
    for h in range(DSA_HEADS):
        sl = slice(h * LANES, (h + 1) * LANES)
        sp = slice(hw + h * LANES, hw + (h + 1) * LANES)
        qcat_ref[h, :, 0:LANES] = (_dot(qn[:, sl], wuk_ref[h]) * scale).astype(BF16)
        qcat_ref[h, :, LANES:2 * LANES] = ((qr[:, sl] * c32 + qr[:, sp] * s32) * scale).astype(BF16)
        qih = (qi[:, sl] * c64 + qi[:, sp] * s64) * float(IDX_DIM ** -0.5)
        for r in range(qih.shape[0] // tq):
            qidxt_ref[r, :, h * tq:(h + 1) * tq] = jnp.transpose(qih[r * tq:(r + 1) * tq]).astype(BF16)
    ckv = _rms(pd_ref[:, 256:384], kvnw_ref[...])
    krope = _rope_lanes(pd_ref[:, 384:512], c32, s32, DSA_ROPE // 2)
    kv_ref[:, 0:LANES] = ckv.astype(BF16)
    kv_ref[:, LANES:2 * LANES] = krope.astype(BF16)
    kvt_ref[0, 0:LANES, :] = jnp.transpose(ckv).astype(BF16)
    kvt_ref[0, LANES:2 * LANES, :] = jnp.transpose(krope).astype(BF16)
    kin = _rms(pd_ref[:, 512:640], kinw_ref[...], n=IDX_DIM)
    kidx_ref[...] = _rope_lanes(kin, c64, s64, IDX_DIM // 2).astype(BF16)
    widx_ref[...] = jnp.transpose(pd_ref[:, 640:768] * float(IDX_HEADS ** -0.5))[0:IDX_HEADS, :]


def _dsa_prep(proj_d, qnw, kvnw, kinw_pad, wuq_pad, wqr_pad, wuk_pad, wqi_pad, tabs, seq, tm):
    n = proj_d.shape[0]
    nt = seq // tm
    full = lambda a: pl.BlockSpec(a.shape, lambda i: (0,) * a.ndim)
    tab = pl.BlockSpec((tm, LANES), lambda i: (i % nt, 0))
    hd = DSA_HEADS
    return pl.pallas_call(
        _dsa_prep_kernel,
        grid=(n // tm,),
        in_specs=[pl.BlockSpec((tm, DSA_GROUP), lambda i: (i, 0)),
                  full(qnw), full(kvnw), full(kinw_pad), full(wuq_pad), full(wqr_pad), full(wuk_pad),
                  full(wqi_pad), tab, tab, tab, tab],
        out_specs=[pl.BlockSpec((hd, tm, 2 * LANES), lambda i: (0, i, 0)),
                   pl.BlockSpec((tm // DSA_TQ, LANES, hd * DSA_TQ), lambda i: (i, 0, 0)),
                   pl.BlockSpec((tm, 2 * LANES), lambda i: (i, 0)),
                   pl.BlockSpec((1, 2 * LANES, tm), lambda i: (i, 0, 0)),
                   pl.BlockSpec((tm, LANES), lambda i: (i, 0)),
                   pl.BlockSpec((IDX_HEADS, tm), lambda i: (0, i))],
        out_shape=[jax.ShapeDtypeStruct((hd, n, 2 * LANES), BF16),
                   jax.ShapeDtypeStruct((n // DSA_TQ, LANES, hd * DSA_TQ), BF16),
                   jax.ShapeDtypeStruct((n, 2 * LANES), BF16),
                   jax.ShapeDtypeStruct((n // tm, 2 * LANES, tm), BF16),
                   jax.ShapeDtypeStruct((n, LANES), BF16),
                   jax.ShapeDtypeStruct((IDX_HEADS, n), F32)],
        compiler_params=_cparams(("parallel",)),
        name="dsa_prep",
    )(proj_d, qnw, kvnw, kinw_pad, wuq_pad, wqr_pad, wuk_pad, wqi_pad, *tabs)


def _dsa_kernel(qcat_ref, qidxt_ref, widx_ref, kv_ref, kvt_ref, kidx_ref, wuv_ref, tril_ref, o_ref,
                sc_ref, x_ref, lg_ref, m_ref, mb_ref, acc_ref, *, topk):
    TQ, TK, H, TS = DSA_TQ, DSA_TK, DSA_HEADS, DSA_TS
    assert TS == TK
    NR = TS // TQ
    step = pl.program_id(1)
    FULL = (TK,) * NR
    DIAG = tuple((r + 1) * TQ for r in range(NR))

    def over_tiles(fn, init):
        carry = lax.fori_loop(0, step, lambda kt, c: fn(kt, FULL, c), init)
        return fn(step, DIAG, carry)

    def key_rows(kt, nk):
        return pl.ds(pl.multiple_of(kt * TK, TK), nk)

    def cols(r):
        return slice(r * TQ, (r + 1) * TQ)

    one, zero = jnp.float32(1.0), jnp.float32(0.0)
    ninf, pinf = jnp.float32(-jnp.inf), jnp.float32(jnp.inf)

    def fold(x, op):
        return op(x.reshape(x.shape[0] // SUBLANES, SUBLANES, x.shape[1]), axis=0)

    def score_tile(kt, nks, carry):
        mx, mn = list(carry[0]), list(carry[1])
        for r, nk in enumerate(nks):
            s_all = jnp.maximum(_dot(kidx_ref[key_rows(kt, nk), :], qidxt_ref[r]), 0.0)
            score = s_all[:, 0:TQ] * widx_ref[0:1, cols(r)]
            for h in range(1, IDX_HEADS):
                score = score + s_all[:, h * TQ:(h + 1) * TQ] * widx_ref[h:h + 1, cols(r)]
            lo_src = score
            if nks is DIAG:
                causal = (lax.broadcasted_iota(I32, (nk, TQ), 0)
                          <= r * TQ + lax.broadcasted_iota(I32, (nk, TQ), 1))
                score = jnp.where(causal, score, ninf)
                lo_src = jnp.where(causal, score, pinf)
            x_ref[kt, 0:nk, cols(r)] = score.astype(BF16)
            mx[r] = jnp.maximum(mx[r], fold(score, jnp.max))
            mn[r] = jnp.minimum(mn[r], fold(lo_src, jnp.min))
        return tuple(mx), tuple(mn)

    mx, mn = over_tiles(score_tile, (tuple(jnp.full((SUBLANES, TQ), ninf, F32) for _ in range(NR)),
                                     tuple(jnp.full((SUBLANES, TQ), pinf, F32) for _ in range(NR))))
    x_max = jnp.max(jnp.concatenate(mx, axis=1), axis=0, keepdims=True)
    x_min = jnp.min(jnp.concatenate(mn, axis=1), axis=0, keepdims=True)

    PK = 2 * SUBLANES

    def count_ge(t16):
        one_b, zero_b = jnp.ones((), BF16), jnp.zeros((), BF16)

        def block(x, t, acc):
            m = jnp.where(x >= t, one_b, zero_b)
            for g in range(x.shape[0] // PK):
                acc = acc + m[g * PK:(g + 1) * PK, :]
            return acc

        def tile(kt, nks, acc):
            if nks is FULL:
                return block(x_ref[kt], t16, acc)
            return jnp.concatenate([block(x_ref[kt, 0:nk, cols(r)], t16[:, cols(r)], acc[:, cols(r)])
                                    for r, nk in enumerate(nks)], axis=1)
        acc = over_tiles(tile, jnp.zeros((PK, TS), BF16))
        return jnp.sum(acc.astype(F32), axis=0, keepdims=True)

    kf = float(topk)

    def bisect(_, carry):
        lo, hi, c_hi = carry
        t16 = (lo + (hi - lo) * 0.5).astype(BF16)
        mid = t16.astype(F32)
        c = count_ge(t16)
        up = c >= kf
        return jnp.where(up, mid, lo), jnp.where(up, hi, mid), jnp.where(up, c_hi, c)

    slack = float(2.0 ** -6)
    lo0 = x_min - (jnp.abs(x_min) + 1.0) * slack
    hi0 = x_max + (jnp.abs(x_max) + 1.0) * slack
    lo, hi, c_hi = lax.fori_loop(0, DSA_BISECT, bisect, (lo0, hi0, jnp.zeros((1, TS), F32)))
    need = kf - c_hi

    def bias_block(kt, nk, c, seen):
        x = x_ref[kt, 0:nk, c].astype(F32)
        tie = (x >= lo[:, c]) & (x < hi[:, c])
        tie_f = jnp.where(tie, one, zero)
        rank = _dot(tril_ref[0:nk, 0:nk], tie_f.astype(BF16)) + seen
        bias_t = jnp.where((x >= hi[:, c]) | (tie & (rank < need[:, c])), zero, ninf)
        return bias_t, rank[nk - 1:nk, :] + tie_f[nk - 1:nk, :]

    def bias_tile(kt, nks, seen):
        if nks is FULL:
            out = []
            for r0 in range(0, NR, 2):
                c = slice(r0 * TQ, (r0 + 2) * TQ)
                bias_t, s = bias_block(kt, TK, c, seen[:, c])
                sc_ref[kt, cols(r0), :] = jnp.transpose(bias_t[:, 0:TQ])
                sc_ref[kt, cols(r0 + 1), :] = jnp.transpose(bias_t[:, TQ:2 * TQ])
                out.append(s)
            return jnp.concatenate(out, axis=1)
        out = []
        for r, nk in enumerate(nks):
            bias_t, s = bias_block(kt, nk, cols(r), seen[:, cols(r)])
            sc_ref[kt, cols(r), 0:nk] = jnp.transpose(bias_t)
            out.append(s)
        return jnp.concatenate(out, axis=1)

    over_tiles(bias_tile, jnp.zeros((1, TS), F32))

    def pass_a(r, kt, nk):
        qcat = qcat_ref[:, cols(r), :].reshape(H * TQ, 2 * LANES)
        lg = _dot(qcat, kvt_ref[kt, :, 0:nk]).reshape(H, TQ, nk) + sc_ref[kt, cols(r), 0:nk][None]
        lg = lg.reshape(H * TQ, nk)
        lg_ref[r % 2, kt, :, 0:nk] = lg
        part = lg[:, 0:LANES]
        for g in range(1, nk // LANES):
            part = jnp.maximum(part, lg[:, g * LANES:(g + 1) * LANES])
        m_ref[r % 2] = jnp.maximum(m_ref[r % 2], part)

    def pass_b(r, kt, nk):
        m_b = mb_ref[...]
        p = jnp.concatenate(
            [jnp.exp((lg_ref[r % 2, kt, :, g * LANES:(g + 1) * LANES] - m_b).astype(BF16))
             for g in range(nk // LANES)], axis=1)
        kv1 = jnp.concatenate([kv_ref[key_rows(kt, nk), 0:LANES], jnp.ones((nk, LANES), BF16)], axis=1)
        acc_ref[...] += _dot(p, kv1)

    def run(fns):
        def tile(kt, nks, carry):
            for fn, r in fns:
                fn(r, kt, nks[r])
            return carry
        over_tiles(tile, 0)

    m_ref[0] = jnp.full((H * TQ, LANES), -jnp.inf, F32)
    run([(pass_a, 0)])
    for r in range(NR):
        rows = cols(r)
        mb_ref[...] = jnp.broadcast_to(jnp.max(m_ref[r % 2], axis=-1, keepdims=True), (H * TQ, LANES))
        acc_ref[...] = jnp.zeros_like(acc_ref)
        if r + 1 < NR:
            m_ref[(r + 1) % 2] = jnp.full((H * TQ, LANES), -jnp.inf, F32)
            run([(pass_a, r + 1), (pass_b, r)])
        else:
            run([(pass_b, r)])
        o_lat = (acc_ref[:, 0:LANES] / acc_ref[:, LANES:2 * LANES]).astype(BF16)
        for pr in range(H // 2):
            h0, h1 = 2 * pr, 2 * pr + 1
            o_ref[rows, pr * LANES:(pr + 1) * LANES] = (
                _dot(o_lat[h0 * TQ:(h0 + 1) * TQ], wuv_ref[h0]) + _dot(o_lat[h1 * TQ:(h1 + 1) * TQ], wuv_ref[h1])
            ).astype(o_ref.dtype)


def _dsa(qcat, qidxt, widx, kv, kvt, kidx, wuv_pad, batch, seq, topk):
    n = kv.shape[0]
    ns = seq // DSA_TS
    nkt_max = seq // DSA_TK
    nr = DSA_TS // DSA_TQ
    H = DSA_HEADS
    return pl.pallas_call(
        functools.partial(_dsa_kernel, topk=topk),
        grid=(batch, ns),
        in_specs=[pl.BlockSpec((H, DSA_TS, 2 * LANES), lambda b, s: (0, b * ns + s, 0)),
                  pl.BlockSpec((nr, LANES, H * DSA_TQ), lambda b, s: (b * ns + s, 0, 0)),
                  pl.BlockSpec((IDX_HEADS, DSA_TS), lambda b, s: (0, b * ns + s)),
                  pl.BlockSpec((seq, 2 * LANES), lambda b, s: (b, 0)),
                  pl.BlockSpec((nkt_max, 2 * LANES, DSA_TK), lambda b, s: (b, 0, 0)),
                  pl.BlockSpec((seq, LANES), lambda b, s: (b, 0)),
                  pl.BlockSpec(wuv_pad.shape, lambda b, s: (0, 0, 0)),
                  pl.BlockSpec((DSA_TK, DSA_TK), lambda b, s: (0, 0))],
        out_specs=pl.BlockSpec((DSA_TS, H * DSA_VDIM), lambda b, s: (b * ns + s, 0)),
        out_shape=jax.ShapeDtypeStruct((n, H * DSA_VDIM), BF16),
        scratch_shapes=[pltpu.VMEM((nkt_max, DSA_TS, DSA_TK), F32),
                        pltpu.VMEM((nkt_max, DSA_TK, DSA_TS), BF16),
                        pltpu.VMEM((2, nkt_max, H * DSA_TQ, DSA_TK), F32),
                        pltpu.VMEM((2, H * DSA_TQ, LANES), F32),
                        pltpu.VMEM((H * DSA_TQ, LANES), F32),
                        pltpu.VMEM((H * DSA_TQ, 2 * LANES), F32)],
        compiler_params=_cparams(("parallel", "arbitrary")),
        name="dsa_attn",
    )(qcat, qidxt, widx, kv, kvt, kidx, wuv_pad, jnp.tri(DSA_TK, k=-1, dtype=BF16))


def _mem_attn(q_ref, k_ref, v_ref):
    scale = float(MEM_DH ** -0.5)
    outs = []
    for h in range(MEM_HEADS):
        sl = slice(h * MEM_DH, (h + 1) * MEM_DH)
        lg = _dot_nt((q_ref[:, sl] * scale).astype(BF16), k_ref[:, sl])
        p = jnp.exp((lg - jnp.max(lg, axis=-1, keepdims=True)).astype(BF16))
        v1 = jnp.concatenate([v_ref[:, sl], jnp.ones((v_ref.shape[0], MEM_DH), BF16)], axis=1)
        acc = _dot(p, v1)
        outs.append((acc[:, 0:MEM_DH] / acc[:, MEM_DH:2 * MEM_DH]).astype(BF16))
    return jnp.concatenate(outs, axis=1)


def _merge_mlp_kernel(x_ref, anw_ref, oa_ref, ob_ref, mq_ref, mk_ref, mv_ref, wg_ref, wbr_ref, wo_ref,
                      mnw_ref, wu_ref, wd_ref, fw_ref, o_ref, *, ff_chunk):
    x = x_ref[...]
    d = x.shape[-1]
    xn = _rms(x, anw_ref[...]).astype(BF16)
    merged = None
    for br, o_br in enumerate((oa_ref[...], ob_ref[...], _mem_attn(mq_ref, mk_ref, mv_ref))):
        gate = _sigmoid(_dot(xn, wg_ref[:, br * d:(br + 1) * d]))
        term = gate * _dot(o_br, wbr_ref[br])
        merged = term if merged is None else merged + term
    h = x + _dot(merged.astype(BF16), wo_ref[...])
    m = _rms(h, mnw_ref[...]).astype(BF16)
    acc = h
    for c0 in range(0, wu_ref.shape[1], ff_chunk):
        u = jnp.maximum(_dot(m, wu_ref[:, c0:c0 + ff_chunk]), 0.0)
        acc = acc + _dot((u * u).astype(BF16), wd_ref[c0:c0 + ff_chunk, :])
    o_ref[...] = _rms(acc, fw_ref[...])


def _merge_mlp(x, anw, o_a, o_b, proj_m, k_m, v_m, w_gate, w_br, w_out, mnw, w_up, w_down, fw, seq, tm,
               ff_chunk=1024):
    n, d = x.shape
    nt = seq // tm
    m = k_m.shape[0] // (n // seq)
    row = lambda wd: pl.BlockSpec((tm, wd), lambda i: (i, 0))
    mem = pl.BlockSpec((m, MEM_WIDTH), lambda i: (i // nt, 0))
    full = lambda a: pl.BlockSpec(a.shape, lambda i: (0,) * a.ndim, pipeline_mode=pl.Buffered(1))
    return pl.pallas_call(
        functools.partial(_merge_mlp_kernel, ff_chunk=ff_chunk),
        grid=(n // tm,),
        in_specs=[row(d), full(anw), row(o_a.shape[1]), row(o_b.shape[1]), row(MEM_WIDTH), mem, mem,
                  full(w_gate), full(w_br), full(w_out), full(mnw), full(w_up), full(w_down), full(fw)],
        out_specs=row(d),
        out_shape=jax.ShapeDtypeStruct((n, d), F32),
        compiler_params=_cparams(("parallel",)),
        name="merge_mlp",
    )(x, anw, o_a, o_b, proj_m, k_m, v_m, w_gate, w_br, w_out, mnw, w_up, w_down, fw)


def _pad_cols(w, width):
    return jnp.pad(w, ((0, 0), (0, width - w.shape[1])))


def _head_pad(w, width):
    return jnp.pad(w, ((0, 0), (0, 0), (0, width - w.shape[2]))).reshape(w.shape[0], -1)


def _with_rope_partner(w):
    half = w.shape[2] // 2
    swapped = jnp.concatenate([w[..., half:], w[..., :half]], axis=-1)
    return jnp.concatenate([_head_pad(w, LANES), _head_pad(swapped, LANES)], axis=1)


def _rope_tables(seq, d):
    inv = ROPE_THETA ** (-jnp.arange(0, d, 2, dtype=F32) / d)
    ang = jnp.arange(seq, dtype=F32)[:, None] * inv[None, :]
    cos, sin = jnp.cos(ang), jnp.sin(ang)
    cosf = _pad_cols(jnp.concatenate([cos, cos], axis=1), LANES)
    sins = _pad_cols(jnp.concatenate([-sin, sin], axis=1), LANES)
    return cosf, sins


def kernel(x, mem, attn_norm_w, w_in, hgrn_lower_bounds, hgrn_out_norm_w, dsa_q_norm_w, dsa_kv_norm_w,
           dsa_kidx_norm_w, dsa_w_uq, dsa_w_qr, dsa_w_uk, dsa_w_uv, dsa_w_qidx, mem_norm_w, w_mem_kv,
           w_br_hgrn, w_br_dsa, w_br_mem, w_out, mlp_norm_w, w_mlp_up, w_mlp_down, final_norm_w):
    B, T, D = x.shape
    N = B * T
    M = mem.shape[1]
    l = 0
    topk = min(TOPK_MAX, T // 4)
    tm = 512
    x2 = x.reshape(N, D)

    wi = w_in[l]
    o = 0
    cols = {}
    for name, wd in (("hq", HG_WIDTH), ("hf", HG_WIDTH), ("hv", HG_WIDTH), ("hg", HG_WIDTH),
                     ("cq", DSA_Q_LORA), ("ckv", DSA_KV_LORA), ("kr", DSA_ROPE), ("kidx", IDX_DIM),
                     ("widx", IDX_HEADS), ("mq", MEM_WIDTH), ("gate", N_BRANCH * D)):
        cols[name] = wi[:, o:o + wd]
        o += wd
    w_proj = jnp.concatenate(
        [cols["hq"], cols["hf"], cols["hv"], cols["hg"], cols["mq"], cols["cq"], cols["ckv"],
         _pad_cols(cols["kr"], LANES), _pad_cols(cols["kidx"], LANES), _pad_cols(cols["widx"], LANES)],
        axis=1).astype(BF16)
    w_gate = cols["gate"].astype(BF16)

    proj_h, proj_m, proj_d = _norm_proj(x2, attn_norm_w[l], w_proj, (4 * HG_WIDTH, MEM_WIDTH, DSA_GROUP),
                                        (F32, F32, F32), tm)

    lb_all = jnp.cumsum(jax.nn.softmax(hgrn_lower_bounds.astype(F32), axis=0), axis=0)
    o_a = _hgrn(proj_h, lb_all[l], hgrn_out_norm_w[l], B, T)

    wuq_pad = _head_pad(dsa_w_uq[l], LANES).astype(BF16)
    wqr_pad = _with_rope_partner(dsa_w_qr[l]).astype(BF16)
    wqi_pad = _with_rope_partner(dsa_w_qidx[l]).astype(BF16)
    wuk_pad = jnp.pad(dsa_w_uk[l], ((0, 0), (0, LANES - DSA_NOPE), (0, 0))).astype(BF16)
    wuv = dsa_w_uv[l]
    wuv_pad = jnp.stack([jnp.pad(wuv[h], ((0, 0), ((h % 2) * DSA_VDIM, LANES - DSA_VDIM - (h % 2) * DSA_VDIM)))
                         for h in range(DSA_HEADS)]).astype(BF16)
    kinw_pad = _pad_cols(dsa_kidx_norm_w[l].reshape(1, IDX_DIM), LANES)
    tabs = _rope_tables(T, DSA_ROPE) + _rope_tables(T, IDX_DIM)
    assert tm == DSA_TK and tm % DSA_TQ == 0
    qcat, qidxt, kv, kvt, kidx, widx = _dsa_prep(
        proj_d, dsa_q_norm_w[l].reshape(1, -1), dsa_kv_norm_w[l].reshape(1, -1), kinw_pad,
        wuq_pad, wqr_pad, wuk_pad, wqi_pad, tabs, T, tm)
    o_b = _dsa(qcat, qidxt, widx, kv, kvt, kidx, wuv_pad, B, T, topk)

    k_m, v_m = _norm_proj(mem.reshape(B * M, D), mem_norm_w[l], w_mem_kv[l].astype(BF16),
                          (MEM_WIDTH, MEM_WIDTH), (BF16, BF16), min(tm, B * M))

    w_br = jnp.stack([w_br_hgrn[l], w_br_dsa[l], w_br_mem[l]]).astype(BF16)
    out = _merge_mlp(x2, attn_norm_w[l].reshape(1, D), o_a, o_b, proj_m, k_m, v_m, w_gate, w_br,
                     w_out[l].astype(BF16), mlp_norm_w[l].reshape(1, D), w_mlp_up[l].astype(BF16),
                     w_mlp_down[l].astype(BF16), final_norm_w.reshape(1, D), T, tm)
    return out.reshape(B, T, D)
```

```python
import functools

import jax
import jax.numpy as jnp
from jax import lax
from jax.experimental import pallas as pl
from jax.experimental.pallas import tpu as pltpu

F32 = jnp.float32
BF16 = jnp.bfloat16
I32 = jnp.int32

EPS = 1e-6
LOG2E = 1.4426950408889634
ROPE_THETA = 10000.0
LANES = 128
SUBLANES = 8
HG_HEADS, HG_D, HG_CHUNK = 4, 128, 64
HG_WIDTH = HG_HEADS * HG_D
HG_SUB = 16
HG_STEP = 512
DSA_HEADS, DSA_Q_LORA, DSA_KV_LORA = 8, 256, 128
DSA_NOPE, DSA_ROPE, DSA_VDIM = 64, 32, 64
IDX_HEADS, IDX_DIM = 8, 64
TOPK_MAX = 256
MEM_HEADS, MEM_DH = 4, 128
MEM_WIDTH = MEM_HEADS * MEM_DH
N_BRANCH = 3
DSA_GROUP = 6 * LANES
VMEM_LIMIT = 56 * 1024 * 1024

DSA_TQ = 128
DSA_TS = 512
DSA_TK = 512
DSA_BISECT = 24


def _cparams(sem):
    return pltpu.CompilerParams(dimension_semantics=sem, vmem_limit_bytes=VMEM_LIMIT)


def _rms(x, w, n=None):
    n = x.shape[-1] if n is None else n
    ms = jnp.sum(x * x, axis=-1, keepdims=True) * (1.0 / n)
    return x * lax.rsqrt(ms + EPS) * w


def _sigmoid(x):
    return 1.0 / (1.0 + jnp.exp(-x))


def _dot(a, b):
    return jnp.dot(a, b, preferred_element_type=F32)


def _dot_nt(a, b):
    return lax.dot_general(a, b, (((1,), (1,)), ((), ())), preferred_element_type=F32)


def _dot_tn(a, b):
    return lax.dot_general(a, b, (((0,), (0,)), ((), ())), preferred_element_type=F32)


def _norm_proj_kernel(x_ref, nw_ref, w_ref, *out_refs):
    xn = _rms(x_ref[...], nw_ref[...]).astype(BF16)
    off = 0
    for o_ref in out_refs:
        wd = o_ref.shape[-1]
        o_ref[...] = _dot(xn, w_ref[:, off:off + wd]).astype(o_ref.dtype)
        off += wd


def _norm_proj(x, nw, w, widths, dtypes, tm):
    n, d = x.shape
    return pl.pallas_call(
        _norm_proj_kernel,
        grid=(n // tm,),
        in_specs=[pl.BlockSpec((tm, d), lambda i: (i, 0)),
                  pl.BlockSpec((1, d), lambda i: (0, 0)),
                  pl.BlockSpec(w.shape, lambda i: (0, 0))],
        out_specs=[pl.BlockSpec((tm, wd), lambda i: (i, 0)) for wd in widths],
        out_shape=[jax.ShapeDtypeStruct((n, wd), dt) for wd, dt in zip(widths, dtypes)],
        compiler_params=_cparams(("parallel",)),
        name="norm_proj",
    )(x, nw.reshape(1, d), w)


def _hgrn_kernel(q_ref, f_ref, v_ref, g_ref, lb_ref, onw_ref, o_ref, st_ref, b_ref, k_ref):
    @pl.when(pl.program_id(1) == 0)
    def _():
        st_ref[...] = jnp.zeros_like(st_ref)

    C, SB = HG_CHUNK, HG_SUB
    lb = lb_ref[...]
    ri = lax.broadcasted_iota(I32, (C, C), 0)
    ci = lax.broadcasted_iota(I32, (C, C), 1)
    tri = jnp.where(ri >= ci, 1.0, 0.0).astype(BF16)
    row_c = lax.broadcasted_iota(I32, (C, HG_D), 0)
    row_1 = lax.broadcasted_iota(I32, (SUBLANES, HG_D), 0)
    lane_1 = lax.broadcasted_iota(I32, (SUBLANES, C), 1)

    def chunk(ci_, carry):
        rows = pl.ds(pl.multiple_of(ci_ * C, C), C)
        f = lb + (1.0 - lb) * _sigmoid(f_ref[rows, :])
        lf = jnp.log(f) * LOG2E
        hi = lf.astype(BF16)
        r1 = lf - hi.astype(F32)
        mid = r1.astype(BF16)
        lo = (r1 - mid.astype(F32)).astype(BF16)
        b_ref[...] = _dot(tri, hi) + _dot(tri, mid) + _dot(tri, lo)
        k_ref[...] = 1.0 - f

        for h in range(HG_HEADS):
            sl = slice(h * HG_D, (h + 1) * HG_D)
            qh = q_ref[rows, sl]
            vh = v_ref[rows, sl]
            bh = b_ref[:, sl]
            kh = k_ref[:, sl]
            a_rows = [jnp.zeros((SB, C), F32)]
            for i in range(1, C // SB):
                bref = b_ref[i * SB:i * SB + 1, sl]
                qi = (qh[i * SB:(i + 1) * SB] * jnp.exp2(bh[i * SB:(i + 1) * SB] - bref)).astype(BF16)
                ki = (kh * jnp.exp2(jnp.where(row_c < i * SB, bref - bh, -jnp.inf))).astype(BF16)
                a_rows.append(_dot_nt(qi, ki))
            for i in range(C // SB):
                b_top, q_top = bh[i * SB:i * SB + SUBLANES], qh[i * SB:i * SB + SUBLANES]
                b_bot, q_bot = bh[i * SB + SUBLANES:(i + 1) * SB], qh[i * SB + SUBLANES:(i + 1) * SB]
                a_top, a_bot = a_rows[i][:SUBLANES], a_rows[i][SUBLANES:]
                for j in range(SB):
                    s = i * SB + j
                    brow = b_ref[s:s + 1, sl]
                    krow = k_ref[s:s + 1, sl]
                    if j < SUBLANES:
                        e = jnp.exp2(jnp.where(row_1 >= j, b_top - brow, -jnp.inf))
                        r = jnp.sum(e * (q_top * krow), axis=-1, keepdims=True)
                        a_top = jnp.where(lane_1 == s, r, a_top)
                        e = jnp.exp2(b_bot - brow)
                    else:
                        e = jnp.exp2(jnp.where(row_1 >= j - SUBLANES, b_bot - brow, -jnp.inf))
                    r = jnp.sum(e * (q_bot * krow), axis=-1, keepdims=True)
                    a_bot = jnp.where(lane_1 == s, r, a_bot)
                a_rows[i] = jnp.concatenate([a_top, a_bot], axis=0)
            a_mat = jnp.concatenate(a_rows, axis=0)
            st = st_ref[h]
            qe = (qh * jnp.exp2(bh)).astype(BF16)
            vb = vh.astype(BF16)
            o = _dot_nt(qe, st.astype(BF16)) + _dot(a_mat.astype(BF16), vb)
            blast = bh[C - 1:C, :]
            kd = (kh * jnp.exp2(blast - bh)).astype(BF16)
            st_ref[h] = st * jnp.exp2(blast) + _dot_tn(vb, kd)
            gh = g_ref[rows, sl]
            o_ref[rows, sl] = (_rms(o, onw_ref[:, sl]) * (gh * _sigmoid(gh))).astype(o_ref.dtype)
        return carry

    lax.fori_loop(0, HG_STEP // C, chunk, 0, unroll=True)


def _hgrn(proj_h, lb, onw, batch, seq):
    n = proj_h.shape[0]
    nc = seq // HG_STEP
    blk = lambda col: pl.BlockSpec((HG_STEP, HG_WIDTH), lambda b, c, col=col: (b * nc + c, col))
    vec = pl.BlockSpec((1, HG_WIDTH), lambda b, c: (0, 0))
    return pl.pallas_call(
        _hgrn_kernel,
        grid=(batch, nc),
        in_specs=[blk(0), blk(1), blk(2), blk(3), vec, vec],
        out_specs=pl.BlockSpec((HG_STEP, HG_WIDTH), lambda b, c: (b * nc + c, 0)),
        out_shape=jax.ShapeDtypeStruct((n, HG_WIDTH), BF16),
        scratch_shapes=[pltpu.VMEM((HG_HEADS, HG_D, HG_D), F32),
                        pltpu.VMEM((HG_CHUNK, HG_WIDTH), F32),
                        pltpu.VMEM((HG_CHUNK, HG_WIDTH), F32)],
        compiler_params=_cparams(("parallel", "arbitrary")),
        name="hgrn2",
    )(proj_h, proj_h, proj_h, proj_h, lb.reshape(1, HG_WIDTH), onw.reshape(1, HG_WIDTH))


def _rope_lanes(x, cosf, sins, half):
    nl = x.shape[-1]
    lane = lax.broadcasted_iota(I32, x.shape, 1)
    partner = jnp.where(lane % LANES < half, pltpu.roll(x, nl - half, 1), pltpu.roll(x, half, 1))
    return x * cosf + partner * sins


def _dsa_prep_kernel(pd_ref, qnw_ref, kvnw_ref, kinw_ref, wuq_ref, wqr_ref, wuk_ref, wqi_ref,
                     c32_ref, s32_ref, c64_ref, s64_ref,
                     qcat_ref, qidxt_ref, kv_ref, kvt_ref, kidx_ref, widx_ref):
    scale = float((DSA_NOPE + DSA_ROPE) ** -0.5)
    tq = DSA_TQ
    c32, s32, c64, s64 = c32_ref[...], s32_ref[...], c64_ref[...], s64_ref[...]
    cqn = _rms(pd_ref[:, 0:DSA_Q_LORA], qnw_ref[...]).astype(BF16)
    qn = _dot(cqn, wuq_ref[...]).astype(BF16)
    qr = _dot(cqn, wqr_ref[...])
    qi = _dot(cqn, wqi_ref[...])
    hw = DSA_HEADS * LANES
    for h in range(DSA_HEADS):
        sl = slice(h * LANES, (h + 1) * LANES)
        sp = slice(hw + h * LANES, hw + (h + 1) * LANES)
        qcat_ref[h, :, 0:LANES] = (_dot(qn[:, sl], wuk_ref[h]) * scale).astype(BF16)
        qcat_ref[h, :, LANES:2 * LANES] = ((qr[:, sl] * c32 + qr[:, sp] * s32) * scale).astype(BF16)
        qih = (qi[:, sl] * c64 + qi[:, sp] * s64) * float(IDX_DIM ** -0.5)
        for r in range(qih.shape[0] // tq):
            qidxt_ref[r, :, h * tq:(h + 1) * tq] = jnp.transpose(qih[r * tq:(r + 1) * tq]).astype(BF16)
    ckv = _rms(pd_ref[:, 256:384], kvnw_ref[...])
    krope = _rope_lanes(pd_ref[:, 384:512], c32, s32, DSA_ROPE // 2)
    kv_ref[:, 0:LANES] = ckv.astype(BF16)
    kv_ref[:, LANES:2 * LANES] = krope.astype(BF16)
    kvt_ref[0, 0:LANES, :] = jnp.transpose(ckv).astype(BF16)
    kvt_ref[0, LANES:2 * LANES, :] = jnp.transpose(krope).astype(BF16)
    kin = _rms(pd_ref[:, 512:640], kinw_ref[...], n=IDX_DIM)
    kidx_ref[...] = _rope_lanes(kin, c64, s64, IDX_DIM // 2).astype(BF16)
    widx_ref[...] = jnp.transpose(pd_ref[:, 640:768] * float(IDX_HEADS ** -0.5))[0:IDX_HEADS, :]


def _dsa_prep(proj_d, qnw, kvnw, kinw_pad, wuq_pad, wqr_pad, wuk_pad, wqi_pad, tabs, seq, tm):
    n = proj_d.shape[0]
    nt = seq // tm
    full = lambda a: pl.BlockSpec(a.shape, lambda i: (0,) * a.ndim)
    tab = pl.BlockSpec((tm, LANES), lambda i: (i % nt, 0))
    hd = DSA_HEADS
    return pl.pallas_call(
        _dsa_prep_kernel,
        grid=(n // tm,),
        in_specs=[pl.BlockSpec((tm, DSA_GROUP), lambda i: (i, 0)),
                  full(qnw), full(kvnw), full(kinw_pad), full(wuq_pad), full(wqr_pad), full(wuk_pad),
                  full(wqi_pad), tab, tab, tab, tab],
        out_specs=[pl.BlockSpec((hd, tm, 2 * LANES), lambda i: (0, i, 0)),
                   pl.BlockSpec((tm // DSA_TQ, LANES, hd * DSA_TQ), lambda i: (i, 0, 0)),
                   pl.BlockSpec((tm, 2 * LANES), lambda i: (i, 0)),
                   pl.BlockSpec((1, 2 * LANES, tm), lambda i: (i, 0, 0)),
                   pl.BlockSpec((tm, LANES), lambda i: (i, 0)),
                   pl.BlockSpec((IDX_HEADS, tm), lambda i: (0, i))],
        out_shape=[jax.ShapeDtypeStruct((hd, n, 2 * LANES), BF16),
                   jax.ShapeDtypeStruct((n // DSA_TQ, LANES, hd * DSA_TQ), BF16),
                   jax.ShapeDtypeStruct((n, 2 * LANES), BF16),
                   jax.ShapeDtypeStruct((n // tm, 2 * LANES, tm), BF16),
                   jax.ShapeDtypeStruct((n, LANES), BF16),
                   jax.ShapeDtypeStruct((IDX_HEADS, n), F32)],
        compiler_params=_cparams(("parallel",)),
        name="dsa_prep",
    )(proj_d, qnw, kvnw, kinw_pad, wuq_pad, wqr_pad, wuk_pad, wqi_pad, *tabs)


def _dsa_kernel(qcat_ref, qidxt_ref, widx_ref, kv_ref, kvt_ref, kidx_ref, wuv_ref, tril_ref, o_ref,
                sc_ref, x_ref, lg_ref, m_ref, mb_ref, acc_ref, *, topk):
    TQ, TK, H, TS = DSA_TQ, DSA_TK, DSA_HEADS, DSA_TS
    assert TS == TK
    NR = TS // TQ
    step = pl.program_id(1)
    FULL = (TK,) * NR
    DIAG = tuple((r + 1) * TQ for r in range(NR))

    def over_tiles(fn, init):
        carry = lax.fori_loop(0, step, lambda kt, c: fn(kt, FULL, c), init)
        return fn(step, DIAG, carry)

    def key_rows(kt, nk):
        return pl.ds(pl.multiple_of(kt * TK, TK), nk)

    def cols(r):
        return slice(r * TQ, (r + 1) * TQ)

    one, zero = jnp.float32(1.0), jnp.float32(0.0)
    ninf, pinf = jnp.float32(-jnp.inf), jnp.float32(jnp.inf)

    def fold(x, op):
        return op(x.reshape(x.shape[0] // SUBLANES, SUBLANES, x.shape[1]), axis=0)

    def score_tile(kt, nks, carry):
        mx, mn = list(carry[0]), list(carry[1])
        for r, nk in enumerate(nks):
            s_all = jnp.maximum(_dot(kidx_ref[key_rows(kt, nk), :], qidxt_ref[r]), 0.0)
            score = s_all[:, 0:TQ] * widx_ref[0:1, cols(r)]
            for h in range(1, IDX_HEADS):
                score = score + s_all[:, h * TQ:(h + 1) * TQ] * widx_ref[h:h + 1, cols(r)]
            lo_src = score
            if nks is DIAG:
                causal = (lax.broadcasted_iota(I32, (nk, TQ), 0)
                          <= r * TQ + lax.broadcasted_iota(I32, (nk, TQ), 1))
                score = jnp.where(causal, score, ninf)
                lo_src = jnp.where(causal, score, pinf)
            x_ref[kt, 0:nk, cols(r)] = score.astype(BF16)
            mx[r] = jnp.maximum(mx[r], fold(score, jnp.max))
            mn[r] = jnp.minimum(mn[r], fold(lo_src, jnp.min))
        return tuple(mx), tuple(mn)

    mx, mn = over_tiles(score_tile, (tuple(jnp.full((SUBLANES, TQ), ninf, F32) for _ in range(NR)),
                                     tuple(jnp.full((SUBLANES, TQ), pinf, F32) for _ in range(NR))))
    x_max = jnp.max(jnp.concatenate(mx, axis=1), axis=0, keepdims=True)
    x_min = jnp.min(jnp.concatenate(mn, axis=1), axis=0, keepdims=True)

    PK = 2 * SUBLANES

    def count_ge(t16):
        one_b, zero_b = jnp.ones((), BF16), jnp.zeros((), BF16)

        def block(x, t, acc):
            m = jnp.where(x >= t, one_b, zero_b)
            for g in range(x.shape[0] // PK):
                acc = acc + m[g * PK:(g + 1) * PK, :]
            return acc

        def tile(kt, nks, acc):
            if nks is FULL:
                return block(x_ref[kt], t16, acc)
            return jnp.concatenate([block(x_ref[kt, 0:nk, cols(r)], t16[:, cols(r)], acc[:, cols(r)])
                                    for r, nk in enumerate(nks)], axis=1)
        acc = over_tiles(tile, jnp.zeros((PK, TS), BF16))
        return jnp.sum(acc.astype(F32), axis=0, keepdims=True)

    kf = float(topk)

    def bisect(_, carry):
        lo, hi, c_hi = carry
        t16 = (lo + (hi - lo) * 0.5).astype(BF16)
        mid = t16.astype(F32)
        c = count_ge(t16)
        up = c >= kf
        return jnp.where(up, mid, lo), jnp.where(up, hi, mid), jnp.where(up, c_hi, c)

    slack = float(2.0 ** -6)
    lo0 = x_min - (jnp.abs(x_min) + 1.0) * slack
    hi0 = x_max + (jnp.abs(x_max) + 1.0) * slack
    lo, hi, c_hi = lax.fori_loop(0, DSA_BISECT, bisect, (lo0, hi0, jnp.zeros((1, TS), F32)))
    need = kf - c_hi

    def bias_block(kt, nk, c, seen):
        x = x_ref[kt, 0:nk, c].astype(F32)
        tie = (x >= lo[:, c]) & (x < hi[:, c])
        tie_f = jnp.where(tie, one, zero)
        rank = _dot(tril_ref[0:nk, 0:nk], tie_f.astype(BF16)) + seen
        bias_t = jnp.where((x >= hi[:, c]) | (tie & (rank < need[:, c])), zero, ninf)
        return bias_t, rank[nk - 1:nk, :] + tie_f[nk - 1:nk, :]

    def bias_tile(kt, nks, seen):
        if nks is FULL:
            out = []
            for r0 in range(0, NR, 2):
                c = slice(r0 * TQ, (r0 + 2) * TQ)
                bias_t, s = bias_block(kt, TK, c, seen[:, c])
                sc_ref[kt, cols(r0), :] = jnp.transpose(bias_t[:, 0:TQ])
                sc_ref[kt, cols(r0 + 1), :] = jnp.transpose(bias_t[:, TQ:2 * TQ])
                out.append(s)
            return jnp.concatenate(out, axis=1)
        out = []
        for r, nk in enumerate(nks):
            bias_t, s = bias_block(kt, nk, cols(r), seen[:, cols(r)])
            sc_ref[kt, cols(r), 0:nk] = jnp.transpose(bias_t)
            out.append(s)
        return jnp.concatenate(out, axis=1)

    over_tiles(bias_tile, jnp.zeros((1, TS), F32))

    def pass_a(r, kt, nk):
        qcat = qcat_ref[:, cols(r), :].reshape(H * TQ, 2 * LANES)
        lg = _dot(qcat, kvt_ref[kt, :, 0:nk]).reshape(H, TQ, nk) + sc_ref[kt, cols(r), 0:nk][None]
        lg = lg.reshape(H * TQ, nk)
        lg_ref[r % 2, kt, :, 0:nk] = lg
        part = lg[:, 0:LANES]
        for g in range(1, nk // LANES):
            part = jnp.maximum(part, lg[:, g * LANES:(g + 1) * LANES])
        m_ref[r % 2] = jnp.maximum(m_ref[r % 2], part)

    def pass_b(r, kt, nk):
        m_b = mb_ref[...]
        p = jnp.concatenate(
            [jnp.exp((lg_ref[r % 2, kt, :, g * LANES:(g + 1) * LANES] - m_b).astype(BF16))
             for g in range(nk // LANES)], axis=1)
        kv1 = jnp.concatenate([kv_ref[key_rows(kt, nk), 0:LANES], jnp.ones((nk, LANES), BF16)], axis=1)
        acc_ref[...] += _dot(p, kv1)

    def run(fns):
        def tile(kt, nks, carry):
            for fn, r in fns:
                fn(r, kt, nks[r])
            return carry
        over_tiles(tile, 0)

    m_ref[0] = jnp.full((H * TQ, LANES), -jnp.inf, F32)
    run([(pass_a, 0)])
    for r in range(NR):
        rows = cols(r)
        mb_ref[...] = jnp.broadcast_to(jnp.max(m_ref[r % 2], axis=-1, keepdims=True), (H * TQ, LANES))
        acc_ref[...] = jnp.zeros_like(acc_ref)
        if r + 1 < NR:
            m_ref[(r + 1) % 2] = jnp.full((H * TQ, LANES), -jnp.inf, F32)
            run([(pass_a, r + 1), (pass_b, r)])
        else:
            run([(pass_b, r)])
        o_lat = (acc_ref[:, 0:LANES] / acc_ref[:, LANES:2 * LANES]).astype(BF16)
        for pr in range(H // 2):
            h0, h1 = 2 * pr, 2 * pr + 1
            o_ref[rows, pr * LANES:(pr + 1) * LANES] = (
                _dot(o_lat[h0 * TQ:(h0 + 1) * TQ], wuv_ref[h0]) + _dot(o_lat[h1 * TQ:(h1 + 1) * TQ], wuv_ref[h1])
            ).astype(o_ref.dtype)


def _dsa(qcat, qidxt, widx, kv, kvt, kidx, wuv_pad, batch, seq, topk):
    n = kv.shape[0]
    ns = seq // DSA_TS
    nkt_max = seq // DSA_TK
    nr = DSA_TS // DSA_TQ
    H = DSA_HEADS
    return pl.pallas_call(
        functools.partial(_dsa_kernel, topk=topk),
        grid=(batch, ns),
        in_specs=[pl.BlockSpec((H, DSA_TS, 2 * LANES), lambda b, s: (0, b * ns + s, 0)),
                  pl.BlockSpec((nr, LANES, H * DSA_TQ), lambda b, s: (b * ns + s, 0, 0)),
                  pl.BlockSpec((IDX_HEADS, DSA_TS), lambda b, s: (0, b * ns + s)),
                  pl.BlockSpec((seq, 2 * LANES), lambda b, s: (b, 0)),
                  pl.BlockSpec((nkt_max, 2 * LANES, DSA_TK), lambda b, s: (b, 0, 0)),
                  pl.BlockSpec((seq, LANES), lambda b, s: (b, 0)),
                  pl.BlockSpec(wuv_pad.shape, lambda b, s: (0, 0, 0)),
                  pl.BlockSpec((DSA_TK, DSA_TK), lambda b, s: (0, 0))],
        out_specs=pl.BlockSpec((DSA_TS, H * DSA_VDIM), lambda b, s: (b * ns + s, 0)),
        out_shape=jax.ShapeDtypeStruct((n, H * DSA_VDIM), BF16),
        scratch_shapes=[pltpu.VMEM((nkt_max, DSA_TS, DSA_TK), F32),
                        pltpu.VMEM((nkt_max, DSA_TK, DSA_TS), BF16),
                        pltpu.VMEM((2, nkt_max, H * DSA_TQ, DSA_TK), F32),
                        pltpu.VMEM((2, H * DSA_TQ, LANES), F32),
                        pltpu.VMEM((H * DSA_TQ, LANES), F32),
                        pltpu.VMEM((H * DSA_TQ, 2 * LANES), F32)],
        compiler_params=_cparams(("parallel", "arbitrary")),
        name="dsa_attn",
    )(qcat, qidxt, widx, kv, kvt, kidx, wuv_pad, jnp.tri(DSA_TK, k=-1, dtype=BF16))


def _mem_attn(q_ref, k_ref, v_ref):
    scale = float(MEM_DH ** -0.5)
    outs = []
    for h in range(MEM_HEADS):
        sl = slice(h * MEM_DH, (h + 1) * MEM_DH)
        lg = _dot_nt((q_ref[:, sl] * scale).astype(BF16), k_ref[:, sl])
        p = jnp.exp((lg - jnp.max(lg, axis=-1, keepdims=True)).astype(BF16))
        v1 = jnp.concatenate([v_ref[:, sl], jnp.ones((v_ref.shape[0], MEM_DH), BF16)], axis=1)
        acc = _dot(p, v1)
        outs.append((acc[:, 0:MEM_DH] / acc[:, MEM_DH:2 * MEM_DH]).astype(BF16))
    return jnp.concatenate(outs, axis=1)


def _merge_mlp_kernel(x_ref, anw_ref, oa_ref, ob_ref, mq_ref, mk_ref, mv_ref, wg_ref, wbr_ref, wo_ref,
                      mnw_ref, wu_ref, wd_ref, fw_ref, o_ref, *, ff_chunk):
    x = x_ref[...]
    d = x.shape[-1]
    xn = _rms(x, anw_ref[...]).astype(BF16)
    merged = None
    for br, o_br in enumerate((oa_ref[...], ob_ref[...], _mem_attn(mq_ref, mk_ref, mv_ref))):
        gate = _sigmoid(_dot(xn, wg_ref[:, br * d:(br + 1) * d]))
        term = gate * _dot(o_br, wbr_ref[br])
        merged = term if merged is None else merged + term
    h = x + _dot(merged.astype(BF16), wo_ref[...])
    m = _rms(h, mnw_ref[...]).astype(BF16)
    acc = h
    for c0 in range(0, wu_ref.shape[1], ff_chunk):
        u = jnp.maximum(_dot(m, wu_ref[:, c0:c0 + ff_chunk]), 0.0)
        acc = acc + _dot((u * u).astype(BF16), wd_ref[c0:c0 + ff_chunk, :])
    o_ref[...] = _rms(acc, fw_ref[...])


def _merge_mlp(x, anw, o_a, o_b, proj_m, k_m, v_m, w_gate, w_br, w_out, mnw, w_up, w_down, fw, seq, tm,
               ff_chunk=1024):
    n, d = x.shape
    nt = seq // tm
    m = k_m.shape[0] // (n // seq)
    row = lambda wd: pl.BlockSpec((tm, wd), lambda i: (i, 0))
    mem = pl.BlockSpec((m, MEM_WIDTH), lambda i: (i // nt, 0))
    full = lambda a: pl.BlockSpec(a.shape, lambda i: (0,) * a.ndim, pipeline_mode=pl.Buffered(1))
    return pl.pallas_call(
        functools.partial(_merge_mlp_kernel, ff_chunk=ff_chunk),
        grid=(n // tm,),
        in_specs=[row(d), full(anw), row(o_a.shape[1]), row(o_b.shape[1]), row(MEM_WIDTH), mem, mem,
                  full(w_gate), full(w_br), full(w_out), full(mnw), full(w_up), full(w_down), full(fw)],
        out_specs=row(d),
        out_shape=jax.ShapeDtypeStruct((n, d), F32),
        compiler_params=_cparams(("parallel",)),
        name="merge_mlp",
    )(x, anw, o_a, o_b, proj_m, k_m, v_m, w_gate, w_br, w_out, mnw, w_up, w_down, fw)


def _pad_cols(w, width):
    return jnp.pad(w, ((0, 0), (0, width - w.shape[1])))


def _head_pad(w, width):
    return jnp.pad(w, ((0, 0), (0, 0), (0, width - w.shape[2]))).reshape(w.shape[0], -1)


def _with_rope_partner(w):
    half = w.shape[2] // 2
    swapped = jnp.concatenate([w[..., half:], w[..., :half]], axis=-1)
    return jnp.concatenate([_head_pad(w, LANES), _head_pad(swapped, LANES)], axis=1)


def _rope_tables(seq, d):
    inv = ROPE_THETA ** (-jnp.arange(0, d, 2, dtype=F32) / d)
    ang = jnp.arange(seq, dtype=F32)[:, None] * inv[None, :]
    cos, sin = jnp.cos(ang), jnp.sin(ang)
    cosf = _pad_cols(jnp.concatenate([cos, cos], axis=1), LANES)
    sins = _pad_cols(jnp.concatenate([-sin, sin], axis=1), LANES)
    return cosf, sins


def kernel(x, mem, attn_norm_w, w_in, hgrn_lower_bounds, hgrn_out_norm_w, dsa_q_norm_w, dsa_kv_norm_w,
           dsa_kidx_norm_w, dsa_w_uq, dsa_w_qr, dsa_w_uk, dsa_w_uv, dsa_w_qidx, mem_norm_w, w_mem_kv,
           w_br_hgrn, w_br_dsa, w_br_mem, w_out, mlp_norm_w, w_mlp_up, w_mlp_down, final_norm_w):
    B, T, D = x.shape
    N = B * T
    M = mem.shape[1]
    l = 0
    topk = min(TOPK_MAX, T // 4)
    tm = 512
    x2 = x.reshape(N, D)

    wi = w_in[l]
    o = 0
    cols = {}
    for name, wd in (("hq", HG_WIDTH), ("hf", HG_WIDTH), ("hv", HG_WIDTH), ("hg", HG_WIDTH),
                     ("cq", DSA_Q_LORA), ("ckv", DSA_KV_LORA), ("kr", DSA_ROPE), ("kidx", IDX_DIM),
                     ("widx", IDX_HEADS), ("mq", MEM_WIDTH), ("gate", N_BRANCH * D)):
        cols[name] = wi[:, o:o + wd]
        o += wd
    w_proj = jnp.concatenate(
        [cols["hq"], cols["hf"], cols["hv"], cols["hg"], cols["mq"], cols["cq"], cols["ckv"],
         _pad_cols(cols["kr"], LANES), _pad_cols(cols["kidx"], LANES), _pad_cols(cols["widx"], LANES)],
        axis=1).astype(BF16)
    w_gate = cols["gate"].astype(BF16)

    proj_h, proj_m, proj_d = _norm_proj(x2, attn_norm_w[l], w_proj, (4 * HG_WIDTH, MEM_WIDTH, DSA_GROUP),
                                        (F32, F32, F32), tm)

    lb_all = jnp.cumsum(jax.nn.softmax(hgrn_lower_bounds.astype(F32), axis=0), axis=0)
    o_a = _hgrn(proj_h, lb_all[l], hgrn_out_norm_w[l], B, T)

    wuq_pad = _head_pad(dsa_w_uq[l], LANES).astype(BF16)
    wqr_pad = _with_rope_partner(dsa_w_qr[l]).astype(BF16)
    wqi_pad = _with_rope_partner(dsa_w_qidx[l]).astype(BF16)
    wuk_pad = jnp.pad(dsa_w_uk[l], ((0, 0), (0, LANES - DSA_NOPE), (0, 0))).astype(BF16)
    wuv = dsa_w_uv[l]
    wuv_pad = jnp.stack([jnp.pad(wuv[h], ((0, 0), ((h % 2) * DSA_VDIM, LANES - DSA_VDIM - (h % 2) * DSA_VDIM)))
                         for h in range(DSA_HEADS)]).astype(BF16)
    kinw_pad = _pad_cols(dsa_kidx_norm_w[l].reshape(1, IDX_DIM), LANES)
    tabs = _rope_tables(T, DSA_ROPE) + _rope_tables(T, IDX_DIM)
    assert tm == DSA_TK and tm % DSA_TQ == 0
    qcat, qidxt, kv, kvt, kidx, widx = _dsa_prep(
        proj_d, dsa_q_norm_w[l].reshape(1, -1), dsa_kv_norm_w[l].reshape(1, -1), kinw_pad,
        wuq_pad, wqr_pad, wuk_pad, wqi_pad, tabs, T, tm)
    o_b = _dsa(qcat, qidxt, widx, kv, kvt, kidx, wuv_pad, B, T, topk)

    k_m, v_m = _norm_proj(mem.reshape(B * M, D), mem_norm_w[l], w_mem_kv[l].astype(BF16),
                          (MEM_WIDTH, MEM_WIDTH), (BF16, BF16), min(tm, B * M))

    w_br = jnp.stack([w_br_hgrn[l], w_br_dsa[l], w_br_mem[l]]).astype(BF16)
    out = _merge_mlp(x2, attn_norm_w[l].reshape(1, D), o_a, o_b, proj_m, k_m, v_m, w_gate, w_br,
                     w_out[l].astype(BF16), mlp_norm_w[l].reshape(1, D), w_mlp_up[l].astype(BF16),
                     w_mlp_down[l].astype(BF16), final_norm_w.reshape(1, D), T, tm)
    return out.reshape(B, T, D)
```

```python
import functools

import jax
import jax.numpy as jnp
from jax import lax
from jax.experimental import pallas as pl
from jax.experimental.pallas import tpu as pltpu

F32 = jnp.float32
BF16 = jnp.bfloat16
I32 = jnp.int32

EPS = 1e-6
LOG2E = 1.4426950408889634
ROPE_THETA = 10000.0
LANES = 128
SUBLANES = 8
HG_HEADS, HG_D, HG_CHUNK = 4, 128, 64
HG_WIDTH = HG_HEADS * HG_D
HG_SUB = 16
HG_STEP = 512
DSA_HEADS, DSA_Q_LORA, DSA_KV_LORA = 8, 256, 128
DSA_NOPE, DSA_ROPE, DSA_VDIM = 64, 32, 64
IDX_HEADS, IDX_DIM = 8, 64
TOPK_MAX = 256
MEM_HEADS, MEM_DH = 4, 128
MEM_WIDTH = MEM_HEADS * MEM_DH
N_BRANCH = 3
DSA_GROUP = 6 * LANES
VMEM_LIMIT = 56 * 1024 * 1024

DSA_TQ = 128
DSA_TS = 512
DSA_TK = 512
DSA_BISECT = 24


def _cparams(sem):
    return pltpu.CompilerParams(dimension_semantics=sem, vmem_limit_bytes=VMEM_LIMIT)


def _rms(x, w, n=None):
    n = x.shape[-1] if n is None else n
    ms = jnp.sum(x * x, axis=-1, keepdims=True) * (1.0 / n)
    return x * lax.rsqrt(ms + EPS) * w


def _sigmoid(x):
    return 1.0 / (1.0 + jnp.exp(-x))


def _dot(a, b):
    return jnp.dot(a, b, preferred_element_type=F32)


def _dot_nt(a, b):
    return lax.dot_general(a, b, (((1,), (1,)), ((), ())), preferred_element_type=F32)


def _dot_tn(a, b):
    return lax.dot_general(a, b, (((0,), (0,)), ((), ())), preferred_element_type=F32)


def _norm_proj_kernel(x_ref, nw_ref, w_ref, *out_refs):
    xn = _rms(x_ref[...], nw_ref[...]).astype(BF16)
    off = 0
    for o_ref in out_refs:
        wd = o_ref.shape[-1]
        o_ref[...] = _dot(xn, w_ref[:, off:off + wd]).astype(o_ref.dtype)
        off += wd


def _norm_proj(x, nw, w, widths, dtypes, tm):
    n, d = x.shape
    return pl.pallas_call(
        _norm_proj_kernel,
        grid=(n // tm,),
        in_specs=[pl.BlockSpec((tm, d), lambda i: (i, 0)),
                  pl.BlockSpec((1, d), lambda i: (0, 0)),
                  pl.BlockSpec(w.shape, lambda i: (0, 0))],
        out_specs=[pl.BlockSpec((tm, wd), lambda i: (i, 0)) for wd in widths],
        out_shape=[jax.ShapeDtypeStruct((n, wd), dt) for wd, dt in zip(widths, dtypes)],
        compiler_params=_cparams(("parallel",)),
        name="norm_proj",
    )(x, nw.reshape(1, d), w)


def _hgrn_kernel(q_ref, f_ref, v_ref, g_ref, lb_ref, onw_ref, o_ref, st_ref, b_ref, k_ref):
    @pl.when(pl.program_id(1) == 0)
    def _():
        st_ref[...] = jnp.zeros_like(st_ref)

    C, SB = HG_CHUNK, HG_SUB
    lb = lb_ref[...]
    ri = lax.broadcasted_iota(I32, (C, C), 0)
    ci = lax.broadcasted_iota(I32, (C, C), 1)
    tri = jnp.where(ri >= ci, 1.0, 0.0).astype(BF16)
    row_c = lax.broadcasted_iota(I32, (C, HG_D), 0)
    row_1 = lax.broadcasted_iota(I32, (SUBLANES, HG_D), 0)
    lane_1 = lax.broadcasted_iota(I32, (SUBLANES, C), 1)

    def chunk(ci_, carry):
        rows = pl.ds(pl.multiple_of(ci_ * C, C), C)
        f = lb + (1.0 - lb) * _sigmoid(f_ref[rows, :])
        lf = jnp.log(f) * LOG2E
        hi = lf.astype(BF16)
        r1 = lf - hi.astype(F32)
        mid = r1.astype(BF16)
        lo = (r1 - mid.astype(F32)).astype(BF16)
        b_ref[...] = _dot(tri, hi) + _dot(tri, mid) + _dot(tri, lo)
        k_ref[...] = 1.0 - f

        for h in range(HG_HEADS):
            sl = slice(h * HG_D, (h + 1) * HG_D)
            qh = q_ref[rows, sl]
            vh = v_ref[rows, sl]
            bh = b_ref[:, sl]
            kh = k_ref[:, sl]
            a_rows = [jnp.zeros((SB, C), F32)]
            for i in range(1, C // SB):
                bref = b_ref[i * SB:i * SB + 1, sl]
                qi = (qh[i * SB:(i + 1) * SB] * jnp.exp2(bh[i * SB:(i + 1) * SB] - bref)).astype(BF16)
                ki = (kh * jnp.exp2(jnp.where(row_c < i * SB, bref - bh, -jnp.inf))).astype(BF16)
                a_rows.append(_dot_nt(qi, ki))
            for i in range(C // SB):
                b_top, q_top = bh[i * SB:i * SB + SUBLANES], qh[i * SB:i * SB + SUBLANES]
                b_bot, q_bot = bh[i * SB + SUBLANES:(i + 1) * SB], qh[i * SB + SUBLANES:(i + 1) * SB]
                a_top, a_bot = a_rows[i][:SUBLANES], a_rows[i][SUBLANES:]
                for j in range(SB):
                    s = i * SB + j
                    brow = b_ref[s:s + 1, sl]
                    krow = k_ref[s:s + 1, sl]
                    if j < SUBLANES:
                        e = jnp.exp2(jnp.where(row_1 >= j, b_top - brow, -jnp.inf))
                        r = jnp.sum(e * (q_top * krow), axis=-1, keepdims=True)
                        a_top = jnp.where(lane_1 == s, r, a_top)
                        e = jnp.exp2(b_bot - brow)
                    else:
                        e = jnp.exp2(jnp.where(row_1 >= j - SUBLANES, b_bot - brow, -jnp.inf))
                    r = jnp.sum(e * (q_bot * krow), axis=-1, keepdims=True)
                    a_bot = jnp.where(lane_1 == s, r, a_bot)
                a_rows[i] = jnp.concatenate([a_top, a_bot], axis=0)
            a_mat = jnp.concatenate(a_rows, axis=0)
            st = st_ref[h]
            qe = (qh * jnp.exp2(bh)).astype(BF16)
            vb = vh.astype(BF16)
            o = _dot_nt(qe, st.astype(BF16)) + _dot(a_mat.astype(BF16), vb)
            blast = bh[C - 1:C, :]
            kd = (kh * jnp.exp2(blast - bh)).astype(BF16)
            st_ref[h] = st * jnp.exp2(blast) + _dot_tn(vb, kd)
            gh = g_ref[rows, sl]
            o_ref[rows, sl] = (_rms(o, onw_ref[:, sl]) * (gh * _sigmoid(gh))).astype(o_ref.dtype)
        return carry

    lax.fori_loop(0, HG_STEP // C, chunk, 0, unroll=True)


def _hgrn(proj_h, lb, onw, batch, seq):
    n = proj_h.shape[0]
    nc = seq // HG_STEP
    blk = lambda col: pl.BlockSpec((HG_STEP, HG_WIDTH), lambda b, c, col=col: (b * nc + c, col))
    vec = pl.BlockSpec((1, HG_WIDTH), lambda b, c: (0, 0))
    return pl.pallas_call(
        _hgrn_kernel,
        grid=(batch, nc),
        in_specs=[blk(0), blk(1), blk(2), blk(3), vec, vec],
        out_specs=pl.BlockSpec((HG_STEP, HG_WIDTH), lambda b, c: (b * nc + c, 0)),
        out_shape=jax.ShapeDtypeStruct((n, HG_WIDTH), BF16),
        scratch_shapes=[pltpu.VMEM((HG_HEADS, HG_D, HG_D), F32),
                        pltpu.VMEM((HG_CHUNK, HG_WIDTH), F32),
                        pltpu.VMEM((HG_CHUNK, HG_WIDTH), F32)],
        compiler_params=_cparams(("parallel", "arbitrary")),
        name="hgrn2",
    )(proj_h, proj_h, proj_h, proj_h, lb.reshape(1, HG_WIDTH), onw.reshape(1, HG_WIDTH))


def _rope_lanes(x, cosf, sins, half):
    nl = x.shape[-1]
    lane = lax.broadcasted_iota(I32, x.shape, 1)
    partner = jnp.where(lane % LANES < half, pltpu.roll(x, nl - half, 1), pltpu.roll(x, half, 1))
    return x * cosf + partner * sins


def _dsa_prep_kernel(pd_ref, qnw_ref, kvnw_ref, kinw_ref, wuq_ref, wqr_ref, wuk_ref, wqi_ref,
                     c32_ref, s32_ref, c64_ref, s64_ref,
                     qcat_ref, qidxt_ref, kv_ref, kvt_ref, kidx_ref, widx_ref):
    scale = float((DSA_NOPE + DSA_ROPE) ** -0.5)
    tq = DSA_TQ
    c32, s32, c64, s64 = c32_ref[...], s32_ref[...], c64_ref[...], s64_ref[...]
    cqn = _rms(pd_ref[:, 0:DSA_Q_LORA], qnw_ref[...]).astype(BF16)
    qn = _dot(cqn, wuq_ref[...]).astype(BF16)
    qr = _dot(cqn, wqr_ref[...])
    qi = _dot(cqn, wqi_ref[...])
    hw = DSA_HEADS * LANES
    for h in range(DSA_HEADS):
        sl = slice(h * LANES, (h + 1) * LANES)
        sp = slice(hw + h * LANES, hw + (h + 1) * LANES)
        qcat_ref[h, :, 0:LANES] = (_dot(qn[:, sl], wuk_ref[h]) * scale).astype(BF16)
        qcat_ref[h, :, LANES:2 * LANES] = ((qr[:, sl] * c32 + qr[:, sp] * s32) * scale).astype(BF16)
        qih = (qi[:, sl] * c64 + qi[:, sp] * s64) * float(IDX_DIM ** -0.5)
        for r in range(qih.shape[0] // tq):
            qidxt_ref[r, :, h * tq:(h + 1) * tq] = jnp.transpose(qih[r * tq:(r + 1) * tq]).astype(BF16)
    ckv = _rms(pd_ref[:, 256:384], kvnw_ref[...])
    krope = _rope_lanes(pd_ref[:, 384:512], c32, s32, DSA_ROPE // 2)
    kv_ref[:, 0:LANES] = ckv.astype(BF16)
    kv_ref[:, LANES:2 * LANES] = krope.astype(BF16)
    kvt_ref[0, 0:LANES, :] = jnp.transpose(ckv).astype(BF16)
    kvt_ref[0, LANES:2 * LANES, :] = jnp.transpose(krope).astype(BF16)
    kin = _rms(pd_ref[:, 512:640], kinw_ref[...], n=IDX_DIM)
    kidx_ref[...] = _rope_lanes(kin, c64, s64, IDX_DIM // 2).astype(BF16)
    widx_ref[...] = jnp.transpose(pd_ref[:, 640:768] * float(IDX_HEADS ** -0.5))[0:IDX_HEADS, :]


def _dsa_prep(proj_d, qnw, kvnw, kinw_pad, wuq_pad, wqr_pad, wuk_pad, wqi_pad, tabs, seq, tm):
    n = proj_d.shape[0]
    nt = seq // tm
    full = lambda a: pl.BlockSpec(a.shape, lambda i: (0,) * a.ndim)
    tab = pl.BlockSpec((tm, LANES), lambda i: (i % nt, 0))
    hd = DSA_HEADS
    return pl.pallas_call(
        _dsa_prep_kernel,
        grid=(n // tm,),
        in_specs=[pl.BlockSpec((tm, DSA_GROUP), lambda i: (i, 0)),
                  full(qnw), full(kvnw), full(kinw_pad), full(wuq_pad), full(wqr_pad), full(wuk_pad),
                  full(wqi_pad), tab, tab, tab, tab],
        out_specs=[pl.BlockSpec((hd, tm, 2 * LANES), lambda i: (0, i, 0)),
                   pl.BlockSpec((tm // DSA_TQ, LANES, hd * DSA_TQ), lambda i: (i, 0, 0)),
                   pl.BlockSpec((tm, 2 * LANES), lambda i: (i, 0)),
                   pl.BlockSpec((1, 2 * LANES, tm), lambda i: (i, 0, 0)),
                   pl.BlockSpec((tm, LANES), lambda i: (i, 0)),
                   pl.BlockSpec((IDX_HEADS, tm), lambda i: (0, i))],
        out_shape=[jax.ShapeDtypeStruct((hd, n, 2 * LANES), BF16),
                   jax.ShapeDtypeStruct((n // DSA_TQ, LANES, hd * DSA_TQ), BF16),
                   jax.ShapeDtypeStruct((n, 2 * LANES), BF16),
                   jax.ShapeDtypeStruct((n // tm, 2 * LANES, tm), BF16),
                   jax.ShapeDtypeStruct((n, LANES), BF16),
                   jax.ShapeDtypeStruct((IDX_HEADS, n), F32)],
        compiler_params=_cparams(("parallel",)),
        name="dsa_prep",
    )(proj_d, qnw, kvnw, kinw_pad, wuq_pad, wqr_pad, wuk_pad, wqi_pad, *tabs)


def _dsa_kernel(qcat_ref, qidxt_ref, widx_ref, kv_ref, kvt_ref, kidx_ref, wuv_ref, tril_ref, o_ref,
                sc_ref, x_ref, lg_ref, m_ref, mb_ref, acc_ref, *, topk):
    TQ, TK, H, TS = DSA_TQ, DSA_TK, DSA_HEADS, DSA_TS
    assert TS == TK
    NR = TS // TQ
    step = pl.program_id(1)
    FULL = (TK,) * NR
    DIAG = tuple((r + 1) * TQ for r in range(NR))

    def over_tiles(fn, init):
        carry = lax.fori_loop(0, step, lambda kt, c: fn(kt, FULL, c), init)
        return fn(step, DIAG, carry)

    def key_rows(kt, nk):
        return pl.ds(pl.multiple_of(kt * TK, TK), nk)

    def cols(r):
        return slice(r * TQ, (r + 1) * TQ)

    one, zero = jnp.float32(1.0), jnp.float32(0.0)
    ninf, pinf = jnp.float32(-jnp.inf), jnp.float32(jnp.inf)

    def fold(x, op):
        return op(x.reshape(x.shape[0] // SUBLANES, SUBLANES, x.shape[1]), axis=0)

    def score_tile(kt, nks, carry):
        mx, mn = list(carry[0]), list(carry[1])
        for r, nk in enumerate(nks):
            s_all = jnp.maximum(_dot(kidx_ref[key_rows(kt, nk), :], qidxt_ref[r]), 0.0)
            score = s_all[:, 0:TQ] * widx_ref[0:1, cols(r)]
            for h in range(1, IDX_HEADS):
                score = score + s_all[:, h * TQ:(h + 1) * TQ] * widx_ref[h:h + 1, cols(r)]
            lo_src = score
            if nks is DIAG:
                causal = (lax.broadcasted_iota(I32, (nk, TQ), 0)
                          <= r * TQ + lax.broadcasted_iota(I32, (nk, TQ), 1))
                score = jnp.where(causal, score, ninf)
                lo_src = jnp.where(causal, score, pinf)
            x_ref[kt, 0:nk, cols(r)] = score.astype(BF16)
            mx[r] = jnp.maximum(mx[r], fold(score, jnp.max))
            mn[r] = jnp.minimum(mn[r], fold(lo_src, jnp.min))
        return tuple(mx), tuple(mn)

    mx, mn = over_tiles(score_tile, (tuple(jnp.full((SUBLANES, TQ), ninf, F32) for _ in range(NR)),
                                     tuple(jnp.full((SUBLANES, TQ), pinf, F32) for _ in range(NR))))
    x_max = jnp.max(jnp.concatenate(mx, axis=1), axis=0, keepdims=True)
    x_min = jnp.min(jnp.concatenate(mn, axis=1), axis=0, keepdims=True)

    PK = 2 * SUBLANES

    def count_ge(t16):
        one_b, zero_b = jnp.ones((), BF16), jnp.zeros((), BF16)

        def block(x, t, acc):
            m = jnp.where(x >= t, one_b, zero_b)
            for g in range(x.shape[0] // PK):
                acc = acc + m[g * PK:(g + 1) * PK, :]
            return acc

        def tile(kt, nks, acc):
            if nks is FULL:
                return block(x_ref[kt], t16, acc)
            return jnp.concatenate([block(x_ref[kt, 0:nk, cols(r)], t16[:, cols(r)], acc[:, cols(r)])
                                    for r, nk in enumerate(nks)], axis=1)
        acc = over_tiles(tile, jnp.zeros((PK, TS), BF16))
        return jnp.sum(acc.astype(F32), axis=0, keepdims=True)

    kf = float(topk)

    def bisect(_, carry):
        lo, hi, c_hi = carry
        t16 = (lo + (hi - lo) * 0.5).astype(BF16)
        mid = t16.astype(F32)
        c = count_ge(t16)
        up = c >= kf
        return jnp.where(up, mid, lo), jnp.where(up, hi, mid), jnp.where(up, c_hi, c)

    slack = float(2.0 ** -6)
    lo0 = x_min - (jnp.abs(x_min) + 1.0) * slack
    hi0 = x_max + (jnp.abs(x_max) + 1.0) * slack
    lo, hi, c_hi = lax.fori_loop(0, DSA_BISECT, bisect, (lo0, hi0, jnp.zeros((1, TS), F32)))
    need = kf - c_hi

    def bias_block(kt, nk, c, seen):
        parts = []
        for k0 in range(0, nk, TQ):
            x = x_ref[kt, k0:k0 + TQ, c].astype(F32)
            tie = (x >= lo[:, c]) & (x < hi[:, c])
            tie_f = jnp.where(tie, one, zero)
            rank = _dot(tril_ref[0:TQ, 0:TQ], tie_f.astype(BF16)) + seen
            parts.append(jnp.where((x >= hi[:, c]) | (tie & (rank < need[:, c])), zero, ninf))
            seen = rank[TQ - 1:TQ, :] + tie_f[TQ - 1:TQ, :]
        return jnp.concatenate(parts, axis=0), seen

    def bias_tile(kt, nks, seen):
        if nks is FULL:
            out = []
            for r0 in range(0, NR, 2):
                c = slice(r0 * TQ, (r0 + 2) * TQ)
                bias_t, s = bias_block(kt, TK, c, seen[:, c])
                sc_ref[kt, cols(r0), :] = jnp.transpose(bias_t[:, 0:TQ])
                sc_ref[kt, cols(r0 + 1), :] = jnp.transpose(bias_t[:, TQ:2 * TQ])
                out.append(s)
            return jnp.concatenate(out, axis=1)
        out = []
        for r, nk in enumerate(nks):
            bias_t, s = bias_block(kt, nk, cols(r), seen[:, cols(r)])
            sc_ref[kt, cols(r), 0:nk] = jnp.transpose(bias_t)
            out.append(s)
        return jnp.concatenate(out, axis=1)

    over_tiles(bias_tile, jnp.zeros((1, TS), F32))

    def pass_a(r, kt, nk):
        qcat = qcat_ref[:, cols(r), :].reshape(H * TQ, 2 * LANES)
        lg = _dot(qcat, kvt_ref[kt, :, 0:nk]).reshape(H, TQ, nk) + sc_ref[kt, cols(r), 0:nk][None]
        lg = lg.reshape(H * TQ, nk)
        lg_ref[r % 2, kt, :, 0:nk] = lg
        part = lg[:, 0:LANES]
        for g in range(1, nk // LANES):
            part = jnp.maximum(part, lg[:, g * LANES:(g + 1) * LANES])
        m_ref[r % 2] = jnp.maximum(m_ref[r % 2], part)

    def pass_b(r, kt, nk):
        m_b = mb_ref[...]
        p = jnp.concatenate(
            [jnp.exp((lg_ref[r % 2, kt, :, g * LANES:(g + 1) * LANES] - m_b).astype(BF16))
             for g in range(nk // LANES)], axis=1)
        kv1 = jnp.concatenate([kv_ref[key_rows(kt, nk), 0:LANES], jnp.ones((nk, LANES), BF16)], axis=1)
        acc_ref[...] += _dot(p, kv1)

    def run(fns):
        def tile(kt, nks, carry):
            for fn, r in fns:
                fn(r, kt, nks[r])
            return carry
        over_tiles(tile, 0)

    m_ref[0] = jnp.full((H * TQ, LANES), -jnp.inf, F32)
    run([(pass_a, 0)])
    for r in range(NR):
        rows = cols(r)
        mb_ref[...] = jnp.broadcast_to(jnp.max(m_ref[r % 2], axis=-1, keepdims=True), (H * TQ, LANES))
        acc_ref[...] = jnp.zeros_like(acc_ref)
        if r + 1 < NR:
            m_ref[(r + 1) % 2] = jnp.full((H * TQ, LANES), -jnp.inf, F32)
            run([(pass_a, r + 1), (pass_b, r)])
        else:
            run([(pass_b, r)])
        o_lat = (acc_ref[:, 0:LANES] / acc_ref[:, LANES:2 * LANES]).astype(BF16)
        for pr in range(H // 2):
            h0, h1 = 2 * pr, 2 * pr + 1
            o_ref[rows, pr * LANES:(pr + 1) * LANES] = (
                _dot(o_lat[h0 * TQ:(h0 + 1) * TQ], wuv_ref[h0]) + _dot(o_lat[h1 * TQ:(h1 + 1) * TQ], wuv_ref[h1])
            ).astype(o_ref.dtype)


def _dsa(qcat, qidxt, widx, kv, kvt, kidx, wuv_pad, batch, seq, topk):
    n = kv.shape[0]
    ns = seq // DSA_TS
    nkt_max = seq // DSA_TK
    nr = DSA_TS // DSA_TQ
    H = DSA_HEADS
    return pl.pallas_call(
        functools.partial(_dsa_kernel, topk=topk),
        grid=(batch, ns),
        in_specs=[pl.BlockSpec((H, DSA_TS, 2 * LANES), lambda b, s: (0, b * ns + s, 0)),
                  pl.BlockSpec((nr, LANES, H * DSA_TQ), lambda b, s: (b * ns + s, 0, 0)),
                  pl.BlockSpec((IDX_HEADS, DSA_TS), lambda b, s: (0, b * ns + s)),
                  pl.BlockSpec((seq, 2 * LANES), lambda b, s: (b, 0)),
                  pl.BlockSpec((nkt_max, 2 * LANES, DSA_TK), lambda b, s: (b, 0, 0)),
                  pl.BlockSpec((seq, LANES), lambda b, s: (b, 0)),
                  pl.BlockSpec(wuv_pad.shape, lambda b, s: (0, 0, 0)),
                  pl.BlockSpec((DSA_TK, DSA_TK), lambda b, s: (0, 0))],
        out_specs=pl.BlockSpec((DSA_TS, H * DSA_VDIM), lambda b, s: (b * ns + s, 0)),
        out_shape=jax.ShapeDtypeStruct((n, H * DSA_VDIM), BF16),
        scratch_shapes=[pltpu.VMEM((nkt_max, DSA_TS, DSA_TK), F32),
                        pltpu.VMEM((nkt_max, DSA_TK, DSA_TS), BF16),
                        pltpu.VMEM((2, nkt_max, H * DSA_TQ, DSA_TK), F32),
                        pltpu.VMEM((2, H * DSA_TQ, LANES), F32),
                        pltpu.VMEM((H * DSA_TQ, LANES), F32),
                        pltpu.VMEM((H * DSA_TQ, 2 * LANES), F32)],
        compiler_params=_cparams(("parallel", "arbitrary")),
        name="dsa_attn",
    )(qcat, qidxt, widx, kv, kvt, kidx, wuv_pad, jnp.tri(DSA_TK, k=-1, dtype=BF16))


def _mem_attn(q_ref, k_ref, v_ref):
    scale = float(MEM_DH ** -0.5)
    outs = []
    for h in range(MEM_HEADS):
        sl = slice(h * MEM_DH, (h + 1) * MEM_DH)
        lg = _dot_nt((q_ref[:, sl] * scale).astype(BF16), k_ref[:, sl])
        p = jnp.exp((lg - jnp.max(lg, axis=-1, keepdims=True)).astype(BF16))
        v1 = jnp.concatenate([v_ref[:, sl], jnp.ones((v_ref.shape[0], MEM_DH), BF16)], axis=1)
        acc = _dot(p, v1)
        outs.append((acc[:, 0:MEM_DH] / acc[:, MEM_DH:2 * MEM_DH]).astype(BF16))
    return jnp.concatenate(outs, axis=1)


def _merge_mlp_kernel(x_ref, anw_ref, oa_ref, ob_ref, mq_ref, mk_ref, mv_ref, wg_ref, wbr_ref, wo_ref,
                      mnw_ref, wu_ref, wd_ref, fw_ref, o_ref, *, ff_chunk):
    x = x_ref[...]
    d = x.shape[-1]
    xn = _rms(x, anw_ref[...]).astype(BF16)
    merged = None
    for br, o_br in enumerate((oa_ref[...], ob_ref[...], _mem_attn(mq_ref, mk_ref, mv_ref))):
        gate = _sigmoid(_dot(xn, wg_ref[:, br * d:(br + 1) * d]))
        term = gate * _dot(o_br, wbr_ref[br])
        merged = term if merged is None else merged + term
    h = x + _dot(merged.astype(BF16), wo_ref[...])
    m = _rms(h, mnw_ref[...]).astype(BF16)
    acc = h
    for c0 in range(0, wu_ref.shape[1], ff_chunk):
        u = jnp.maximum(_dot(m, wu_ref[:, c0:c0 + ff_chunk]), 0.0)
        acc = acc + _dot((u * u).astype(BF16), wd_ref[c0:c0 + ff_chunk, :])
    o_ref[...] = _rms(acc, fw_ref[...])


def _merge_mlp(x, anw, o_a, o_b, proj_m, k_m, v_m, w_gate, w_br, w_out, mnw, w_up, w_down, fw, seq, tm,
               ff_chunk=1024):
    n, d = x.shape
    nt = seq // tm
    m = k_m.shape[0] // (n // seq)
    row = lambda wd: pl.BlockSpec((tm, wd), lambda i: (i, 0))
    mem = pl.BlockSpec((m, MEM_WIDTH), lambda i: (i // nt, 0))
    full = lambda a: pl.BlockSpec(a.shape, lambda i: (0,) * a.ndim, pipeline_mode=pl.Buffered(1))
    return pl.pallas_call(
        functools.partial(_merge_mlp_kernel, ff_chunk=ff_chunk),
        grid=(n // tm,),
        in_specs=[row(d), full(anw), row(o_a.shape[1]), row(o_b.shape[1]), row(MEM_WIDTH), mem, mem,
                  full(w_gate), full(w_br), full(w_out), full(mnw), full(w_up), full(w_down), full(fw)],
        out_specs=row(d),
        out_shape=jax.ShapeDtypeStruct((n, d), F32),
        compiler_params=_cparams(("parallel",)),
        name="merge_mlp",
    )(x, anw, o_a, o_b, proj_m, k_m, v_m, w_gate, w_br, w_out, mnw, w_up, w_down, fw)


def _pad_cols(w, width):
    return jnp.pad(w, ((0, 0), (0, width - w.shape[1])))


def _head_pad(w, width):
    return jnp.pad(w, ((0, 0), (0, 0), (0, width - w.shape[2]))).reshape(w.shape[0], -1)


def _with_rope_partner(w):
    half = w.shape[2] // 2
    swapped = jnp.concatenate([w[..., half:], w[..., :half]], axis=-1)
    return jnp.concatenate([_head_pad(w, LANES), _head_pad(swapped, LANES)], axis=1)


def _rope_tables(seq, d):
    inv = ROPE_THETA ** (-jnp.arange(0, d, 2, dtype=F32) / d)
    ang = jnp.arange(seq, dtype=F32)[:, None] * inv[None, :]
    cos, sin = jnp.cos(ang), jnp.sin(ang)
    cosf = _pad_cols(jnp.concatenate([cos, cos], axis=1), LANES)
    sins = _pad_cols(jnp.concatenate([-sin, sin], axis=1), LANES)
    return cosf, sins


def kernel(x, mem, attn_norm_w, w_in, hgrn_lower_bounds, hgrn_out_norm_w, dsa_q_norm_w, dsa_kv_norm_w,
           dsa_kidx_norm_w, dsa_w_uq, dsa_w_qr, dsa_w_uk, dsa_w_uv, dsa_w_qidx, mem_norm_w, w_mem_kv,
           w_br_hgrn, w_br_dsa, w_br_mem, w_out, mlp_norm_w, w_mlp_up, w_mlp_down, final_norm_w):
    B, T, D = x.shape
    N = B * T
    M = mem.shape[1]
    l = 0
    topk = min(TOPK_MAX, T // 4)
    tm = 512
    x2 = x.reshape(N, D)

    wi = w_in[l]
    o = 0
    cols = {}
    for name, wd in (("hq", HG_WIDTH), ("hf", HG_WIDTH), ("hv", HG_WIDTH), ("hg", HG_WIDTH),
                     ("cq", DSA_Q_LORA), ("ckv", DSA_KV_LORA), ("kr", DSA_ROPE), ("kidx", IDX_DIM),
                     ("widx", IDX_HEADS), ("mq", MEM_WIDTH), ("gate", N_BRANCH * D)):
        cols[name] = wi[:, o:o + wd]
        o += wd
    w_proj = jnp.concatenate(
        [cols["hq"], cols["hf"], cols["hv"], cols["hg"], cols["mq"], cols["cq"], cols["ckv"],
         _pad_cols(cols["kr"], LANES), _pad_cols(cols["kidx"], LANES), _pad_cols(cols["widx"], LANES)],
        axis=1).astype(BF16)
    w_gate = cols["gate"].astype(BF16)

    proj_h, proj_m, proj_d = _norm_proj(x2, attn_norm_w[l], w_proj, (4 * HG_WIDTH, MEM_WIDTH, DSA_GROUP),
                                        (F32, F32, F32), tm)

    lb_all = jnp.cumsum(jax.nn.softmax(hgrn_lower_bounds.astype(F32), axis=0), axis=0)
    o_a = _hgrn(proj_h, lb_all[l], hgrn_out_norm_w[l], B, T)

    wuq_pad = _head_pad(dsa_w_uq[l], LANES).astype(BF16)
    wqr_pad = _with_rope_partner(dsa_w_qr[l]).astype(BF16)
    wqi_pad = _with_rope_partner(dsa_w_qidx[l]).astype(BF16)
    wuk_pad = jnp.pad(dsa_w_uk[l], ((0, 0), (0, LANES - DSA_NOPE), (0, 0))).astype(BF16)
    wuv = dsa_w_uv[l]
    wuv_pad = jnp.stack([jnp.pad(wuv[h], ((0, 0), ((h % 2) * DSA_VDIM, LANES - DSA_VDIM - (h % 2) * DSA_VDIM)))
                         for h in range(DSA_HEADS)]).astype(BF16)
    kinw_pad = _pad_cols(dsa_kidx_norm_w[l].reshape(1, IDX_DIM), LANES)
    tabs = _rope_tables(T, DSA_ROPE) + _rope_tables(T, IDX_DIM)
    assert tm == DSA_TK and tm % DSA_TQ == 0
    qcat, qidxt, kv, kvt, kidx, widx = _dsa_prep(
        proj_d, dsa_q_norm_w[l].reshape(1, -1), dsa_kv_norm_w[l].reshape(1, -1), kinw_pad,
        wuq_pad, wqr_pad, wuk_pad, wqi_pad, tabs, T, tm)
    o_b = _dsa(qcat, qidxt, widx, kv, kvt, kidx, wuv_pad, B, T, topk)

    k_m, v_m = _norm_proj(mem.reshape(B * M, D), mem_norm_w[l], w_mem_kv[l].astype(BF16),
                          (MEM_WIDTH, MEM_WIDTH), (BF16, BF16), min(tm, B * M))

    w_br = jnp.stack([w_br_hgrn[l], w_br_dsa[l], w_br_mem[l]]).astype(BF16)
    out = _merge_mlp(x2, attn_norm_w[l].reshape(1, D), o_a, o_b, proj_m, k_m, v_m, w_gate, w_br,
                     w_out[l].astype(BF16), mlp_norm_w[l].reshape(1, D), w_mlp_up[l].astype(BF16),
                     w_mlp_down[l].astype(BF16), final_norm_w.reshape(1, D), T, tm)
    return out.reshape(B, T, D)
```
